```python
import math
import jax, jax.numpy as jnp
from jax import lax
import numpy as np

D_MODEL = 1024
BATCH = 8
SEQ = 4096
DEPTH = 4

EXPAND = 2
D_INNER = EXPAND * D_MODEL
HEAD_DIM = 64
HYENA_WIDTH = D_INNER // 2
ATTN_WIDTH = D_INNER - HYENA_WIDTH
N_HEADS_B = ATTN_WIDTH // HEAD_DIM
N_HEADS_C = D_INNER // HEAD_DIM
HYENA_SHORT = 3
HYENA_EMB_DIM = 33
HYENA_FILTER_ORDER = 64
HYENA_FAST_DECAY = 0.3
HYENA_SLOW_DECAY = 1.5
HYENA_TARGET = 1e-2
DILATED_PATTERNS = ((128, 1), (512, 4), (2048, 16))
QUERY_BLOCK = 64
GRID_W = 64
NA_ROWS_MAX = 8
NA_COLS = 16
REL_BUCKETS = 32
REL_MAX_DIST = 1024
PLE_DIM = 256
N_EVEN = (DEPTH + 1) // 2
N_ODD = DEPTH // 2
RMS_EPS = 1e-6
NEG_INF = -1e30

kernel_name = 'hybrid_hyena_dilated_natten_encoder'


def _rmsnorm(x, g):
    xf = x.astype(jnp.float32)
    y = xf * lax.rsqrt(jnp.mean(xf * xf, axis=-1, keepdims=True) + RMS_EPS)
    return (y * g.astype(jnp.float32)).astype(x.dtype)


def _short_conv(u, w, b):
    up = jnp.pad(u, ((0, 0), (1, 1), (0, 0)))
    return up[:, :-2] * w[0] + up[:, 1:-1] * w[1] + up[:, 2:] * w[2] + b


def _hyena_filters(L, w1, b1, w2, b2, w3, b3, w4, freq):
    f32 = jnp.float32
    t = jnp.linspace(0.0, 1.0, L, dtype=f32)[:, None]
    bands = (HYENA_EMB_DIM - 1) // 2
    fr = jnp.linspace(1e-4, bands - 1, bands, dtype=f32)[None, :]
    wpos = (2.0 * math.pi / L) * jnp.arange(L, dtype=f32)[:, None]
    z = jnp.concatenate([t, jnp.cos(fr * wpos), -jnp.sin(fr * wpos)], axis=-1)
    fq = freq.astype(f32)
    a = jnp.sin(fq * (z @ w1.astype(f32) + b1.astype(f32)))
    a = jnp.sin(fq * (a @ w2.astype(f32) + b2.astype(f32)))
    a = jnp.sin(fq * (a @ w3.astype(f32) + b3.astype(f32)))
    h = (a @ w4.astype(f32)).reshape(L, 2, 2, HYENA_WIDTH)
    min_decay = math.log(HYENA_TARGET) / HYENA_SLOW_DECAY
    max_decay = math.log(HYENA_TARGET) / HYENA_FAST_DECAY
    deltas = jnp.abs(jnp.linspace(min_decay, max_decay, HYENA_WIDTH, dtype=f32))
    h = h * jnp.exp(-t * deltas[None, :])[:, None, None, :]
    fwd = h[:, :, 0]
    bwd = h[1:, :, 1][::-1]
    k = jnp.concatenate([fwd, jnp.zeros((1, 2, HYENA_WIDTH), f32), bwd], axis=0)
    return k / jnp.sum(jnp.abs(k), axis=0, keepdims=True)


def _fft_conv(u, k, skip):
    L = u.shape[1]
    uf = u.astype(jnp.float32)
    spec = jnp.fft.rfft(uf, n=2 * L, axis=1) * jnp.fft.rfft(k, n=2 * L, axis=0)[None]
    y = jnp.fft.irfft(spec, n=2 * L, axis=1)[:, :L]
    return (y + uf * skip.astype(jnp.float32)).astype(u.dtype)


def _hyena(proj, conv_w, conv_b, w1, b1, w2, b2, w3, b3, w4, freq, skip):
    L = proj.shape[1]
    k = _hyena_filters(L, w1, b1, w2, b2, w3, b3, w4, freq)
    v, x1, x2 = jnp.split(_short_conv(proj, conv_w, conv_b), 3, axis=-1)
    z = x1 * _fft_conv(v, k[:, 0], skip[0])
    return x2 * _fft_conv(z, k[:, 1], skip[1])


def _t5_bucket(rel):
    half_b = REL_BUCKETS // 2
    max_exact = half_b // 2
    ret = jnp.where(rel > 0, half_b, 0)
    n = jnp.abs(rel)
    nf = jnp.maximum(n, 1).astype(jnp.float32)
    large = max_exact + (jnp.log(nf / max_exact) / math.log(REL_MAX_DIST / max_exact)
                         * (half_b - max_exact)).astype(jnp.int32)
    large = jnp.minimum(large, half_b - 1)
    return ret + jnp.where(n < max_exact, n, large)


def _dilated_band(q, k, v, window, dil, rel_bias):
    B, H, L, hd = q.shape
    n = L // dil
    half = window // (2 * dil)
    def to_res(a):
        return a.reshape(B, H, n, dil, hd).transpose(0, 1, 3, 2, 4)
    qr, kr, vr = to_res(q), to_res(k), to_res(v)
    qb = math.gcd(n, QUERY_BLOCK)
    nb = n // qb
    kw = qb + 2 * half
    pad = ((0, 0), (0, 0), (0, 0), (half, half), (0, 0))
    kp, vp = jnp.pad(kr, pad), jnp.pad(vr, pad)
    idx = (jnp.arange(nb) * qb)[:, None] + jnp.arange(kw)[None, :]
    kb, vb = kp[:, :, :, idx], vp[:, :, :, idx]
    qblk = qr.reshape(B, H, dil, nb, qb, hd)
    s = jnp.einsum('bhrnqd,bhrnkd->bhrnqk', qblk, kb).astype(jnp.float32) * (hd ** -0.5)
    rel = jnp.arange(kw)[None, :] - half - jnp.arange(qb)[:, None]
    bias = rel_bias[_t5_bucket(rel * dil)].astype(jnp.float32)
    s = s + bias.transpose(2, 0, 1)[None, :, None, None]
    keypos = idx - half
    valid = (jnp.abs(rel) <= half)[None] & ((keypos >= 0) & (keypos < n))[:, None, :]
    s = jnp.where(valid, s, NEG_INF)
    m = jnp.max(s, axis=-1, keepdims=True)
    pe = jnp.exp(s - m)
    den = jnp.sum(pe, axis=-1, keepdims=True)
    o = jnp.einsum('bhrnqk,bhrnkd->bhrnqd', (pe / den).astype(v.dtype), vb)
    lse = (m + jnp.log(den))[..., 0]
    o = o.reshape(B, H, dil, n, hd).transpose(0, 1, 3, 2, 4).reshape(B, H, L, hd)
    lse = lse.reshape(B, H, dil, n).transpose(0, 1, 3, 2).reshape(B, H, L)
    return o, lse


def _dilated_attention(qkv, rel_bias):
    B, L, _ = qkv.shape
    q, k, v = [a.reshape(B, L, N_HEADS_B, HEAD_DIM).transpose(0, 2, 1, 3)
               for a in jnp.split(qkv, 3, axis=-1)]
    outs, lses = [], []
    for window, dil in DILATED_PATTERNS:
        o, lse = _dilated_band(q, k, v, window, dil, rel_bias)
        outs.append(o.astype(jnp.float32))
        lses.append(lse)
    wts = jax.nn.softmax(jnp.stack(lses, axis=0), axis=0)
    o = jnp.einsum('pbhl,pbhld->bhld', wts, jnp.stack(outs, axis=0)).astype(qkv.dtype)
    return o.transpose(0, 2, 1, 3).reshape(B, L, ATTN_WIDTH)


def _neighbourhood_attention(qkv, rpb):
    B, L, _ = qkv.shape
    rows = L // GRID_W
    kr_n = min(NA_ROWS_MAX, rows)
    q, k, v = [a.reshape(B, rows, GRID_W, N_HEADS_C, HEAD_DIM).transpose(0, 3, 1, 2, 4)
               for a in jnp.split(qkv, 3, axis=-1)]
    cols = jnp.arange(GRID_W)
    cs = jnp.clip(cols - NA_COLS // 2, 0, GRID_W - NA_COLS)
    col_idx = cs[:, None] + jnp.arange(NA_COLS)[None, :]
    col_bias_idx = col_idx - cols[:, None] + NA_COLS - 1
    scale = HEAD_DIM ** -0.5
    def row_fn(r):
        rs = jnp.clip(r - kr_n // 2, 0, rows - kr_n)
        qr = lax.dynamic_index_in_dim(q, r, axis=2, keepdims=False)
        kr = lax.dynamic_slice_in_dim(k, rs, kr_n, axis=2)[:, :, :, col_idx]
        vr = lax.dynamic_slice_in_dim(v, rs, kr_n, axis=2)[:, :, :, col_idx]
        s = jnp.einsum('bhqd,bhrqcd->bhqrc', qr, kr).astype(jnp.float32) * scale
        row_bias_idx = rs + jnp.arange(kr_n) - r + NA_ROWS_MAX - 1
        bias = rpb[:, row_bias_idx[None, :, None], col_bias_idx[:, None, :]]
        s = s + bias[None].astype(jnp.float32)
        pr = jax.nn.softmax(s.reshape(B, N_HEADS_C, GRID_W, kr_n * NA_COLS), axis=-1).reshape(s.shape)
        return jnp.einsum('bhqrc,bhrqcd->bhqd', pr.astype(v.dtype), vr)
    o = lax.map(row_fn, jnp.arange(rows))
    return o.transpose(1, 0, 3, 2, 4).reshape(B, L, D_INNER)


def setup_inputs(seed: int = 0) -> dict:
    key = jax.random.key(seed)
    ks = jax.random.split(key, 22)
    f32 = jnp.float32
    def nrm(k, shape, s):
        return jax.random.normal(k, shape, f32) * s
    fo = HYENA_FILTER_ORDER
    return {
        'x': nrm(ks[0], (BATCH, SEQ, D_MODEL), 1.0),
        'p': nrm(ks[1], (DEPTH, BATCH, SEQ, PLE_DIM), 1.0),
        'w_in': nrm(ks[2], (DEPTH, D_MODEL, 4 * D_INNER), D_MODEL ** -0.5),
        'w_out': nrm(ks[3], (DEPTH, D_INNER, D_MODEL), D_INNER ** -0.5),
        'norm_pre': 1.0 + nrm(ks[4], (DEPTH, D_MODEL), 0.02),
        'norm_post': 1.0 + nrm(ks[5], (DEPTH, D_MODEL), 0.02),
        'hyena_conv_w': nrm(ks[6], (N_EVEN, HYENA_SHORT, 3 * HYENA_WIDTH), HYENA_SHORT ** -0.5),
        'hyena_conv_b': nrm(ks[7], (N_EVEN, 3 * HYENA_WIDTH), 0.02),
        'hyena_w1': nrm(ks[8], (N_EVEN, HYENA_EMB_DIM, fo), HYENA_EMB_DIM ** -0.5),
        'hyena_b1': nrm(ks[9], (N_EVEN, fo), 0.02),
        'hyena_w2': nrm(ks[10], (N_EVEN, fo, fo), fo ** -0.5),
        'hyena_b2': nrm(ks[11], (N_EVEN, fo), 0.02),
        'hyena_w3': nrm(ks[12], (N_EVEN, fo, fo), fo ** -0.5),
        'hyena_b3': nrm(ks[13], (N_EVEN, fo), 0.02),
        'hyena_w4': nrm(ks[14], (N_EVEN, fo, 4 * HYENA_WIDTH), fo ** -0.5),
        'hyena_freq': 1.0 + nrm(ks[15], (N_EVEN, fo), 0.02),
        'hyena_skip': nrm(ks[16], (N_EVEN, 2, HYENA_WIDTH), 0.5),
        'rel_bias': nrm(ks[17], (REL_BUCKETS, N_HEADS_B), 0.1),
        'na_rpb': nrm(ks[18], (N_ODD, N_HEADS_C, 2 * NA_ROWS_MAX - 1, 2 * NA_COLS - 1), 0.1),
        'ple_proj': nrm(ks[19], (DEPTH, PLE_DIM, D_MODEL), PLE_DIM ** -0.5),
        'ple_norm': 1.0 + nrm(ks[20], (DEPTH, D_MODEL), 0.02),
        'ple_gate': nrm(ks[21], (DEPTH, D_MODEL, D_MODEL), D_MODEL ** -0.5),
    }


def reference(x, p, w_in, w_out, norm_pre, norm_post, hyena_conv_w, hyena_conv_b,
              hyena_w1, hyena_b1, hyena_w2, hyena_b2, hyena_w3, hyena_b3, hyena_w4,
              hyena_freq, hyena_skip, rel_bias, na_rpb, ple_proj, ple_norm, ple_gate):
    h = x
    for i in range(DEPTH):
        j = i // 2
        u = _rmsnorm(h, norm_pre[i])
        proj = u @ w_in[i]
        if i % 2 == 0:
            a_in, a_gate, b_qkv, b_gate = jnp.split(
                proj, [3 * HYENA_WIDTH, 4 * HYENA_WIDTH, 4 * HYENA_WIDTH + 3 * ATTN_WIDTH], axis=-1)
            ya = _hyena(a_in, hyena_conv_w[j], hyena_conv_b[j], hyena_w1[j], hyena_b1[j],
                        hyena_w2[j], hyena_b2[j], hyena_w3[j], hyena_b3[j], hyena_w4[j],
                        hyena_freq[j], hyena_skip[j]) * jax.nn.silu(a_gate)
            yb = _dilated_attention(b_qkv, rel_bias) * jax.nn.silu(b_gate)
            y = jnp.concatenate([ya, yb], axis=-1)
        else:
            c_qkv, c_gate = jnp.split(proj, [3 * D_INNER], axis=-1)
            y = _neighbourhood_attention(c_qkv, na_rpb[j]) * jax.nn.silu(c_gate)
        h = h + _rmsnorm(y @ w_out[i], norm_post[i])
        e = _rmsnorm(p[i] @ ple_proj[i], ple_norm[i])
        h = h + jax.nn.sigmoid(h @ ple_gate[i]) * e
    return h
```

```python
import functools
import math

import numpy as np
import jax
import jax.numpy as jnp
from jax import lax
from jax.experimental import pallas as pl
from jax.experimental.pallas import tpu as pltpu

F32 = jnp.float32
BF16 = jnp.bfloat16

HEAD_DIM = 64
LANES = 128
HYENA_EMB_DIM = 33
HYENA_FAST_DECAY = 0.3
HYENA_SLOW_DECAY = 1.5
HYENA_TARGET = 1e-2
DILATED_PATTERNS = ((128, 1), (512, 4), (2048, 16))
GRID_W = 64
NA_ROWS = 8
NA_COLS = 16
REL_BUCKETS = 32
REL_MAX_DIST = 1024
RMS_EPS = 1e-6
NEG_INF = -1e30
ATT_QB = 128
ATT_HALF = 64
ATT_KW = ATT_QB + 2 * ATT_HALF
VMEM_LIMIT = 56 * 1024 * 1024


def _cparams(sem):
    return pltpu.CompilerParams(dimension_semantics=sem, vmem_limit_bytes=VMEM_LIMIT)


def _rms(x, g):
    return x * lax.rsqrt(jnp.mean(x * x, axis=-1, keepdims=True) + RMS_EPS) * g


def _rmsnorm_kernel(x_ref, g_ref, o_ref):
    o_ref[...] = _rms(x_ref[...], g_ref[...]).astype(o_ref.dtype)


def _rmsnorm_bf16(x2d, g, tm=1024):
    m, d = x2d.shape
    return pl.pallas_call(
        _rmsnorm_kernel,
        grid=(m // tm,),
        in_specs=[pl.BlockSpec((tm, d), lambda i: (i, 0)), pl.BlockSpec((1, d), lambda i: (0, 0))],
        out_specs=pl.BlockSpec((tm, d), lambda i: (i, 0)),
        out_shape=jax.ShapeDtypeStruct((m, d), BF16),
        compiler_params=_cparams(("parallel",)),
        name="rmsnorm",
    )(x2d, g.reshape(1, d))


def _matmul_kernel(a_ref, b_ref, o_ref):
    o_ref[...] = jnp.dot(a_ref[...], b_ref[...], preferred_element_type=F32).astype(o_ref.dtype)


def _matmul_bf16(a, b, tm=1024, tn=1024):
    m, k = a.shape
    _, n = b.shape
    return pl.pallas_call(
        _matmul_kernel,
        grid=(m // tm, n // tn),
        in_specs=[pl.BlockSpec((tm, k), lambda i, j: (i, 0)), pl.BlockSpec((k, tn), lambda i, j: (0, j))],
        out_specs=pl.BlockSpec((tm, tn), lambda i, j: (i, j)),
        out_shape=jax.ShapeDtypeStruct((m, n), BF16),
        compiler_params=_cparams(("parallel", "parallel")),
        name="in_proj",
    )(a, b)


def _proj_t_kernel(w_ref, u_ref, cw_ref, cb_ref, o_ref):
    p = lax.dot_general(w_ref[...], u_ref[0], (((1,), (1,)), ((), ())), preferred_element_type=F32)
    seq = p.shape[1]
    t = lax.broadcasted_iota(jnp.int32, p.shape, 1)
    prev = jnp.where(t == 0, 0.0, pltpu.roll(p, 1, axis=1))
    nxt = jnp.where(t == seq - 1, 0.0, pltpu.roll(p, seq - 1, axis=1))
    cw = cw_ref[...]
    o_ref[0] = (prev * cw[:, 0:1] + p * cw[:, 1:2] + nxt * cw[:, 2:3] + cb_ref[...]).astype(o_ref.dtype)


def _proj_t(w_t, u, conv_w, conv_b, tc=256):
    c, d = w_t.shape
    b, seq, _ = u.shape
    return pl.pallas_call(
        _proj_t_kernel,
        grid=(b, c // tc),
        in_specs=[
            pl.BlockSpec((tc, d), lambda i, j: (j, 0)),
            pl.BlockSpec((1, seq, d), lambda i, j: (i, 0, 0)),
            pl.BlockSpec((tc, 3), lambda i, j: (j, 0)),
            pl.BlockSpec((tc, 1), lambda i, j: (j, 0)),
        ],
        out_specs=pl.BlockSpec((1, tc, seq), lambda i, j: (i, j, 0)),
        out_shape=jax.ShapeDtypeStruct((b, c, seq), BF16),
        compiler_params=_cparams(("parallel", "parallel")),
        name="in_proj_t_conv3",
    )(w_t, u, conv_w, conv_b)


def _dft_matrices(seq):
    n_fft = 2 * seq
    hi, lo = seq // LANES, LANES
    k = jnp.arange(seq, dtype=jnp.int32)[None, :]

    def table(rows):
        ang = ((rows[:, None] * k) % n_fft).astype(F32) * (2.0 * math.pi / n_fft)
        return jnp.cos(ang), -jnp.sin(ang)

    ar, ai = table(jnp.arange(hi, dtype=jnp.int32) * lo)
    br, bi = table(jnp.arange(lo, dtype=jnp.int32))
    wr = (ar[:, None, :] * br[None, :, :] - ai[:, None, :] * bi[None, :, :]).reshape(seq, seq)
    wi = (ar[:, None, :] * bi[None, :, :] + ai[:, None, :] * br[None, :, :]).reshape(seq, seq)
    sign = jnp.where(jnp.arange(seq) % 2 == 0, 1.0, -1.0).astype(F32)
    wi = wi.at[:, 0].set(sign)
    fwd = jnp.stack([wr, wi]).astype(BF16)
    inv = jnp.stack([wr, wi.T]).astype(BF16)
    return fwd, inv


def _dft_raw_kernel(v_ref, w_ref, o_ref):
    v = v_ref[...]
    o_ref[0] = jnp.dot(v, w_ref[0], preferred_element_type=F32)
    o_ref[1] = jnp.dot(v, w_ref[1], preferred_element_type=F32)


def _dft_raw(rows, fwd, tm=512, tk=512):
    r, seq = rows.shape
    return pl.pallas_call(
        _dft_raw_kernel,
        grid=(r // tm, seq // tk),
        in_specs=[pl.BlockSpec((tm, seq), lambda i, j: (i, 0)), pl.BlockSpec((2, seq, tk), lambda i, j: (0, 0, j))],
        out_specs=pl.BlockSpec((2, tm, tk), lambda i, j: (0, i, j)),
        out_shape=jax.ShapeDtypeStruct((2, r, seq), F32),
        compiler_params=_cparams(("parallel", "parallel")),
        name="filter_dft",
    )(rows, fwd)


def _dft_filter_kernel(v_ref, w_ref, k_ref, o_ref):
    v = v_ref[0]
    sr = jnp.dot(v, w_ref[0], preferred_element_type=F32)
    si = jnp.dot(v, w_ref[1], preferred_element_type=F32)
    o_ref[0, 0] = (sr * k_ref[0] - si * k_ref[1]).astype(o_ref.dtype)
    o_ref[1, 0] = (sr * k_ref[1] + si * k_ref[2]).astype(o_ref.dtype)


def _dft_filter(src, row_block0, c, fwd, kspec, tm=512, tk=256):
    b, _, seq = src.shape
    nc = c // tm
    return pl.pallas_call(
        _dft_filter_kernel,
        grid=(b, nc, seq // tk),
        in_specs=[
            pl.BlockSpec((1, tm, seq), lambda i, j, q: (i, row_block0 + j, 0)),
            pl.BlockSpec((2, seq, tk), lambda i, j, q: (0, 0, q)),
            pl.BlockSpec((3, tm, tk), lambda i, j, q: (0, j, q)),
        ],
        out_specs=pl.BlockSpec((2, 1, tm, tk), lambda i, j, q: (0, i, j, q)),
        out_shape=jax.ShapeDtypeStruct((2, b, c, seq), BF16),
        compiler_params=_cparams(("parallel", "parallel", "parallel")),
        name="hyena_dft_filter",
    )(src, fwd, kspec)


def _idft(y_ref, w_ref):
    return (jnp.dot(y_ref[0, 0], w_ref[0], preferred_element_type=F32)
            + jnp.dot(y_ref[1, 0], w_ref[1], preferred_element_type=F32))


def _idft_gate1_kernel(y_ref, w_ref, v_ref, x1_ref, skip_ref, o_ref):
    conv = _idft(y_ref, w_ref) + v_ref[0].astype(F32) * skip_ref[...]
    o_ref[0] = (x1_ref[0].astype(F32) * conv).astype(o_ref.dtype)


def _idft_gate2_kernel(y_ref, w_ref, z_ref, x2_ref, g_ref, skip_ref, o_ref):
    conv = _idft(y_ref, w_ref) + z_ref[0].astype(F32) * skip_ref[...]
    g = g_ref[0].astype(F32)
    ya = x2_ref[0].astype(F32) * conv * (g * jax.nn.sigmoid(g))
    o_ref[0] = ya.T.astype(o_ref.dtype)


def _idft_gate1(y, inv, hy, skip, c, tm=512, tn=256):
    _, b, _, seq = y.shape
    nc = c // tm
    return pl.pallas_call(
        _idft_gate1_kernel,
        grid=(b, nc, seq // tn),
        in_specs=[
            pl.BlockSpec((2, 1, tm, seq), lambda i, j, q: (0, i, j, 0)),
            pl.BlockSpec((2, seq, tn), lambda i, j, q: (0, 0, q)),
            pl.BlockSpec((1, tm, tn), lambda i, j, q: (i, j, q)),
            pl.BlockSpec((1, tm, tn), lambda i, j, q: (i, nc + j, q)),
            pl.BlockSpec((tm, 1), lambda i, j, q: (j, 0)),
        ],
        out_specs=pl.BlockSpec((1, tm, tn), lambda i, j, q: (i, j, q)),
        out_shape=jax.ShapeDtypeStruct((b, c, seq), BF16),
        compiler_params=_cparams(("parallel", "parallel", "parallel")),
        name="hyena_idft_gate1",
    )(y, inv, hy, hy, skip)


def _idft_gate2(y, inv, z, hy, skip, c, tm=512, tn=256):
    _, b, _, seq = y.shape
    nc = c // tm
    return pl.pallas_call(
        _idft_gate2_kernel,
        grid=(b, nc, seq // tn),
        in_specs=[
            pl.BlockSpec((2, 1, tm, seq), lambda i, j, q: (0, i, j, 0)),
            pl.BlockSpec((2, seq, tn), lambda i, j, q: (0, 0, q)),
            pl.BlockSpec((1, tm, tn), lambda i, j, q: (i, j, q)),
            pl.BlockSpec((1, tm, tn), lambda i, j, q: (i, 2 * nc + j, q)),
            pl.BlockSpec((1, tm, tn), lambda i, j, q: (i, 3 * nc + j, q)),
            pl.BlockSpec((tm, 1), lambda i, j, q: (j, 0)),
        ],
        out_specs=pl.BlockSpec((1, tn, tm), lambda i, j, q: (i, q, j)),
        out_shape=jax.ShapeDtypeStruct((b, seq, c), BF16),
        compiler_params=_cparams(("parallel", "parallel", "parallel")),
        name="hyena_idft_gate2",
    )(y, inv, z, hy, hy, skip)


def _hyena_filters_t(seq, c, w1, b1, w2, b2, w3, b3, w4, freq):
    t = jnp.linspace(0.0, 1.0, seq, dtype=F32)[:, None]
    bands = (HYENA_EMB_DIM - 1) // 2
    fr = jnp.linspace(1e-4, bands - 1, bands, dtype=F32)[None, :]
    wpos = (2.0 * math.pi / seq) * jnp.arange(seq, dtype=F32)[:, None]
    z = jnp.concatenate([t, jnp.cos(fr * wpos), -jnp.sin(fr * wpos)], axis=-1)
    mm = functools.partial(jnp.dot, precision=lax.Precision.HIGHEST)
    a = jnp.sin(freq * (mm(z, w1) + b1))
    a = jnp.sin(freq * (mm(a, w2) + b2))
    a = jnp.sin(freq * (mm(a, w3) + b3))
    h = mm(a, w4).reshape(seq, 2, 2, c)
    min_decay = math.log(HYENA_TARGET) / HYENA_SLOW_DECAY
    max_decay = math.log(HYENA_TARGET) / HYENA_FAST_DECAY
    deltas = jnp.abs(jnp.linspace(min_decay, max_decay, c, dtype=F32))
    h = h * jnp.exp(-t * deltas[None, :])[:, None, None, :]
    fwd = h[:, :, 0]
    bwd = h[1:, :, 1][::-1]
    hi = jnp.concatenate([jnp.zeros((1, 2, c), F32), bwd], axis=0)
    norm = jnp.sum(jnp.abs(fwd), axis=0, keepdims=True) + jnp.sum(jnp.abs(hi), axis=0, keepdims=True)
    lo_t = jnp.transpose(fwd / norm, (1, 2, 0))
    hi_t = jnp.transpose(hi / norm, (1, 2, 0))
    return lo_t, hi_t


def _filter_spectra(lo_t, hi_t, fwd):
    _, c, seq = lo_t.shape
    rows = jnp.concatenate([lo_t.reshape(2 * c, seq), hi_t.reshape(2 * c, seq)], axis=0).astype(BF16)
    spec = _dft_raw(rows, fwd).reshape(2, 2, 2, c, seq)
    sign = jnp.where(jnp.arange(seq) % 2 == 0, 1.0, -1.0).astype(F32)
    full = spec[:, 0] + sign * spec[:, 1]
    first = jnp.arange(seq) == 0
    scale = jnp.where(first, 1.0, 2.0).astype(F32) / (2 * seq)
    kr = full[0] * scale
    ka = jnp.where(first, 0.0, full[1]) * scale
    kb = jnp.where(first, full[1], full[0]) * scale
    return jnp.stack([kr, ka, kb], axis=1)


def _t5_bucket(rel):
    half_b = REL_BUCKETS // 2
    max_exact = half_b // 2
    ret = jnp.where(rel > 0, half_b, 0)
    n = jnp.abs(rel)
    nf = jnp.maximum(n, 1).astype(F32)
    large = max_exact + (jnp.log(nf / max_exact) / math.log(REL_MAX_DIST / max_exact)
                         * (half_b - max_exact)).astype(jnp.int32)
    large = jnp.minimum(large, half_b - 1)
    return ret + jnp.where(n < max_exact, n, large)


def _dilated_bias(rel_bias):
    rel = jnp.arange(ATT_KW)[None, :] - ATT_HALF - jnp.arange(ATT_QB)[:, None]
    out = []
    for _, dil in DILATED_PATTERNS:
        bias = rel_bias[_t5_bucket(rel * dil)].astype(F32)
        bias = jnp.where((jnp.abs(rel) <= ATT_HALF)[:, :, None], bias, NEG_INF)
        out.append(jnp.transpose(bias, (2, 0, 1)))
    return jnp.stack(out)


def _head_select(first, second):
    lane = lax.broadcasted_iota(jnp.int32, first.shape[:-1] + (LANES,), len(first.shape) - 1)
    return jnp.where(lane < HEAD_DIM, first, second)


def _dilated_kernel(q_ref, k_ref, v_ref, g_ref, bias_ref, o_ref,
                    qf, kf, vf, qd, kd, vd, acc, mx, den, acc_t, mx_t, den_t):
    seq = q_ref.shape[1]
    scale = HEAD_DIM ** -0.5
    qf[...] = q_ref[0].astype(F32)
    kf[...] = k_ref[0].astype(F32)
    vf[...] = v_ref[0].astype(F32)
    lane_q = lax.broadcasted_iota(jnp.int32, (ATT_QB, LANES), 1)
    col = lax.broadcasted_iota(jnp.int32, (ATT_QB, ATT_KW), 1)

    for pi, (_, dil) in enumerate(DILATED_PATTERNS):
        n = seq // dil
        nblk = n // ATT_QB
        stride_k = n + 2 * ATT_HALF
        zero_halo = jnp.zeros((ATT_HALF, LANES), BF16)
        for r in range(dil):
            rows = pl.ds(r, n, stride=dil) if dil > 1 else pl.ds(0, n)
            qd[r * n:(r + 1) * n, :] = qf[rows, :].astype(BF16)
            base = r * stride_k
            kd[base:base + ATT_HALF, :] = zero_halo
            vd[base:base + ATT_HALF, :] = zero_halo
            kd[base + ATT_HALF:base + ATT_HALF + n, :] = kf[rows, :].astype(BF16)
            vd[base + ATT_HALF:base + ATT_HALF + n, :] = vf[rows, :].astype(BF16)
            kd[base + ATT_HALF + n:base + stride_k, :] = zero_halo
            vd[base + ATT_HALF + n:base + stride_k, :] = zero_halo
        first = pi == 0
        acc_w, mx_w, den_w = (acc, mx, den) if first else (acc_t, mx_t, den_t)

        def block(blk, carry, pi=pi, nblk=nblk, acc_w=acc_w, mx_w=mx_w, den_w=den_w):
            r = blk // nblk
            qb = blk % nblk
            q0 = pl.multiple_of(blk * ATT_QB, ATT_QB)
            k0 = pl.multiple_of(q0 + r * (2 * ATT_HALF), ATT_QB)
            qblk = qd[pl.ds(q0, ATT_QB), :]
            kblk = kd[pl.ds(k0, ATT_KW), :]
            vblk = vd[pl.ds(k0, ATT_KW), :]
            lo = jnp.where(qb == 0, ATT_HALF, 0)
            hi = jnp.where(qb == nblk - 1, ATT_KW - ATT_HALF, ATT_KW)
            in_seq = (col >= lo) & (col < hi)
            res = []
            for h in range(2):
                head = (lane_q < HEAD_DIM) if h == 0 else (lane_q >= HEAD_DIM)
                qh = jnp.where(head, qblk, jnp.zeros_like(qblk))
                s = lax.dot_general(qh, kblk, (((1,), (1,)), ((), ())), preferred_element_type=F32)
                s = s * scale + bias_ref[pi, h]
                s = jnp.where(in_seq, s, NEG_INF)
                m = jnp.max(s, axis=1, keepdims=True)
                p = jnp.exp(s - m)
                l = jnp.sum(p, axis=1, keepdims=True)
                pv = jnp.dot(p.astype(BF16), vblk, preferred_element_type=F32)
                res.append((pv, m, l))
            acc_w[pl.ds(q0, ATT_QB), :] = _head_select(res[0][0], res[1][0])
            mx_w[pl.ds(q0, ATT_QB), :] = _head_select(res[0][1], res[1][1])
            den_w[pl.ds(q0, ATT_QB), :] = _head_select(res[0][2], res[1][2])
            return carry

        lax.fori_loop(0, dil * nblk, block, 0)

        if not first:
            for r in range(dil):
                rows = pl.ds(r, n, stride=dil)
                m_old = mx[rows, :]
                m_new = mx_t[r * n:(r + 1) * n, :]
                m_all = jnp.maximum(m_old, m_new)
                a_old = jnp.exp(m_old - m_all)
                a_new = jnp.exp(m_new - m_all)
                acc[rows, :] = acc[rows, :] * a_old + acc_t[r * n:(r + 1) * n, :] * a_new
                den[rows, :] = den[rows, :] * a_old + den_t[r * n:(r + 1) * n, :] * a_new
                mx[rows, :] = m_all

    g = g_ref[0].astype(F32)
    o_ref[0] = (acc[...] / den[...] * (g * jax.nn.sigmoid(g))).astype(o_ref.dtype)


def _dilated_attention(proj, bias, n_heads):
    b, seq, _ = proj.shape
    npair = n_heads // 2
    max_dil = max(d for _, d in DILATED_PATTERNS)
    kd_rows = seq + 2 * ATT_HALF * max_dil
    blk = lambda off: pl.BlockSpec((1, seq, LANES), lambda i, j: (i, 0, off * npair + j))
    state = pltpu.VMEM((seq, LANES), F32)
    return pl.pallas_call(
        _dilated_kernel,
        grid=(b, npair),
        in_specs=[blk(0), blk(1), blk(2), blk(3),
                  pl.BlockSpec((len(DILATED_PATTERNS), 2, ATT_QB, ATT_KW), lambda i, j: (0, j, 0, 0))],
        out_specs=pl.BlockSpec((1, seq, LANES), lambda i, j: (i, 0, j)),
        out_shape=jax.ShapeDtypeStruct((b, seq, n_heads * HEAD_DIM), BF16),
        scratch_shapes=[state, state, state,
                        pltpu.VMEM((seq, LANES), BF16), pltpu.VMEM((kd_rows, LANES), BF16),
                        pltpu.VMEM((kd_rows, LANES), BF16),
                        state, state, state, state, state, state],
        compiler_params=_cparams(("parallel", "parallel")),
        name="dilated_attention",
    )(proj, proj, proj, proj, bias)


def _na_bias(rpb, rows):
    kr_n = min(NA_ROWS, rows)
    cols = np.arange(GRID_W)
    cs = np.clip(cols - NA_COLS // 2, 0, GRID_W - NA_COLS)
    kc = np.arange(GRID_W)
    inside = (kc[None, :] >= cs[:, None]) & (kc[None, :] < cs[:, None] + NA_COLS)
    col_idx = np.clip(kc[None, :] - cols[:, None] + NA_COLS - 1, 0, 2 * NA_COLS - 2)
    row_idx = np.arange(NA_ROWS)[:, None] + np.arange(kr_n)[None, :]
    g = rpb[:, row_idx[:, None, :, None], col_idx[None, :, None, :]]
    g = jnp.where(inside[None, None, :, None, :], g.astype(F32), NEG_INF)
    return g.reshape(rpb.shape[0], NA_ROWS, GRID_W, kr_n * GRID_W)


def _na_kernel(q_ref, k_ref, v_ref, g_ref, bias_ref, o_ref):
    seq = q_ref.shape[1]
    rows = seq // GRID_W
    kr_n = min(NA_ROWS, rows)
    scale = HEAD_DIM ** -0.5
    lane_q = lax.broadcasted_iota(jnp.int32, (GRID_W, LANES), 1)

    def row(r, carry):
        rs = jnp.clip(r - kr_n // 2, 0, rows - kr_n)
        shift = rs - r + NA_ROWS - 1
        q0 = pl.multiple_of(r * GRID_W, GRID_W)
        k0 = pl.multiple_of(rs * GRID_W, GRID_W)
        qblk = q_ref[0, pl.ds(q0, GRID_W), :]
        kblk = k_ref[0, pl.ds(k0, kr_n * GRID_W), :]
        vblk = v_ref[0, pl.ds(k0, kr_n * GRID_W), :]
        outs = []
        for h in range(2):
            head = (lane_q < HEAD_DIM) if h == 0 else (lane_q >= HEAD_DIM)
            qh = jnp.where(head, qblk, jnp.zeros_like(qblk))
            s = lax.dot_general(qh, kblk, (((1,), (1,)), ((), ())), preferred_element_type=F32)
            s = s * scale + bias_ref[h, shift]
            m = jnp.max(s, axis=1, keepdims=True)
            p = jnp.exp(s - m)
            l = jnp.sum(p, axis=1, keepdims=True)
            pv = jnp.dot(p.astype(BF16), vblk, preferred_element_type=F32)
            outs.append(pv / l)
        g = g_ref[0, pl.ds(q0, GRID_W), :].astype(F32)
        o_ref[0, pl.ds(q0, GRID_W), :] = (_head_select(outs[0], outs[1]) * (g * jax.nn.sigmoid(g))).astype(o_ref.dtype)
        return carry

    lax.fori_loop(0, rows, row, 0)


def _neighbourhood_attention(proj, bias, n_heads):
    b, seq, _ = proj.shape
    npair = n_heads // 2
    blk = lambda off: pl.BlockSpec((1, seq, LANES), lambda j, i: (i, 0, off * npair + j))
    return pl.pallas_call(
        _na_kernel,
        grid=(npair, b),
        in_specs=[blk(0), blk(1), blk(2), blk(3),
                  pl.BlockSpec((2,) + bias.shape[1:], lambda j, i: (j, 0, 0, 0))],
        out_specs=pl.BlockSpec((1, seq, LANES), lambda j, i: (i, 0, j)),
        out_shape=jax.ShapeDtypeStruct((b, seq, n_heads * HEAD_DIM), BF16),
        compiler_params=_cparams(("parallel", "parallel")),
        name="neighbourhood_attention",
    )(proj, proj, proj, proj, bias)


def _out_kernel(n_y, *refs):
    y_refs = refs[:n_y]
    w_refs = refs[n_y:2 * n_y]
    h_ref, p_ref, wp_ref, wg_ref, gpost_ref, gple_ref, gnext_ref, h_out, u_out = refs[2 * n_y:]
    t = jnp.dot(y_refs[0][...], w_refs[0][...], preferred_element_type=F32)
    for y_ref, w_ref in zip(y_refs[1:], w_refs[1:]):
        t = t + jnp.dot(y_ref[...], w_ref[...], preferred_element_type=F32)
    h = h_ref[...] + _rms(t, gpost_ref[...])
    e = _rms(jnp.dot(p_ref[...].astype(BF16), wp_ref[...], preferred_element_type=F32), gple_ref[...])
    gate = jax.nn.sigmoid(jnp.dot(h.astype(BF16), wg_ref[...], preferred_element_type=F32))
    h = h + gate * e
    h_out[...] = h
    u_out[...] = _rms(h, gnext_ref[...]).astype(u_out.dtype)


def _out_block(ys, ws, h, p, w_ple, w_gate, g_post, g_ple, g_next, tm=512):
    m, d = h.shape
    n_y = len(ys)
    row = lambda a: pl.BlockSpec((tm, a.shape[1]), lambda i: (i, 0))
    full = lambda a: pl.BlockSpec(a.shape, lambda i: (0, 0))
    vec = lambda g: g.reshape(1, d)
    return pl.pallas_call(
        functools.partial(_out_kernel, n_y),
        grid=(m // tm,),
        in_specs=[row(y) for y in ys] + [full(w) for w in ws]
                 + [row(h), row(p), full(w_ple), full(w_gate), full(vec(g_post)), full(vec(g_ple)), full(vec(g_next))],
        out_specs=[pl.BlockSpec((tm, d), lambda i: (i, 0)), pl.BlockSpec((tm, d), lambda i: (i, 0))],
        out_shape=[jax.ShapeDtypeStruct((m, d), F32), jax.ShapeDtypeStruct((m, d), BF16)],
        compiler_params=_cparams(("parallel",)),
        name="out_proj_residual_ple",
    )(*ys, *ws, h, p, w_ple, w_gate, vec(g_post), vec(g_ple), vec(g_next))


def kernel(x, p, w_in, w_out, norm_pre, norm_post, hyena_conv_w, hyena_conv_b, hyena_w1, hyena_b1, hyena_w2, hyena_b2, hyena_w3, hyena_b3, hyena_w4, hyena_freq, hyena_skip, rel_bias, na_rpb, ple_proj, ple_norm, ple_gate):
    b, seq, d = x.shape
    depth = w_in.shape[0]
    d_inner = w_out.shape[1]
    hw = d_inner // 2
    m = b * seq
    fwd, inv = _dft_matrices(seq)
    dil_bias = _dilated_bias(rel_bias)
    h = x.reshape(m, d)
    u = _rmsnorm_bf16(h, norm_pre[0])
    for i in range(depth):
        j = i // 2
        w = w_in[i].astype(BF16)
        if i % 2 == 0:
            w_a = w[:, :4 * hw].T
            ident = jnp.tile(jnp.array([0.0, 1.0, 0.0], F32)[:, None], (1, hw))
            conv_w = jnp.concatenate([hyena_conv_w[j], ident], axis=1).T
            conv_b = jnp.concatenate([hyena_conv_b[j], jnp.zeros((hw,), F32)])[:, None]
            hy = _proj_t(w_a, u.reshape(b, seq, d), conv_w, conv_b)
            lo_t, hi_t = _hyena_filters_t(seq, hw, hyena_w1[j], hyena_b1[j], hyena_w2[j], hyena_b2[j],
                                          hyena_w3[j], hyena_b3[j], hyena_w4[j], hyena_freq[j])
            kspec = _filter_spectra(lo_t, hi_t, fwd)
            skip = hyena_skip[j][:, :, None]
            y1 = _dft_filter(hy, 0, hw, fwd, kspec[0])
            z = _idft_gate1(y1, inv, hy, skip[0], hw)
            y2 = _dft_filter(z, 0, hw, fwd, kspec[1])
            ya = _idft_gate2(y2, inv, z, hy, skip[1], hw)
            proj_b = _matmul_bf16(u, w[:, 4 * hw:]).reshape(b, seq, 4 * hw)
            yb = _dilated_attention(proj_b, dil_bias, hw // HEAD_DIM)
            ys = [ya.reshape(m, hw), yb.reshape(m, hw)]
            ws = [w_out[i, :hw].astype(BF16), w_out[i, hw:].astype(BF16)]
        else:
            proj = _matmul_bf16(u, w).reshape(b, seq, 4 * d_inner)
            yc = _neighbourhood_attention(proj, _na_bias(na_rpb[j], seq // GRID_W), d_inner // HEAD_DIM)
            ys = [yc.reshape(m, d_inner)]
            ws = [w_out[i].astype(BF16)]
        g_next = norm_pre[i + 1] if i + 1 < depth else norm_pre[i]
        h, u = _out_block(ys, ws, h, p[i].reshape(m, -1), ple_proj[i].astype(BF16), ple_gate[i].astype(BF16),
                          norm_post[i], ple_norm[i], g_next)
    return h.reshape(b, seq, d)
```

```python
import functools
import math

import numpy as np
import jax
import jax.numpy as jnp
from jax import lax
from jax.experimental import pallas as pl
from jax.experimental.pallas import tpu as pltpu

F32 = jnp.float32
BF16 = jnp.bfloat16

HEAD_DIM = 64
LANES = 128
HYENA_EMB_DIM = 33
HYENA_FAST_DECAY = 0.3
HYENA_SLOW_DECAY = 1.5
HYENA_TARGET = 1e-2
DILATED_PATTERNS = ((128, 1), (512, 4), (2048, 16))
GRID_W = 64
NA_ROWS = 8
NA_COLS = 16
REL_BUCKETS = 32
REL_MAX_DIST = 1024
RMS_EPS = 1e-6
NEG_INF = -1e30
ATT_QB = 128
ATT_HALF = 64
ATT_KW = ATT_QB + 2 * ATT_HALF
ATT_UNROLL = 2
ATT_STEP = 4
assert all(b[1] == a[1] * ATT_STEP for a, b in zip(DILATED_PATTERNS, DILATED_PATTERNS[1:])) and DILATED_PATTERNS[0][1] == 1
NA_UNROLL = 4
VMEM_LIMIT = 56 * 1024 * 1024


def _cparams(sem):
    return pltpu.CompilerParams(dimension_semantics=sem, vmem_limit_bytes=VMEM_LIMIT)


def _rms(x, g):
    return x * lax.rsqrt(jnp.mean(x * x, axis=-1, keepdims=True) + RMS_EPS) * g


def _rmsnorm_kernel(x_ref, g_ref, o_ref):
    o_ref[...] = _rms(x_ref[...], g_ref[...]).astype(o_ref.dtype)


def _rmsnorm_bf16(x2d, g, tm=1024):
    m, d = x2d.shape
    return pl.pallas_call(
        _rmsnorm_kernel,
        grid=(m // tm,),
        in_specs=[pl.BlockSpec((tm, d), lambda i: (i, 0)), pl.BlockSpec((1, d), lambda i: (0, 0))],
        out_specs=pl.BlockSpec((tm, d), lambda i: (i, 0)),
        out_shape=jax.ShapeDtypeStruct((m, d), BF16),
        compiler_params=_cparams(("parallel",)),
        name="rmsnorm",
    )(x2d, g.reshape(1, d))


def _matmul_kernel(a_ref, b_ref, o_ref):
    o_ref[...] = jnp.dot(a_ref[...], b_ref[...], preferred_element_type=F32).astype(o_ref.dtype)


def _matmul_bf16(a, b, tm=1024, tn=1024):
    m, k = a.shape
    _, n = b.shape
    return pl.pallas_call(
        _matmul_kernel,
        grid=(m // tm, n // tn),
        in_specs=[pl.BlockSpec((tm, k), lambda i, j: (i, 0)), pl.BlockSpec((k, tn), lambda i, j: (0, j))],
        out_specs=pl.BlockSpec((tm, tn), lambda i, j: (i, j)),
        out_shape=jax.ShapeDtypeStruct((m, n), BF16),
        compiler_params=_cparams(("parallel", "parallel")),
        name="in_proj",
    )(a, b)


def _proj_t_kernel(w_ref, u_ref, cw_ref, cb_ref, o_ref):
    p = lax.dot_general(w_ref[...], u_ref[0], (((1,), (1,)), ((), ())), preferred_element_type=F32)
    seq = p.shape[1]
    t = lax.broadcasted_iota(jnp.int32, p.shape, 1)
    prev = jnp.where(t == 0, 0.0, pltpu.roll(p, 1, axis=1))
    nxt = jnp.where(t == seq - 1, 0.0, pltpu.roll(p, seq - 1, axis=1))
    cw = cw_ref[...]
    o_ref[0] = (prev * cw[:, 0:1] + p * cw[:, 1:2] + nxt * cw[:, 2:3] + cb_ref[...]).astype(o_ref.dtype)


def _proj_t(w_t, u, conv_w, conv_b, tc=256):
    c, d = w_t.shape
    b, seq, _ = u.shape
    return pl.pallas_call(
        _proj_t_kernel,
        grid=(b, c // tc),
        in_specs=[
            pl.BlockSpec((tc, d), lambda i, j: (j, 0)),
            pl.BlockSpec((1, seq, d), lambda i, j: (i, 0, 0)),
            pl.BlockSpec((tc, 3), lambda i, j: (j, 0)),
            pl.BlockSpec((tc, 1), lambda i, j: (j, 0)),
        ],
        out_specs=pl.BlockSpec((1, tc, seq), lambda i, j: (i, j, 0)),
        out_shape=jax.ShapeDtypeStruct((b, c, seq), BF16),
        compiler_params=_cparams(("parallel", "parallel")),
        name="in_proj_t_conv3",
    )(w_t, u, conv_w, conv_b)


def _dft_matrices(seq):
    n_fft = 2 * seq
    hi, lo = seq // LANES, LANES
    k = jnp.arange(seq, dtype=jnp.int32)[None, :]

    def table(rows):
        ang = ((rows[:, None] * k) % n_fft).astype(F32) * (2.0 * math.pi / n_fft)
        return jnp.cos(ang), -jnp.sin(ang)

    ar, ai = table(jnp.arange(hi, dtype=jnp.int32) * lo)
    br, bi = table(jnp.arange(lo, dtype=jnp.int32))
    wr = (ar[:, None, :] * br[None, :, :] - ai[:, None, :] * bi[None, :, :]).reshape(seq, seq)
    wi = (ar[:, None, :] * bi[None, :, :] + ai[:, None, :] * br[None, :, :]).reshape(seq, seq)
    sign = jnp.where(jnp.arange(seq) % 2 == 0, 1.0, -1.0).astype(F32)
    wi = wi.at[:, 0].set(sign)
    fwd = jnp.stack([wr, wi]).astype(BF16)
    inv = jnp.stack([wr, wi.T]).astype(BF16)
    return fwd, inv


def _dft_raw_kernel(v_ref, w_ref, o_ref):
    v = v_ref[...]
    o_ref[0] = jnp.dot(v, w_ref[0], preferred_element_type=F32)
    o_ref[1] = jnp.dot(v, w_ref[1], preferred_element_type=F32)


def _dft_raw(rows, fwd, tm=512, tk=512):
    r, seq = rows.shape
    return pl.pallas_call(
        _dft_raw_kernel,
        grid=(r // tm, seq // tk),
        in_specs=[pl.BlockSpec((tm, seq), lambda i, j: (i, 0)), pl.BlockSpec((2, seq, tk), lambda i, j: (0, 0, j))],
        out_specs=pl.BlockSpec((2, tm, tk), lambda i, j: (0, i, j)),
        out_shape=jax.ShapeDtypeStruct((2, r, seq), F32),
        compiler_params=_cparams(("parallel", "parallel")),
        name="filter_dft",
    )(rows, fwd)


def _dft_filter_kernel(v_ref, w_ref, k_ref, o_ref):
    v = v_ref[0]
    sr = jnp.dot(v, w_ref[0], preferred_element_type=F32)
    si = jnp.dot(v, w_ref[1], preferred_element_type=F32)
    o_ref[0, 0] = (sr * k_ref[0] - si * k_ref[1]).astype(o_ref.dtype)
    o_ref[1, 0] = (sr * k_ref[1] + si * k_ref[2]).astype(o_ref.dtype)


def _dft_filter(src, row_block0, c, fwd, kspec, tm=512, tk=256):
    b, _, seq = src.shape
    nc = c // tm
    return pl.pallas_call(
        _dft_filter_kernel,
        grid=(b, nc, seq // tk),
        in_specs=[
            pl.BlockSpec((1, tm, seq), lambda i, j, q: (i, row_block0 + j, 0)),
            pl.BlockSpec((2, seq, tk), lambda i, j, q: (0, 0, q)),
            pl.BlockSpec((3, tm, tk), lambda i, j, q: (0, j, q)),
        ],
        out_specs=pl.BlockSpec((2, 1, tm, tk), lambda i, j, q: (0, i, j, q)),
        out_shape=jax.ShapeDtypeStruct((2, b, c, seq), BF16),
        compiler_params=_cparams(("parallel", "parallel", "parallel")),
        name="hyena_dft_filter",
    )(src, fwd, kspec)


def _idft(y_ref, w_ref):
    return (jnp.dot(y_ref[0, 0], w_ref[0], preferred_element_type=F32)
            + jnp.dot(y_ref[1, 0], w_ref[1], preferred_element_type=F32))


def _idft_gate1_kernel(y_ref, w_ref, v_ref, x1_ref, skip_ref, o_ref):
    conv = _idft(y_ref, w_ref) + v_ref[0].astype(F32) * skip_ref[...]
    o_ref[0] = (x1_ref[0].astype(F32) * conv).astype(o_ref.dtype)


def _idft_gate2_kernel(y_ref, w_ref, z_ref, x2_ref, g_ref, skip_ref, o_ref):
    conv = _idft(y_ref, w_ref) + z_ref[0].astype(F32) * skip_ref[...]
    g = g_ref[0].astype(F32)
    ya = x2_ref[0].astype(F32) * conv * (g * jax.nn.sigmoid(g))
    o_ref[0] = ya.T.astype(o_ref.dtype)


def _idft_gate1(y, inv, hy, skip, c, tm=512, tn=256):
    _, b, _, seq = y.shape
    nc = c // tm
    return pl.pallas_call(
        _idft_gate1_kernel,
        grid=(b, nc, seq // tn),
        in_specs=[
            pl.BlockSpec((2, 1, tm, seq), lambda i, j, q: (0, i, j, 0)),
            pl.BlockSpec((2, seq, tn), lambda i, j, q: (0, 0, q)),
            pl.BlockSpec((1, tm, tn), lambda i, j, q: (i, j, q)),
            pl.BlockSpec((1, tm, tn), lambda i, j, q: (i, nc + j, q)),
            pl.BlockSpec((tm, 1), lambda i, j, q: (j, 0)),
        ],
        out_specs=pl.BlockSpec((1, tm, tn), lambda i, j, q: (i, j, q)),
        out_shape=jax.ShapeDtypeStruct((b, c, seq), BF16),
        compiler_params=_cparams(("parallel", "parallel", "parallel")),
        name="hyena_idft_gate1",
    )(y, inv, hy, hy, skip)


def _idft_gate2(y, inv, z, hy, skip, c, tm=512, tn=256):
    _, b, _, seq = y.shape
    nc = c // tm
    return pl.pallas_call(
        _idft_gate2_kernel,
        grid=(b, nc, seq // tn),
        in_specs=[
            pl.BlockSpec((2, 1, tm, seq), lambda i, j, q: (0, i, j, 0)),
            pl.BlockSpec((2, seq, tn), lambda i, j, q: (0, 0, q)),
            pl.BlockSpec((1, tm, tn), lambda i, j, q: (i, j, q)),
            pl.BlockSpec((1, tm, tn), lambda i, j, q: (i, 2 * nc + j, q)),
            pl.BlockSpec((1, tm, tn), lambda i, j, q: (i, 3 * nc + j, q)),
            pl.BlockSpec((tm, 1), lambda i, j, q: (j, 0)),
        ],
        out_specs=pl.BlockSpec((1, tn, tm), lambda i, j, q: (i, q, j)),
        out_shape=jax.ShapeDtypeStruct((b, seq, c), BF16),
        compiler_params=_cparams(("parallel", "parallel", "parallel")),
        name="hyena_idft_gate2",
    )(y, inv, z, hy, hy, skip)


def _hyena_filter_kernel(z_ref, w1_ref, b1_ref, w2_ref, b2_ref, w3_ref, b3_ref, fq_ref, w4_ref, dl_ref, t_ref,
                         o_ref, a_scr):
    hp = lax.Precision.HIGHEST

    @pl.when((pl.program_id(0) == 0) & (pl.program_id(1) == 0))
    def _():
        for d in range(2):
            a = z_ref[d]
            for w_ref, b_ref in ((w1_ref, b1_ref), (w2_ref, b2_ref), (w3_ref, b3_ref)):
                a = jnp.sin(fq_ref[...] * (jnp.dot(w_ref[...], a, precision=hp, preferred_element_type=F32) + b_ref[...]))
            a_scr[d] = a

    seq = o_ref.shape[-1]
    lo = jnp.dot(w4_ref[0, 0], a_scr[0], precision=hp, preferred_element_type=F32) * jnp.exp(-t_ref[0] * dl_ref[...])
    hi = jnp.dot(w4_ref[0, 1], a_scr[1], precision=hp, preferred_element_type=F32) * jnp.exp(-t_ref[1] * dl_ref[...])
    hi = jnp.where(lax.broadcasted_iota(jnp.int32, (1, seq), 1) == 0, 0.0, hi)
    norm = jnp.sum(jnp.abs(lo), axis=1, keepdims=True) + jnp.sum(jnp.abs(hi), axis=1, keepdims=True)
    o_ref[0, 0] = (lo / norm).astype(o_ref.dtype)
    o_ref[1, 0] = (hi / norm).astype(o_ref.dtype)


def _hyena_filters_t(seq, c, w1, b1, w2, b2, w3, b3, w4, freq, tc=256):
    tc = min(tc, c)
    fo = w2.shape[0]
    emb = -(-HYENA_EMB_DIM // 8) * 8
    t = jnp.linspace(0.0, 1.0, seq, dtype=F32)
    bands = (HYENA_EMB_DIM - 1) // 2
    fr = jnp.linspace(1e-4, bands - 1, bands, dtype=F32)[:, None]
    mirror = (seq - jnp.arange(seq)) % seq
    tables, times = [], []
    for pos, tp in ((jnp.arange(seq), t), (mirror, jnp.roll(t[::-1], 1))):
        wpos = (2.0 * math.pi / seq) * pos.astype(F32)[None, :]
        tp = tp[None, :]
        z = jnp.concatenate([tp, jnp.cos(fr * wpos), -jnp.sin(fr * wpos),
                             jnp.zeros((emb - HYENA_EMB_DIM, seq), F32)], axis=0)
        tables.append(z)
        times.append(tp)
    z = jnp.stack(tables)
    tt = jnp.stack(times)
    w1_t = jnp.pad(w1.astype(F32).T, ((0, 0), (0, emb - HYENA_EMB_DIM)))
    w4_t = jnp.transpose(w4.astype(F32).reshape(fo, 2, 2, c), (1, 2, 3, 0))
    min_decay = math.log(HYENA_TARGET) / HYENA_SLOW_DECAY
    max_decay = math.log(HYENA_TARGET) / HYENA_FAST_DECAY
    deltas = jnp.abs(jnp.linspace(min_decay, max_decay, c, dtype=F32))[:, None]
    col = lambda v: v.astype(F32)[:, None]
    full = lambda a: pl.BlockSpec(a.shape, lambda f, i: (0,) * a.ndim)
    args = (z, w1_t, col(b1), w2.astype(F32).T, col(b2), w3.astype(F32).T, col(b3), col(freq))
    return pl.pallas_call(
        _hyena_filter_kernel,
        grid=(2, c // tc),
        in_specs=[full(a) for a in args]
                 + [pl.BlockSpec((1, 2, tc, fo), lambda f, i: (f, 0, i, 0)),
                    pl.BlockSpec((tc, 1), lambda f, i: (i, 0)), full(tt)],
        out_specs=pl.BlockSpec((2, 1, tc, seq), lambda f, i: (0, f, i, 0)),
        out_shape=jax.ShapeDtypeStruct((2, 2, c, seq), BF16),
        scratch_shapes=[pltpu.VMEM((2, fo, seq), F32)],
        compiler_params=_cparams(("arbitrary", "arbitrary")),
        name="hyena_filter_mlp",
    )(*args, w4_t, deltas, tt)


def _filter_spectra(halves, fwd):
    _, _, c, seq = halves.shape
    spec = _dft_raw(halves.reshape(4 * c, seq), fwd).reshape(2, 2, 2, c, seq)
    sign = jnp.where(jnp.arange(seq) % 2 == 0, 1.0, -1.0).astype(F32)
    full = spec[:, 0] + sign * spec[:, 1]
    first = jnp.arange(seq) == 0
    scale = jnp.where(first, 1.0, 2.0).astype(F32) / (2 * seq)
    kr = full[0] * scale
    ka = jnp.where(first, 0.0, full[1]) * scale
    kb = jnp.where(first, full[1], full[0]) * scale
    return jnp.stack([kr, ka, kb], axis=1)


def _t5_bucket(rel):
    half_b = REL_BUCKETS // 2
    max_exact = half_b // 2
    ret = jnp.where(rel > 0, half_b, 0)
    n = jnp.abs(rel)
    nf = jnp.maximum(n, 1).astype(F32)
    large = max_exact + (jnp.log(nf / max_exact) / math.log(REL_MAX_DIST / max_exact)
                         * (half_b - max_exact)).astype(jnp.int32)
    large = jnp.minimum(large, half_b - 1)
    return ret + jnp.where(n < max_exact, n, large)


def _dilated_bias(rel_bias):
    rel = jnp.arange(ATT_KW)[None, :] - ATT_HALF - jnp.arange(ATT_QB)[:, None]
    col = jnp.arange(ATT_KW)
    edge = jnp.stack([col >= ATT_HALF, col >= 0, col < ATT_KW - ATT_HALF])
    out = []
    for _, dil in DILATED_PATTERNS:
        bias = rel_bias[_t5_bucket(rel * dil)].astype(F32)
        bias = jnp.where((jnp.abs(rel) <= ATT_HALF)[:, :, None], bias, NEG_INF)
        bias = jnp.transpose(bias, (2, 0, 1))[:, None]
        out.append(jnp.where(edge[None, :, None, :], bias, NEG_INF))
    return jnp.stack(out)


def _head_select(first, second):
    lane = lax.broadcasted_iota(jnp.int32, first.shape[:-1] + (LANES,), len(first.shape) - 1)
    return jnp.where(lane < HEAD_DIM, first, second)


def _dilated_kernel(q_ref, k_ref, v_ref, g_ref, bias_ref, o_ref, *scratch):
    stage_a, stage_b = scratch[0:3], scratch[3:6]
    qd, kd, vd = scratch[6:9]
    last = len(DILATED_PATTERNS) - 1
    states = [scratch[9 + 3 * p:12 + 3 * p] for p in range(last)]
    seq = q_ref.shape[1]
    stage_a[0][...] = q_ref[0].astype(F32) * (HEAD_DIM ** -0.5)
    stage_a[1][...] = k_ref[0].astype(F32)
    stage_a[2][...] = v_ref[0].astype(F32)
    lane_q = lax.broadcasted_iota(jnp.int32, (ATT_QB, LANES), 1)
    zero_halo = jnp.zeros((ATT_HALF, LANES), BF16)

    src, dst = stage_a, stage_b
    for pi, (_, dil) in enumerate(DILATED_PATTERNS):
        n = seq // dil
        nblk = n // ATT_QB
        if pi > 0:
            prev_dil = dil // ATT_STEP
            for r1 in range(prev_dil):
                for r2 in range(ATT_STEP):
                    r = r1 + prev_dil * r2
                    for a_src, a_dst in zip(src, dst):
                        a_dst[r * n:(r + 1) * n, :] = a_src[pl.ds(r1 * n * ATT_STEP + r2, n, stride=ATT_STEP), :]
            src, dst = dst, src
        stride_k = n + 2 * ATT_HALF
        qd[...] = src[0][...].astype(BF16)
        for r in range(dil):
            base = r * stride_k
            for a_src, a_dst in ((src[1], kd), (src[2], vd)):
                a_dst[base:base + ATT_HALF, :] = zero_halo
                a_dst[base + ATT_HALF:base + ATT_HALF + n, :] = a_src[r * n:(r + 1) * n, :].astype(BF16)
                a_dst[base + ATT_HALF + n:base + stride_k, :] = zero_halo
        if pi == last:
            states.append(dst)
        acc_w, mx_w, den_w = states[pi]

        def block_group(it, carry, pi=pi, nblk=nblk, acc_w=acc_w, mx_w=mx_w, den_w=den_w):
            chains = []
            for u in range(ATT_UNROLL):
                blk = it * ATT_UNROLL + u
                r = blk // nblk
                qb = blk % nblk
                q0 = pl.multiple_of(blk * ATT_QB, ATT_QB)
                k0 = pl.multiple_of(q0 + r * (2 * ATT_HALF), ATT_QB)
                qblk = qd[pl.ds(q0, ATT_QB), :]
                kblk = kd[pl.ds(k0, ATT_KW), :]
                edge = jnp.where(qb == 0, 0, jnp.where(qb == nblk - 1, 2, 1))
                for h in range(2):
                    head = (lane_q < HEAD_DIM) if h == 0 else (lane_q >= HEAD_DIM)
                    qh = jnp.where(head, qblk, jnp.zeros_like(qblk))
                    s = lax.dot_general(qh, kblk, (((1,), (1,)), ((), ())), preferred_element_type=F32)
                    chains.append((q0, k0, h, edge, s))
            probs = []
            for q0, k0, h, edge, s in chains:
                s = s + bias_ref[pi, h, edge]
                m = jnp.max(s, axis=1, keepdims=True)
                p = jnp.exp(s - m)
                probs.append((p.astype(BF16), m, jnp.sum(p, axis=1, keepdims=True)))
            pvs = [jnp.dot(p, vd[pl.ds(k0, ATT_KW), :], preferred_element_type=F32)
                   for (q0, k0, h, edge, s), (p, m, l) in zip(chains, probs)]
            for u in range(ATT_UNROLL):
                q0 = chains[2 * u][0]
                (_, m0, l0), (_, m1, l1) = probs[2 * u], probs[2 * u + 1]
                acc_w[pl.ds(q0, ATT_QB), :] = _head_select(pvs[2 * u], pvs[2 * u + 1])
                mx_w[pl.ds(q0, ATT_QB), :] = _head_select(m0, m1)
                den_w[pl.ds(q0, ATT_QB), :] = _head_select(l0, l1)
            return carry

        lax.fori_loop(0, dil * nblk // ATT_UNROLL, block_group, 0)

    for pi in range(len(DILATED_PATTERNS) - 1, 0, -1):
        dil = DILATED_PATTERNS[pi][1]
        n = seq // dil
        prev_dil = dil // ATT_STEP
        acc_c, mx_c, den_c = states[pi - 1]
        acc_f, mx_f, den_f = states[pi]
        for r1 in range(prev_dil):
            for r2 in range(ATT_STEP):
                r = r1 + prev_dil * r2
                rows = pl.ds(r1 * n * ATT_STEP + r2, n, stride=ATT_STEP)
                fine = slice(r * n, (r + 1) * n)
                m_old = mx_c[rows, :]
                m_new = mx_f[fine, :]
                m_all = jnp.maximum(m_old, m_new)
                a_old = jnp.exp(m_old - m_all)
                a_new = jnp.exp(m_new - m_all)
                acc_c[rows, :] = acc_c[rows, :] * a_old + acc_f[fine, :] * a_new
                den_c[rows, :] = den_c[rows, :] * a_old + den_f[fine, :] * a_new
                mx_c[rows, :] = m_all

    acc, _, den = states[0]
    g = g_ref[0].astype(F32)
    o_ref[0] = (acc[...] / den[...] * (g * jax.nn.sigmoid(g))).astype(o_ref.dtype)


def _dilated_attention(proj, bias, n_heads):
    b, seq, _ = proj.shape
    npair = n_heads // 2
    max_dil = max(d for _, d in DILATED_PATTERNS)
    kd_rows = seq + 2 * ATT_HALF * max_dil
    blk = lambda off: pl.BlockSpec((1, seq, LANES), lambda i, j: (i, 0, off * npair + j))
    state = pltpu.VMEM((seq, LANES), F32)
    return pl.pallas_call(
        _dilated_kernel,
        grid=(b, npair),
        in_specs=[blk(0), blk(1), blk(2), blk(3),
                  pl.BlockSpec((len(DILATED_PATTERNS), 2, 3, ATT_QB, ATT_KW), lambda i, j: (0, j, 0, 0, 0))],
        out_specs=pl.BlockSpec((1, seq, LANES), lambda i, j: (i, 0, j)),
        out_shape=jax.ShapeDtypeStruct((b, seq, n_heads * HEAD_DIM), BF16),
        scratch_shapes=[state] * 6
                       + [pltpu.VMEM((seq, LANES), BF16), pltpu.VMEM((kd_rows, LANES), BF16),
                          pltpu.VMEM((kd_rows, LANES), BF16)]
                       + [state] * (3 * (len(DILATED_PATTERNS) - 1)),
        compiler_params=_cparams(("parallel", "parallel")),
        name="dilated_attention",
    )(proj, proj, proj, proj, bias)


def _na_bias(rpb, rows):
    kr_n = min(NA_ROWS, rows)
    n_heads = rpb.shape[0]
    cols = np.arange(GRID_W)
    cs = np.clip(cols - NA_COLS // 2, 0, GRID_W - NA_COLS)
    kc = np.arange(GRID_W)
    inside = (kc[None, :] >= cs[:, None]) & (kc[None, :] < cs[:, None] + NA_COLS)
    col_idx = kc[None, :] - cols[:, None] + NA_COLS - 1
    onehot = (col_idx.reshape(-1)[None, :] == np.arange(2 * NA_COLS - 1)[:, None]).astype(np.float32)
    e = jnp.einsum("hrd,dq->hrq", rpb.astype(F32), jnp.asarray(onehot), precision=lax.Precision.HIGHEST)
    e = jnp.where(jnp.asarray(inside)[None, None], e.reshape(n_heads, -1, GRID_W, GRID_W), NEG_INF)
    g = jnp.stack([e[:, s:s + kr_n] for s in range(NA_ROWS)], axis=1)
    return jnp.transpose(g, (0, 1, 3, 2, 4)).reshape(n_heads, NA_ROWS, GRID_W, kr_n * GRID_W)


def _na_kernel(q_ref, k_ref, v_ref, g_ref, bias_ref, o_ref):
    seq = q_ref.shape[1]
    rows = seq // GRID_W
    kr_n = min(NA_ROWS, rows)
    scale = HEAD_DIM ** -0.5
    lane_q = lax.broadcasted_iota(jnp.int32, (GRID_W, LANES), 1)

    def row_group(it, carry):
        chains = []
        for u in range(NA_UNROLL):
            r = it * NA_UNROLL + u
            rs = jnp.clip(r - kr_n // 2, 0, rows - kr_n)
            shift = rs - r + NA_ROWS - 1
            q0 = pl.multiple_of(r * GRID_W, GRID_W)
            k0 = pl.multiple_of(rs * GRID_W, GRID_W)
            qblk = q_ref[0, pl.ds(q0, GRID_W), :] * scale
            kblk = k_ref[0, pl.ds(k0, kr_n * GRID_W), :]
            for h in range(2):
                head = (lane_q < HEAD_DIM) if h == 0 else (lane_q >= HEAD_DIM)
                qh = jnp.where(head, qblk, jnp.zeros_like(qblk))
                s = lax.dot_general(qh, kblk, (((1,), (1,)), ((), ())), preferred_element_type=F32)
                chains.append((q0, k0, h, shift, s))
        probs = []
        for q0, k0, h, shift, s in chains:
            s = s + bias_ref[h, shift]
            p = jnp.exp(s - jnp.max(s, axis=1, keepdims=True))
            probs.append((p.astype(BF16), jnp.sum(p, axis=1, keepdims=True)))
        outs = []
        for (q0, k0, h, shift, s), (p, l) in zip(chains, probs):
            vblk = v_ref[0, pl.ds(k0, kr_n * GRID_W), :]
            outs.append(jnp.dot(p, vblk, preferred_element_type=F32) / l)
        for u in range(NA_UNROLL):
            q0 = chains[2 * u][0]
            g = g_ref[0, pl.ds(q0, GRID_W), :].astype(F32)
            o = _head_select(outs[2 * u], outs[2 * u + 1]) * (g * jax.nn.sigmoid(g))
            o_ref[0, pl.ds(q0, GRID_W), :] = o.astype(o_ref.dtype)
        return carry

    lax.fori_loop(0, rows // NA_UNROLL, row_group, 0)


def _neighbourhood_attention(proj, bias, n_heads):
    b, seq, _ = proj.shape
    npair = n_heads // 2
    blk = lambda off: pl.BlockSpec((1, seq, LANES), lambda j, i: (i, 0, off * npair + j))
    return pl.pallas_call(
        _na_kernel,
        grid=(npair, b),
        in_specs=[blk(0), blk(1), blk(2), blk(3),
                  pl.BlockSpec((2,) + bias.shape[1:], lambda j, i: (j, 0, 0, 0))],
        out_specs=pl.BlockSpec((1, seq, LANES), lambda j, i: (i, 0, j)),
        out_shape=jax.ShapeDtypeStruct((b, seq, n_heads * HEAD_DIM), BF16),
        compiler_params=_cparams(("parallel", "parallel")),
        name="neighbourhood_attention",
    )(proj, proj, proj, proj, bias)


def _out_kernel(n_y, *refs):
    y_refs = refs[:n_y]
    w_refs = refs[n_y:2 * n_y]
    h_ref, p_ref, wp_ref, wg_ref, gpost_ref, gple_ref, gnext_ref, h_out, u_out = refs[2 * n_y:]
    t = jnp.dot(y_refs[0][...], w_refs[0][...], preferred_element_type=F32)
    for y_ref, w_ref in zip(y_refs[1:], w_refs[1:]):
        t = t + jnp.dot(y_ref[...], w_ref[...], preferred_element_type=F32)
    h = h_ref[...] + _rms(t, gpost_ref[...])
    e = _rms(jnp.dot(p_ref[...].astype(BF16), wp_ref[...], preferred_element_type=F32), gple_ref[...])
    gate = jax.nn.sigmoid(jnp.dot(h.astype(BF16), wg_ref[...], preferred_element_type=F32))
    h = h + gate * e
    h_out[...] = h
    u_out[...] = _rms(h, gnext_ref[...]).astype(u_out.dtype)


def _out_block(ys, ws, h, p, w_ple, w_gate, g_post, g_ple, g_next, tm=512):
    m, d = h.shape
    n_y = len(ys)
    row = lambda a: pl.BlockSpec((tm, a.shape[1]), lambda i: (i, 0))
    full = lambda a: pl.BlockSpec(a.shape, lambda i: (0, 0))
    vec = lambda g: g.reshape(1, d)
    return pl.pallas_call(
        functools.partial(_out_kernel, n_y),
        grid=(m // tm,),
        in_specs=[row(y) for y in ys] + [full(w) for w in ws]
                 + [row(h), row(p), full(w_ple), full(w_gate), full(vec(g_post)), full(vec(g_ple)), full(vec(g_next))],
        out_specs=[pl.BlockSpec((tm, d), lambda i: (i, 0)), pl.BlockSpec((tm, d), lambda i: (i, 0))],
        out_shape=[jax.ShapeDtypeStruct((m, d), F32), jax.ShapeDtypeStruct((m, d), BF16)],
        compiler_params=_cparams(("parallel",)),
        name="out_proj_residual_ple",
    )(*ys, *ws, h, p, w_ple, w_gate, vec(g_post), vec(g_ple), vec(g_next))


def kernel(x, p, w_in, w_out, norm_pre, norm_post, hyena_conv_w, hyena_conv_b, hyena_w1, hyena_b1, hyena_w2, hyena_b2, hyena_w3, hyena_b3, hyena_w4, hyena_freq, hyena_skip, rel_bias, na_rpb, ple_proj, ple_norm, ple_gate):
    b, seq, d = x.shape
    depth = w_in.shape[0]
    d_inner = w_out.shape[1]
    hw = d_inner // 2
    m = b * seq
    fwd, inv = _dft_matrices(seq)
    dil_bias = _dilated_bias(rel_bias)
    h = x.reshape(m, d)
    u = _rmsnorm_bf16(h, norm_pre[0])
    for i in range(depth):
        j = i // 2
        w = w_in[i].astype(BF16)
        if i % 2 == 0:
            w_a = w[:, :4 * hw].T
            ident = jnp.tile(jnp.array([0.0, 1.0, 0.0], F32)[:, None], (1, hw))
            conv_w = jnp.concatenate([hyena_conv_w[j], ident], axis=1).T
            conv_b = jnp.concatenate([hyena_conv_b[j], jnp.zeros((hw,), F32)])[:, None]
            hy = _proj_t(w_a, u.reshape(b, seq, d), conv_w, conv_b)
            halves = _hyena_filters_t(seq, hw, hyena_w1[j], hyena_b1[j], hyena_w2[j], hyena_b2[j],
                                      hyena_w3[j], hyena_b3[j], hyena_w4[j], hyena_freq[j])
            kspec = _filter_spectra(halves, fwd)
            skip = hyena_skip[j][:, :, None]
            y1 = _dft_filter(hy, 0, hw, fwd, kspec[0])
            z = _idft_gate1(y1, inv, hy, skip[0], hw)
            y2 = _dft_filter(z, 0, hw, fwd, kspec[1])
            ya = _idft_gate2(y2, inv, z, hy, skip[1], hw)
            proj_b = _matmul_bf16(u, w[:, 4 * hw:]).reshape(b, seq, 4 * hw)
            yb = _dilated_attention(proj_b, dil_bias, hw // HEAD_DIM)
            ys = [ya.reshape(m, hw), yb.reshape(m, hw)]
            ws = [w_out[i, :hw].astype(BF16), w_out[i, hw:].astype(BF16)]
        else:
            proj = _matmul_bf16(u, w).reshape(b, seq, 4 * d_inner)
            yc = _neighbourhood_attention(proj, _na_bias(na_rpb[j], seq // GRID_W), d_inner // HEAD_DIM)
            ys = [yc.reshape(m, d_inner)]
            ws = [w_out[i].astype(BF16)]
        g_next = norm_pre[i + 1] if i + 1 < depth else norm_pre[i]
        h, u = _out_block(ys, ws, h, p[i].reshape(m, -1), ple_proj[i].astype(BF16), ple_gate[i].astype(BF16),
                          norm_post[i], ple_norm[i], g_next)
    return h.reshape(b, seq, d)
```

```python
import functools
import math

import numpy as np
import jax
import jax.numpy as jnp
from jax import lax
from jax.experimental import pallas as pl
from jax.experimental.pallas import tpu as pltpu

F32 = jnp.float32
BF16 = jnp.bfloat16

HEAD_DIM = 64
LANES = 128
HYENA_EMB_DIM = 33
HYENA_FAST_DECAY = 0.3
HYENA_SLOW_DECAY = 1.5
HYENA_TARGET = 1e-2
DILATED_PATTERNS = ((128, 1), (512, 4), (2048, 16))
GRID_W = 64
NA_ROWS = 8
NA_COLS = 16
REL_BUCKETS = 32
REL_MAX_DIST = 1024
RMS_EPS = 1e-6
NEG_INF = -1e30
ATT_QB = 128
ATT_HALF = 64
ATT_KW = ATT_QB + 2 * ATT_HALF
ATT_UNROLL = 2
ATT_STEP = 4
assert all(b[1] == a[1] * ATT_STEP for a, b in zip(DILATED_PATTERNS, DILATED_PATTERNS[1:])) and DILATED_PATTERNS[0][1] == 1
NA_UNROLL = 4
VMEM_LIMIT = 56 * 1024 * 1024


def _cparams(sem):
    return pltpu.CompilerParams(dimension_semantics=sem, vmem_limit_bytes=VMEM_LIMIT)


def _rms(x, g):
    return x * lax.rsqrt(jnp.mean(x * x, axis=-1, keepdims=True) + RMS_EPS) * g


def _rmsnorm_kernel(x_ref, g_ref, o_ref):
    o_ref[...] = _rms(x_ref[...], g_ref[...]).astype(o_ref.dtype)


def _rmsnorm_bf16(x2d, g, tm=1024):
    m, d = x2d.shape
    return pl.pallas_call(
        _rmsnorm_kernel,
        grid=(m // tm,),
        in_specs=[pl.BlockSpec((tm, d), lambda i: (i, 0)), pl.BlockSpec((1, d), lambda i: (0, 0))],
        out_specs=pl.BlockSpec((tm, d), lambda i: (i, 0)),
        out_shape=jax.ShapeDtypeStruct((m, d), BF16),
        compiler_params=_cparams(("parallel",)),
        name="rmsnorm",
    )(x2d, g.reshape(1, d))


def _matmul_kernel(a_ref, b_ref, o_ref):
    o_ref[...] = jnp.dot(a_ref[...], b_ref[...], preferred_element_type=F32).astype(o_ref.dtype)


def _matmul_bf16(a, b, tm=1024, tn=1024):
    m, k = a.shape
    _, n = b.shape
    return pl.pallas_call(
        _matmul_kernel,
        grid=(m // tm, n // tn),
        in_specs=[pl.BlockSpec((tm, k), lambda i, j: (i, 0)), pl.BlockSpec((k, tn), lambda i, j: (0, j))],
        out_specs=pl.BlockSpec((tm, tn), lambda i, j: (i, j)),
        out_shape=jax.ShapeDtypeStruct((m, n), BF16),
        compiler_params=_cparams(("parallel", "parallel")),
        name="in_proj",
    )(a, b)


def _proj_t_kernel(w_ref, u_ref, cw_ref, cb_ref, o_ref):
    p = lax.dot_general(w_ref[...], u_ref[0], (((1,), (1,)), ((), ())), preferred_element_type=F32)
    seq = p.shape[1]
    t = lax.broadcasted_iota(jnp.int32, p.shape, 1)
    prev = jnp.where(t == 0, 0.0, pltpu.roll(p, 1, axis=1))
    nxt = jnp.where(t == seq - 1, 0.0, pltpu.roll(p, seq - 1, axis=1))
    cw = cw_ref[...]
    o_ref[0] = (prev * cw[:, 0:1] + p * cw[:, 1:2] + nxt * cw[:, 2:3] + cb_ref[...]).astype(o_ref.dtype)


def _proj_t(w_t, u, conv_w, conv_b, tc=256):
    c, d = w_t.shape
    b, seq, _ = u.shape
    return pl.pallas_call(
        _proj_t_kernel,
        grid=(b, c // tc),
        in_specs=[
            pl.BlockSpec((tc, d), lambda i, j: (j, 0)),
            pl.BlockSpec((1, seq, d), lambda i, j: (i, 0, 0)),
            pl.BlockSpec((tc, 3), lambda i, j: (j, 0)),
            pl.BlockSpec((tc, 1), lambda i, j: (j, 0)),
        ],
        out_specs=pl.BlockSpec((1, tc, seq), lambda i, j: (i, j, 0)),
        out_shape=jax.ShapeDtypeStruct((b, c, seq), BF16),
        compiler_params=_cparams(("parallel", "parallel")),
        name="in_proj_t_conv3",
    )(w_t, u, conv_w, conv_b)


def _hyena_filter_kernel(z_ref, w1_ref, b1_ref, w2_ref, b2_ref, w3_ref, b3_ref, fq_ref, w4_ref, dl_ref, t_ref,
                         o_ref, a_scr):
    hp = lax.Precision.HIGHEST

    @pl.when((pl.program_id(0) == 0) & (pl.program_id(1) == 0))
    def _():
        for d in range(2):
            a = z_ref[d]
            for w_ref, b_ref in ((w1_ref, b1_ref), (w2_ref, b2_ref), (w3_ref, b3_ref)):
                a = jnp.sin(fq_ref[...] * (jnp.dot(w_ref[...], a, precision=hp, preferred_element_type=F32) + b_ref[...]))
            a_scr[d] = a

    seq = o_ref.shape[-1]
    lo = jnp.dot(w4_ref[0, 0], a_scr[0], precision=hp, preferred_element_type=F32) * jnp.exp(-t_ref[0] * dl_ref[...])
    hi = jnp.dot(w4_ref[0, 1], a_scr[1], precision=hp, preferred_element_type=F32) * jnp.exp(-t_ref[1] * dl_ref[...])
    hi = jnp.where(lax.broadcasted_iota(jnp.int32, (1, seq), 1) == 0, 0.0, hi)
    norm = jnp.sum(jnp.abs(lo), axis=1, keepdims=True) + jnp.sum(jnp.abs(hi), axis=1, keepdims=True)
    o_ref[0, 0] = (lo / norm).astype(o_ref.dtype)
    o_ref[1, 0] = (hi / norm).astype(o_ref.dtype)


def _hyena_filters_t(seq, c, w1, b1, w2, b2, w3, b3, w4, freq, tc=256):
    tc = min(tc, c)
    fo = w2.shape[0]
    emb = -(-HYENA_EMB_DIM // 8) * 8
    t = jnp.linspace(0.0, 1.0, seq, dtype=F32)
    bands = (HYENA_EMB_DIM - 1) // 2
    fr = jnp.linspace(1e-4, bands - 1, bands, dtype=F32)[:, None]
    mirror = (seq - jnp.arange(seq)) % seq
    tables, times = [], []
    for pos, tp in ((jnp.arange(seq), t), (mirror, jnp.roll(t[::-1], 1))):
        wpos = (2.0 * math.pi / seq) * pos.astype(F32)[None, :]
        tp = tp[None, :]
        z = jnp.concatenate([tp, jnp.cos(fr * wpos), -jnp.sin(fr * wpos),
                             jnp.zeros((emb - HYENA_EMB_DIM, seq), F32)], axis=0)
        tables.append(z)
        times.append(tp)
    z = jnp.stack(tables)
    tt = jnp.stack(times)
    w1_t = jnp.pad(w1.astype(F32).T, ((0, 0), (0, emb - HYENA_EMB_DIM)))
    w4_t = jnp.transpose(w4.astype(F32).reshape(fo, 2, 2, c), (1, 2, 3, 0))
    min_decay = math.log(HYENA_TARGET) / HYENA_SLOW_DECAY
    max_decay = math.log(HYENA_TARGET) / HYENA_FAST_DECAY
    deltas = jnp.abs(jnp.linspace(min_decay, max_decay, c, dtype=F32))[:, None]
    col = lambda v: v.astype(F32)[:, None]
    full = lambda a: pl.BlockSpec(a.shape, lambda f, i: (0,) * a.ndim)
    args = (z, w1_t, col(b1), w2.astype(F32).T, col(b2), w3.astype(F32).T, col(b3), col(freq))
    return pl.pallas_call(
        _hyena_filter_kernel,
        grid=(2, c // tc),
        in_specs=[full(a) for a in args]
                 + [pl.BlockSpec((1, 2, tc, fo), lambda f, i: (f, 0, i, 0)),
                    pl.BlockSpec((tc, 1), lambda f, i: (i, 0)), full(tt)],
        out_specs=pl.BlockSpec((2, 1, tc, seq), lambda f, i: (0, f, i, 0)),
        out_shape=jax.ShapeDtypeStruct((2, 2, c, seq), BF16),
        scratch_shapes=[pltpu.VMEM((2, fo, seq), F32)],
        compiler_params=_cparams(("arbitrary", "arbitrary")),
        name="hyena_filter_mlp",
    )(*args, w4_t, deltas, tt)


def _fft_consts(seq):
    n_fft = 2 * seq
    n1 = n_fft // LANES
    nb = n1 // 2 + 1
    kp = -(-nb // 8) * 8
    k1 = jnp.arange(kp, dtype=jnp.int32)
    keep = (k1 < nb)[:, None]
    two_pi = 2.0 * math.pi

    th = ((k1[:, None] * jnp.arange(n1, dtype=jnp.int32)[None, :]) % n1).astype(F32) * (two_pi / n1)
    f1 = jnp.concatenate([jnp.where(keep, jnp.cos(th), 0.0), jnp.where(keep, -jnp.sin(th), 0.0)], axis=0)
    ph = (k1[:, None] * jnp.arange(LANES, dtype=jnp.int32)[None, :]).astype(F32) * (two_pi / n_fft)
    tw = jnp.stack([jnp.cos(ph), -jnp.sin(ph)])
    om = ((jnp.arange(LANES, dtype=jnp.int32)[:, None] * jnp.arange(LANES, dtype=jnp.int32)[None, :]) % LANES
          ).astype(F32) * (two_pi / LANES)
    f2 = jnp.concatenate([jnp.cos(om), -jnp.sin(om)], axis=1)
    f2i = jnp.concatenate([jnp.cos(om), jnp.sin(om)], axis=1)
    weight = jnp.where((k1 == 0) | (k1 == nb - 1), 1.0, 2.0)[:, None]
    g = jnp.concatenate([jnp.where(keep, weight * jnp.cos(th[:, :n1 // 2]), 0.0),
                         jnp.where(keep, -weight * jnp.sin(th[:, :n1 // 2]), 0.0)], axis=0).T
    return dict(f1=f1.astype(BF16), tw=tw, f2=f2.astype(BF16), f2i=f2i.astype(BF16), g=g.astype(BF16), kp=kp)


def _per_channel_matmul(mat_ref, planes, out_scr, ct):
    def body(i, carry):
        c = 2 * i
        pair = jnp.concatenate([planes(c), planes(c + 1)], axis=1)
        res = jnp.dot(mat_ref[...], pair, preferred_element_type=F32)
        out_scr[c] = res[:, :LANES]
        out_scr[c + 1] = res[:, LANES:]
        return carry
    lax.fori_loop(0, ct // 2, body, 0, unroll=16)


def _twiddle_stage2(a, tw_ref, f2_ref):
    ct, kp2, _ = a.shape
    kp = kp2 // 2
    ar, ai = a[:, :kp], a[:, kp:]
    twr, twi = tw_ref[0], tw_ref[1]
    st = jnp.concatenate([ar * twr - ai * twi, ar * twi + ai * twr], axis=1).astype(BF16)
    p = jnp.dot(st.reshape(ct * kp2, LANES), f2_ref[...], preferred_element_type=F32).reshape(ct, kp2, 2 * LANES)
    return p[:, :kp, :LANES] - p[:, kp:, LANES:], p[:, :kp, LANES:] + p[:, kp:, :LANES]


def _stage2_twiddle_inv(yr, yi, tw_ref, f2i_ref):
    ct, kp, _ = yr.shape
    st = jnp.concatenate([yr, yi], axis=1).astype(BF16)
    p = jnp.dot(st.reshape(ct * 2 * kp, LANES), f2i_ref[...], preferred_element_type=F32).reshape(ct, 2 * kp, 2 * LANES)
    tr, ti = p[:, :kp, :LANES] - p[:, kp:, LANES:], p[:, :kp, LANES:] + p[:, kp:, :LANES]
    twr, twi = tw_ref[0], tw_ref[1]
    return jnp.concatenate([tr * twr + ti * twi, ti * twr - tr * twi], axis=1).astype(BF16)


def _filter_spec_kernel(lo_ref, hi_ref, f1_ref, tw_ref, f2_ref, kr_ref, ki_ref, a_scr):
    ct = lo_ref.shape[2]
    planes = lambda c: jnp.concatenate([lo_ref[0, 0, c], hi_ref[0, 0, c]], axis=0)
    _per_channel_matmul(f1_ref, planes, a_scr, ct)
    xr, xi = _twiddle_stage2(a_scr[...], tw_ref, f2_ref)
    n_fft = f1_ref.shape[1] * LANES
    kr_ref[0] = xr * (1.0 / n_fft)
    ki_ref[0] = xi * (1.0 / n_fft)


def _filter_spectrum(halves, consts, ct=64):
    _, nf, c, seq = halves.shape
    ct = min(ct, c)
    kp = consts["kp"]
    h5 = halves.reshape(2, nf, c, seq // LANES, LANES)
    half = lambda s: pl.BlockSpec((1, 1, ct, seq // LANES, LANES), lambda f, i: (s, f, i, 0, 0))
    full = lambda a: pl.BlockSpec(a.shape, lambda f, i: (0,) * a.ndim)
    spec = jax.ShapeDtypeStruct((nf, c, kp, LANES), F32)
    out = pl.BlockSpec((1, ct, kp, LANES), lambda f, i: (f, i, 0, 0))
    return pl.pallas_call(
        _filter_spec_kernel,
        grid=(nf, c // ct),
        in_specs=[half(0), half(1), full(consts["f1"]), full(consts["tw"]), full(consts["f2"])],
        out_specs=[out, out],
        out_shape=[spec, spec],
        scratch_shapes=[pltpu.VMEM((ct, 2 * kp, LANES), F32)],
        compiler_params=_cparams(("parallel", "parallel")),
        name="hyena_filter_spectrum",
    )(h5, h5, consts["f1"], consts["tw"], consts["f2"])


def _hyena_conv_kernel(second, x_ref, m_ref, e_ref, skip_ref, kr_ref, ki_ref, f1_ref, tw_ref, f2_ref, f2i_ref, g_ref,
                       o_ref, a_scr, t_scr, y_scr):
    ct = x_ref.shape[1]
    _per_channel_matmul(f1_ref, lambda c: x_ref[0, c], a_scr, ct)
    xr, xi = _twiddle_stage2(a_scr[...], tw_ref, f2_ref)
    kr, ki = kr_ref[0], ki_ref[0]
    t_scr[...] = _stage2_twiddle_inv(xr * kr - xi * ki, xr * ki + xi * kr, tw_ref, f2i_ref)

    _per_channel_matmul(g_ref, lambda c: t_scr[c], y_scr, ct)

    out = m_ref[0].astype(F32) * (y_scr[...] + x_ref[0].astype(F32) * skip_ref[...])
    if second:
        e = e_ref[0].astype(F32)
        out = out * (e * jax.nn.sigmoid(e))
    o_ref[0] = out.astype(o_ref.dtype)


def _hyena_conv(second, x, x_row0, hy, m_row0, e_row0, skip, kr, ki, filt, consts, ct=64):
    b, _, seq = x.shape
    c = kr.shape[1]
    ct = min(ct, c)
    kp = consts["kp"]
    n1h = seq // LANES
    view = lambda a: a.reshape(a.shape[0], a.shape[1], n1h, LANES)
    rows = lambda blk0: pl.BlockSpec((1, ct, n1h, LANES), lambda j, i: (i, blk0 + j, 0, 0))
    full = lambda a: pl.BlockSpec(a.shape, lambda j, i: (0,) * a.ndim)
    kspec = pl.BlockSpec((1, ct, kp, LANES), lambda j, i: (filt, j, 0, 0))
    mats = [consts["f1"][:, :n1h], consts["tw"], consts["f2"], consts["f2i"], consts["g"]]
    out = pl.pallas_call(
        functools.partial(_hyena_conv_kernel, second),
        grid=(c // ct, b),
        in_specs=[rows(x_row0 // ct), rows(m_row0 // ct), rows(e_row0 // ct),
                  pl.BlockSpec((ct, 1, 1), lambda j, i: (j, 0, 0)),
                  kspec, kspec] + [full(a) for a in mats],
        out_specs=pl.BlockSpec((1, ct, n1h, LANES), lambda j, i: (i, j, 0, 0)),
        out_shape=jax.ShapeDtypeStruct((b, c, n1h, LANES), BF16),
        scratch_shapes=[pltpu.VMEM((ct, 2 * kp, LANES), F32), pltpu.VMEM((ct, 2 * kp, LANES), BF16),
                        pltpu.VMEM((ct, n1h, LANES), F32)],
        compiler_params=_cparams(("parallel", "parallel")),
        name="hyena_conv2" if second else "hyena_conv1",
    )(view(x), view(hy), view(hy), skip, kr, ki, *mats)
    return out.reshape(b, c, seq)


def _t5_bucket(rel):
    half_b = REL_BUCKETS // 2
    max_exact = half_b // 2
    ret = jnp.where(rel > 0, half_b, 0)
    n = jnp.abs(rel)
    nf = jnp.maximum(n, 1).astype(F32)
    large = max_exact + (jnp.log(nf / max_exact) / math.log(REL_MAX_DIST / max_exact)
                         * (half_b - max_exact)).astype(jnp.int32)
    large = jnp.minimum(large, half_b - 1)
    return ret + jnp.where(n < max_exact, n, large)


def _dilated_bias(rel_bias):
    rel = jnp.arange(ATT_KW)[None, :] - ATT_HALF - jnp.arange(ATT_QB)[:, None]
    col = jnp.arange(ATT_KW)
    edge = jnp.stack([col >= ATT_HALF, col >= 0, col < ATT_KW - ATT_HALF])
    out = []
    for _, dil in DILATED_PATTERNS:
        bias = rel_bias[_t5_bucket(rel * dil)].astype(F32)
        bias = jnp.where((jnp.abs(rel) <= ATT_HALF)[:, :, None], bias, NEG_INF)
        bias = jnp.transpose(bias, (2, 0, 1))[:, None]
        out.append(jnp.where(edge[None, :, None, :], bias, NEG_INF))
    return jnp.stack(out)


def _head_select(first, second):
    lane = lax.broadcasted_iota(jnp.int32, first.shape[:-1] + (LANES,), len(first.shape) - 1)
    return jnp.where(lane < HEAD_DIM, first, second)


def _dilated_kernel(q_ref, k_ref, v_ref, g_ref, bias_ref, o_ref, *scratch):
    stage_a, stage_b = scratch[0:3], scratch[3:6]
    qd, kd, vd = scratch[6:9]
    last = len(DILATED_PATTERNS) - 1
    states = [scratch[9 + 3 * p:12 + 3 * p] for p in range(last)]
    seq = q_ref.shape[1]
    stage_a[0][...] = q_ref[0].astype(F32) * (HEAD_DIM ** -0.5)
    stage_a[1][...] = k_ref[0].astype(F32)
    stage_a[2][...] = v_ref[0].astype(F32)
    lane_q = lax.broadcasted_iota(jnp.int32, (ATT_QB, LANES), 1)
    zero_halo = jnp.zeros((ATT_HALF, LANES), BF16)

    src, dst = stage_a, stage_b
    for pi, (_, dil) in enumerate(DILATED_PATTERNS):
        n = seq // dil
        nblk = n // ATT_QB
        if pi > 0:
            prev_dil = dil // ATT_STEP
            for r1 in range(prev_dil):
                for r2 in range(ATT_STEP):
                    r = r1 + prev_dil * r2
                    for a_src, a_dst in zip(src, dst):
                        a_dst[r * n:(r + 1) * n, :] = a_src[pl.ds(r1 * n * ATT_STEP + r2, n, stride=ATT_STEP), :]
            src, dst = dst, src
        stride_k = n + 2 * ATT_HALF
        qd[...] = src[0][...].astype(BF16)
        for r in range(dil):
            base = r * stride_k
            for a_src, a_dst in ((src[1], kd), (src[2], vd)):
                a_dst[base:base + ATT_HALF, :] = zero_halo
                a_dst[base + ATT_HALF:base + ATT_HALF + n, :] = a_src[r * n:(r + 1) * n, :].astype(BF16)
                a_dst[base + ATT_HALF + n:base + stride_k, :] = zero_halo
        if pi == last:
            states.append(dst)
        acc_w, mx_w, den_w = states[pi]

        def block_group(it, carry, pi=pi, nblk=nblk, acc_w=acc_w, mx_w=mx_w, den_w=den_w):
            chains = []
            for u in range(ATT_UNROLL):
                blk = it * ATT_UNROLL + u
                r = blk // nblk
                qb = blk % nblk
                q0 = pl.multiple_of(blk * ATT_QB, ATT_QB)
                k0 = pl.multiple_of(q0 + r * (2 * ATT_HALF), ATT_QB)
                qblk = qd[pl.ds(q0, ATT_QB), :]
                kblk = kd[pl.ds(k0, ATT_KW), :]
                edge = jnp.where(qb == 0, 0, jnp.where(qb == nblk - 1, 2, 1))
                for h in range(2):
                    head = (lane_q < HEAD_DIM) if h == 0 else (lane_q >= HEAD_DIM)
                    qh = jnp.where(head, qblk, jnp.zeros_like(qblk))
                    s = lax.dot_general(qh, kblk, (((1,), (1,)), ((), ())), preferred_element_type=F32)
                    chains.append((q0, k0, h, edge, s))
            probs = []
            for q0, k0, h, edge, s in chains:
                s = s + bias_ref[pi, h, edge]
                m = jnp.max(s, axis=1, keepdims=True)
                p = jnp.exp(s - m)
                probs.append((p.astype(BF16), m, jnp.sum(p, axis=1, keepdims=True)))
            pvs = [jnp.dot(p, vd[pl.ds(k0, ATT_KW), :], preferred_element_type=F32)
                   for (q0, k0, h, edge, s), (p, m, l) in zip(chains, probs)]
            for u in range(ATT_UNROLL):
                q0 = chains[2 * u][0]
                (_, m0, l0), (_, m1, l1) = probs[2 * u], probs[2 * u + 1]
                acc_w[pl.ds(q0, ATT_QB), :] = _head_select(pvs[2 * u], pvs[2 * u + 1])
                mx_w[pl.ds(q0, ATT_QB), :] = _head_select(m0, m1)
                den_w[pl.ds(q0, ATT_QB), :] = _head_select(l0, l1)
            return carry

        lax.fori_loop(0, dil * nblk // ATT_UNROLL, block_group, 0)

    for pi in range(len(DILATED_PATTERNS) - 1, 0, -1):
        dil = DILATED_PATTERNS[pi][1]
        n = seq // dil
        prev_dil = dil // ATT_STEP
        acc_c, mx_c, den_c = states[pi - 1]
        acc_f, mx_f, den_f = states[pi]
        for r1 in range(prev_dil):
            for r2 in range(ATT_STEP):
                r = r1 + prev_dil * r2
                rows = pl.ds(r1 * n * ATT_STEP + r2, n, stride=ATT_STEP)
                fine = slice(r * n, (r + 1) * n)
                m_old = mx_c[rows, :]
                m_new = mx_f[fine, :]
                m_all = jnp.maximum(m_old, m_new)
                a_old = jnp.exp(m_old - m_all)
                a_new = jnp.exp(m_new - m_all)
                acc_c[rows, :] = acc_c[rows, :] * a_old + acc_f[fine, :] * a_new
                den_c[rows, :] = den_c[rows, :] * a_old + den_f[fine, :] * a_new
                mx_c[rows, :] = m_all

    acc, _, den = states[0]
    g = g_ref[0].astype(F32)
    o_ref[0] = (acc[...] / den[...] * (g * jax.nn.sigmoid(g))).astype(o_ref.dtype)


def _dilated_attention(proj, bias, n_heads):
    b, seq, _ = proj.shape
    npair = n_heads // 2
    max_dil = max(d for _, d in DILATED_PATTERNS)
    kd_rows = seq + 2 * ATT_HALF * max_dil
    blk = lambda off: pl.BlockSpec((1, seq, LANES), lambda i, j: (i, 0, off * npair + j))
    state = pltpu.VMEM((seq, LANES), F32)
    return pl.pallas_call(
        _dilated_kernel,
        grid=(b, npair),
        in_specs=[blk(0), blk(1), blk(2), blk(3),
                  pl.BlockSpec((len(DILATED_PATTERNS), 2, 3, ATT_QB, ATT_KW), lambda i, j: (0, j, 0, 0, 0))],
        out_specs=pl.BlockSpec((1, seq, LANES), lambda i, j: (i, 0, j)),
        out_shape=jax.ShapeDtypeStruct((b, seq, n_heads * HEAD_DIM), BF16),
        scratch_shapes=[state] * 6
                       + [pltpu.VMEM((seq, LANES), BF16), pltpu.VMEM((kd_rows, LANES), BF16),
                          pltpu.VMEM((kd_rows, LANES), BF16)]
                       + [state] * (3 * (len(DILATED_PATTERNS) - 1)),
        compiler_params=_cparams(("parallel", "parallel")),
        name="dilated_attention",
    )(proj, proj, proj, proj, bias)


def _na_bias(rpb, rows):
    kr_n = min(NA_ROWS, rows)
    n_heads = rpb.shape[0]
    cols = np.arange(GRID_W)
    cs = np.clip(cols - NA_COLS // 2, 0, GRID_W - NA_COLS)
    kc = np.arange(GRID_W)
    inside = (kc[None, :] >= cs[:, None]) & (kc[None, :] < cs[:, None] + NA_COLS)
    col_idx = kc[None, :] - cols[:, None] + NA_COLS - 1
    onehot = (col_idx.reshape(-1)[None, :] == np.arange(2 * NA_COLS - 1)[:, None]).astype(np.float32)
    e = jnp.einsum("hrd,dq->hrq", rpb.astype(F32), jnp.asarray(onehot), precision=lax.Precision.HIGHEST)
    e = jnp.where(jnp.asarray(inside)[None, None], e.reshape(n_heads, -1, GRID_W, GRID_W), NEG_INF)
    g = jnp.stack([e[:, s:s + kr_n] for s in range(NA_ROWS)], axis=1)
    return jnp.transpose(g, (0, 1, 3, 2, 4)).reshape(n_heads, NA_ROWS, GRID_W, kr_n * GRID_W)


def _na_kernel(q_ref, k_ref, v_ref, g_ref, bias_ref, o_ref):
    seq = q_ref.shape[1]
    rows = seq // GRID_W
    kr_n = min(NA_ROWS, rows)
    scale = HEAD_DIM ** -0.5
    lane_q = lax.broadcasted_iota(jnp.int32, (GRID_W, LANES), 1)

    def row_group(it, carry):
        chains = []
        for u in range(NA_UNROLL):
            r = it * NA_UNROLL + u
            rs = jnp.clip(r - kr_n // 2, 0, rows - kr_n)
            shift = rs - r + NA_ROWS - 1
            q0 = pl.multiple_of(r * GRID_W, GRID_W)
            k0 = pl.multiple_of(rs * GRID_W, GRID_W)
            qblk = q_ref[0, pl.ds(q0, GRID_W), :] * scale
            kblk = k_ref[0, pl.ds(k0, kr_n * GRID_W), :]
            for h in range(2):
                head = (lane_q < HEAD_DIM) if h == 0 else (lane_q >= HEAD_DIM)
                qh = jnp.where(head, qblk, jnp.zeros_like(qblk))
                s = lax.dot_general(qh, kblk, (((1,), (1,)), ((), ())), preferred_element_type=F32)
                chains.append((q0, k0, h, shift, s))
        probs = []
        for q0, k0, h, shift, s in chains:
            s = s + bias_ref[h, shift]
            p = jnp.exp(s - jnp.max(s, axis=1, keepdims=True))
            probs.append((p.astype(BF16), jnp.sum(p, axis=1, keepdims=True)))
        outs = []
        for (q0, k0, h, shift, s), (p, l) in zip(chains, probs):
            vblk = v_ref[0, pl.ds(k0, kr_n * GRID_W), :]
            outs.append(jnp.dot(p, vblk, preferred_element_type=F32) / l)
        for u in range(NA_UNROLL):
            q0 = chains[2 * u][0]
            g = g_ref[0, pl.ds(q0, GRID_W), :].astype(F32)
            o = _head_select(outs[2 * u], outs[2 * u + 1]) * (g * jax.nn.sigmoid(g))
            o_ref[0, pl.ds(q0, GRID_W), :] = o.astype(o_ref.dtype)
        return carry

    lax.fori_loop(0, rows // NA_UNROLL, row_group, 0)


def _neighbourhood_attention(proj, bias, n_heads):
    b, seq, _ = proj.shape
    npair = n_heads // 2
    blk = lambda off: pl.BlockSpec((1, seq, LANES), lambda j, i: (i, 0, off * npair + j))
    return pl.pallas_call(
        _na_kernel,
        grid=(npair, b),
        in_specs=[blk(0), blk(1), blk(2), blk(3),
                  pl.BlockSpec((2,) + bias.shape[1:], lambda j, i: (j, 0, 0, 0))],
        out_specs=pl.BlockSpec((1, seq, LANES), lambda j, i: (i, 0, j)),
        out_shape=jax.ShapeDtypeStruct((b, seq, n_heads * HEAD_DIM), BF16),
        compiler_params=_cparams(("parallel", "parallel")),
        name="neighbourhood_attention",
    )(proj, proj, proj, proj, bias)


def _out_kernel(n_y, n_t, *refs):
    n = n_y + n_t
    y_refs = refs[:n]
    w_refs = refs[n:2 * n]
    h_ref, p_ref, wp_ref, wg_ref, gpost_ref, gple_ref, gnext_ref, h_out, u_out = refs[2 * n:]
    t = None
    for idx, (y_ref, w_ref) in enumerate(zip(y_refs, w_refs)):
        if idx < n_y:
            part = jnp.dot(y_ref[...], w_ref[...], preferred_element_type=F32)
        else:
            part = lax.dot_general(y_ref[0], w_ref[...], (((0,), (0,)), ((), ())), preferred_element_type=F32)
        t = part if t is None else t + part
    h = h_ref[...] + _rms(t, gpost_ref[...])
    e = _rms(jnp.dot(p_ref[...].astype(BF16), wp_ref[...], preferred_element_type=F32), gple_ref[...])
    gate = jax.nn.sigmoid(jnp.dot(h.astype(BF16), wg_ref[...], preferred_element_type=F32))
    h = h + gate * e
    h_out[...] = h
    u_out[...] = _rms(h, gnext_ref[...]).astype(u_out.dtype)


def _out_block(ys, ys_t, ws, h, p, w_ple, w_gate, g_post, g_ple, g_next, tm=512):
    m, d = h.shape
    row = lambda a: pl.BlockSpec((tm, a.shape[1]), lambda i: (i, 0))
    full = lambda a: pl.BlockSpec(a.shape, lambda i: (0, 0))
    vec = lambda g: g.reshape(1, d)

    def col(a):
        tiles = a.shape[2] // tm
        return pl.BlockSpec((1, a.shape[1], tm), lambda i: (i // tiles, 0, i % tiles))

    return pl.pallas_call(
        functools.partial(_out_kernel, len(ys), len(ys_t)),
        grid=(m // tm,),
        in_specs=[row(y) for y in ys] + [col(y) for y in ys_t] + [full(w) for w in ws]
                 + [row(h), row(p), full(w_ple), full(w_gate), full(vec(g_post)), full(vec(g_ple)), full(vec(g_next))],
        out_specs=[pl.BlockSpec((tm, d), lambda i: (i, 0)), pl.BlockSpec((tm, d), lambda i: (i, 0))],
        out_shape=[jax.ShapeDtypeStruct((m, d), F32), jax.ShapeDtypeStruct((m, d), BF16)],
        compiler_params=_cparams(("parallel",)),
        name="out_proj_residual_ple",
    )(*ys, *ys_t, *ws, h, p, w_ple, w_gate, vec(g_post), vec(g_ple), vec(g_next))


def kernel(x, p, w_in, w_out, norm_pre, norm_post, hyena_conv_w, hyena_conv_b, hyena_w1, hyena_b1, hyena_w2, hyena_b2, hyena_w3, hyena_b3, hyena_w4, hyena_freq, hyena_skip, rel_bias, na_rpb, ple_proj, ple_norm, ple_gate):
    b, seq, d = x.shape
    depth = w_in.shape[0]
    d_inner = w_out.shape[1]
    hw = d_inner // 2
    m = b * seq
    consts = _fft_consts(seq)
    dil_bias = _dilated_bias(rel_bias)
    h = x.reshape(m, d)
    u = _rmsnorm_bf16(h, norm_pre[0])
    for i in range(depth):
        j = i // 2
        w = w_in[i].astype(BF16)
        if i % 2 == 0:
            w_a = w[:, :4 * hw].T
            ident = jnp.tile(jnp.array([0.0, 1.0, 0.0], F32)[:, None], (1, hw))
            conv_w = jnp.concatenate([hyena_conv_w[j], ident], axis=1).T
            conv_b = jnp.concatenate([hyena_conv_b[j], jnp.zeros((hw,), F32)])[:, None]
            hy = _proj_t(w_a, u.reshape(b, seq, d), conv_w, conv_b)
            halves = _hyena_filters_t(seq, hw, hyena_w1[j], hyena_b1[j], hyena_w2[j], hyena_b2[j],
                                      hyena_w3[j], hyena_b3[j], hyena_w4[j], hyena_freq[j])
            kr, ki = _filter_spectrum(halves, consts)
            skip = hyena_skip[j][:, :, None, None]
            z = _hyena_conv(False, hy, 0, hy, hw, hw, skip[0], kr, ki, 0, consts)
            ya_t = _hyena_conv(True, z, 0, hy, 2 * hw, 3 * hw, skip[1], kr, ki, 1, consts)
            proj_b = _matmul_bf16(u, w[:, 4 * hw:]).reshape(b, seq, 4 * hw)
            yb = _dilated_attention(proj_b, dil_bias, hw // HEAD_DIM)
            ys, ys_t = [yb.reshape(m, hw)], [ya_t]
            ws = [w_out[i, hw:].astype(BF16), w_out[i, :hw].astype(BF16)]
        else:
            proj = _matmul_bf16(u, w).reshape(b, seq, 4 * d_inner)
            yc = _neighbourhood_attention(proj, _na_bias(na_rpb[j], seq // GRID_W), d_inner // HEAD_DIM)
            ys, ys_t = [yc.reshape(m, d_inner)], []
            ws = [w_out[i].astype(BF16)]
        g_next = norm_pre[i + 1] if i + 1 < depth else norm_pre[i]
        h, u = _out_block(ys, ys_t, ws, h, p[i].reshape(m, -1), ple_proj[i].astype(BF16), ple_gate[i].astype(BF16),
                          norm_post[i], ple_norm[i], g_next)
    return h.reshape(b, seq, d)
```

```python
import functools
import math

import numpy as np
import jax
import jax.numpy as jnp
from jax import lax
from jax.experimental import pallas as pl
from jax.experimental.pallas import tpu as pltpu

F32 = jnp.float32
BF16 = jnp.bfloat16

HEAD_DIM = 64
LANES = 128
HYENA_EMB_DIM = 33
HYENA_FAST_DECAY = 0.3
HYENA_SLOW_DECAY = 1.5
HYENA_TARGET = 1e-2
DILATED_PATTERNS = ((128, 1), (512, 4), (2048, 16))
GRID_W = 64
NA_ROWS = 8
NA_COLS = 16
REL_BUCKETS = 32
REL_MAX_DIST = 1024
RMS_EPS = 1e-6
NEG_INF = -1e30
ATT_QB = 128
ATT_HALF = 64
ATT_KW = ATT_QB + 2 * ATT_HALF
ATT_UNROLL = 4
ATT_STEP = 4
assert all(b[1] == a[1] * ATT_STEP for a, b in zip(DILATED_PATTERNS, DILATED_PATTERNS[1:])) and DILATED_PATTERNS[0][1] == 1
NA_UNROLL = 8
VMEM_LIMIT = 56 * 1024 * 1024


def _cparams(sem):
    return pltpu.CompilerParams(dimension_semantics=sem, vmem_limit_bytes=VMEM_LIMIT)


def _rms(x, g):
    return x * lax.rsqrt(jnp.mean(x * x, axis=-1, keepdims=True) + RMS_EPS) * g


def _rmsnorm_kernel(x_ref, g_ref, o_ref):
    o_ref[...] = _rms(x_ref[...], g_ref[...]).astype(o_ref.dtype)


def _rmsnorm_bf16(x2d, g, tm=1024):
    m, d = x2d.shape
    return pl.pallas_call(
        _rmsnorm_kernel,
        grid=(m // tm,),
        in_specs=[pl.BlockSpec((tm, d), lambda i: (i, 0)), pl.BlockSpec((1, d), lambda i: (0, 0))],
        out_specs=pl.BlockSpec((tm, d), lambda i: (i, 0)),
        out_shape=jax.ShapeDtypeStruct((m, d), BF16),
        compiler_params=_cparams(("parallel",)),
        name="rmsnorm",
    )(x2d, g.reshape(1, d))


def _matmul_kernel(a_ref, b_ref, o_ref):
    o_ref[...] = jnp.dot(a_ref[...], b_ref[...], preferred_element_type=F32).astype(o_ref.dtype)


def _matmul_bf16(a, b, tm=1024, tn=1024):
    m, k = a.shape
    _, n = b.shape
    return pl.pallas_call(
        _matmul_kernel,
        grid=(m // tm, n // tn),
        in_specs=[pl.BlockSpec((tm, k), lambda i, j: (i, 0)), pl.BlockSpec((k, tn), lambda i, j: (0, j))],
        out_specs=pl.BlockSpec((tm, tn), lambda i, j: (i, j)),
        out_shape=jax.ShapeDtypeStruct((m, n), BF16),
        compiler_params=_cparams(("parallel", "parallel")),
        name="in_proj",
    )(a, b)


def _proj_t_kernel(w_ref, u_ref, cw_ref, cb_ref, o_ref):
    p = lax.dot_general(w_ref[...], u_ref[0], (((1,), (1,)), ((), ())), preferred_element_type=F32)
    seq = p.shape[1]
    t = lax.broadcasted_iota(jnp.int32, p.shape, 1)
    prev = jnp.where(t == 0, 0.0, pltpu.roll(p, 1, axis=1))
    nxt = jnp.where(t == seq - 1, 0.0, pltpu.roll(p, seq - 1, axis=1))
    cw = cw_ref[...]
    o_ref[0] = (prev * cw[:, 0:1] + p * cw[:, 1:2] + nxt * cw[:, 2:3] + cb_ref[...]).astype(o_ref.dtype)


def _proj_t(w_t, u, conv_w, conv_b, tc=256):
    c, d = w_t.shape
    b, seq, _ = u.shape
    return pl.pallas_call(
        _proj_t_kernel,
        grid=(b, c // tc),
        in_specs=[
            pl.BlockSpec((tc, d), lambda i, j: (j, 0)),
            pl.BlockSpec((1, seq, d), lambda i, j: (i, 0, 0)),
            pl.BlockSpec((tc, 3), lambda i, j: (j, 0)),
            pl.BlockSpec((tc, 1), lambda i, j: (j, 0)),
        ],
        out_specs=pl.BlockSpec((1, tc, seq), lambda i, j: (i, j, 0)),
        out_shape=jax.ShapeDtypeStruct((b, c, seq), BF16),
        compiler_params=_cparams(("parallel", "parallel")),
        name="in_proj_t_conv3",
    )(w_t, u, conv_w, conv_b)


def _hyena_filter_kernel(z_ref, w1_ref, b1_ref, w2_ref, b2_ref, w3_ref, b3_ref, fq_ref, w4_ref, dl_ref, t_ref,
                         o_ref, a_scr):
    hp = lax.Precision.HIGHEST

    @pl.when((pl.program_id(0) == 0) & (pl.program_id(1) == 0))
    def _():
        for d in range(2):
            a = z_ref[d]
            for w_ref, b_ref in ((w1_ref, b1_ref), (w2_ref, b2_ref), (w3_ref, b3_ref)):
                a = jnp.sin(fq_ref[...] * (jnp.dot(w_ref[...], a, precision=hp, preferred_element_type=F32) + b_ref[...]))
            a_scr[d] = a

    seq = o_ref.shape[-1]
    lo = jnp.dot(w4_ref[0, 0], a_scr[0], precision=hp, preferred_element_type=F32) * jnp.exp(-t_ref[0] * dl_ref[...])
    hi = jnp.dot(w4_ref[0, 1], a_scr[1], precision=hp, preferred_element_type=F32) * jnp.exp(-t_ref[1] * dl_ref[...])
    hi = jnp.where(lax.broadcasted_iota(jnp.int32, (1, seq), 1) == 0, 0.0, hi)
    norm = jnp.sum(jnp.abs(lo), axis=1, keepdims=True) + jnp.sum(jnp.abs(hi), axis=1, keepdims=True)
    o_ref[0, 0] = (lo / norm).astype(o_ref.dtype)
    o_ref[1, 0] = (hi / norm).astype(o_ref.dtype)


def _hyena_filters_t(seq, c, w1, b1, w2, b2, w3, b3, w4, freq, tc=256):
    tc = min(tc, c)
    fo = w2.shape[0]
    emb = -(-HYENA_EMB_DIM // 8) * 8
    t = jnp.linspace(0.0, 1.0, seq, dtype=F32)
    bands = (HYENA_EMB_DIM - 1) // 2
    fr = jnp.linspace(1e-4, bands - 1, bands, dtype=F32)[:, None]
    mirror = (seq - jnp.arange(seq)) % seq
    tables, times = [], []
    for pos, tp in ((jnp.arange(seq), t), (mirror, jnp.roll(t[::-1], 1))):
        wpos = (2.0 * math.pi / seq) * pos.astype(F32)[None, :]
        tp = tp[None, :]
        z = jnp.concatenate([tp, jnp.cos(fr * wpos), -jnp.sin(fr * wpos),
                             jnp.zeros((emb - HYENA_EMB_DIM, seq), F32)], axis=0)
        tables.append(z)
        times.append(tp)
    z = jnp.stack(tables)
    tt = jnp.stack(times)
    w1_t = jnp.pad(w1.astype(F32).T, ((0, 0), (0, emb - HYENA_EMB_DIM)))
    w4_t = jnp.transpose(w4.astype(F32).reshape(fo, 2, 2, c), (1, 2, 3, 0))
    min_decay = math.log(HYENA_TARGET) / HYENA_SLOW_DECAY
    max_decay = math.log(HYENA_TARGET) / HYENA_FAST_DECAY
    deltas = jnp.abs(jnp.linspace(min_decay, max_decay, c, dtype=F32))[:, None]
    col = lambda v: v.astype(F32)[:, None]
    full = lambda a: pl.BlockSpec(a.shape, lambda f, i: (0,) * a.ndim)
    args = (z, w1_t, col(b1), w2.astype(F32).T, col(b2), w3.astype(F32).T, col(b3), col(freq))
    return pl.pallas_call(
        _hyena_filter_kernel,
        grid=(2, c // tc),
        in_specs=[full(a) for a in args]
                 + [pl.BlockSpec((1, 2, tc, fo), lambda f, i: (f, 0, i, 0)),
                    pl.BlockSpec((tc, 1), lambda f, i: (i, 0)), full(tt)],
        out_specs=pl.BlockSpec((2, 1, tc, seq), lambda f, i: (0, f, i, 0)),
        out_shape=jax.ShapeDtypeStruct((2, 2, c, seq), BF16),
        scratch_shapes=[pltpu.VMEM((2, fo, seq), F32)],
        compiler_params=_cparams(("arbitrary", "arbitrary")),
        name="hyena_filter_mlp",
    )(*args, w4_t, deltas, tt)


def _fft_consts(seq):
    n_fft = 2 * seq
    n1 = n_fft // LANES
    nb = n1 // 2 + 1
    kp = -(-nb // 8) * 8
    k1 = jnp.arange(kp, dtype=jnp.int32)
    keep = (k1 < nb)[:, None]
    two_pi = 2.0 * math.pi

    th = ((k1[:, None] * jnp.arange(n1, dtype=jnp.int32)[None, :]) % n1).astype(F32) * (two_pi / n1)
    f1 = jnp.concatenate([jnp.where(keep, jnp.cos(th), 0.0), jnp.where(keep, -jnp.sin(th), 0.0)], axis=0)
    ph = (k1[:, None] * jnp.arange(LANES, dtype=jnp.int32)[None, :]).astype(F32) * (two_pi / n_fft)
    tw = jnp.stack([jnp.cos(ph), -jnp.sin(ph)])
    om = ((jnp.arange(LANES, dtype=jnp.int32)[:, None] * jnp.arange(LANES, dtype=jnp.int32)[None, :]) % LANES
          ).astype(F32) * (two_pi / LANES)
    f2 = jnp.concatenate([jnp.cos(om), -jnp.sin(om)], axis=1)
    f2i = jnp.concatenate([jnp.cos(om), jnp.sin(om)], axis=1)
    weight = jnp.where((k1 == 0) | (k1 == nb - 1), 1.0, 2.0)[:, None]
    g = jnp.concatenate([jnp.where(keep, weight * jnp.cos(th[:, :n1 // 2]), 0.0),
                         jnp.where(keep, -weight * jnp.sin(th[:, :n1 // 2]), 0.0)], axis=0).T
    return dict(f1=f1.astype(BF16), tw=tw, f2=f2.astype(BF16), f2i=f2i.astype(BF16), g=g.astype(BF16), kp=kp)


def _per_channel_matmul(mat_ref, planes, out_scr, ct):
    def body(i, carry):
        c = 2 * i
        pair = jnp.concatenate([planes(c), planes(c + 1)], axis=1)
        res = jnp.dot(mat_ref[...], pair, preferred_element_type=F32)
        out_scr[c] = res[:, :LANES]
        out_scr[c + 1] = res[:, LANES:]
        return carry
    lax.fori_loop(0, ct // 2, body, 0, unroll=16)


def _twiddle_stage2(a, tw_ref, f2_ref):
    ct, kp2, _ = a.shape
    kp = kp2 // 2
    ar, ai = a[:, :kp], a[:, kp:]
    twr, twi = tw_ref[0], tw_ref[1]
    st = jnp.concatenate([ar * twr - ai * twi, ar * twi + ai * twr], axis=1).astype(BF16)
    p = jnp.dot(st.reshape(ct * kp2, LANES), f2_ref[...], preferred_element_type=F32).reshape(ct, kp2, 2 * LANES)
    return p[:, :kp, :LANES] - p[:, kp:, LANES:], p[:, :kp, LANES:] + p[:, kp:, :LANES]


def _stage2_twiddle_inv(yr, yi, tw_ref, f2i_ref):
    ct, kp, _ = yr.shape
    st = jnp.concatenate([yr, yi], axis=1).astype(BF16)
    p = jnp.dot(st.reshape(ct * 2 * kp, LANES), f2i_ref[...], preferred_element_type=F32).reshape(ct, 2 * kp, 2 * LANES)
    tr, ti = p[:, :kp, :LANES] - p[:, kp:, LANES:], p[:, :kp, LANES:] + p[:, kp:, :LANES]
    twr, twi = tw_ref[0], tw_ref[1]
    return jnp.concatenate([tr * twr + ti * twi, ti * twr - tr * twi], axis=1).astype(BF16)


def _filter_spec_kernel(lo_ref, hi_ref, f1_ref, tw_ref, f2_ref, kr_ref, ki_ref, a_scr):
    ct = lo_ref.shape[2]
    planes = lambda c: jnp.concatenate([lo_ref[0, 0, c], hi_ref[0, 0, c]], axis=0)
    _per_channel_matmul(f1_ref, planes, a_scr, ct)
    xr, xi = _twiddle_stage2(a_scr[...], tw_ref, f2_ref)
    n_fft = f1_ref.shape[1] * LANES
    kr_ref[0] = xr * (1.0 / n_fft)
    ki_ref[0] = xi * (1.0 / n_fft)


def _filter_spectrum(halves, consts, ct=64):
    _, nf, c, seq = halves.shape
    ct = min(ct, c)
    kp = consts["kp"]
    h5 = halves.reshape(2, nf, c, seq // LANES, LANES)
    half = lambda s: pl.BlockSpec((1, 1, ct, seq // LANES, LANES), lambda f, i: (s, f, i, 0, 0))
    full = lambda a: pl.BlockSpec(a.shape, lambda f, i: (0,) * a.ndim)
    spec = jax.ShapeDtypeStruct((nf, c, kp, LANES), F32)
    out = pl.BlockSpec((1, ct, kp, LANES), lambda f, i: (f, i, 0, 0))
    return pl.pallas_call(
        _filter_spec_kernel,
        grid=(nf, c // ct),
        in_specs=[half(0), half(1), full(consts["f1"]), full(consts["tw"]), full(consts["f2"])],
        out_specs=[out, out],
        out_shape=[spec, spec],
        scratch_shapes=[pltpu.VMEM((ct, 2 * kp, LANES), F32)],
        compiler_params=_cparams(("parallel", "parallel")),
        name="hyena_filter_spectrum",
    )(h5, h5, consts["f1"], consts["tw"], consts["f2"])


def _hyena_conv_kernel(second, x_ref, m_ref, e_ref, skip_ref, kr_ref, ki_ref, f1_ref, tw_ref, f2_ref, f2i_ref, g_ref,
                       o_ref, a_scr, t_scr, y_scr):
    ct = x_ref.shape[1]
    _per_channel_matmul(f1_ref, lambda c: x_ref[0, c], a_scr, ct)
    xr, xi = _twiddle_stage2(a_scr[...], tw_ref, f2_ref)
    kr, ki = kr_ref[0], ki_ref[0]
    t_scr[...] = _stage2_twiddle_inv(xr * kr - xi * ki, xr * ki + xi * kr, tw_ref, f2i_ref)

    _per_channel_matmul(g_ref, lambda c: t_scr[c], y_scr, ct)

    out = m_ref[0].astype(F32) * (y_scr[...] + x_ref[0].astype(F32) * skip_ref[...])
    if second:
        e = e_ref[0].astype(F32)
        out = out * (e * jax.nn.sigmoid(e))
    o_ref[0] = out.astype(o_ref.dtype)


def _hyena_conv(second, x, x_row0, hy, m_row0, e_row0, skip, kr, ki, filt, consts, ct=64):
    b, _, seq = x.shape
    c = kr.shape[1]
    ct = min(ct, c)
    kp = consts["kp"]
    n1h = seq // LANES
    view = lambda a: a.reshape(a.shape[0], a.shape[1], n1h, LANES)
    rows = lambda blk0: pl.BlockSpec((1, ct, n1h, LANES), lambda j, i: (i, blk0 + j, 0, 0))
    full = lambda a: pl.BlockSpec(a.shape, lambda j, i: (0,) * a.ndim)
    kspec = pl.BlockSpec((1, ct, kp, LANES), lambda j, i: (filt, j, 0, 0))
    mats = [consts["f1"][:, :n1h], consts["tw"], consts["f2"], consts["f2i"], consts["g"]]
    out = pl.pallas_call(
        functools.partial(_hyena_conv_kernel, second),
        grid=(c // ct, b),
        in_specs=[rows(x_row0 // ct), rows(m_row0 // ct), rows(e_row0 // ct),
                  pl.BlockSpec((ct, 1, 1), lambda j, i: (j, 0, 0)),
                  kspec, kspec] + [full(a) for a in mats],
        out_specs=pl.BlockSpec((1, ct, n1h, LANES), lambda j, i: (i, j, 0, 0)),
        out_shape=jax.ShapeDtypeStruct((b, c, n1h, LANES), BF16),
        scratch_shapes=[pltpu.VMEM((ct, 2 * kp, LANES), F32), pltpu.VMEM((ct, 2 * kp, LANES), BF16),
                        pltpu.VMEM((ct, n1h, LANES), F32)],
        compiler_params=_cparams(("parallel", "parallel")),
        name="hyena_conv2" if second else "hyena_conv1",
    )(view(x), view(hy), view(hy), skip, kr, ki, *mats)
    return out.reshape(b, c, seq)


def _t5_bucket(rel):
    half_b = REL_BUCKETS // 2
    max_exact = half_b // 2
    ret = jnp.where(rel > 0, half_b, 0)
    n = jnp.abs(rel)
    nf = jnp.maximum(n, 1).astype(F32)
    large = max_exact + (jnp.log(nf / max_exact) / math.log(REL_MAX_DIST / max_exact)
                         * (half_b - max_exact)).astype(jnp.int32)
    large = jnp.minimum(large, half_b - 1)
    return ret + jnp.where(n < max_exact, n, large)


def _dilated_bias(rel_bias):
    rel = jnp.arange(ATT_KW)[None, :] - ATT_HALF - jnp.arange(ATT_QB)[:, None]
    col = jnp.arange(ATT_KW)
    edge = jnp.stack([col >= ATT_HALF, col >= 0, col < ATT_KW - ATT_HALF])
    out = []
    for _, dil in DILATED_PATTERNS:
        onehot = jax.nn.one_hot(_t5_bucket(rel * dil), REL_BUCKETS, dtype=F32)
        bias = jnp.einsum("qkb,bh->qkh", onehot, rel_bias.astype(F32), precision=lax.Precision.HIGHEST)
        bias = jnp.where((jnp.abs(rel) <= ATT_HALF)[:, :, None], bias, NEG_INF)
        bias = jnp.transpose(bias, (2, 0, 1))[:, None]
        out.append(jnp.where(edge[None, :, None, :], bias, NEG_INF))
    return jnp.stack(out)


def _head_select(first, second):
    lane = lax.broadcasted_iota(jnp.int32, first.shape[:-1] + (LANES,), len(first.shape) - 1)
    return jnp.where(lane < HEAD_DIM, first, second)


def _dilated_kernel(q_ref, k_ref, v_ref, g_ref, bias_ref, o_ref, *scratch):
    stage_a, stage_b = scratch[0:3], scratch[3:6]
    qd, kd, vd = scratch[6:9]
    last = len(DILATED_PATTERNS) - 1
    states = [scratch[9 + 3 * p:12 + 3 * p] for p in range(last)]
    seq = q_ref.shape[1]
    stage_a[0][...] = q_ref[0].astype(F32) * (HEAD_DIM ** -0.5)
    stage_a[1][...] = k_ref[0].astype(F32)
    stage_a[2][...] = v_ref[0].astype(F32)
    zero_halo = jnp.zeros((ATT_HALF, LANES), BF16)

    src, dst = stage_a, stage_b
    for pi, (_, dil) in enumerate(DILATED_PATTERNS):
        n = seq // dil
        nblk = n // ATT_QB
        if pi > 0:
            prev_dil = dil // ATT_STEP
            for r1 in range(prev_dil):
                for r2 in range(ATT_STEP):
                    r = r1 + prev_dil * r2
                    for a_src, a_dst in zip(src, dst):
                        a_dst[r * n:(r + 1) * n, :] = a_src[pl.ds(r1 * n * ATT_STEP + r2, n, stride=ATT_STEP), :]
            src, dst = dst, src
        stride_k = n + 2 * ATT_HALF
        q_all = src[0][...]
        lane_s = lax.broadcasted_iota(jnp.int32, (seq, LANES), 1)
        lane_n = lax.broadcasted_iota(jnp.int32, (n, LANES), 1)
        for h in range(2):
            qd[h] = jnp.where((lane_s < HEAD_DIM) == (h == 0), q_all, 0.0).astype(BF16)
        for r in range(dil):
            base = r * stride_k
            k_r = src[1][r * n:(r + 1) * n, :].astype(BF16)
            v_r = src[2][r * n:(r + 1) * n, :]
            operands = [(kd, k_r)] + [(vd.at[h], jnp.where((lane_n < HEAD_DIM) == (h == 0), v_r, 1.0).astype(BF16))
                                      for h in range(2)]
            for a_dst, val in operands:
                a_dst[base:base + ATT_HALF, :] = zero_halo
                a_dst[base + ATT_HALF:base + ATT_HALF + n, :] = val
                a_dst[base + ATT_HALF + n:base + stride_k, :] = zero_halo
        if pi == last:
            states.append(dst)
        acc_w, mx_w, den_w = states[pi]

        def block_group(it, carry, pi=pi, nblk=nblk, acc_w=acc_w, mx_w=mx_w, den_w=den_w):
            chains = []
            for u in range(ATT_UNROLL):
                blk = it * ATT_UNROLL + u
                r = blk // nblk
                qb = blk % nblk
                q0 = pl.multiple_of(blk * ATT_QB, ATT_QB)
                k0 = pl.multiple_of(q0 + r * (2 * ATT_HALF), ATT_QB)
                kblk = kd[pl.ds(k0, ATT_KW), :]
                edge = jnp.where(qb == 0, 0, jnp.where(qb == nblk - 1, 2, 1))
                for h in range(2):
                    s = lax.dot_general(qd[h, pl.ds(q0, ATT_QB), :], kblk, (((1,), (1,)), ((), ())),
                                        preferred_element_type=F32)
                    chains.append((q0, k0, h, edge, s))
            probs = []
            for q0, k0, h, edge, s in chains:
                s = s + bias_ref[pi, h, edge]
                m = jnp.max(s, axis=1, keepdims=True)
                probs.append((jnp.exp(s - m).astype(BF16), m))
            pvs = [jnp.dot(p, vd[h, pl.ds(k0, ATT_KW), :], preferred_element_type=F32)
                   for (q0, k0, h, edge, s), (p, m) in zip(chains, probs)]
            for u in range(ATT_UNROLL):
                q0 = chains[2 * u][0]
                acc_w[pl.ds(q0, ATT_QB), :] = _head_select(pvs[2 * u], pvs[2 * u + 1])
                mx_w[pl.ds(q0, ATT_QB), :] = _head_select(probs[2 * u][1], probs[2 * u + 1][1])
                den_w[pl.ds(q0, ATT_QB), :] = pltpu.roll(_head_select(pvs[2 * u + 1], pvs[2 * u]), HEAD_DIM, axis=1)
            return carry

        lax.fori_loop(0, dil * nblk // ATT_UNROLL, block_group, 0)

    for pi in range(len(DILATED_PATTERNS) - 1, 0, -1):
        dil = DILATED_PATTERNS[pi][1]
        n = seq // dil
        prev_dil = dil // ATT_STEP
        acc_c, mx_c, den_c = states[pi - 1]
        acc_f, mx_f, den_f = states[pi]
        for r1 in range(prev_dil):
            for r2 in range(ATT_STEP):
                r = r1 + prev_dil * r2
                rows = pl.ds(r1 * n * ATT_STEP + r2, n, stride=ATT_STEP)
                fine = slice(r * n, (r + 1) * n)
                m_old = mx_c[rows, :]
                m_new = mx_f[fine, :]
                m_all = jnp.maximum(m_old, m_new)
                a_old = jnp.exp(m_old - m_all)
                a_new = jnp.exp(m_new - m_all)
                acc_c[rows, :] = acc_c[rows, :] * a_old + acc_f[fine, :] * a_new
                den_c[rows, :] = den_c[rows, :] * a_old + den_f[fine, :] * a_new
                mx_c[rows, :] = m_all

    acc, _, den = states[0]
    g = g_ref[0].astype(F32)
    o_ref[0] = (acc[...] / den[...] * (g * jax.nn.sigmoid(g))).astype(o_ref.dtype)


def _dilated_attention(proj, bias, n_heads):
    b, seq, _ = proj.shape
    npair = n_heads // 2
    max_dil = max(d for _, d in DILATED_PATTERNS)
    kd_rows = seq + 2 * ATT_HALF * max_dil
    blk = lambda off: pl.BlockSpec((1, seq, LANES), lambda i, j: (i, 0, off * npair + j))
    state = pltpu.VMEM((seq, LANES), F32)
    return pl.pallas_call(
        _dilated_kernel,
        grid=(b, npair),
        in_specs=[blk(0), blk(1), blk(2), blk(3),
                  pl.BlockSpec((len(DILATED_PATTERNS), 2, 3, ATT_QB, ATT_KW), lambda i, j: (0, j, 0, 0, 0))],
        out_specs=pl.BlockSpec((1, seq, LANES), lambda i, j: (i, 0, j)),
        out_shape=jax.ShapeDtypeStruct((b, seq, n_heads * HEAD_DIM), BF16),
        scratch_shapes=[state] * 6
                       + [pltpu.VMEM((2, seq, LANES), BF16), pltpu.VMEM((kd_rows, LANES), BF16),
                          pltpu.VMEM((2, kd_rows, LANES), BF16)]
                       + [state] * (3 * (len(DILATED_PATTERNS) - 1)),
        compiler_params=_cparams(("parallel", "parallel")),
        name="dilated_attention",
    )(proj, proj, proj, proj, bias)


def _na_bias(rpb, rows):
    kr_n = min(NA_ROWS, rows)
    n_heads = rpb.shape[0]
    cols = np.arange(GRID_W)
    cs = np.clip(cols - NA_COLS // 2, 0, GRID_W - NA_COLS)
    kc = np.arange(GRID_W)
    inside = (kc[None, :] >= cs[:, None]) & (kc[None, :] < cs[:, None] + NA_COLS)
    col_idx = kc[None, :] - cols[:, None] + NA_COLS - 1
    onehot = (col_idx.reshape(-1)[None, :] == np.arange(2 * NA_COLS - 1)[:, None]).astype(np.float32)
    e = jnp.einsum("hrd,dq->hrq", rpb.astype(F32), jnp.asarray(onehot), precision=lax.Precision.HIGHEST)
    e = jnp.where(jnp.asarray(inside)[None, None], e.reshape(n_heads, -1, GRID_W, GRID_W), NEG_INF)
    g = jnp.stack([e[:, s:s + kr_n] for s in range(NA_ROWS)], axis=1)
    return jnp.transpose(g, (0, 1, 3, 2, 4)).reshape(n_heads, NA_ROWS, GRID_W, kr_n * GRID_W)


def _na_kernel(q_ref, k_ref, v_ref, g_ref, bias_ref, o_ref, qm, va):
    seq = q_ref.shape[1]
    rows = seq // GRID_W
    kr_n = min(NA_ROWS, rows)
    lane = lax.broadcasted_iota(jnp.int32, (seq, LANES), 1)
    q = q_ref[0] * (HEAD_DIM ** -0.5)
    v = v_ref[0]
    for h in range(2):
        head = (lane < HEAD_DIM) if h == 0 else (lane >= HEAD_DIM)
        qm[h] = jnp.where(head, q, jnp.zeros_like(q))
        va[h] = jnp.where(head, v, jnp.ones_like(v))

    def row_group(it, carry):
        chains = []
        for u in range(NA_UNROLL):
            r = it * NA_UNROLL + u
            rs = jnp.clip(r - kr_n // 2, 0, rows - kr_n)
            shift = rs - r + NA_ROWS - 1
            q0 = pl.multiple_of(r * GRID_W, GRID_W)
            k0 = pl.multiple_of(rs * GRID_W, GRID_W)
            kblk = k_ref[0, pl.ds(k0, kr_n * GRID_W), :]
            for h in range(2):
                s = lax.dot_general(qm[h, pl.ds(q0, GRID_W), :], kblk, (((1,), (1,)), ((), ())),
                                    preferred_element_type=F32)
                chains.append((q0, k0, h, shift, s))
        probs = []
        for q0, k0, h, shift, s in chains:
            s = s + bias_ref[h, shift]
            probs.append(jnp.exp(s - jnp.max(s, axis=1, keepdims=True)).astype(BF16))
        pvs = [jnp.dot(p, va[h, pl.ds(k0, kr_n * GRID_W), :], preferred_element_type=F32)
               for (q0, k0, h, shift, s), p in zip(chains, probs)]
        for u in range(NA_UNROLL):
            q0 = chains[2 * u][0]
            num = _head_select(pvs[2 * u], pvs[2 * u + 1])
            den = pltpu.roll(_head_select(pvs[2 * u + 1], pvs[2 * u]), HEAD_DIM, axis=1)
            g = g_ref[0, pl.ds(q0, GRID_W), :].astype(F32)
            o_ref[0, pl.ds(q0, GRID_W), :] = (num / den * (g * jax.nn.sigmoid(g))).astype(o_ref.dtype)
        return carry

    lax.fori_loop(0, rows // NA_UNROLL, row_group, 0)


def _neighbourhood_attention(proj, bias, n_heads):
    b, seq, _ = proj.shape
    npair = n_heads // 2
    blk = lambda off: pl.BlockSpec((1, seq, LANES), lambda j, i: (i, 0, off * npair + j))
    return pl.pallas_call(
        _na_kernel,
        grid=(npair, b),
        in_specs=[blk(0), blk(1), blk(2), blk(3),
                  pl.BlockSpec((2,) + bias.shape[1:], lambda j, i: (j, 0, 0, 0))],
        out_specs=pl.BlockSpec((1, seq, LANES), lambda j, i: (i, 0, j)),
        out_shape=jax.ShapeDtypeStruct((b, seq, n_heads * HEAD_DIM), BF16),
        scratch_shapes=[pltpu.VMEM((2, seq, LANES), BF16), pltpu.VMEM((2, seq, LANES), BF16)],
        compiler_params=_cparams(("parallel", "parallel")),
        name="neighbourhood_attention",
    )(proj, proj, proj, proj, bias)


def _out_kernel(n_y, n_t, *refs):
    n = n_y + n_t
    y_refs = refs[:n]
    w_refs = refs[n:2 * n]
    h_ref, p_ref, wp_ref, wg_ref, gpost_ref, gple_ref, gnext_ref, h_out, u_out = refs[2 * n:]
    t = None
    for idx, (y_ref, w_ref) in enumerate(zip(y_refs, w_refs)):
        if idx < n_y:
            part = jnp.dot(y_ref[...], w_ref[...], preferred_element_type=F32)
        else:
            part = lax.dot_general(y_ref[0], w_ref[...], (((0,), (0,)), ((), ())), preferred_element_type=F32)
        t = part if t is None else t + part
    h = h_ref[...] + _rms(t, gpost_ref[...])
    e = _rms(jnp.dot(p_ref[...].astype(BF16), wp_ref[...], preferred_element_type=F32), gple_ref[...])
    gate = jax.nn.sigmoid(jnp.dot(h.astype(BF16), wg_ref[...], preferred_element_type=F32))
    h = h + gate * e
    h_out[...] = h
    u_out[...] = _rms(h, gnext_ref[...]).astype(u_out.dtype)


def _out_block(ys, ys_t, ws, h, p, w_ple, w_gate, g_post, g_ple, g_next, tm=512):
    m, d = h.shape
    row = lambda a: pl.BlockSpec((tm, a.shape[1]), lambda i: (i, 0))
    full = lambda a: pl.BlockSpec(a.shape, lambda i: (0, 0))
    vec = lambda g: g.reshape(1, d)

    def col(a):
        tiles = a.shape[2] // tm
        return pl.BlockSpec((1, a.shape[1], tm), lambda i: (i // tiles, 0, i % tiles))

    return pl.pallas_call(
        functools.partial(_out_kernel, len(ys), len(ys_t)),
        grid=(m // tm,),
        in_specs=[row(y) for y in ys] + [col(y) for y in ys_t] + [full(w) for w in ws]
                 + [row(h), row(p), full(w_ple), full(w_gate), full(vec(g_post)), full(vec(g_ple)), full(vec(g_next))],
        out_specs=[pl.BlockSpec((tm, d), lambda i: (i, 0)), pl.BlockSpec((tm, d), lambda i: (i, 0))],
        out_shape=[jax.ShapeDtypeStruct((m, d), F32), jax.ShapeDtypeStruct((m, d), BF16)],
        compiler_params=_cparams(("parallel",)),
        name="out_proj_residual_ple",
    )(*ys, *ys_t, *ws, h, p, w_ple, w_gate, vec(g_post), vec(g_ple), vec(g_next))


def kernel(x, p, w_in, w_out, norm_pre, norm_post, hyena_conv_w, hyena_conv_b, hyena_w1, hyena_b1, hyena_w2, hyena_b2, hyena_w3, hyena_b3, hyena_w4, hyena_freq, hyena_skip, rel_bias, na_rpb, ple_proj, ple_norm, ple_gate):
    b, seq, d = x.shape
    depth = w_in.shape[0]
    d_inner = w_out.shape[1]
    hw = d_inner // 2
    m = b * seq
    consts = _fft_consts(seq)
    dil_bias = _dilated_bias(rel_bias)
    h = x.reshape(m, d)
    u = _rmsnorm_bf16(h, norm_pre[0])
    for i in range(depth):
        j = i // 2
        w = w_in[i].astype(BF16)
        if i % 2 == 0:
            w_a = w[:, :4 * hw].T
            ident = jnp.tile(jnp.array([0.0, 1.0, 0.0], F32)[:, None], (1, hw))
            conv_w = jnp.concatenate([hyena_conv_w[j], ident], axis=1).T
            conv_b = jnp.concatenate([hyena_conv_b[j], jnp.zeros((hw,), F32)])[:, None]
            hy = _proj_t(w_a, u.reshape(b, seq, d), conv_w, conv_b)
            halves = _hyena_filters_t(seq, hw, hyena_w1[j], hyena_b1[j], hyena_w2[j], hyena_b2[j],
                                      hyena_w3[j], hyena_b3[j], hyena_w4[j], hyena_freq[j])
            kr, ki = _filter_spectrum(halves, consts)
            skip = hyena_skip[j][:, :, None, None]
            z = _hyena_conv(False, hy, 0, hy, hw, hw, skip[0], kr, ki, 0, consts)
            ya_t = _hyena_conv(True, z, 0, hy, 2 * hw, 3 * hw, skip[1], kr, ki, 1, consts)
            proj_b = _matmul_bf16(u, w[:, 4 * hw:]).reshape(b, seq, 4 * hw)
            yb = _dilated_attention(proj_b, dil_bias, hw // HEAD_DIM)
            ys, ys_t = [yb.reshape(m, hw)], [ya_t]
            ws = [w_out[i, hw:].astype(BF16), w_out[i, :hw].astype(BF16)]
        else:
            proj = _matmul_bf16(u, w).reshape(b, seq, 4 * d_inner)
            yc = _neighbourhood_attention(proj, _na_bias(na_rpb[j], seq // GRID_W), d_inner // HEAD_DIM)
            ys, ys_t = [yc.reshape(m, d_inner)], []
            ws = [w_out[i].astype(BF16)]
        g_next = norm_pre[i + 1] if i + 1 < depth else norm_pre[i]
        h, u = _out_block(ys, ys_t, ws, h, p[i].reshape(m, -1), ple_proj[i].astype(BF16), ple_gate[i].astype(BF16),
                          norm_post[i], ple_norm[i], g_next)
    return h.reshape(b, seq, d)
```

```python
import functools
import math

import numpy as np
import jax
import jax.numpy as jnp
from jax import lax
from jax.experimental import pallas as pl
from jax.experimental.pallas import tpu as pltpu

F32 = jnp.float32
BF16 = jnp.bfloat16

HEAD_DIM = 64
LANES = 128
HYENA_EMB_DIM = 33
HYENA_FAST_DECAY = 0.3
HYENA_SLOW_DECAY = 1.5
HYENA_TARGET = 1e-2
DILATED_PATTERNS = ((128, 1), (512, 4), (2048, 16))
GRID_W = 64
NA_ROWS = 8
NA_COLS = 16
REL_BUCKETS = 32
REL_MAX_DIST = 1024
RMS_EPS = 1e-6
NEG_INF = -1e30
ATT_QB = 128
ATT_HALF = 64
ATT_KW = ATT_QB + 2 * ATT_HALF
ATT_UNROLL = 4
ATT_STEP = 4
assert all(b[1] == a[1] * ATT_STEP for a, b in zip(DILATED_PATTERNS, DILATED_PATTERNS[1:])) and DILATED_PATTERNS[0][1] == 1
NA_UNROLL = 8
VMEM_LIMIT = 56 * 1024 * 1024


def _cparams(sem):
    return pltpu.CompilerParams(dimension_semantics=sem, vmem_limit_bytes=VMEM_LIMIT)


def _rms(x, g):
    return x * lax.rsqrt(jnp.mean(x * x, axis=-1, keepdims=True) + RMS_EPS) * g


def _rmsnorm_kernel(x_ref, g_ref, o_ref):
    o_ref[...] = _rms(x_ref[...], g_ref[...]).astype(o_ref.dtype)


def _rmsnorm_bf16(x2d, g, tm=1024):
    m, d = x2d.shape
    return pl.pallas_call(
        _rmsnorm_kernel,
        grid=(m // tm,),
        in_specs=[pl.BlockSpec((tm, d), lambda i: (i, 0)), pl.BlockSpec((1, d), lambda i: (0, 0))],
        out_specs=pl.BlockSpec((tm, d), lambda i: (i, 0)),
        out_shape=jax.ShapeDtypeStruct((m, d), BF16),
        compiler_params=_cparams(("parallel",)),
        name="rmsnorm",
    )(x2d, g.reshape(1, d))


def _matmul_kernel(a_ref, b_ref, o_ref):
    o_ref[...] = jnp.dot(a_ref[...], b_ref[...], preferred_element_type=F32).astype(o_ref.dtype)


def _matmul_bf16(a, b, tm=1024, tn=1024):
    m, k = a.shape
    _, n = b.shape
    return pl.pallas_call(
        _matmul_kernel,
        grid=(m // tm, n // tn),
        in_specs=[pl.BlockSpec((tm, k), lambda i, j: (i, 0)), pl.BlockSpec((k, tn), lambda i, j: (0, j))],
        out_specs=pl.BlockSpec((tm, tn), lambda i, j: (i, j)),
        out_shape=jax.ShapeDtypeStruct((m, n), BF16),
        compiler_params=_cparams(("parallel", "parallel")),
        name="in_proj",
    )(a, b)


def _proj_t_kernel(w_ref, u_ref, cw_ref, cb_ref, o_ref):
    p = lax.dot_general(w_ref[...], u_ref[0], (((0,), (1,)), ((), ())), preferred_element_type=F32)
    seq = p.shape[1]
    t = lax.broadcasted_iota(jnp.int32, p.shape, 1)
    prev = jnp.where(t == 0, 0.0, pltpu.roll(p, 1, axis=1))
    nxt = jnp.where(t == seq - 1, 0.0, pltpu.roll(p, seq - 1, axis=1))
    cw = cw_ref[...]
    o_ref[0] = (prev * cw[:, 0:1] + p * cw[:, 1:2] + nxt * cw[:, 2:3] + cb_ref[...]).astype(o_ref.dtype)


def _proj_t(w, u, conv_w, conv_b, tc=256):
    d, c = w.shape
    b, seq, _ = u.shape
    return pl.pallas_call(
        _proj_t_kernel,
        grid=(b, c // tc),
        in_specs=[
            pl.BlockSpec((d, tc), lambda i, j: (0, j)),
            pl.BlockSpec((1, seq, d), lambda i, j: (i, 0, 0)),
            pl.BlockSpec((tc, 3), lambda i, j: (j, 0)),
            pl.BlockSpec((tc, 1), lambda i, j: (j, 0)),
        ],
        out_specs=pl.BlockSpec((1, tc, seq), lambda i, j: (i, j, 0)),
        out_shape=jax.ShapeDtypeStruct((b, c, seq), BF16),
        compiler_params=_cparams(("parallel", "parallel")),
        name="in_proj_t_conv3",
    )(w, u, conv_w, conv_b)


def _hyena_filter_kernel(z_ref, w1_ref, b1_ref, w2_ref, b2_ref, w3_ref, b3_ref, fq_ref, w4_ref, dl_ref, t_ref,
                         o_ref, a_scr):
    hp = lax.Precision.HIGHEST

    @pl.when((pl.program_id(0) == 0) & (pl.program_id(1) == 0))
    def _():
        for d in range(2):
            a = z_ref[d]
            for w_ref, b_ref in ((w1_ref, b1_ref), (w2_ref, b2_ref), (w3_ref, b3_ref)):
                a = jnp.sin(fq_ref[...] * (jnp.dot(w_ref[...], a, precision=hp, preferred_element_type=F32) + b_ref[...]))
            a_scr[d] = a

    seq = o_ref.shape[-1]
    lo = jnp.dot(w4_ref[0, 0], a_scr[0], precision=hp, preferred_element_type=F32) * jnp.exp(-t_ref[0] * dl_ref[...])
    hi = jnp.dot(w4_ref[0, 1], a_scr[1], precision=hp, preferred_element_type=F32) * jnp.exp(-t_ref[1] * dl_ref[...])
    hi = jnp.where(lax.broadcasted_iota(jnp.int32, (1, seq), 1) == 0, 0.0, hi)
    norm = jnp.sum(jnp.abs(lo), axis=1, keepdims=True) + jnp.sum(jnp.abs(hi), axis=1, keepdims=True)
    o_ref[0, 0] = (lo / norm).astype(o_ref.dtype)
    o_ref[1, 0] = (hi / norm).astype(o_ref.dtype)


def _hyena_filters_t(seq, c, w1, b1, w2, b2, w3, b3, w4, freq, tc=256):
    tc = min(tc, c)
    fo = w2.shape[0]
    emb = -(-HYENA_EMB_DIM // 8) * 8
    t = jnp.linspace(0.0, 1.0, seq, dtype=F32)
    bands = (HYENA_EMB_DIM - 1) // 2
    fr = jnp.linspace(1e-4, bands - 1, bands, dtype=F32)[:, None]
    mirror = (seq - jnp.arange(seq)) % seq
    tables, times = [], []
    for pos, tp in ((jnp.arange(seq), t), (mirror, jnp.roll(t[::-1], 1))):
        wpos = (2.0 * math.pi / seq) * pos.astype(F32)[None, :]
        tp = tp[None, :]
        z = jnp.concatenate([tp, jnp.cos(fr * wpos), -jnp.sin(fr * wpos),
                             jnp.zeros((emb - HYENA_EMB_DIM, seq), F32)], axis=0)
        tables.append(z)
        times.append(tp)
    z = jnp.stack(tables)
    tt = jnp.stack(times)
    w1_t = jnp.pad(w1.astype(F32).T, ((0, 0), (0, emb - HYENA_EMB_DIM)))
    w4_t = jnp.transpose(w4.astype(F32).reshape(fo, 2, 2, c), (1, 2, 3, 0))
    min_decay = math.log(HYENA_TARGET) / HYENA_SLOW_DECAY
    max_decay = math.log(HYENA_TARGET) / HYENA_FAST_DECAY
    deltas = jnp.abs(jnp.linspace(min_decay, max_decay, c, dtype=F32))[:, None]
    col = lambda v: v.astype(F32)[:, None]
    full = lambda a: pl.BlockSpec(a.shape, lambda f, i: (0,) * a.ndim)
    args = (z, w1_t, col(b1), w2.astype(F32).T, col(b2), w3.astype(F32).T, col(b3), col(freq))
    return pl.pallas_call(
        _hyena_filter_kernel,
        grid=(2, c // tc),
        in_specs=[full(a) for a in args]
                 + [pl.BlockSpec((1, 2, tc, fo), lambda f, i: (f, 0, i, 0)),
                    pl.BlockSpec((tc, 1), lambda f, i: (i, 0)), full(tt)],
        out_specs=pl.BlockSpec((2, 1, tc, seq), lambda f, i: (0, f, i, 0)),
        out_shape=jax.ShapeDtypeStruct((2, 2, c, seq), BF16),
        scratch_shapes=[pltpu.VMEM((2, fo, seq), F32)],
        compiler_params=_cparams(("arbitrary", "arbitrary")),
        name="hyena_filter_mlp",
    )(*args, w4_t, deltas, tt)


def _fft_consts(seq):
    n_fft = 2 * seq
    n1 = n_fft // LANES
    nb = n1 // 2 + 1
    kp = -(-nb // 8) * 8
    k1 = jnp.arange(kp, dtype=jnp.int32)
    keep = (k1 < nb)[:, None]
    two_pi = 2.0 * math.pi

    th = ((k1[:, None] * jnp.arange(n1, dtype=jnp.int32)[None, :]) % n1).astype(F32) * (two_pi / n1)
    f1 = jnp.concatenate([jnp.where(keep, jnp.cos(th), 0.0), jnp.where(keep, -jnp.sin(th), 0.0)], axis=0)
    ph = (k1[:, None] * jnp.arange(LANES, dtype=jnp.int32)[None, :]).astype(F32) * (two_pi / n_fft)
    tw = jnp.stack([jnp.cos(ph), -jnp.sin(ph)])
    om = ((jnp.arange(LANES, dtype=jnp.int32)[:, None] * jnp.arange(LANES, dtype=jnp.int32)[None, :]) % LANES
          ).astype(F32) * (two_pi / LANES)
    f2 = jnp.concatenate([jnp.cos(om), -jnp.sin(om)], axis=1)
    f2i = jnp.concatenate([jnp.cos(om), jnp.sin(om)], axis=1)
    weight = jnp.where((k1 == 0) | (k1 == nb - 1), 1.0, 2.0)[:, None]
    g = jnp.concatenate([jnp.where(keep, weight * jnp.cos(th[:, :n1 // 2]), 0.0),
                         jnp.where(keep, -weight * jnp.sin(th[:, :n1 // 2]), 0.0)], axis=0).T
    return dict(f1=f1.astype(BF16), tw=tw, f2=f2.astype(BF16), f2i=f2i.astype(BF16), g=g.astype(BF16), kp=kp)


def _per_channel_matmul(mat_ref, planes, out_scr, ct):
    def body(i, carry):
        c = 2 * i
        pair = jnp.concatenate([planes(c), planes(c + 1)], axis=1)
        res = jnp.dot(mat_ref[...], pair, preferred_element_type=F32)
        out_scr[c] = res[:, :LANES]
        out_scr[c + 1] = res[:, LANES:]
        return carry
    lax.fori_loop(0, ct // 2, body, 0, unroll=16)


def _twiddle_stage2(a, tw_ref, f2_ref):
    ct, kp2, _ = a.shape
    kp = kp2 // 2
    ar, ai = a[:, :kp], a[:, kp:]
    twr, twi = tw_ref[0], tw_ref[1]
    st = jnp.concatenate([ar * twr - ai * twi, ar * twi + ai * twr], axis=1).astype(BF16)
    p = jnp.dot(st.reshape(ct * kp2, LANES), f2_ref[...], preferred_element_type=F32).reshape(ct, kp2, 2 * LANES)
    return p[:, :kp, :LANES] - p[:, kp:, LANES:], p[:, :kp, LANES:] + p[:, kp:, :LANES]


def _stage2_twiddle_inv(yr, yi, tw_ref, f2i_ref):
    ct, kp, _ = yr.shape
    st = jnp.concatenate([yr, yi], axis=1).astype(BF16)
    p = jnp.dot(st.reshape(ct * 2 * kp, LANES), f2i_ref[...], preferred_element_type=F32).reshape(ct, 2 * kp, 2 * LANES)
    tr, ti = p[:, :kp, :LANES] - p[:, kp:, LANES:], p[:, :kp, LANES:] + p[:, kp:, :LANES]
    twr, twi = tw_ref[0], tw_ref[1]
    return jnp.concatenate([tr * twr + ti * twi, ti * twr - tr * twi], axis=1).astype(BF16)


def _filter_spec_kernel(lo_ref, hi_ref, f1_ref, tw_ref, f2_ref, kr_ref, ki_ref, a_scr):
    ct = lo_ref.shape[2]
    planes = lambda c: jnp.concatenate([lo_ref[0, 0, c], hi_ref[0, 0, c]], axis=0)
    _per_channel_matmul(f1_ref, planes, a_scr, ct)
    xr, xi = _twiddle_stage2(a_scr[...], tw_ref, f2_ref)
    n_fft = f1_ref.shape[1] * LANES
    kr_ref[0] = xr * (1.0 / n_fft)
    ki_ref[0] = xi * (1.0 / n_fft)


def _filter_spectrum(halves, consts, ct=64):
    _, nf, c, seq = halves.shape
    ct = min(ct, c)
    kp = consts["kp"]
    h5 = halves.reshape(2, nf, c, seq // LANES, LANES)
    half = lambda s: pl.BlockSpec((1, 1, ct, seq // LANES, LANES), lambda f, i: (s, f, i, 0, 0))
    full = lambda a: pl.BlockSpec(a.shape, lambda f, i: (0,) * a.ndim)
    spec = jax.ShapeDtypeStruct((nf, c, kp, LANES), F32)
    out = pl.BlockSpec((1, ct, kp, LANES), lambda f, i: (f, i, 0, 0))
    return pl.pallas_call(
        _filter_spec_kernel,
        grid=(nf, c // ct),
        in_specs=[half(0), half(1), full(consts["f1"]), full(consts["tw"]), full(consts["f2"])],
        out_specs=[out, out],
        out_shape=[spec, spec],
        scratch_shapes=[pltpu.VMEM((ct, 2 * kp, LANES), F32)],
        compiler_params=_cparams(("parallel", "parallel")),
        name="hyena_filter_spectrum",
    )(h5, h5, consts["f1"], consts["tw"], consts["f2"])


def _hyena_conv_kernel(second, x_ref, m_ref, e_ref, skip_ref, kr_ref, ki_ref, f1_ref, tw_ref, f2_ref, f2i_ref, g_ref,
                       o_ref, a_scr, t_scr, y_scr):
    ct = x_ref.shape[1]
    _per_channel_matmul(f1_ref, lambda c: x_ref[0, c], a_scr, ct)
    xr, xi = _twiddle_stage2(a_scr[...], tw_ref, f2_ref)
    kr, ki = kr_ref[0], ki_ref[0]
    t_scr[...] = _stage2_twiddle_inv(xr * kr - xi * ki, xr * ki + xi * kr, tw_ref, f2i_ref)

    _per_channel_matmul(g_ref, lambda c: t_scr[c], y_scr, ct)

    out = m_ref[0].astype(F32) * (y_scr[...] + x_ref[0].astype(F32) * skip_ref[...])
    if second:
        e = e_ref[0].astype(F32)
        out = out * (e * jax.nn.sigmoid(e))
    o_ref[0] = out.astype(o_ref.dtype)


def _hyena_conv(second, x, x_row0, hy, m_row0, e_row0, skip, kr, ki, filt, consts, ct=64):
    b, _, seq = x.shape
    c = kr.shape[1]
    ct = min(ct, c)
    kp = consts["kp"]
    n1h = seq // LANES
    view = lambda a: a.reshape(a.shape[0], a.shape[1], n1h, LANES)
    rows = lambda blk0: pl.BlockSpec((1, ct, n1h, LANES), lambda j, i: (i, blk0 + j, 0, 0))
    full = lambda a: pl.BlockSpec(a.shape, lambda j, i: (0,) * a.ndim)
    kspec = pl.BlockSpec((1, ct, kp, LANES), lambda j, i: (filt, j, 0, 0))
    mats = [consts["f1"][:, :n1h], consts["tw"], consts["f2"], consts["f2i"], consts["g"]]
    out = pl.pallas_call(
        functools.partial(_hyena_conv_kernel, second),
        grid=(c // ct, b),
        in_specs=[rows(x_row0 // ct), rows(m_row0 // ct), rows(e_row0 // ct),
                  pl.BlockSpec((ct, 1, 1), lambda j, i: (j, 0, 0)),
                  kspec, kspec] + [full(a) for a in mats],
        out_specs=pl.BlockSpec((1, ct, n1h, LANES), lambda j, i: (i, j, 0, 0)),
        out_shape=jax.ShapeDtypeStruct((b, c, n1h, LANES), BF16),
        scratch_shapes=[pltpu.VMEM((ct, 2 * kp, LANES), F32), pltpu.VMEM((ct, 2 * kp, LANES), BF16),
                        pltpu.VMEM((ct, n1h, LANES), F32)],
        compiler_params=_cparams(("parallel", "parallel")),
        name="hyena_conv2" if second else "hyena_conv1",
    )(view(x), view(hy), view(hy), skip, kr, ki, *mats)
    return out.reshape(b, c, seq)


def _t5_bucket(rel):
    half_b = REL_BUCKETS // 2
    max_exact = half_b // 2
    ret = jnp.where(rel > 0, half_b, 0)
    n = jnp.abs(rel)
    nf = jnp.maximum(n, 1).astype(F32)
    large = max_exact + (jnp.log(nf / max_exact) / math.log(REL_MAX_DIST / max_exact)
                         * (half_b - max_exact)).astype(jnp.int32)
    large = jnp.minimum(large, half_b - 1)
    return ret + jnp.where(n < max_exact, n, large)


def _dilated_bias(rel_bias):
    rel = jnp.arange(ATT_KW)[None, :] - ATT_HALF - jnp.arange(ATT_QB)[:, None]
    col = jnp.arange(ATT_KW)
    edge = jnp.stack([col >= ATT_HALF, col >= 0, col < ATT_KW - ATT_HALF])
    out = []
    for _, dil in DILATED_PATTERNS:
        onehot = jax.nn.one_hot(_t5_bucket(rel * dil), REL_BUCKETS, dtype=F32)
        bias = jnp.einsum("qkb,bh->qkh", onehot, rel_bias.astype(F32), precision=lax.Precision.HIGHEST)
        bias = jnp.where((jnp.abs(rel) <= ATT_HALF)[:, :, None], bias, NEG_INF)
        bias = jnp.transpose(bias, (2, 0, 1))[:, None]
        out.append(jnp.where(edge[None, :, None, :], bias, NEG_INF))
    return jnp.stack(out)


def _head_select(first, second):
    lane = lax.broadcasted_iota(jnp.int32, first.shape[:-1] + (LANES,), len(first.shape) - 1)
    return jnp.where(lane < HEAD_DIM, first, second)


def _dilated_kernel(q_ref, k_ref, v_ref, g_ref, bias_ref, o_ref, *scratch):
    stage_a, stage_b = scratch[0:3], scratch[3:6]
    qd, kd, vd = scratch[6:9]
    last = len(DILATED_PATTERNS) - 1
    states = [scratch[9 + 3 * p:12 + 3 * p] for p in range(last)]
    seq = q_ref.shape[1]
    stage_a[0][...] = q_ref[0].astype(F32) * (HEAD_DIM ** -0.5)
    stage_a[1][...] = k_ref[0].astype(F32)
    stage_a[2][...] = v_ref[0].astype(F32)
    zero_halo = jnp.zeros((ATT_HALF, LANES), BF16)

    src, dst = stage_a, stage_b
    for pi, (_, dil) in enumerate(DILATED_PATTERNS):
        n = seq // dil
        nblk = n // ATT_QB
        if pi > 0:
            prev_dil = dil // ATT_STEP
            for r1 in range(prev_dil):
                for r2 in range(ATT_STEP):
                    r = r1 + prev_dil * r2
                    for a_src, a_dst in zip(src, dst):
                        a_dst[r * n:(r + 1) * n, :] = a_src[pl.ds(r1 * n * ATT_STEP + r2, n, stride=ATT_STEP), :]
            src, dst = dst, src
        stride_k = n + 2 * ATT_HALF
        q_all = src[0][...]
        lane_s = lax.broadcasted_iota(jnp.int32, (seq, LANES), 1)
        for h in range(2):
            qd[h] = jnp.where((lane_s < HEAD_DIM) == (h == 0), q_all, 0.0).astype(BF16)
        for r in range(dil):
            base = r * stride_k
            for a_src, a_dst in ((src[1], kd), (src[2], vd)):
                a_dst[base:base + ATT_HALF, :] = zero_halo
                a_dst[base + ATT_HALF:base + ATT_HALF + n, :] = a_src[r * n:(r + 1) * n, :].astype(BF16)
                a_dst[base + ATT_HALF + n:base + stride_k, :] = zero_halo
        if pi == last:
            states.append(dst)
        acc_w, mx_w, den_w = states[pi]

        def block_group(it, carry, pi=pi, nblk=nblk, acc_w=acc_w, mx_w=mx_w, den_w=den_w):
            chains = []
            for u in range(ATT_UNROLL):
                blk = it * ATT_UNROLL + u
                r = blk // nblk
                qb = blk % nblk
                q0 = pl.multiple_of(blk * ATT_QB, ATT_QB)
                k0 = pl.multiple_of(q0 + r * (2 * ATT_HALF), ATT_QB)
                edge = jnp.where(qb == 0, 0, jnp.where(qb == nblk - 1, 2, 1))
                qcat = jnp.concatenate([qd[0, pl.ds(q0, ATT_QB), :], qd[1, pl.ds(q0, ATT_QB), :]], axis=0)
                s = lax.dot_general(qcat, kd[pl.ds(k0, ATT_KW), :], (((1,), (1,)), ((), ())),
                                    preferred_element_type=F32)
                chains.append((q0, k0, edge, s))
            probs = []
            for q0, k0, edge, s in chains:
                s = s + jnp.concatenate([bias_ref[pi, 0, edge], bias_ref[pi, 1, edge]], axis=0)
                m = jnp.max(s, axis=1, keepdims=True)
                p = jnp.exp(s - m)
                probs.append((p.astype(BF16), m, jnp.sum(p, axis=1, keepdims=True)))
            for (q0, k0, edge, s), (p, m, l) in zip(chains, probs):
                pv = jnp.dot(p, vd[pl.ds(k0, ATT_KW), :], preferred_element_type=F32)
                acc_w[pl.ds(q0, ATT_QB), :] = _head_select(pv[:ATT_QB], pv[ATT_QB:])
                mx_w[pl.ds(q0, ATT_QB), :] = _head_select(m[:ATT_QB], m[ATT_QB:])
                den_w[pl.ds(q0, ATT_QB), :] = _head_select(l[:ATT_QB], l[ATT_QB:])
            return carry

        lax.fori_loop(0, dil * nblk // ATT_UNROLL, block_group, 0)

    for pi in range(len(DILATED_PATTERNS) - 1, 0, -1):
        dil = DILATED_PATTERNS[pi][1]
        n = seq // dil
        prev_dil = dil // ATT_STEP
        acc_c, mx_c, den_c = states[pi - 1]
        acc_f, mx_f, den_f = states[pi]
        for r1 in range(prev_dil):
            for r2 in range(ATT_STEP):
                r = r1 + prev_dil * r2
                rows = pl.ds(r1 * n * ATT_STEP + r2, n, stride=ATT_STEP)
                fine = slice(r * n, (r + 1) * n)
                m_old = mx_c[rows, :]
                m_new = mx_f[fine, :]
                m_all = jnp.maximum(m_old, m_new)
                a_old = jnp.exp(m_old - m_all)
                a_new = jnp.exp(m_new - m_all)
                acc_c[rows, :] = acc_c[rows, :] * a_old + acc_f[fine, :] * a_new
                den_c[rows, :] = den_c[rows, :] * a_old + den_f[fine, :] * a_new
                mx_c[rows, :] = m_all

    acc, _, den = states[0]
    g = g_ref[0].astype(F32)
    o_ref[0] = (acc[...] / den[...] * (g * jax.nn.sigmoid(g))).astype(o_ref.dtype)


def _dilated_attention(proj, bias, n_heads):
    b, seq, _ = proj.shape
    npair = n_heads // 2
    max_dil = max(d for _, d in DILATED_PATTERNS)
    kd_rows = seq + 2 * ATT_HALF * max_dil
    blk = lambda off: pl.BlockSpec((1, seq, LANES), lambda i, j: (i, 0, off * npair + j))
    state = pltpu.VMEM((seq, LANES), F32)
    return pl.pallas_call(
        _dilated_kernel,
        grid=(b, npair),
        in_specs=[blk(0), blk(1), blk(2), blk(3),
                  pl.BlockSpec((len(DILATED_PATTERNS), 2, 3, ATT_QB, ATT_KW), lambda i, j: (0, j, 0, 0, 0))],
        out_specs=pl.BlockSpec((1, seq, LANES), lambda i, j: (i, 0, j)),
        out_shape=jax.ShapeDtypeStruct((b, seq, n_heads * HEAD_DIM), BF16),
        scratch_shapes=[state] * 6
                       + [pltpu.VMEM((2, seq, LANES), BF16), pltpu.VMEM((kd_rows, LANES), BF16),
                          pltpu.VMEM((kd_rows, LANES), BF16)]
                       + [state] * (3 * (len(DILATED_PATTERNS) - 1)),
        compiler_params=_cparams(("parallel", "parallel")),
        name="dilated_attention",
    )(proj, proj, proj, proj, bias)


def _na_bias(rpb, rows):
    kr_n = min(NA_ROWS, rows)
    n_heads = rpb.shape[0]
    cols = np.arange(GRID_W)
    cs = np.clip(cols - NA_COLS // 2, 0, GRID_W - NA_COLS)
    kc = np.arange(GRID_W)
    inside = (kc[None, :] >= cs[:, None]) & (kc[None, :] < cs[:, None] + NA_COLS)
    col_idx = kc[None, :] - cols[:, None] + NA_COLS - 1
    onehot = (col_idx.reshape(-1)[None, :] == np.arange(2 * NA_COLS - 1)[:, None]).astype(np.float32)
    e = jnp.einsum("hrd,dq->hrq", rpb.astype(F32), jnp.asarray(onehot), precision=lax.Precision.HIGHEST)
    e = jnp.where(jnp.asarray(inside)[None, None], e.reshape(n_heads, -1, GRID_W, GRID_W), NEG_INF)
    g = jnp.stack([e[:, s:s + kr_n] for s in range(NA_ROWS)], axis=1)
    return jnp.transpose(g, (0, 1, 3, 2, 4)).reshape(n_heads, NA_ROWS, GRID_W, kr_n * GRID_W)


def _na_kernel(q_ref, k_ref, v_ref, g_ref, bias_ref, o_ref, qm):
    seq = q_ref.shape[1]
    rows = seq // GRID_W
    kr_n = min(NA_ROWS, rows)
    lane = lax.broadcasted_iota(jnp.int32, (seq, LANES), 1)
    q = q_ref[0] * (HEAD_DIM ** -0.5)
    for h in range(2):
        qm[h] = jnp.where((lane < HEAD_DIM) == (h == 0), q, jnp.zeros_like(q))

    def row_group(it, carry):
        chains = []
        for u in range(NA_UNROLL):
            r = it * NA_UNROLL + u
            rs = jnp.clip(r - kr_n // 2, 0, rows - kr_n)
            shift = rs - r + NA_ROWS - 1
            q0 = pl.multiple_of(r * GRID_W, GRID_W)
            k0 = pl.multiple_of(rs * GRID_W, GRID_W)
            qcat = jnp.concatenate([qm[0, pl.ds(q0, GRID_W), :], qm[1, pl.ds(q0, GRID_W), :]], axis=0)
            s = lax.dot_general(qcat, k_ref[0, pl.ds(k0, kr_n * GRID_W), :], (((1,), (1,)), ((), ())),
                                preferred_element_type=F32)
            chains.append((q0, k0, shift, s))
        probs = []
        for q0, k0, shift, s in chains:
            s = s + jnp.concatenate([bias_ref[0, shift], bias_ref[1, shift]], axis=0)
            p = jnp.exp(s - jnp.max(s, axis=1, keepdims=True))
            probs.append((p.astype(BF16), jnp.sum(p, axis=1, keepdims=True)))
        for (q0, k0, shift, s), (p, l) in zip(chains, probs):
            o = jnp.dot(p, v_ref[0, pl.ds(k0, kr_n * GRID_W), :], preferred_element_type=F32) / l
            g = g_ref[0, pl.ds(q0, GRID_W), :].astype(F32)
            o = _head_select(o[:GRID_W], o[GRID_W:]) * (g * jax.nn.sigmoid(g))
            o_ref[0, pl.ds(q0, GRID_W), :] = o.astype(o_ref.dtype)
        return carry

    lax.fori_loop(0, rows // NA_UNROLL, row_group, 0)


def _neighbourhood_attention(proj, bias, n_heads):
    b, seq, _ = proj.shape
    npair = n_heads // 2
    blk = lambda off: pl.BlockSpec((1, seq, LANES), lambda j, i: (i, 0, off * npair + j))
    return pl.pallas_call(
        _na_kernel,
        grid=(npair, b),
        in_specs=[blk(0), blk(1), blk(2), blk(3),
                  pl.BlockSpec((2,) + bias.shape[1:], lambda j, i: (j, 0, 0, 0))],
        out_specs=pl.BlockSpec((1, seq, LANES), lambda j, i: (i, 0, j)),
        out_shape=jax.ShapeDtypeStruct((b, seq, n_heads * HEAD_DIM), BF16),
        scratch_shapes=[pltpu.VMEM((2, seq, LANES), BF16)],
        compiler_params=_cparams(("parallel", "parallel")),
        name="neighbourhood_attention",
    )(proj, proj, proj, proj, bias)


def _out_kernel(n_y, n_t, *refs):
    n = n_y + n_t
    y_refs = refs[:n]
    w_refs = refs[n:2 * n]
    h_ref, p_ref, wp_ref, wg_ref, gpost_ref, gple_ref, gnext_ref, h_out, u_out = refs[2 * n:]
    t = None
    for idx, (y_ref, w_ref) in enumerate(zip(y_refs, w_refs)):
        if idx < n_y:
            part = jnp.dot(y_ref[...], w_ref[...], preferred_element_type=F32)
        else:
            part = lax.dot_general(y_ref[0], w_ref[...], (((0,), (0,)), ((), ())), preferred_element_type=F32)
        t = part if t is None else t + part
    h = h_ref[...] + _rms(t, gpost_ref[...])
    e = _rms(jnp.dot(p_ref[...].astype(BF16), wp_ref[...], preferred_element_type=F32), gple_ref[...])
    gate = jax.nn.sigmoid(jnp.dot(h.astype(BF16), wg_ref[...], preferred_element_type=F32))
    h = h + gate * e
    h_out[...] = h
    u_out[...] = _rms(h, gnext_ref[...]).astype(u_out.dtype)


def _out_block(ys, ys_t, ws, h, p, w_ple, w_gate, g_post, g_ple, g_next, tm=512):
    m, d = h.shape
    row = lambda a: pl.BlockSpec((tm, a.shape[1]), lambda i: (i, 0))
    full = lambda a: pl.BlockSpec(a.shape, lambda i: (0, 0))
    vec = lambda g: g.reshape(1, d)

    def col(a):
        tiles = a.shape[2] // tm
        return pl.BlockSpec((1, a.shape[1], tm), lambda i: (i // tiles, 0, i % tiles))

    return pl.pallas_call(
        functools.partial(_out_kernel, len(ys), len(ys_t)),
        grid=(m // tm,),
        in_specs=[row(y) for y in ys] + [col(y) for y in ys_t] + [full(w) for w in ws]
                 + [row(h), row(p), full(w_ple), full(w_gate), full(vec(g_post)), full(vec(g_ple)), full(vec(g_next))],
        out_specs=[pl.BlockSpec((tm, d), lambda i: (i, 0)), pl.BlockSpec((tm, d), lambda i: (i, 0))],
        out_shape=[jax.ShapeDtypeStruct((m, d), F32), jax.ShapeDtypeStruct((m, d), BF16)],
        compiler_params=_cparams(("parallel",)),
        name="out_proj_residual_ple",
    )(*ys, *ys_t, *ws, h, p, w_ple, w_gate, vec(g_post), vec(g_ple), vec(g_next))


def kernel(x, p, w_in, w_out, norm_pre, norm_post, hyena_conv_w, hyena_conv_b, hyena_w1, hyena_b1, hyena_w2, hyena_b2, hyena_w3, hyena_b3, hyena_w4, hyena_freq, hyena_skip, rel_bias, na_rpb, ple_proj, ple_norm, ple_gate):
    b, seq, d = x.shape
    depth = w_in.shape[0]
    d_inner = w_out.shape[1]
    hw = d_inner // 2
    m = b * seq
    consts = _fft_consts(seq)
    dil_bias = _dilated_bias(rel_bias)
    h = x.reshape(m, d)
    u = _rmsnorm_bf16(h, norm_pre[0])
    for i in range(depth):
        j = i // 2
        w = w_in[i].astype(BF16)
        if i % 2 == 0:
            w_a = w[:, :4 * hw]
            ident = jnp.tile(jnp.array([0.0, 1.0, 0.0], F32)[:, None], (1, hw))
            conv_w = jnp.concatenate([hyena_conv_w[j], ident], axis=1).T
            conv_b = jnp.concatenate([hyena_conv_b[j], jnp.zeros((hw,), F32)])[:, None]
            hy = _proj_t(w_a, u.reshape(b, seq, d), conv_w, conv_b)
            halves = _hyena_filters_t(seq, hw, hyena_w1[j], hyena_b1[j], hyena_w2[j], hyena_b2[j],
                                      hyena_w3[j], hyena_b3[j], hyena_w4[j], hyena_freq[j])
            kr, ki = _filter_spectrum(halves, consts)
            skip = hyena_skip[j][:, :, None, None]
            z = _hyena_conv(False, hy, 0, hy, hw, hw, skip[0], kr, ki, 0, consts)
            ya_t = _hyena_conv(True, z, 0, hy, 2 * hw, 3 * hw, skip[1], kr, ki, 1, consts)
            proj_b = _matmul_bf16(u, w[:, 4 * hw:]).reshape(b, seq, 4 * hw)
            yb = _dilated_attention(proj_b, dil_bias, hw // HEAD_DIM)
            ys, ys_t = [yb.reshape(m, hw)], [ya_t]
            ws = [w_out[i, hw:].astype(BF16), w_out[i, :hw].astype(BF16)]
        else:
            proj = _matmul_bf16(u, w).reshape(b, seq, 4 * d_inner)
            yc = _neighbourhood_attention(proj, _na_bias(na_rpb[j], seq // GRID_W), d_inner // HEAD_DIM)
            ys, ys_t = [yc.reshape(m, d_inner)], []
            ws = [w_out[i].astype(BF16)]
        g_next = norm_pre[i + 1] if i + 1 < depth else norm_pre[i]
        h, u = _out_block(ys, ys_t, ws, h, p[i].reshape(m, -1), ple_proj[i].astype(BF16), ple_gate[i].astype(BF16),
                          norm_post[i], ple_norm[i], g_next)
    return h.reshape(b, seq, d)
```

```python
import functools
import math

import numpy as np
import jax
import jax.numpy as jnp
from jax import lax
from jax.experimental import pallas as pl
from jax.experimental.pallas import tpu as pltpu

F32 = jnp.float32
BF16 = jnp.bfloat16

HEAD_DIM = 64
LANES = 128
HYENA_EMB_DIM = 33
HYENA_FAST_DECAY = 0.3
HYENA_SLOW_DECAY = 1.5
HYENA_TARGET = 1e-2
DILATED_PATTERNS = ((128, 1), (512, 4), (2048, 16))
GRID_W = 64
NA_ROWS = 8
NA_COLS = 16
REL_BUCKETS = 32
REL_MAX_DIST = 1024
RMS_EPS = 1e-6
NEG_INF = -1e30
ATT_QB = 128
ATT_HALF = 64
ATT_KW = ATT_QB + 2 * ATT_HALF
ATT_UNROLL = 4
ATT_STEP = 4
assert all(b[1] == a[1] * ATT_STEP for a, b in zip(DILATED_PATTERNS, DILATED_PATTERNS[1:])) and DILATED_PATTERNS[0][1] == 1
NA_UNROLL = 8
VMEM_LIMIT = 56 * 1024 * 1024


def _cparams(sem):
    return pltpu.CompilerParams(dimension_semantics=sem, vmem_limit_bytes=VMEM_LIMIT)


def _rms(x, g):
    return x * lax.rsqrt(jnp.mean(x * x, axis=-1, keepdims=True) + RMS_EPS) * g


def _rmsnorm_kernel(x_ref, g_ref, o_ref):
    o_ref[...] = _rms(x_ref[...], g_ref[...]).astype(o_ref.dtype)


def _rmsnorm_bf16(x2d, g, tm=1024):
    m, d = x2d.shape
    return pl.pallas_call(
        _rmsnorm_kernel,
        grid=(m // tm,),
        in_specs=[pl.BlockSpec((tm, d), lambda i: (i, 0)), pl.BlockSpec((1, d), lambda i: (0, 0))],
        out_specs=pl.BlockSpec((tm, d), lambda i: (i, 0)),
        out_shape=jax.ShapeDtypeStruct((m, d), BF16),
        compiler_params=_cparams(("parallel",)),
        name="rmsnorm",
    )(x2d, g.reshape(1, d))


def _matmul_kernel(a_ref, b_ref, o_ref):
    o_ref[...] = jnp.dot(a_ref[...], b_ref[...], preferred_element_type=F32).astype(o_ref.dtype)


def _matmul_bf16(a, w, layer, col0, n, tm=1024, tn=1024):
    m, k = a.shape
    return pl.pallas_call(
        _matmul_kernel,
        grid=(m // tm, n // tn),
        in_specs=[pl.BlockSpec((tm, k), lambda i, j: (i, 0)), pl.BlockSpec((None, k, tn), lambda i, j: (layer, 0, col0 // tn + j))],
        out_specs=pl.BlockSpec((tm, tn), lambda i, j: (i, j)),
        out_shape=jax.ShapeDtypeStruct((m, n), BF16),
        compiler_params=_cparams(("parallel", "parallel")),
        name="in_proj",
    )(a, w)


def _proj_t_kernel(w_ref, u_ref, cw_ref, cb_ref, o_ref):
    p = lax.dot_general(w_ref[...], u_ref[0], (((1,), (1,)), ((), ())), preferred_element_type=F32)
    seq = p.shape[1]
    t = lax.broadcasted_iota(jnp.int32, p.shape, 1)
    prev = jnp.where(t == 0, 0.0, pltpu.roll(p, 1, axis=1))
    nxt = jnp.where(t == seq - 1, 0.0, pltpu.roll(p, seq - 1, axis=1))
    cw = cw_ref[...]
    o_ref[0] = (prev * cw[:, 0:1] + p * cw[:, 1:2] + nxt * cw[:, 2:3] + cb_ref[...]).astype(o_ref.dtype)


def _proj_t(w_t, u, conv_w, conv_b, tc=256):
    c, d = w_t.shape
    b, seq, _ = u.shape
    return pl.pallas_call(
        _proj_t_kernel,
        grid=(b, c // tc),
        in_specs=[
            pl.BlockSpec((tc, d), lambda i, j: (j, 0)),
            pl.BlockSpec((1, seq, d), lambda i, j: (i, 0, 0)),
            pl.BlockSpec((tc, 3), lambda i, j: (j, 0)),
            pl.BlockSpec((tc, 1), lambda i, j: (j, 0)),
        ],
        out_specs=pl.BlockSpec((1, tc, seq), lambda i, j: (i, j, 0)),
        out_shape=jax.ShapeDtypeStruct((b, c, seq), BF16),
        compiler_params=_cparams(("parallel", "parallel")),
        name="in_proj_t_conv3",
    )(w_t, u, conv_w, conv_b)


def _hyena_filter_kernel(z_ref, w1_ref, b1_ref, w2_ref, b2_ref, w3_ref, b3_ref, fq_ref, w4_ref, dl_ref, t_ref,
                         o_ref, a_scr):
    hp = lax.Precision.HIGHEST

    @pl.when((pl.program_id(0) == 0) & (pl.program_id(1) == 0))
    def _():
        for d in range(2):
            a = z_ref[d]
            for w_ref, b_ref in ((w1_ref, b1_ref), (w2_ref, b2_ref), (w3_ref, b3_ref)):
                a = jnp.sin(fq_ref[...] * (jnp.dot(w_ref[...], a, precision=hp, preferred_element_type=F32) + b_ref[...]))
            a_scr[d] = a

    seq = o_ref.shape[-1]
    lo = jnp.dot(w4_ref[0, 0], a_scr[0], precision=hp, preferred_element_type=F32) * jnp.exp(-t_ref[0] * dl_ref[...])
    hi = jnp.dot(w4_ref[0, 1], a_scr[1], precision=hp, preferred_element_type=F32) * jnp.exp(-t_ref[1] * dl_ref[...])
    hi = jnp.where(lax.broadcasted_iota(jnp.int32, (1, seq), 1) == 0, 0.0, hi)
    norm = jnp.sum(jnp.abs(lo), axis=1, keepdims=True) + jnp.sum(jnp.abs(hi), axis=1, keepdims=True)
    o_ref[0, 0] = (lo / norm).astype(o_ref.dtype)
    o_ref[1, 0] = (hi / norm).astype(o_ref.dtype)


def _hyena_filters_t(seq, c, w1, b1, w2, b2, w3, b3, w4, freq, tc=256):
    tc = min(tc, c)
    fo = w2.shape[0]
    emb = -(-HYENA_EMB_DIM // 8) * 8
    t = jnp.linspace(0.0, 1.0, seq, dtype=F32)
    bands = (HYENA_EMB_DIM - 1) // 2
    fr = jnp.linspace(1e-4, bands - 1, bands, dtype=F32)[:, None]
    mirror = (seq - jnp.arange(seq)) % seq
    tables, times = [], []
    for pos, tp in ((jnp.arange(seq), t), (mirror, jnp.roll(t[::-1], 1))):
        wpos = (2.0 * math.pi / seq) * pos.astype(F32)[None, :]
        tp = tp[None, :]
        z = jnp.concatenate([tp, jnp.cos(fr * wpos), -jnp.sin(fr * wpos),
                             jnp.zeros((emb - HYENA_EMB_DIM, seq), F32)], axis=0)
        tables.append(z)
        times.append(tp)
    z = jnp.stack(tables)
    tt = jnp.stack(times)
    w1_t = jnp.pad(w1.astype(F32).T, ((0, 0), (0, emb - HYENA_EMB_DIM)))
    w4_t = jnp.transpose(w4.astype(F32).reshape(fo, 2, 2, c), (1, 2, 3, 0))
    min_decay = math.log(HYENA_TARGET) / HYENA_SLOW_DECAY
    max_decay = math.log(HYENA_TARGET) / HYENA_FAST_DECAY
    deltas = jnp.abs(jnp.linspace(min_decay, max_decay, c, dtype=F32))[:, None]
    col = lambda v: v.astype(F32)[:, None]
    full = lambda a: pl.BlockSpec(a.shape, lambda f, i: (0,) * a.ndim)
    args = (z, w1_t, col(b1), w2.astype(F32).T, col(b2), w3.astype(F32).T, col(b3), col(freq))
    return pl.pallas_call(
        _hyena_filter_kernel,
        grid=(2, c // tc),
        in_specs=[full(a) for a in args]
                 + [pl.BlockSpec((1, 2, tc, fo), lambda f, i: (f, 0, i, 0)),
                    pl.BlockSpec((tc, 1), lambda f, i: (i, 0)), full(tt)],
        out_specs=pl.BlockSpec((2, 1, tc, seq), lambda f, i: (0, f, i, 0)),
        out_shape=jax.ShapeDtypeStruct((2, 2, c, seq), BF16),
        scratch_shapes=[pltpu.VMEM((2, fo, seq), F32)],
        compiler_params=_cparams(("arbitrary", "arbitrary")),
        name="hyena_filter_mlp",
    )(*args, w4_t, deltas, tt)


def _fft_consts(seq):
    n_fft = 2 * seq
    n1 = n_fft // LANES
    nb = n1 // 2 + 1
    kp = -(-nb // 8) * 8
    k1 = jnp.arange(kp, dtype=jnp.int32)
    keep = (k1 < nb)[:, None]
    two_pi = 2.0 * math.pi

    th = ((k1[:, None] * jnp.arange(n1, dtype=jnp.int32)[None, :]) % n1).astype(F32) * (two_pi / n1)
    f1 = jnp.concatenate([jnp.where(keep, jnp.cos(th), 0.0), jnp.where(keep, -jnp.sin(th), 0.0)], axis=0)
    ph = (k1[:, None] * jnp.arange(LANES, dtype=jnp.int32)[None, :]).astype(F32) * (two_pi / n_fft)
    tw = jnp.stack([jnp.cos(ph), -jnp.sin(ph)])
    om = ((jnp.arange(LANES, dtype=jnp.int32)[:, None] * jnp.arange(LANES, dtype=jnp.int32)[None, :]) % LANES
          ).astype(F32) * (two_pi / LANES)
    f2 = jnp.concatenate([jnp.cos(om), -jnp.sin(om)], axis=1)
    f2i = jnp.concatenate([jnp.cos(om), jnp.sin(om)], axis=1)
    weight = jnp.where((k1 == 0) | (k1 == nb - 1), 1.0, 2.0)[:, None]
    g = jnp.concatenate([jnp.where(keep, weight * jnp.cos(th[:, :n1 // 2]), 0.0),
                         jnp.where(keep, -weight * jnp.sin(th[:, :n1 // 2]), 0.0)], axis=0).T
    return dict(f1=f1.astype(BF16), tw=tw, f2=f2.astype(BF16), f2i=f2i.astype(BF16), g=g.astype(BF16), kp=kp)


def _per_channel_matmul(mat_ref, planes, out_scr, ct):
    def body(i, carry):
        c = 2 * i
        pair = jnp.concatenate([planes(c), planes(c + 1)], axis=1)
        res = jnp.dot(mat_ref[...], pair, preferred_element_type=F32)
        out_scr[c] = res[:, :LANES]
        out_scr[c + 1] = res[:, LANES:]
        return carry
    lax.fori_loop(0, ct // 2, body, 0, unroll=16)


def _twiddle_stage2(a, tw_ref, f2_ref):
    ct, kp2, _ = a.shape
    kp = kp2 // 2
    ar, ai = a[:, :kp], a[:, kp:]
    twr, twi = tw_ref[0], tw_ref[1]
    st = jnp.concatenate([ar * twr - ai * twi, ar * twi + ai * twr], axis=1).astype(BF16)
    p = jnp.dot(st.reshape(ct * kp2, LANES), f2_ref[...], preferred_element_type=F32).reshape(ct, kp2, 2 * LANES)
    return p[:, :kp, :LANES] - p[:, kp:, LANES:], p[:, :kp, LANES:] + p[:, kp:, :LANES]


def _stage2_twiddle_inv(yr, yi, tw_ref, f2i_ref):
    ct, kp, _ = yr.shape
    st = jnp.concatenate([yr, yi], axis=1).astype(BF16)
    p = jnp.dot(st.reshape(ct * 2 * kp, LANES), f2i_ref[...], preferred_element_type=F32).reshape(ct, 2 * kp, 2 * LANES)
    tr, ti = p[:, :kp, :LANES] - p[:, kp:, LANES:], p[:, :kp, LANES:] + p[:, kp:, :LANES]
    twr, twi = tw_ref[0], tw_ref[1]
    return jnp.concatenate([tr * twr + ti * twi, ti * twr - tr * twi], axis=1).astype(BF16)


def _filter_spec_kernel(lo_ref, hi_ref, f1_ref, tw_ref, f2_ref, kr_ref, ki_ref, a_scr):
    ct = lo_ref.shape[2]
    planes = lambda c: jnp.concatenate([lo_ref[0, 0, c], hi_ref[0, 0, c]], axis=0)
    _per_channel_matmul(f1_ref, planes, a_scr, ct)
    xr, xi = _twiddle_stage2(a_scr[...], tw_ref, f2_ref)
    n_fft = f1_ref.shape[1] * LANES
    kr_ref[0] = xr * (1.0 / n_fft)
    ki_ref[0] = xi * (1.0 / n_fft)


def _filter_spectrum(halves, consts, ct=64):
    _, nf, c, seq = halves.shape
    ct = min(ct, c)
    kp = consts["kp"]
    h5 = halves.reshape(2, nf, c, seq // LANES, LANES)
    half = lambda s: pl.BlockSpec((1, 1, ct, seq // LANES, LANES), lambda f, i: (s, f, i, 0, 0))
    full = lambda a: pl.BlockSpec(a.shape, lambda f, i: (0,) * a.ndim)
    spec = jax.ShapeDtypeStruct((nf, c, kp, LANES), F32)
    out = pl.BlockSpec((1, ct, kp, LANES), lambda f, i: (f, i, 0, 0))
    return pl.pallas_call(
        _filter_spec_kernel,
        grid=(nf, c // ct),
        in_specs=[half(0), half(1), full(consts["f1"]), full(consts["tw"]), full(consts["f2"])],
        out_specs=[out, out],
        out_shape=[spec, spec],
        scratch_shapes=[pltpu.VMEM((ct, 2 * kp, LANES), F32)],
        compiler_params=_cparams(("parallel", "parallel")),
        name="hyena_filter_spectrum",
    )(h5, h5, consts["f1"], consts["tw"], consts["f2"])


def _hyena_conv_kernel(second, x_ref, m_ref, e_ref, skip_ref, kr_ref, ki_ref, f1_ref, tw_ref, f2_ref, f2i_ref, g_ref,
                       o_ref, a_scr, t_scr, y_scr):
    ct = x_ref.shape[1]
    _per_channel_matmul(f1_ref, lambda c: x_ref[0, c], a_scr, ct)
    xr, xi = _twiddle_stage2(a_scr[...], tw_ref, f2_ref)
    kr, ki = kr_ref[0], ki_ref[0]
    t_scr[...] = _stage2_twiddle_inv(xr * kr - xi * ki, xr * ki + xi * kr, tw_ref, f2i_ref)

    _per_channel_matmul(g_ref, lambda c: t_scr[c], y_scr, ct)

    out = m_ref[0].astype(F32) * (y_scr[...] + x_ref[0].astype(F32) * skip_ref[...])
    if second:
        e = e_ref[0].astype(F32)
        out = out * (e * jax.nn.sigmoid(e))
    o_ref[0] = out.astype(o_ref.dtype)


def _hyena_conv(second, x, x_row0, hy, m_row0, e_row0, skip, kr, ki, filt, consts, ct=64):
    b, _, seq = x.shape
    c = kr.shape[1]
    ct = min(ct, c)
    kp = consts["kp"]
    n1h = seq // LANES
    view = lambda a: a.reshape(a.shape[0], a.shape[1], n1h, LANES)
    rows = lambda blk0: pl.BlockSpec((1, ct, n1h, LANES), lambda j, i: (i, blk0 + j, 0, 0))
    full = lambda a: pl.BlockSpec(a.shape, lambda j, i: (0,) * a.ndim)
    kspec = pl.BlockSpec((1, ct, kp, LANES), lambda j, i: (filt, j, 0, 0))
    mats = [consts["f1"][:, :n1h], consts["tw"], consts["f2"], consts["f2i"], consts["g"]]
    out = pl.pallas_call(
        functools.partial(_hyena_conv_kernel, second),
        grid=(c // ct, b),
        in_specs=[rows(x_row0 // ct), rows(m_row0 // ct), rows(e_row0 // ct),
                  pl.BlockSpec((ct, 1, 1), lambda j, i: (j, 0, 0)),
                  kspec, kspec] + [full(a) for a in mats],
        out_specs=pl.BlockSpec((1, ct, n1h, LANES), lambda j, i: (i, j, 0, 0)),
        out_shape=jax.ShapeDtypeStruct((b, c, n1h, LANES), BF16),
        scratch_shapes=[pltpu.VMEM((ct, 2 * kp, LANES), F32), pltpu.VMEM((ct, 2 * kp, LANES), BF16),
                        pltpu.VMEM((ct, n1h, LANES), F32)],
        compiler_params=_cparams(("parallel", "parallel")),
        name="hyena_conv2" if second else "hyena_conv1",
    )(view(x), view(hy), view(hy), skip, kr, ki, *mats)
    return out.reshape(b, c, seq)


def _t5_bucket(rel):
    half_b = REL_BUCKETS // 2
    max_exact = half_b // 2
    ret = jnp.where(rel > 0, half_b, 0)
    n = jnp.abs(rel)
    nf = jnp.maximum(n, 1).astype(F32)
    large = max_exact + (jnp.log(nf / max_exact) / math.log(REL_MAX_DIST / max_exact)
                         * (half_b - max_exact)).astype(jnp.int32)
    large = jnp.minimum(large, half_b - 1)
    return ret + jnp.where(n < max_exact, n, large)


def _dilated_bias(rel_bias):
    rel = jnp.arange(ATT_KW)[None, :] - ATT_HALF - jnp.arange(ATT_QB)[:, None]
    col = jnp.arange(ATT_KW)
    edge = jnp.stack([col >= ATT_HALF, col >= 0, col < ATT_KW - ATT_HALF])
    out = []
    for _, dil in DILATED_PATTERNS:
        onehot = jax.nn.one_hot(_t5_bucket(rel * dil), REL_BUCKETS, dtype=F32)
        bias = jnp.einsum("qkb,bh->qkh", onehot, rel_bias.astype(F32), precision=lax.Precision.HIGHEST)
        bias = jnp.where((jnp.abs(rel) <= ATT_HALF)[:, :, None], bias, NEG_INF)
        bias = jnp.transpose(bias, (2, 0, 1))[:, None]
        out.append(jnp.where(edge[None, :, None, :], bias, NEG_INF))
    return jnp.stack(out)


def _head_select(first, second):
    lane = lax.broadcasted_iota(jnp.int32, first.shape[:-1] + (LANES,), len(first.shape) - 1)
    return jnp.where(lane < HEAD_DIM, first, second)


def _dilated_kernel(q_ref, k_ref, v_ref, g_ref, bias_ref, o_ref, *scratch):
    stage_a, stage_b = scratch[0:3], scratch[3:6]
    qd, kd, vd = scratch[6:9]
    last = len(DILATED_PATTERNS) - 1
    states = [scratch[9 + 3 * p:12 + 3 * p] for p in range(last)]
    seq = q_ref.shape[1]
    stage_a[0][...] = q_ref[0].astype(F32) * (HEAD_DIM ** -0.5)
    stage_a[1][...] = k_ref[0].astype(F32)
    stage_a[2][...] = v_ref[0].astype(F32)
    zero_halo = jnp.zeros((ATT_HALF, LANES), BF16)

    src, dst = stage_a, stage_b
    for pi, (_, dil) in enumerate(DILATED_PATTERNS):
        n = seq // dil
        nblk = n // ATT_QB
        if pi > 0:
            prev_dil = dil // ATT_STEP
            for r1 in range(prev_dil):
                for r2 in range(ATT_STEP):
                    r = r1 + prev_dil * r2
                    for a_src, a_dst in zip(src, dst):
                        a_dst[r * n:(r + 1) * n, :] = a_src[pl.ds(r1 * n * ATT_STEP + r2, n, stride=ATT_STEP), :]
            src, dst = dst, src
        stride_k = n + 2 * ATT_HALF
        q_all = src[0][...]
        lane_s = lax.broadcasted_iota(jnp.int32, (seq, LANES), 1)
        for h in range(2):
            qd[h] = jnp.where((lane_s < HEAD_DIM) == (h == 0), q_all, 0.0).astype(BF16)
        for r in range(dil):
            base = r * stride_k
            for a_src, a_dst in ((src[1], kd), (src[2], vd)):
                a_dst[base:base + ATT_HALF, :] = zero_halo
                a_dst[base + ATT_HALF:base + ATT_HALF + n, :] = a_src[r * n:(r + 1) * n, :].astype(BF16)
                a_dst[base + ATT_HALF + n:base + stride_k, :] = zero_halo
        if pi == last:
            states.append(dst)
        acc_w, mx_w, den_w = states[pi]

        def block_group(it, carry, pi=pi, nblk=nblk, acc_w=acc_w, mx_w=mx_w, den_w=den_w):
            chains = []
            for u in range(ATT_UNROLL):
                blk = it * ATT_UNROLL + u
                r = blk // nblk
                qb = blk % nblk
                q0 = pl.multiple_of(blk * ATT_QB, ATT_QB)
                k0 = pl.multiple_of(q0 + r * (2 * ATT_HALF), ATT_QB)
                edge = jnp.where(qb == 0, 0, jnp.where(qb == nblk - 1, 2, 1))
                qcat = jnp.concatenate([qd[0, pl.ds(q0, ATT_QB), :], qd[1, pl.ds(q0, ATT_QB), :]], axis=0)
                s = lax.dot_general(qcat, kd[pl.ds(k0, ATT_KW), :], (((1,), (1,)), ((), ())),
                                    preferred_element_type=F32)
                chains.append((q0, k0, edge, s))
            probs = []
            for q0, k0, edge, s in chains:
                s = s + jnp.concatenate([bias_ref[pi, 0, edge], bias_ref[pi, 1, edge]], axis=0)
                m = jnp.max(s, axis=1, keepdims=True)
                p = jnp.exp(s - m)
                probs.append((p.astype(BF16), m, jnp.sum(p, axis=1, keepdims=True)))
            for (q0, k0, edge, s), (p, m, l) in zip(chains, probs):
                pv = jnp.dot(p, vd[pl.ds(k0, ATT_KW), :], preferred_element_type=F32)
                acc_w[pl.ds(q0, ATT_QB), :] = _head_select(pv[:ATT_QB], pv[ATT_QB:])
                mx_w[pl.ds(q0, ATT_QB), :] = _head_select(m[:ATT_QB], m[ATT_QB:])
                den_w[pl.ds(q0, ATT_QB), :] = _head_select(l[:ATT_QB], l[ATT_QB:])
            return carry

        lax.fori_loop(0, dil * nblk // ATT_UNROLL, block_group, 0)

    for pi in range(len(DILATED_PATTERNS) - 1, 0, -1):
        dil = DILATED_PATTERNS[pi][1]
        n = seq // dil
        prev_dil = dil // ATT_STEP
        acc_c, mx_c, den_c = states[pi - 1]
        acc_f, mx_f, den_f = states[pi]
        for r1 in range(prev_dil):
            for r2 in range(ATT_STEP):
                r = r1 + prev_dil * r2
                rows = pl.ds(r1 * n * ATT_STEP + r2, n, stride=ATT_STEP)
                fine = slice(r * n, (r + 1) * n)
                m_old = mx_c[rows, :]
                m_new = mx_f[fine, :]
                m_all = jnp.maximum(m_old, m_new)
                decay = jnp.exp(-jnp.abs(m_old - m_new))
                a_old = jnp.where(m_old >= m_new, 1.0, decay)
                a_new = jnp.where(m_old >= m_new, decay, 1.0)
                acc_c[rows, :] = acc_c[rows, :] * a_old + acc_f[fine, :] * a_new
                den_c[rows, :] = den_c[rows, :] * a_old + den_f[fine, :] * a_new
                mx_c[rows, :] = m_all

    acc, _, den = states[0]
    g = g_ref[0].astype(F32)
    o_ref[0] = (acc[...] / den[...] * (g * jax.nn.sigmoid(g))).astype(o_ref.dtype)


def _dilated_attention(proj, bias, n_heads):
    b, seq, _ = proj.shape
    npair = n_heads // 2
    max_dil = max(d for _, d in DILATED_PATTERNS)
    kd_rows = seq + 2 * ATT_HALF * max_dil
    blk = lambda off: pl.BlockSpec((1, seq, LANES), lambda i, j: (i, 0, off * npair + j))
    state = pltpu.VMEM((seq, LANES), F32)
    return pl.pallas_call(
        _dilated_kernel,
        grid=(b, npair),
        in_specs=[blk(0), blk(1), blk(2), blk(3),
                  pl.BlockSpec((len(DILATED_PATTERNS), 2, 3, ATT_QB, ATT_KW), lambda i, j: (0, j, 0, 0, 0))],
        out_specs=pl.BlockSpec((1, seq, LANES), lambda i, j: (i, 0, j)),
        out_shape=jax.ShapeDtypeStruct((b, seq, n_heads * HEAD_DIM), BF16),
        scratch_shapes=[state] * 6
                       + [pltpu.VMEM((2, seq, LANES), BF16), pltpu.VMEM((kd_rows, LANES), BF16),
                          pltpu.VMEM((kd_rows, LANES), BF16)]
                       + [state] * (3 * (len(DILATED_PATTERNS) - 1)),
        compiler_params=_cparams(("parallel", "parallel")),
        name="dilated_attention",
    )(proj, proj, proj, proj, bias)


def _na_bias(rpb, rows):
    assert min(NA_ROWS, rows) % 2 == 0
    cols = np.arange(GRID_W)
    cs = np.clip(cols - NA_COLS // 2, 0, GRID_W - NA_COLS)
    kc = np.arange(GRID_W)
    inside = (kc[None, :] >= cs[:, None]) & (kc[None, :] < cs[:, None] + NA_COLS)
    col_idx = kc[None, :] - cols[:, None] + NA_COLS - 1
    onehot = (col_idx[:, :, None] == np.arange(2 * NA_COLS - 1)[None, None, :]).astype(np.float32)
    e = jnp.einsum("hrd,cqd->hrcq", rpb.astype(F32), jnp.asarray(onehot), precision=lax.Precision.HIGHEST)
    e = jnp.where(jnp.asarray(inside)[None, None], e, NEG_INF)
    return jnp.concatenate([e[:, :-1], e[:, 1:]], axis=-1)


def _na_kernel(q_ref, k_ref, v_ref, g_ref, bias_ref, o_ref, qm):
    seq = q_ref.shape[1]
    rows = seq // GRID_W
    kr_n = min(NA_ROWS, rows)
    lane = lax.broadcasted_iota(jnp.int32, (seq, LANES), 1)
    q = q_ref[0] * (HEAD_DIM ** -0.5)
    for h in range(2):
        qm[h] = jnp.where((lane < HEAD_DIM) == (h == 0), q, jnp.zeros_like(q))

    def row_group(it, carry):
        chains = []
        for u in range(NA_UNROLL):
            r = it * NA_UNROLL + u
            rs = jnp.clip(r - kr_n // 2, 0, rows - kr_n)
            shift = rs - r + NA_ROWS - 1
            q0 = pl.multiple_of(r * GRID_W, GRID_W)
            k0 = pl.multiple_of(rs * GRID_W, GRID_W)
            qcat = jnp.concatenate([qm[0, pl.ds(q0, GRID_W), :], qm[1, pl.ds(q0, GRID_W), :]], axis=0)
            s = lax.dot_general(qcat, k_ref[0, pl.ds(k0, kr_n * GRID_W), :], (((1,), (1,)), ((), ())),
                                preferred_element_type=F32)
            chains.append((q0, k0, shift, s))
        probs = []
        for q0, k0, shift, s in chains:
            bias = [jnp.concatenate([bias_ref[h, shift + i] for i in range(0, kr_n, 2)], axis=1) for h in range(2)]
            s = s + jnp.concatenate(bias, axis=0)
            p = jnp.exp(s - jnp.max(s, axis=1, keepdims=True))
            probs.append((p.astype(BF16), jnp.sum(p, axis=1, keepdims=True)))
        for (q0, k0, shift, s), (p, l) in zip(chains, probs):
            o = jnp.dot(p, v_ref[0, pl.ds(k0, kr_n * GRID_W), :], preferred_element_type=F32) / l
            g = g_ref[0, pl.ds(q0, GRID_W), :].astype(F32)
            o = _head_select(o[:GRID_W], o[GRID_W:]) * (g * jax.nn.sigmoid(g))
            o_ref[0, pl.ds(q0, GRID_W), :] = o.astype(o_ref.dtype)
        return carry

    lax.fori_loop(0, rows // NA_UNROLL, row_group, 0)


def _neighbourhood_attention(proj, bias, n_heads):
    b, seq, _ = proj.shape
    npair = n_heads // 2
    blk = lambda off: pl.BlockSpec((1, seq, LANES), lambda j, i: (i, 0, off * npair + j))
    return pl.pallas_call(
        _na_kernel,
        grid=(npair, b),
        in_specs=[blk(0), blk(1), blk(2), blk(3),
                  pl.BlockSpec((2,) + bias.shape[1:], lambda j, i: (j, 0, 0, 0))],
        out_specs=pl.BlockSpec((1, seq, LANES), lambda j, i: (i, 0, j)),
        out_shape=jax.ShapeDtypeStruct((b, seq, n_heads * HEAD_DIM), BF16),
        scratch_shapes=[pltpu.VMEM((2, seq, LANES), BF16)],
        compiler_params=_cparams(("parallel", "parallel")),
        name="neighbourhood_attention",
    )(proj, proj, proj, proj, bias)


def _out_kernel(n_y, n_t, *refs):
    n = n_y + n_t
    y_refs = refs[:n]
    w_refs = refs[n:2 * n]
    h_ref, p_ref, wp_ref, wg_ref, gpost_ref, gple_ref, gnext_ref, h_out, u_out = refs[2 * n:]
    t = None
    for idx, (y_ref, w_ref) in enumerate(zip(y_refs, w_refs)):
        if idx < n_y:
            part = jnp.dot(y_ref[...], w_ref[...], preferred_element_type=F32)
        else:
            part = lax.dot_general(y_ref[0], w_ref[...], (((0,), (0,)), ((), ())), preferred_element_type=F32)
        t = part if t is None else t + part
    h = h_ref[...] + _rms(t, gpost_ref[...])
    e = _rms(jnp.dot(p_ref[...].astype(BF16), wp_ref[...], preferred_element_type=F32), gple_ref[...])
    gate = jax.nn.sigmoid(jnp.dot(h.astype(BF16), wg_ref[...], preferred_element_type=F32))
    h = h + gate * e
    h_out[...] = h
    u_out[...] = _rms(h, gnext_ref[...]).astype(u_out.dtype)


def _out_block(ys, ys_t, ws, h, p, layer, w_ple, w_gate, g_post, g_ple, g_next, tm=512):
    m, d = h.shape
    row = lambda a: pl.BlockSpec((tm, a.shape[1]), lambda i: (i, 0))
    p_spec = pl.BlockSpec((None, tm, p.shape[2]), lambda i: (layer, i, 0))
    full = lambda a: pl.BlockSpec(a.shape, lambda i: (0, 0))
    vec = lambda g: g.reshape(1, d)

    def col(a):
        tiles = a.shape[2] // tm
        return pl.BlockSpec((1, a.shape[1], tm), lambda i: (i // tiles, 0, i % tiles))

    return pl.pallas_call(
        functools.partial(_out_kernel, len(ys), len(ys_t)),
        grid=(m // tm,),
        in_specs=[row(y) for y in ys] + [col(y) for y in ys_t] + [full(w) for w in ws]
                 + [row(h), p_spec, full(w_ple), full(w_gate), full(vec(g_post)), full(vec(g_ple)), full(vec(g_next))],
        out_specs=[pl.BlockSpec((tm, d), lambda i: (i, 0)), pl.BlockSpec((tm, d), lambda i: (i, 0))],
        out_shape=[jax.ShapeDtypeStruct((m, d), F32), jax.ShapeDtypeStruct((m, d), BF16)],
        compiler_params=_cparams(("parallel",)),
        name="out_proj_residual_ple",
    )(*ys, *ys_t, *ws, h, p, w_ple, w_gate, vec(g_post), vec(g_ple), vec(g_next))


def kernel(x, p, w_in, w_out, norm_pre, norm_post, hyena_conv_w, hyena_conv_b, hyena_w1, hyena_b1, hyena_w2, hyena_b2, hyena_w3, hyena_b3, hyena_w4, hyena_freq, hyena_skip, rel_bias, na_rpb, ple_proj, ple_norm, ple_gate):
    b, seq, d = x.shape
    depth = w_in.shape[0]
    d_inner = w_out.shape[1]
    hw = d_inner // 2
    m = b * seq
    consts = _fft_consts(seq)
    dil_bias = _dilated_bias(rel_bias)
    h = x.reshape(m, d)
    u = _rmsnorm_bf16(h, norm_pre[0])
    w_all = w_in.astype(BF16)
    p_all = p.reshape(depth, m, -1)
    for i in range(depth):
        j = i // 2
        if i % 2 == 0:
            w_a = w_all[i, :, :4 * hw].T
            ident = jnp.tile(jnp.array([0.0, 1.0, 0.0], F32)[:, None], (1, hw))
            conv_w = jnp.concatenate([hyena_conv_w[j], ident], axis=1).T
            conv_b = jnp.concatenate([hyena_conv_b[j], jnp.zeros((hw,), F32)])[:, None]
            hy = _proj_t(w_a, u.reshape(b, seq, d), conv_w, conv_b)
            halves = _hyena_filters_t(seq, hw, hyena_w1[j], hyena_b1[j], hyena_w2[j], hyena_b2[j],
                                      hyena_w3[j], hyena_b3[j], hyena_w4[j], hyena_freq[j])
            kr, ki = _filter_spectrum(halves, consts)
            skip = hyena_skip[j][:, :, None, None]
            z = _hyena_conv(False, hy, 0, hy, hw, hw, skip[0], kr, ki, 0, consts)
            ya_t = _hyena_conv(True, z, 0, hy, 2 * hw, 3 * hw, skip[1], kr, ki, 1, consts)
            proj_b = _matmul_bf16(u, w_all, i, 4 * hw, 4 * hw).reshape(b, seq, 4 * hw)
            yb = _dilated_attention(proj_b, dil_bias, hw // HEAD_DIM)
            ys, ys_t = [yb.reshape(m, hw)], [ya_t]
            ws = [w_out[i, hw:].astype(BF16), w_out[i, :hw].astype(BF16)]
        else:
            proj = _matmul_bf16(u, w_all, i, 0, 4 * d_inner).reshape(b, seq, 4 * d_inner)
            yc = _neighbourhood_attention(proj, _na_bias(na_rpb[j], seq // GRID_W), d_inner // HEAD_DIM)
            ys, ys_t = [yc.reshape(m, d_inner)], []
            ws = [w_out[i].astype(BF16)]
        g_next = norm_pre[i + 1] if i + 1 < depth else norm_pre[i]
        h, u = _out_block(ys, ys_t, ws, h, p_all, i, ple_proj[i].astype(BF16), ple_gate[i].astype(BF16),
                          norm_post[i], ple_norm[i], g_next)
    return h.reshape(b, seq, d)
```

```python
import functools
import math

import numpy as np
import jax
import jax.numpy as jnp
from jax import lax
from jax.experimental import pallas as pl
from jax.experimental.pallas import tpu as pltpu

F32 = jnp.float32
BF16 = jnp.bfloat16

HEAD_DIM = 64
LANES = 128
HYENA_EMB_DIM = 33
HYENA_FAST_DECAY = 0.3
HYENA_SLOW_DECAY = 1.5
HYENA_TARGET = 1e-2
DILATED_PATTERNS = ((128, 1), (512, 4), (2048, 16))
GRID_W = 64
NA_ROWS = 8
NA_COLS = 16
REL_BUCKETS = 32
REL_MAX_DIST = 1024
RMS_EPS = 1e-6
NEG_INF = -1e30
ATT_QB = 128
ATT_HALF = 64
ATT_KW = ATT_QB + 2 * ATT_HALF
ATT_UNROLL = 8
ATT_STEP = 4
assert all(b[1] == a[1] * ATT_STEP for a, b in zip(DILATED_PATTERNS, DILATED_PATTERNS[1:])) and DILATED_PATTERNS[0][1] == 1
NA_UNROLL = 16
VMEM_LIMIT = 56 * 1024 * 1024


def _cparams(sem):
    return pltpu.CompilerParams(dimension_semantics=sem, vmem_limit_bytes=VMEM_LIMIT)


def _rms(x, g):
    return x * lax.rsqrt(jnp.mean(x * x, axis=-1, keepdims=True) + RMS_EPS) * g


def _rmsnorm_kernel(x_ref, g_ref, o_ref):
    o_ref[...] = _rms(x_ref[...], g_ref[...]).astype(o_ref.dtype)


def _rmsnorm_bf16(x2d, g, tm=1024):
    m, d = x2d.shape
    return pl.pallas_call(
        _rmsnorm_kernel,
        grid=(m // tm,),
        in_specs=[pl.BlockSpec((tm, d), lambda i: (i, 0)), pl.BlockSpec((1, d), lambda i: (0, 0))],
        out_specs=pl.BlockSpec((tm, d), lambda i: (i, 0)),
        out_shape=jax.ShapeDtypeStruct((m, d), BF16),
        compiler_params=_cparams(("parallel",)),
        name="rmsnorm",
    )(x2d, g.reshape(1, d))


def _matmul_kernel(a_ref, b_ref, o_ref):
    o_ref[...] = jnp.dot(a_ref[...], b_ref[...], preferred_element_type=F32).astype(o_ref.dtype)


def _matmul_bf16(a, w, layer, col0, n, tm=1024, tn=1024):
    m, k = a.shape
    return pl.pallas_call(
        _matmul_kernel,
        grid=(m // tm, n // tn),
        in_specs=[pl.BlockSpec((tm, k), lambda i, j: (i, 0)), pl.BlockSpec((None, k, tn), lambda i, j: (layer, 0, col0 // tn + j))],
        out_specs=pl.BlockSpec((tm, tn), lambda i, j: (i, j)),
        out_shape=jax.ShapeDtypeStruct((m, n), BF16),
        compiler_params=_cparams(("parallel", "parallel")),
        name="in_proj",
    )(a, w)


def _proj_t_kernel(w_ref, u_ref, cw_ref, cb_ref, o_ref):
    p = lax.dot_general(w_ref[...], u_ref[0], (((1,), (1,)), ((), ())), preferred_element_type=F32)
    seq = p.shape[1]
    t = lax.broadcasted_iota(jnp.int32, p.shape, 1)
    prev = jnp.where(t == 0, 0.0, pltpu.roll(p, 1, axis=1))
    nxt = jnp.where(t == seq - 1, 0.0, pltpu.roll(p, seq - 1, axis=1))
    cw = cw_ref[...]
    o_ref[0] = (prev * cw[:, 0:1] + p * cw[:, 1:2] + nxt * cw[:, 2:3] + cb_ref[...]).astype(o_ref.dtype)


def _proj_t(w_t, u, conv_w, conv_b, tc=256):
    c, d = w_t.shape
    b, seq, _ = u.shape
    return pl.pallas_call(
        _proj_t_kernel,
        grid=(b, c // tc),
        in_specs=[
            pl.BlockSpec((tc, d), lambda i, j: (j, 0)),
            pl.BlockSpec((1, seq, d), lambda i, j: (i, 0, 0)),
            pl.BlockSpec((tc, 3), lambda i, j: (j, 0)),
            pl.BlockSpec((tc, 1), lambda i, j: (j, 0)),
        ],
        out_specs=pl.BlockSpec((1, tc, seq), lambda i, j: (i, j, 0)),
        out_shape=jax.ShapeDtypeStruct((b, c, seq), BF16),
        compiler_params=_cparams(("parallel", "parallel")),
        name="in_proj_t_conv3",
    )(w_t, u, conv_w, conv_b)


def _hyena_filter_kernel(z_ref, w1_ref, b1_ref, w2_ref, b2_ref, w3_ref, b3_ref, fq_ref, w4_ref, dl_ref, t_ref,
                         o_ref, a_scr):
    hp = lax.Precision.HIGHEST

    @pl.when((pl.program_id(0) == 0) & (pl.program_id(1) == 0))
    def _():
        for d in range(2):
            a = z_ref[d]
            for w_ref, b_ref in ((w1_ref, b1_ref), (w2_ref, b2_ref), (w3_ref, b3_ref)):
                a = jnp.sin(fq_ref[...] * (jnp.dot(w_ref[...], a, precision=hp, preferred_element_type=F32) + b_ref[...]))
            a_scr[d] = a

    seq = o_ref.shape[-1]
    proj = lambda d: jnp.dot(w4_ref[0, d].astype(BF16), a_scr[d].astype(BF16), preferred_element_type=F32)
    lo = proj(0) * jnp.exp(-t_ref[0] * dl_ref[...])
    hi = proj(1) * jnp.exp(-t_ref[1] * dl_ref[...])
    hi = jnp.where(lax.broadcasted_iota(jnp.int32, (1, seq), 1) == 0, 0.0, hi)
    norm = jnp.sum(jnp.abs(lo), axis=1, keepdims=True) + jnp.sum(jnp.abs(hi), axis=1, keepdims=True)
    o_ref[0, 0] = (lo / norm).astype(o_ref.dtype)
    o_ref[1, 0] = (hi / norm).astype(o_ref.dtype)


def _hyena_filters_t(seq, c, w1, b1, w2, b2, w3, b3, w4, freq, tc=256):
    tc = min(tc, c)
    fo = w2.shape[0]
    emb = -(-HYENA_EMB_DIM // 8) * 8
    t = jnp.linspace(0.0, 1.0, seq, dtype=F32)
    bands = (HYENA_EMB_DIM - 1) // 2
    fr = jnp.linspace(1e-4, bands - 1, bands, dtype=F32)[:, None]
    mirror = (seq - jnp.arange(seq)) % seq
    tables, times = [], []
    for pos, tp in ((jnp.arange(seq), t), (mirror, jnp.roll(t[::-1], 1))):
        wpos = (2.0 * math.pi / seq) * pos.astype(F32)[None, :]
        tp = tp[None, :]
        z = jnp.concatenate([tp, jnp.cos(fr * wpos), -jnp.sin(fr * wpos),
                             jnp.zeros((emb - HYENA_EMB_DIM, seq), F32)], axis=0)
        tables.append(z)
        times.append(tp)
    z = jnp.stack(tables)
    tt = jnp.stack(times)
    w1_t = jnp.pad(w1.astype(F32).T, ((0, 0), (0, emb - HYENA_EMB_DIM)))
    w4_t = jnp.transpose(w4.astype(F32).reshape(fo, 2, 2, c), (1, 2, 3, 0))
    min_decay = math.log(HYENA_TARGET) / HYENA_SLOW_DECAY
    max_decay = math.log(HYENA_TARGET) / HYENA_FAST_DECAY
    deltas = jnp.abs(jnp.linspace(min_decay, max_decay, c, dtype=F32))[:, None]
    col = lambda v: v.astype(F32)[:, None]
    full = lambda a: pl.BlockSpec(a.shape, lambda f, i: (0,) * a.ndim)
    args = (z, w1_t, col(b1), w2.astype(F32).T, col(b2), w3.astype(F32).T, col(b3), col(freq))
    return pl.pallas_call(
        _hyena_filter_kernel,
        grid=(2, c // tc),
        in_specs=[full(a) for a in args]
                 + [pl.BlockSpec((1, 2, tc, fo), lambda f, i: (f, 0, i, 0)),
                    pl.BlockSpec((tc, 1), lambda f, i: (i, 0)), full(tt)],
        out_specs=pl.BlockSpec((2, 1, tc, seq), lambda f, i: (0, f, i, 0)),
        out_shape=jax.ShapeDtypeStruct((2, 2, c, seq), BF16),
        scratch_shapes=[pltpu.VMEM((2, fo, seq), F32)],
        compiler_params=_cparams(("arbitrary", "arbitrary")),
        name="hyena_filter_mlp",
    )(*args, w4_t, deltas, tt)


def _fft_consts(seq):
    n_fft = 2 * seq
    n1 = n_fft // LANES
    nb = n1 // 2 + 1
    kp = -(-nb // 8) * 8
    k1 = jnp.arange(kp, dtype=jnp.int32)
    keep = (k1 < nb)[:, None]
    two_pi = 2.0 * math.pi

    th = ((k1[:, None] * jnp.arange(n1, dtype=jnp.int32)[None, :]) % n1).astype(F32) * (two_pi / n1)
    f1 = jnp.concatenate([jnp.where(keep, jnp.cos(th), 0.0), jnp.where(keep, -jnp.sin(th), 0.0)], axis=0)
    ph = (k1[:, None] * jnp.arange(LANES, dtype=jnp.int32)[None, :]).astype(F32) * (two_pi / n_fft)
    tw = jnp.stack([jnp.cos(ph), -jnp.sin(ph)])
    om = ((jnp.arange(LANES, dtype=jnp.int32)[:, None] * jnp.arange(LANES, dtype=jnp.int32)[None, :]) % LANES
          ).astype(F32) * (two_pi / LANES)
    f2 = jnp.concatenate([jnp.cos(om), -jnp.sin(om)], axis=1)
    f2i = jnp.concatenate([jnp.cos(om), jnp.sin(om)], axis=1)
    weight = jnp.where((k1 == 0) | (k1 == nb - 1), 1.0, 2.0)[:, None]
    g = jnp.concatenate([jnp.where(keep, weight * jnp.cos(th[:, :n1 // 2]), 0.0),
                         jnp.where(keep, -weight * jnp.sin(th[:, :n1 // 2]), 0.0)], axis=0).T
    return dict(f1=f1.astype(BF16), tw=tw, f2=f2.astype(BF16), f2i=f2i.astype(BF16), g=g.astype(BF16), kp=kp)


def _per_channel_matmul(mat_ref, planes, out_scr, ct):
    def body(i, carry):
        c = 2 * i
        pair = jnp.concatenate([planes(c), planes(c + 1)], axis=1)
        res = jnp.dot(mat_ref[...], pair, preferred_element_type=F32)
        out_scr[c] = res[:, :LANES]
        out_scr[c + 1] = res[:, LANES:]
        return carry
    lax.fori_loop(0, ct // 2, body, 0, unroll=16)


def _twiddle_stage2(a, tw_ref, f2_ref):
    ct, kp2, _ = a.shape
    kp = kp2 // 2
    ar, ai = a[:, :kp], a[:, kp:]
    twr, twi = tw_ref[0], tw_ref[1]
    st = jnp.concatenate([ar * twr - ai * twi, ar * twi + ai * twr], axis=1).astype(BF16)
    p = jnp.dot(st.reshape(ct * kp2, LANES), f2_ref[...], preferred_element_type=F32).reshape(ct, kp2, 2 * LANES)
    return p[:, :kp, :LANES] - p[:, kp:, LANES:], p[:, :kp, LANES:] + p[:, kp:, :LANES]


def _stage2_twiddle_inv(yr, yi, tw_ref, f2i_ref):
    ct, kp, _ = yr.shape
    st = jnp.concatenate([yr, yi], axis=1).astype(BF16)
    p = jnp.dot(st.reshape(ct * 2 * kp, LANES), f2i_ref[...], preferred_element_type=F32).reshape(ct, 2 * kp, 2 * LANES)
    tr, ti = p[:, :kp, :LANES] - p[:, kp:, LANES:], p[:, :kp, LANES:] + p[:, kp:, :LANES]
    twr, twi = tw_ref[0], tw_ref[1]
    return jnp.concatenate([tr * twr + ti * twi, ti * twr - tr * twi], axis=1).astype(BF16)


def _filter_spec_kernel(lo_ref, hi_ref, f1_ref, tw_ref, f2_ref, kr_ref, ki_ref, a_scr):
    ct = lo_ref.shape[2]
    planes = lambda c: jnp.concatenate([lo_ref[0, 0, c], hi_ref[0, 0, c]], axis=0)
    _per_channel_matmul(f1_ref, planes, a_scr, ct)
    xr, xi = _twiddle_stage2(a_scr[...], tw_ref, f2_ref)
    n_fft = f1_ref.shape[1] * LANES
    kr_ref[0] = xr * (1.0 / n_fft)
    ki_ref[0] = xi * (1.0 / n_fft)


def _filter_spectrum(halves, consts, ct=64):
    _, nf, c, seq = halves.shape
    ct = min(ct, c)
    kp = consts["kp"]
    h5 = halves.reshape(2, nf, c, seq // LANES, LANES)
    half = lambda s: pl.BlockSpec((1, 1, ct, seq // LANES, LANES), lambda f, i: (s, f, i, 0, 0))
    full = lambda a: pl.BlockSpec(a.shape, lambda f, i: (0,) * a.ndim)
    spec = jax.ShapeDtypeStruct((nf, c, kp, LANES), F32)
    out = pl.BlockSpec((1, ct, kp, LANES), lambda f, i: (f, i, 0, 0))
    return pl.pallas_call(
        _filter_spec_kernel,
        grid=(nf, c // ct),
        in_specs=[half(0), half(1), full(consts["f1"]), full(consts["tw"]), full(consts["f2"])],
        out_specs=[out, out],
        out_shape=[spec, spec],
        scratch_shapes=[pltpu.VMEM((ct, 2 * kp, LANES), F32)],
        compiler_params=_cparams(("parallel", "parallel")),
        name="hyena_filter_spectrum",
    )(h5, h5, consts["f1"], consts["tw"], consts["f2"])


def _hyena_conv_kernel(second, x_ref, m_ref, e_ref, skip_ref, kr_ref, ki_ref, f1_ref, tw_ref, f2_ref, f2i_ref, g_ref,
                       o_ref, a_scr, t_scr, y_scr):
    ct = x_ref.shape[1]
    _per_channel_matmul(f1_ref, lambda c: x_ref[0, c], a_scr, ct)
    xr, xi = _twiddle_stage2(a_scr[...], tw_ref, f2_ref)
    kr, ki = kr_ref[0], ki_ref[0]
    t_scr[...] = _stage2_twiddle_inv(xr * kr - xi * ki, xr * ki + xi * kr, tw_ref, f2i_ref)

    _per_channel_matmul(g_ref, lambda c: t_scr[c], y_scr, ct)

    out = m_ref[0].astype(F32) * (y_scr[...] + x_ref[0].astype(F32) * skip_ref[...])
    if second:
        e = e_ref[0].astype(F32)
        out = out * (e * jax.nn.sigmoid(e))
    o_ref[0] = out.astype(o_ref.dtype)


def _hyena_conv(second, x, x_row0, hy, m_row0, e_row0, skip, kr, ki, filt, consts, ct=64):
    b, _, seq = x.shape
    c = kr.shape[1]
    ct = min(ct, c)
    kp = consts["kp"]
    n1h = seq // LANES
    view = lambda a: a.reshape(a.shape[0], a.shape[1], n1h, LANES)
    rows = lambda blk0: pl.BlockSpec((1, ct, n1h, LANES), lambda j, i: (i, blk0 + j, 0, 0))
    full = lambda a: pl.BlockSpec(a.shape, lambda j, i: (0,) * a.ndim)
    kspec = pl.BlockSpec((1, ct, kp, LANES), lambda j, i: (filt, j, 0, 0))
    mats = [consts["f1"][:, :n1h], consts["tw"], consts["f2"], consts["f2i"], consts["g"]]
    out = pl.pallas_call(
        functools.partial(_hyena_conv_kernel, second),
        grid=(c // ct, b),
        in_specs=[rows(x_row0 // ct), rows(m_row0 // ct), rows(e_row0 // ct),
                  pl.BlockSpec((ct, 1, 1), lambda j, i: (j, 0, 0)),
                  kspec, kspec] + [full(a) for a in mats],
        out_specs=pl.BlockSpec((1, ct, n1h, LANES), lambda j, i: (i, j, 0, 0)),
        out_shape=jax.ShapeDtypeStruct((b, c, n1h, LANES), BF16),
        scratch_shapes=[pltpu.VMEM((ct, 2 * kp, LANES), F32), pltpu.VMEM((ct, 2 * kp, LANES), BF16),
                        pltpu.VMEM((ct, n1h, LANES), F32)],
        compiler_params=_cparams(("parallel", "parallel")),
        name="hyena_conv2" if second else "hyena_conv1",
    )(view(x), view(hy), view(hy), skip, kr, ki, *mats)
    return out.reshape(b, c, seq)


def _t5_bucket(rel):
    half_b = REL_BUCKETS // 2
    max_exact = half_b // 2
    ret = jnp.where(rel > 0, half_b, 0)
    n = jnp.abs(rel)
    nf = jnp.maximum(n, 1).astype(F32)
    large = max_exact + (jnp.log(nf / max_exact) / math.log(REL_MAX_DIST / max_exact)
                         * (half_b - max_exact)).astype(jnp.int32)
    large = jnp.minimum(large, half_b - 1)
    return ret + jnp.where(n < max_exact, n, large)


def _dilated_bias(rel_bias):
    rel = jnp.arange(ATT_KW)[None, :] - ATT_HALF - jnp.arange(ATT_QB)[:, None]
    col = jnp.arange(ATT_KW)
    edge = jnp.stack([col >= ATT_HALF, col >= 0, col < ATT_KW - ATT_HALF])
    out = []
    for _, dil in DILATED_PATTERNS:
        onehot = jax.nn.one_hot(_t5_bucket(rel * dil), REL_BUCKETS, dtype=F32)
        bias = jnp.einsum("qkb,bh->qkh", onehot, rel_bias.astype(F32), precision=lax.Precision.HIGHEST)
        bias = jnp.where((jnp.abs(rel) <= ATT_HALF)[:, :, None], bias, NEG_INF)
        bias = jnp.transpose(bias, (2, 0, 1))[:, None]
        out.append(jnp.where(edge[None, :, None, :], bias, NEG_INF))
    return jnp.stack(out)


def _head_select(first, second):
    lane = lax.broadcasted_iota(jnp.int32, first.shape[:-1] + (LANES,), len(first.shape) - 1)
    return jnp.where(lane < HEAD_DIM, first, second)


def _dilated_kernel(q_ref, k_ref, v_ref, g_ref, bias_ref, o_ref, *scratch):
    stage_a, stage_b = scratch[0:3], scratch[3:6]
    qd, kd, vd = scratch[6:9]
    last = len(DILATED_PATTERNS) - 1
    states = [scratch[9 + 3 * p:12 + 3 * p] for p in range(last)]
    seq = q_ref.shape[1]
    stage_a[0][...] = q_ref[0].astype(F32) * (HEAD_DIM ** -0.5)
    stage_a[1][...] = k_ref[0].astype(F32)
    stage_a[2][...] = v_ref[0].astype(F32)
    zero_halo = jnp.zeros((ATT_HALF, LANES), BF16)
    vd[:, LANES:] = jnp.ones((vd.shape[0], LANES), BF16)

    src, dst = stage_a, stage_b
    for pi, (_, dil) in enumerate(DILATED_PATTERNS):
        n = seq // dil
        nblk = n // ATT_QB
        if pi > 0:
            prev_dil = dil // ATT_STEP
            for r1 in range(prev_dil):
                for r2 in range(ATT_STEP):
                    r = r1 + prev_dil * r2
                    for a_src, a_dst in zip(src, dst):
                        a_dst[r * n:(r + 1) * n, :] = a_src[pl.ds(r1 * n * ATT_STEP + r2, n, stride=ATT_STEP), :]
            src, dst = dst, src
        stride_k = n + 2 * ATT_HALF
        q_all = src[0][...]
        lane_s = lax.broadcasted_iota(jnp.int32, (seq, LANES), 1)
        for h in range(2):
            qd[h] = jnp.where((lane_s < HEAD_DIM) == (h == 0), q_all, 0.0).astype(BF16)
        for r in range(dil):
            base = r * stride_k
            for a_src, a_dst in ((src[1], kd), (src[2], vd.at[:, :LANES])):
                a_dst[base:base + ATT_HALF, :] = zero_halo
                a_dst[base + ATT_HALF:base + ATT_HALF + n, :] = a_src[r * n:(r + 1) * n, :].astype(BF16)
                a_dst[base + ATT_HALF + n:base + stride_k, :] = zero_halo
        if pi == last:
            states.append(dst)
        acc_w, mx_w, den_w = states[pi]

        def block_group(it, carry, pi=pi, nblk=nblk, acc_w=acc_w, mx_w=mx_w, den_w=den_w):
            chains = []
            for u in range(ATT_UNROLL):
                blk = it * ATT_UNROLL + u
                r = blk // nblk
                qb = blk % nblk
                q0 = pl.multiple_of(blk * ATT_QB, ATT_QB)
                k0 = pl.multiple_of(q0 + r * (2 * ATT_HALF), ATT_QB)
                edge = jnp.where(qb == 0, 0, jnp.where(qb == nblk - 1, 2, 1))
                qcat = jnp.concatenate([qd[0, pl.ds(q0, ATT_QB), :], qd[1, pl.ds(q0, ATT_QB), :]], axis=0)
                s = lax.dot_general(qcat, kd[pl.ds(k0, ATT_KW), :], (((1,), (1,)), ((), ())),
                                    preferred_element_type=F32)
                chains.append((q0, k0, edge, s))
            probs = []
            for q0, k0, edge, s in chains:
                s = s + jnp.concatenate([bias_ref[pi, 0, edge], bias_ref[pi, 1, edge]], axis=0)
                m = jnp.max(s, axis=1, keepdims=True)
                probs.append((jnp.exp(s - m).astype(BF16), m))
            for (q0, k0, edge, s), (p, m) in zip(chains, probs):
                pv = jnp.dot(p, vd[pl.ds(k0, ATT_KW), :], preferred_element_type=F32)
                acc_w[pl.ds(q0, ATT_QB), :] = _head_select(pv[:ATT_QB, :LANES], pv[ATT_QB:, :LANES])
                mx_w[pl.ds(q0, ATT_QB), :] = _head_select(m[:ATT_QB], m[ATT_QB:])
                den_w[pl.ds(q0, ATT_QB), :] = _head_select(pv[:ATT_QB, LANES:], pv[ATT_QB:, LANES:])
            return carry

        lax.fori_loop(0, dil * nblk // ATT_UNROLL, block_group, 0)

    for pi in range(len(DILATED_PATTERNS) - 1, 0, -1):
        dil = DILATED_PATTERNS[pi][1]
        n = seq // dil
        prev_dil = dil // ATT_STEP
        acc_c, mx_c, den_c = states[pi - 1]
        acc_f, mx_f, den_f = states[pi]
        for r1 in range(prev_dil):
            for r2 in range(ATT_STEP):
                r = r1 + prev_dil * r2
                rows = pl.ds(r1 * n * ATT_STEP + r2, n, stride=ATT_STEP)
                fine = slice(r * n, (r + 1) * n)
                m_old = mx_c[rows, :]
                m_new = mx_f[fine, :]
                m_all = jnp.maximum(m_old, m_new)
                decay = jnp.exp(-jnp.abs(m_old - m_new))
                a_old = jnp.where(m_old >= m_new, 1.0, decay)
                a_new = jnp.where(m_old >= m_new, decay, 1.0)
                acc_c[rows, :] = acc_c[rows, :] * a_old + acc_f[fine, :] * a_new
                den_c[rows, :] = den_c[rows, :] * a_old + den_f[fine, :] * a_new
                mx_c[rows, :] = m_all

    acc, _, den = states[0]
    g = g_ref[0].astype(F32)
    o_ref[0] = (acc[...] / den[...] * (g * jax.nn.sigmoid(g))).astype(o_ref.dtype)


def _dilated_attention(proj, bias, n_heads):
    b, seq, _ = proj.shape
    npair = n_heads // 2
    max_dil = max(d for _, d in DILATED_PATTERNS)
    kd_rows = seq + 2 * ATT_HALF * max_dil
    blk = lambda off: pl.BlockSpec((1, seq, LANES), lambda i, j: (i, 0, off * npair + j))
    state = pltpu.VMEM((seq, LANES), F32)
    return pl.pallas_call(
        _dilated_kernel,
        grid=(b, npair),
        in_specs=[blk(0), blk(1), blk(2), blk(3),
                  pl.BlockSpec((len(DILATED_PATTERNS), 2, 3, ATT_QB, ATT_KW), lambda i, j: (0, j, 0, 0, 0))],
        out_specs=pl.BlockSpec((1, seq, LANES), lambda i, j: (i, 0, j)),
        out_shape=jax.ShapeDtypeStruct((b, seq, n_heads * HEAD_DIM), BF16),
        scratch_shapes=[state] * 6
                       + [pltpu.VMEM((2, seq, LANES), BF16), pltpu.VMEM((kd_rows, LANES), BF16),
                          pltpu.VMEM((kd_rows, 2 * LANES), BF16)]
                       + [state] * (3 * (len(DILATED_PATTERNS) - 1)),
        compiler_params=_cparams(("parallel", "parallel")),
        name="dilated_attention",
    )(proj, proj, proj, proj, bias)


def _na_bias(rpb, rows):
    assert min(NA_ROWS, rows) % 2 == 0
    cols = np.arange(GRID_W)
    cs = np.clip(cols - NA_COLS // 2, 0, GRID_W - NA_COLS)
    kc = np.arange(GRID_W)
    inside = (kc[None, :] >= cs[:, None]) & (kc[None, :] < cs[:, None] + NA_COLS)
    col_idx = kc[None, :] - cols[:, None] + NA_COLS - 1
    onehot = (col_idx[:, :, None] == np.arange(2 * NA_COLS - 1)[None, None, :]).astype(np.float32)
    e = jnp.einsum("hrd,cqd->hrcq", rpb.astype(F32), jnp.asarray(onehot), precision=lax.Precision.HIGHEST)
    e = jnp.where(jnp.asarray(inside)[None, None], e, NEG_INF)
    return jnp.concatenate([e[:, :-1], e[:, 1:]], axis=-1)


def _na_kernel(q_ref, k_ref, v_ref, g_ref, bias_ref, o_ref, qm, va):
    seq = q_ref.shape[1]
    rows = seq // GRID_W
    kr_n = min(NA_ROWS, rows)
    lane = lax.broadcasted_iota(jnp.int32, (seq, LANES), 1)
    q = q_ref[0] * (HEAD_DIM ** -0.5)
    for h in range(2):
        qm[h] = jnp.where((lane < HEAD_DIM) == (h == 0), q, jnp.zeros_like(q))
    va[:, :LANES] = v_ref[0]
    va[:, LANES:] = jnp.ones((seq, LANES), BF16)

    def row_group(it, carry):
        chains = []
        for u in range(NA_UNROLL):
            r = it * NA_UNROLL + u
            rs = jnp.clip(r - kr_n // 2, 0, rows - kr_n)
            shift = rs - r + NA_ROWS - 1
            q0 = pl.multiple_of(r * GRID_W, GRID_W)
            k0 = pl.multiple_of(rs * GRID_W, GRID_W)
            qcat = jnp.concatenate([qm[0, pl.ds(q0, GRID_W), :], qm[1, pl.ds(q0, GRID_W), :]], axis=0)
            s = lax.dot_general(qcat, k_ref[0, pl.ds(k0, kr_n * GRID_W), :], (((1,), (1,)), ((), ())),
                                preferred_element_type=F32)
            chains.append((q0, k0, shift, s))
        probs = []
        for q0, k0, shift, s in chains:
            bias = [jnp.concatenate([bias_ref[h, shift + i] for i in range(0, kr_n, 2)], axis=1) for h in range(2)]
            s = s + jnp.concatenate(bias, axis=0)
            probs.append(jnp.exp(s - jnp.max(s, axis=1, keepdims=True)).astype(BF16))
        for (q0, k0, shift, s), p in zip(chains, probs):
            pv = jnp.dot(p, va[pl.ds(k0, kr_n * GRID_W), :], preferred_element_type=F32)
            o = pv[:, :LANES] / pv[:, LANES:]
            g = g_ref[0, pl.ds(q0, GRID_W), :].astype(F32)
            o = _head_select(o[:GRID_W], o[GRID_W:]) * (g * jax.nn.sigmoid(g))
            o_ref[0, pl.ds(q0, GRID_W), :] = o.astype(o_ref.dtype)
        return carry

    lax.fori_loop(0, rows // NA_UNROLL, row_group, 0)


def _neighbourhood_attention(proj, bias, n_heads):
    b, seq, _ = proj.shape
    npair = n_heads // 2
    blk = lambda off: pl.BlockSpec((1, seq, LANES), lambda j, i: (i, 0, off * npair + j))
    return pl.pallas_call(
        _na_kernel,
        grid=(npair, b),
        in_specs=[blk(0), blk(1), blk(2), blk(3),
                  pl.BlockSpec((2,) + bias.shape[1:], lambda j, i: (j, 0, 0, 0))],
        out_specs=pl.BlockSpec((1, seq, LANES), lambda j, i: (i, 0, j)),
        out_shape=jax.ShapeDtypeStruct((b, seq, n_heads * HEAD_DIM), BF16),
        scratch_shapes=[pltpu.VMEM((2, seq, LANES), BF16), pltpu.VMEM((seq, 2 * LANES), BF16)],
        compiler_params=_cparams(("parallel", "parallel")),
        name="neighbourhood_attention",
    )(proj, proj, proj, proj, bias)


def _out_kernel(n_y, n_t, *refs):
    n = n_y + n_t
    y_refs = refs[:n]
    w_refs = refs[n:2 * n]
    h_ref, p_ref, wp_ref, wg_ref, gpost_ref, gple_ref, gnext_ref, h_out, u_out = refs[2 * n:]
    t = None
    for idx, (y_ref, w_ref) in enumerate(zip(y_refs, w_refs)):
        if idx < n_y:
            part = jnp.dot(y_ref[...], w_ref[...], preferred_element_type=F32)
        else:
            part = lax.dot_general(y_ref[0], w_ref[...], (((0,), (0,)), ((), ())), preferred_element_type=F32)
        t = part if t is None else t + part
    h = h_ref[...] + _rms(t, gpost_ref[...])
    e = _rms(jnp.dot(p_ref[...].astype(BF16), wp_ref[...], preferred_element_type=F32), gple_ref[...])
    gate = jax.nn.sigmoid(jnp.dot(h.astype(BF16), wg_ref[...], preferred_element_type=F32))
    h = h + gate * e
    h_out[...] = h
    u_out[...] = _rms(h, gnext_ref[...]).astype(u_out.dtype)


def _out_block(ys, ys_t, ws, h, p, layer, w_ple, w_gate, g_post, g_ple, g_next, tm=512):
    m, d = h.shape
    row = lambda a: pl.BlockSpec((tm, a.shape[1]), lambda i: (i, 0))
    p_spec = pl.BlockSpec((None, tm, p.shape[2]), lambda i: (layer, i, 0))
    full = lambda a: pl.BlockSpec(a.shape, lambda i: (0, 0))
    vec = lambda g: g.reshape(1, d)

    def col(a):
        tiles = a.shape[2] // tm
        return pl.BlockSpec((1, a.shape[1], tm), lambda i: (i // tiles, 0, i % tiles))

    return pl.pallas_call(
        functools.partial(_out_kernel, len(ys), len(ys_t)),
        grid=(m // tm,),
        in_specs=[row(y) for y in ys] + [col(y) for y in ys_t] + [full(w) for w in ws]
                 + [row(h), p_spec, full(w_ple), full(w_gate), full(vec(g_post)), full(vec(g_ple)), full(vec(g_next))],
        out_specs=[pl.BlockSpec((tm, d), lambda i: (i, 0)), pl.BlockSpec((tm, d), lambda i: (i, 0))],
        out_shape=[jax.ShapeDtypeStruct((m, d), F32), jax.ShapeDtypeStruct((m, d), BF16)],
        compiler_params=_cparams(("parallel",)),
        name="out_proj_residual_ple",
    )(*ys, *ys_t, *ws, h, p, w_ple, w_gate, vec(g_post), vec(g_ple), vec(g_next))


def kernel(x, p, w_in, w_out, norm_pre, norm_post, hyena_conv_w, hyena_conv_b, hyena_w1, hyena_b1, hyena_w2, hyena_b2, hyena_w3, hyena_b3, hyena_w4, hyena_freq, hyena_skip, rel_bias, na_rpb, ple_proj, ple_norm, ple_gate):
    b, seq, d = x.shape
    depth = w_in.shape[0]
    d_inner = w_out.shape[1]
    hw = d_inner // 2
    m = b * seq
    consts = _fft_consts(seq)
    dil_bias = _dilated_bias(rel_bias)
    h = x.reshape(m, d)
    u = _rmsnorm_bf16(h, norm_pre[0])
    w_all = w_in.astype(BF16)
    p_all = p.reshape(depth, m, -1)
    for i in range(depth):
        j = i // 2
        if i % 2 == 0:
            w_a = w_all[i, :, :4 * hw].T
            ident = jnp.tile(jnp.array([0.0, 1.0, 0.0], F32)[:, None], (1, hw))
            conv_w = jnp.concatenate([hyena_conv_w[j], ident], axis=1).T
            conv_b = jnp.concatenate([hyena_conv_b[j], jnp.zeros((hw,), F32)])[:, None]
            hy = _proj_t(w_a, u.reshape(b, seq, d), conv_w, conv_b)
            halves = _hyena_filters_t(seq, hw, hyena_w1[j], hyena_b1[j], hyena_w2[j], hyena_b2[j],
                                      hyena_w3[j], hyena_b3[j], hyena_w4[j], hyena_freq[j])
            kr, ki = _filter_spectrum(halves, consts)
            skip = hyena_skip[j][:, :, None, None]
            z = _hyena_conv(False, hy, 0, hy, hw, hw, skip[0], kr, ki, 0, consts)
            ya_t = _hyena_conv(True, z, 0, hy, 2 * hw, 3 * hw, skip[1], kr, ki, 1, consts)
            proj_b = _matmul_bf16(u, w_all[i:i + 1, :, 4 * hw:], 0, 0, 4 * hw).reshape(b, seq, 4 * hw)
            yb = _dilated_attention(proj_b, dil_bias, hw // HEAD_DIM)
            ys, ys_t = [yb.reshape(m, hw)], [ya_t]
            ws = [w_out[i, hw:].astype(BF16), w_out[i, :hw].astype(BF16)]
        else:
            proj = _matmul_bf16(u, w_all, i, 0, 4 * d_inner).reshape(b, seq, 4 * d_inner)
            yc = _neighbourhood_attention(proj, _na_bias(na_rpb[j], seq // GRID_W), d_inner // HEAD_DIM)
            ys, ys_t = [yc.reshape(m, d_inner)], []
            ws = [w_out[i].astype(BF16)]
        g_next = norm_pre[i + 1] if i + 1 < depth else norm_pre[i]
        h, u = _out_block(ys, ys_t, ws, h, p_all, i, ple_proj[i].astype(BF16), ple_gate[i].astype(BF16),
                          norm_post[i], ple_norm[i], g_next)
    return h.reshape(b, seq, d)
```

```python
import functools
import math

import numpy as np
import jax
import jax.numpy as jnp
from jax import lax
from jax.experimental import pallas as pl
from jax.experimental.pallas import tpu as pltpu

F32 = jnp.float32
BF16 = jnp.bfloat16

HEAD_DIM = 64
LANES = 128
HYENA_EMB_DIM = 33
HYENA_FAST_DECAY = 0.3
HYENA_SLOW_DECAY = 1.5
HYENA_TARGET = 1e-2
DILATED_PATTERNS = ((128, 1), (512, 4), (2048, 16))
GRID_W = 64
NA_ROWS = 8
NA_COLS = 16
REL_BUCKETS = 32
REL_MAX_DIST = 1024
RMS_EPS = 1e-6
NEG_INF = -1e30
ATT_QB = 128
ATT_HALF = 64
ATT_KW = ATT_QB + 2 * ATT_HALF
ATT_UNROLL = 8
ATT_STEP = 4
assert all(b[1] == a[1] * ATT_STEP for a, b in zip(DILATED_PATTERNS, DILATED_PATTERNS[1:])) and DILATED_PATTERNS[0][1] == 1
NA_UNROLL = 16
VMEM_LIMIT = 56 * 1024 * 1024


def _cparams(sem):
    return pltpu.CompilerParams(dimension_semantics=sem, vmem_limit_bytes=VMEM_LIMIT)


def _rms(x, g):
    return x * lax.rsqrt(jnp.mean(x * x, axis=-1, keepdims=True) + RMS_EPS) * g


def _rmsnorm_kernel(x_ref, g_ref, o_ref):
    o_ref[...] = _rms(x_ref[...], g_ref[...]).astype(o_ref.dtype)


def _rmsnorm_bf16(x2d, g, tm=1024):
    m, d = x2d.shape
    return pl.pallas_call(
        _rmsnorm_kernel,
        grid=(m // tm,),
        in_specs=[pl.BlockSpec((tm, d), lambda i: (i, 0)), pl.BlockSpec((1, d), lambda i: (0, 0))],
        out_specs=pl.BlockSpec((tm, d), lambda i: (i, 0)),
        out_shape=jax.ShapeDtypeStruct((m, d), BF16),
        compiler_params=_cparams(("parallel",)),
        name="rmsnorm",
    )(x2d, g.reshape(1, d))


def _matmul_kernel(a_ref, b_ref, o_ref):
    res = jnp.dot(a_ref[...], b_ref[...], preferred_element_type=F32).astype(o_ref.dtype)
    for c in range(o_ref.shape[0]):
        o_ref[c] = res[:, c * LANES:(c + 1) * LANES]


def _matmul_bf16(a, w, layer, col0, n, tm=1024, tn=1024):
    m, k = a.shape
    return pl.pallas_call(
        _matmul_kernel,
        grid=(m // tm, n // tn),
        in_specs=[pl.BlockSpec((tm, k), lambda i, j: (i, 0)), pl.BlockSpec((None, k, tn), lambda i, j: (layer, 0, col0 // tn + j))],
        out_specs=pl.BlockSpec((tn // LANES, tm, LANES), lambda i, j: (j, i, 0)),
        out_shape=jax.ShapeDtypeStruct((n // LANES, m, LANES), BF16),
        compiler_params=_cparams(("parallel", "parallel")),
        name="in_proj",
    )(a, w)


def _proj_t_kernel(w_ref, u_ref, cw_ref, cb_ref, o_ref):
    p = lax.dot_general(w_ref[...], u_ref[0], (((1,), (1,)), ((), ())), preferred_element_type=F32)
    seq = p.shape[1]
    t = lax.broadcasted_iota(jnp.int32, p.shape, 1)
    prev = jnp.where(t == 0, 0.0, pltpu.roll(p, 1, axis=1))
    nxt = jnp.where(t == seq - 1, 0.0, pltpu.roll(p, seq - 1, axis=1))
    cw = cw_ref[...]
    o_ref[0] = (prev * cw[:, 0:1] + p * cw[:, 1:2] + nxt * cw[:, 2:3] + cb_ref[...]).astype(o_ref.dtype)


def _proj_t(w_t, u, conv_w, conv_b, tc=256):
    c, d = w_t.shape
    b, seq, _ = u.shape
    return pl.pallas_call(
        _proj_t_kernel,
        grid=(b, c // tc),
        in_specs=[
            pl.BlockSpec((tc, d), lambda i, j: (j, 0)),
            pl.BlockSpec((1, seq, d), lambda i, j: (i, 0, 0)),
            pl.BlockSpec((tc, 3), lambda i, j: (j, 0)),
            pl.BlockSpec((tc, 1), lambda i, j: (j, 0)),
        ],
        out_specs=pl.BlockSpec((1, tc, seq), lambda i, j: (i, j, 0)),
        out_shape=jax.ShapeDtypeStruct((b, c, seq), BF16),
        compiler_params=_cparams(("parallel", "parallel")),
        name="in_proj_t_conv3",
    )(w_t, u, conv_w, conv_b)


def _hyena_filter_kernel(z_ref, w1_ref, b1_ref, w2_ref, b2_ref, w3_ref, b3_ref, fq_ref, w4_ref, dl_ref, t_ref,
                         o_ref, a_scr):
    hp = lax.Precision.HIGHEST

    @pl.when((pl.program_id(0) == 0) & (pl.program_id(1) == 0))
    def _():
        for d in range(2):
            a = z_ref[d]
            for w_ref, b_ref in ((w1_ref, b1_ref), (w2_ref, b2_ref), (w3_ref, b3_ref)):
                a = jnp.sin(fq_ref[...] * (jnp.dot(w_ref[...], a, precision=hp, preferred_element_type=F32) + b_ref[...]))
            a_scr[d] = a

    seq = o_ref.shape[-1]
    proj = lambda d: jnp.dot(w4_ref[0, d].astype(BF16), a_scr[d].astype(BF16), preferred_element_type=F32)
    lo = proj(0) * jnp.exp(-t_ref[0] * dl_ref[...])
    hi = proj(1) * jnp.exp(-t_ref[1] * dl_ref[...])
    hi = jnp.where(lax.broadcasted_iota(jnp.int32, (1, seq), 1) == 0, 0.0, hi)
    norm = jnp.sum(jnp.abs(lo), axis=1, keepdims=True) + jnp.sum(jnp.abs(hi), axis=1, keepdims=True)
    o_ref[0, 0] = (lo / norm).astype(o_ref.dtype)
    o_ref[1, 0] = (hi / norm).astype(o_ref.dtype)


def _hyena_filters_t(seq, c, w1, b1, w2, b2, w3, b3, w4, freq, tc=256):
    tc = min(tc, c)
    fo = w2.shape[0]
    emb = -(-HYENA_EMB_DIM // 8) * 8
    t = jnp.linspace(0.0, 1.0, seq, dtype=F32)
    bands = (HYENA_EMB_DIM - 1) // 2
    fr = jnp.linspace(1e-4, bands - 1, bands, dtype=F32)[:, None]
    mirror = (seq - jnp.arange(seq)) % seq
    tables, times = [], []
    for pos, tp in ((jnp.arange(seq), t), (mirror, jnp.roll(t[::-1], 1))):
        wpos = (2.0 * math.pi / seq) * pos.astype(F32)[None, :]
        tp = tp[None, :]
        z = jnp.concatenate([tp, jnp.cos(fr * wpos), -jnp.sin(fr * wpos),
                             jnp.zeros((emb - HYENA_EMB_DIM, seq), F32)], axis=0)
        tables.append(z)
        times.append(tp)
    z = jnp.stack(tables)
    tt = jnp.stack(times)
    w1_t = jnp.pad(w1.astype(F32).T, ((0, 0), (0, emb - HYENA_EMB_DIM)))
    w4_t = jnp.transpose(w4.astype(F32).reshape(fo, 2, 2, c), (1, 2, 3, 0))
    min_decay = math.log(HYENA_TARGET) / HYENA_SLOW_DECAY
    max_decay = math.log(HYENA_TARGET) / HYENA_FAST_DECAY
    deltas = jnp.abs(jnp.linspace(min_decay, max_decay, c, dtype=F32))[:, None]
    col = lambda v: v.astype(F32)[:, None]
    full = lambda a: pl.BlockSpec(a.shape, lambda f, i: (0,) * a.ndim)
    args = (z, w1_t, col(b1), w2.astype(F32).T, col(b2), w3.astype(F32).T, col(b3), col(freq))
    return pl.pallas_call(
        _hyena_filter_kernel,
        grid=(2, c // tc),
        in_specs=[full(a) for a in args]
                 + [pl.BlockSpec((1, 2, tc, fo), lambda f, i: (f, 0, i, 0)),
                    pl.BlockSpec((tc, 1), lambda f, i: (i, 0)), full(tt)],
        out_specs=pl.BlockSpec((2, 1, tc, seq), lambda f, i: (0, f, i, 0)),
        out_shape=jax.ShapeDtypeStruct((2, 2, c, seq), BF16),
        scratch_shapes=[pltpu.VMEM((2, fo, seq), F32)],
        compiler_params=_cparams(("arbitrary", "arbitrary")),
        name="hyena_filter_mlp",
    )(*args, w4_t, deltas, tt)


def _fft_consts(seq):
    n_fft = 2 * seq
    n1 = n_fft // LANES
    nb = n1 // 2 + 1
    kp = -(-nb // 8) * 8
    k1 = jnp.arange(kp, dtype=jnp.int32)
    keep = (k1 < nb)[:, None]
    two_pi = 2.0 * math.pi

    th = ((k1[:, None] * jnp.arange(n1, dtype=jnp.int32)[None, :]) % n1).astype(F32) * (two_pi / n1)
    f1 = jnp.concatenate([jnp.where(keep, jnp.cos(th), 0.0), jnp.where(keep, -jnp.sin(th), 0.0)], axis=0)
    ph = (k1[:, None] * jnp.arange(LANES, dtype=jnp.int32)[None, :]).astype(F32) * (two_pi / n_fft)
    tw = jnp.stack([jnp.cos(ph), -jnp.sin(ph)])
    om = ((jnp.arange(LANES, dtype=jnp.int32)[:, None] * jnp.arange(LANES, dtype=jnp.int32)[None, :]) % LANES
          ).astype(F32) * (two_pi / LANES)
    f2 = jnp.concatenate([jnp.cos(om), -jnp.sin(om)], axis=1)
    f2i = jnp.concatenate([jnp.cos(om), jnp.sin(om)], axis=1)
    weight = jnp.where((k1 == 0) | (k1 == nb - 1), 1.0, 2.0)[:, None]
    g = jnp.concatenate([jnp.where(keep, weight * jnp.cos(th[:, :n1 // 2]), 0.0),
                         jnp.where(keep, -weight * jnp.sin(th[:, :n1 // 2]), 0.0)], axis=0).T
    return dict(f1=f1.astype(BF16), tw=tw, f2=f2.astype(BF16), f2i=f2i.astype(BF16), g=g.astype(BF16), kp=kp)


def _per_channel_matmul(mat_ref, planes, out_scr, ct):
    def body(i, carry):
        c = 2 * i
        pair = jnp.concatenate([planes(c), planes(c + 1)], axis=1)
        res = jnp.dot(mat_ref[...], pair, preferred_element_type=F32)
        out_scr[c] = res[:, :LANES]
        out_scr[c + 1] = res[:, LANES:]
        return carry
    lax.fori_loop(0, ct // 2, body, 0, unroll=16)


def _twiddle_stage2(a, tw_ref, f2_ref):
    ct, kp2, _ = a.shape
    kp = kp2 // 2
    ar, ai = a[:, :kp], a[:, kp:]
    twr, twi = tw_ref[0], tw_ref[1]
    st = jnp.concatenate([ar * twr - ai * twi, ar * twi + ai * twr], axis=1).astype(BF16)
    p = jnp.dot(st.reshape(ct * kp2, LANES), f2_ref[...], preferred_element_type=F32).reshape(ct, kp2, 2 * LANES)
    return p[:, :kp, :LANES] - p[:, kp:, LANES:], p[:, :kp, LANES:] + p[:, kp:, :LANES]


def _stage2_twiddle_inv(yr, yi, tw_ref, f2i_ref):
    ct, kp, _ = yr.shape
    st = jnp.concatenate([yr, yi], axis=1).astype(BF16)
    p = jnp.dot(st.reshape(ct * 2 * kp, LANES), f2i_ref[...], preferred_element_type=F32).reshape(ct, 2 * kp, 2 * LANES)
    tr, ti = p[:, :kp, :LANES] - p[:, kp:, LANES:], p[:, :kp, LANES:] + p[:, kp:, :LANES]
    twr, twi = tw_ref[0], tw_ref[1]
    return jnp.concatenate([tr * twr + ti * twi, ti * twr - tr * twi], axis=1).astype(BF16)


def _filter_spec_kernel(lo_ref, hi_ref, f1_ref, tw_ref, f2_ref, kr_ref, ki_ref, a_scr):
    ct = lo_ref.shape[2]
    planes = lambda c: jnp.concatenate([lo_ref[0, 0, c], hi_ref[0, 0, c]], axis=0)
    _per_channel_matmul(f1_ref, planes, a_scr, ct)
    xr, xi = _twiddle_stage2(a_scr[...], tw_ref, f2_ref)
    n_fft = f1_ref.shape[1] * LANES
    kr_ref[0] = xr * (1.0 / n_fft)
    ki_ref[0] = xi * (1.0 / n_fft)


def _filter_spectrum(halves, consts, ct=64):
    _, nf, c, seq = halves.shape
    ct = min(ct, c)
    kp = consts["kp"]
    h5 = halves.reshape(2, nf, c, seq // LANES, LANES)
    half = lambda s: pl.BlockSpec((1, 1, ct, seq // LANES, LANES), lambda f, i: (s, f, i, 0, 0))
    full = lambda a: pl.BlockSpec(a.shape, lambda f, i: (0,) * a.ndim)
    spec = jax.ShapeDtypeStruct((nf, c, kp, LANES), F32)
    out = pl.BlockSpec((1, ct, kp, LANES), lambda f, i: (f, i, 0, 0))
    return pl.pallas_call(
        _filter_spec_kernel,
        grid=(nf, c // ct),
        in_specs=[half(0), half(1), full(consts["f1"]), full(consts["tw"]), full(consts["f2"])],
        out_specs=[out, out],
        out_shape=[spec, spec],
        scratch_shapes=[pltpu.VMEM((ct, 2 * kp, LANES), F32)],
        compiler_params=_cparams(("parallel", "parallel")),
        name="hyena_filter_spectrum",
    )(h5, h5, consts["f1"], consts["tw"], consts["f2"])


def _hyena_conv_kernel(second, x_ref, m_ref, e_ref, skip_ref, kr_ref, ki_ref, f1_ref, tw_ref, f2_ref, f2i_ref, g_ref,
                       o_ref, a_scr, t_scr, y_scr):
    ct = x_ref.shape[1]
    _per_channel_matmul(f1_ref, lambda c: x_ref[0, c], a_scr, ct)
    xr, xi = _twiddle_stage2(a_scr[...], tw_ref, f2_ref)
    kr, ki = kr_ref[0], ki_ref[0]
    t_scr[...] = _stage2_twiddle_inv(xr * kr - xi * ki, xr * ki + xi * kr, tw_ref, f2i_ref)

    _per_channel_matmul(g_ref, lambda c: t_scr[c], y_scr, ct)

    out = m_ref[0].astype(F32) * (y_scr[...] + x_ref[0].astype(F32) * skip_ref[...])
    if second:
        e = e_ref[0].astype(F32)
        out = out * (e * jax.nn.sigmoid(e))
    o_ref[0] = out.astype(o_ref.dtype)


def _hyena_conv(second, x, x_row0, hy, m_row0, e_row0, skip, kr, ki, filt, consts, ct=64):
    b, _, seq = x.shape
    c = kr.shape[1]
    ct = min(ct, c)
    kp = consts["kp"]
    n1h = seq // LANES
    view = lambda a: a.reshape(a.shape[0], a.shape[1], n1h, LANES)
    rows = lambda blk0: pl.BlockSpec((1, ct, n1h, LANES), lambda j, i: (i, blk0 + j, 0, 0))
    full = lambda a: pl.BlockSpec(a.shape, lambda j, i: (0,) * a.ndim)
    kspec = pl.BlockSpec((1, ct, kp, LANES), lambda j, i: (filt, j, 0, 0))
    mats = [consts["f1"][:, :n1h], consts["tw"], consts["f2"], consts["f2i"], consts["g"]]
    out = pl.pallas_call(
        functools.partial(_hyena_conv_kernel, second),
        grid=(c // ct, b),
        in_specs=[rows(x_row0 // ct), rows(m_row0 // ct), rows(e_row0 // ct),
                  pl.BlockSpec((ct, 1, 1), lambda j, i: (j, 0, 0)),
                  kspec, kspec] + [full(a) for a in mats],
        out_specs=pl.BlockSpec((1, ct, n1h, LANES), lambda j, i: (i, j, 0, 0)),
        out_shape=jax.ShapeDtypeStruct((b, c, n1h, LANES), BF16),
        scratch_shapes=[pltpu.VMEM((ct, 2 * kp, LANES), F32), pltpu.VMEM((ct, 2 * kp, LANES), BF16),
                        pltpu.VMEM((ct, n1h, LANES), F32)],
        compiler_params=_cparams(("parallel", "parallel")),
        name="hyena_conv2" if second else "hyena_conv1",
    )(view(x), view(hy), view(hy), skip, kr, ki, *mats)
    return out.reshape(b, c, seq)


def _t5_bucket(rel):
    half_b = REL_BUCKETS // 2
    max_exact = half_b // 2
    ret = jnp.where(rel > 0, half_b, 0)
    n = jnp.abs(rel)
    nf = jnp.maximum(n, 1).astype(F32)
    large = max_exact + (jnp.log(nf / max_exact) / math.log(REL_MAX_DIST / max_exact)
                         * (half_b - max_exact)).astype(jnp.int32)
    large = jnp.minimum(large, half_b - 1)
    return ret + jnp.where(n < max_exact, n, large)


def _dilated_bias(rel_bias):
    rel = jnp.arange(ATT_KW)[None, :] - ATT_HALF - jnp.arange(ATT_QB)[:, None]
    col = jnp.arange(ATT_KW)
    edge = jnp.stack([col >= ATT_HALF, col >= 0, col < ATT_KW - ATT_HALF])
    out = []
    for _, dil in DILATED_PATTERNS:
        onehot = jax.nn.one_hot(_t5_bucket(rel * dil), REL_BUCKETS, dtype=F32)
        bias = jnp.einsum("qkb,bh->qkh", onehot, rel_bias.astype(F32), precision=lax.Precision.HIGHEST)
        bias = jnp.where((jnp.abs(rel) <= ATT_HALF)[:, :, None], bias, NEG_INF)
        bias = jnp.transpose(bias, (2, 0, 1))[:, None]
        out.append(jnp.where(edge[None, :, None, :], bias, NEG_INF))
    return jnp.stack(out)


def _head_select(first, second):
    lane = lax.broadcasted_iota(jnp.int32, first.shape[:-1] + (LANES,), len(first.shape) - 1)
    return jnp.where(lane < HEAD_DIM, first, second)


def _dilated_kernel(q_ref, k_ref, v_ref, g_ref, bias_ref, o_ref, *scratch):
    stage_a, stage_b = scratch[0:3], scratch[3:6]
    qd, kd, vd = scratch[6:9]
    last = len(DILATED_PATTERNS) - 1
    states = [scratch[9 + 3 * p:12 + 3 * p] for p in range(last)]
    seq = q_ref.shape[1]
    stage_a[0][...] = q_ref[0].astype(F32) * (HEAD_DIM ** -0.5)
    stage_a[1][...] = k_ref[0].astype(F32)
    stage_a[2][...] = v_ref[0].astype(F32)
    zero_halo = jnp.zeros((ATT_HALF, LANES), BF16)
    vd[:, LANES:] = jnp.ones((vd.shape[0], LANES), BF16)

    src, dst = stage_a, stage_b
    for pi, (_, dil) in enumerate(DILATED_PATTERNS):
        n = seq // dil
        nblk = n // ATT_QB
        if pi > 0:
            prev_dil = dil // ATT_STEP
            for r1 in range(prev_dil):
                for r2 in range(ATT_STEP):
                    r = r1 + prev_dil * r2
                    for a_src, a_dst in zip(src, dst):
                        a_dst[r * n:(r + 1) * n, :] = a_src[pl.ds(r1 * n * ATT_STEP + r2, n, stride=ATT_STEP), :]
            src, dst = dst, src
        stride_k = n + 2 * ATT_HALF
        q_all = src[0][...]
        lane_s = lax.broadcasted_iota(jnp.int32, (seq, LANES), 1)
        for h in range(2):
            qd[h] = jnp.where((lane_s < HEAD_DIM) == (h == 0), q_all, 0.0).astype(BF16)
        for r in range(dil):
            base = r * stride_k
            for a_src, a_dst in ((src[1], kd), (src[2], vd.at[:, :LANES])):
                a_dst[base:base + ATT_HALF, :] = zero_halo
                a_dst[base + ATT_HALF:base + ATT_HALF + n, :] = a_src[r * n:(r + 1) * n, :].astype(BF16)
                a_dst[base + ATT_HALF + n:base + stride_k, :] = zero_halo
        if pi == last:
            states.append(dst)
        acc_w, mx_w, den_w = states[pi]

        def block_group(it, carry, pi=pi, nblk=nblk, acc_w=acc_w, mx_w=mx_w, den_w=den_w):
            chains = []
            for u in range(ATT_UNROLL):
                blk = it * ATT_UNROLL + u
                r = blk // nblk
                qb = blk % nblk
                q0 = pl.multiple_of(blk * ATT_QB, ATT_QB)
                k0 = pl.multiple_of(q0 + r * (2 * ATT_HALF), ATT_QB)
                edge = jnp.where(qb == 0, 0, jnp.where(qb == nblk - 1, 2, 1))
                qcat = jnp.concatenate([qd[0, pl.ds(q0, ATT_QB), :], qd[1, pl.ds(q0, ATT_QB), :]], axis=0)
                s = lax.dot_general(qcat, kd[pl.ds(k0, ATT_KW), :], (((1,), (1,)), ((), ())),
                                    preferred_element_type=F32)
                chains.append((q0, k0, edge, s))
            probs = []
            for q0, k0, edge, s in chains:
                s = s + jnp.concatenate([bias_ref[pi, 0, edge], bias_ref[pi, 1, edge]], axis=0)
                m = jnp.max(s, axis=1, keepdims=True)
                probs.append((jnp.exp(s - m).astype(BF16), m))
            for (q0, k0, edge, s), (p, m) in zip(chains, probs):
                pv = jnp.dot(p, vd[pl.ds(k0, ATT_KW), :], preferred_element_type=F32)
                acc_w[pl.ds(q0, ATT_QB), :] = _head_select(pv[:ATT_QB, :LANES], pv[ATT_QB:, :LANES])
                mx_w[pl.ds(q0, ATT_QB), :] = _head_select(m[:ATT_QB], m[ATT_QB:])
                den_w[pl.ds(q0, ATT_QB), :] = _head_select(pv[:ATT_QB, LANES:], pv[ATT_QB:, LANES:])
            return carry

        lax.fori_loop(0, dil * nblk // ATT_UNROLL, block_group, 0)

    for pi in range(len(DILATED_PATTERNS) - 1, 0, -1):
        dil = DILATED_PATTERNS[pi][1]
        n = seq // dil
        prev_dil = dil // ATT_STEP
        acc_c, mx_c, den_c = states[pi - 1]
        acc_f, mx_f, den_f = states[pi]
        for r1 in range(prev_dil):
            for r2 in range(ATT_STEP):
                r = r1 + prev_dil * r2
                rows = pl.ds(r1 * n * ATT_STEP + r2, n, stride=ATT_STEP)
                fine = slice(r * n, (r + 1) * n)
                m_old = mx_c[rows, :]
                m_new = mx_f[fine, :]
                m_all = jnp.maximum(m_old, m_new)
                decay = jnp.exp(-jnp.abs(m_old - m_new))
                a_old = jnp.where(m_old >= m_new, 1.0, decay)
                a_new = jnp.where(m_old >= m_new, decay, 1.0)
                acc_c[rows, :] = acc_c[rows, :] * a_old + acc_f[fine, :] * a_new
                den_c[rows, :] = den_c[rows, :] * a_old + den_f[fine, :] * a_new
                mx_c[rows, :] = m_all

    acc, _, den = states[0]
    g = g_ref[0].astype(F32)
    o_ref[0] = (acc[...] / den[...] * (g * jax.nn.sigmoid(g))).astype(o_ref.dtype)


def _dilated_attention(proj, bias, n_heads):
    _, b, seq, _ = proj.shape
    npair = n_heads // 2
    max_dil = max(d for _, d in DILATED_PATTERNS)
    kd_rows = seq + 2 * ATT_HALF * max_dil
    blk = lambda off: pl.BlockSpec((None, 1, seq, LANES), lambda i, j: (off * npair + j, i, 0, 0))
    state = pltpu.VMEM((seq, LANES), F32)
    return pl.pallas_call(
        _dilated_kernel,
        grid=(b, npair),
        in_specs=[blk(0), blk(1), blk(2), blk(3),
                  pl.BlockSpec((len(DILATED_PATTERNS), 2, 3, ATT_QB, ATT_KW), lambda i, j: (0, j, 0, 0, 0))],
        out_specs=pl.BlockSpec((None, 1, seq, LANES), lambda i, j: (j, i, 0, 0)),
        out_shape=jax.ShapeDtypeStruct((npair, b, seq, LANES), BF16),
        scratch_shapes=[state] * 6
                       + [pltpu.VMEM((2, seq, LANES), BF16), pltpu.VMEM((kd_rows, LANES), BF16),
                          pltpu.VMEM((kd_rows, 2 * LANES), BF16)]
                       + [state] * (3 * (len(DILATED_PATTERNS) - 1)),
        compiler_params=_cparams(("parallel", "parallel")),
        name="dilated_attention",
    )(proj, proj, proj, proj, bias)


def _na_bias(rpb, rows):
    assert min(NA_ROWS, rows) % 2 == 0
    cols = np.arange(GRID_W)
    cs = np.clip(cols - NA_COLS // 2, 0, GRID_W - NA_COLS)
    kc = np.arange(GRID_W)
    inside = (kc[None, :] >= cs[:, None]) & (kc[None, :] < cs[:, None] + NA_COLS)
    col_idx = kc[None, :] - cols[:, None] + NA_COLS - 1
    onehot = (col_idx[:, :, None] == np.arange(2 * NA_COLS - 1)[None, None, :]).astype(np.float32)
    e = jnp.einsum("hrd,cqd->hrcq", rpb.astype(F32), jnp.asarray(onehot), precision=lax.Precision.HIGHEST)
    e = jnp.where(jnp.asarray(inside)[None, None], e, NEG_INF)
    return jnp.concatenate([e[:, :-1], e[:, 1:]], axis=-1)


def _na_kernel(q_ref, k_ref, v_ref, g_ref, bias_ref, o_ref, qm, va):
    seq = q_ref.shape[1]
    rows = seq // GRID_W
    kr_n = min(NA_ROWS, rows)
    lane = lax.broadcasted_iota(jnp.int32, (seq, LANES), 1)
    q = q_ref[0] * (HEAD_DIM ** -0.5)
    for h in range(2):
        qm[h] = jnp.where((lane < HEAD_DIM) == (h == 0), q, jnp.zeros_like(q))
    va[:, :LANES] = v_ref[0]
    va[:, LANES:] = jnp.ones((seq, LANES), BF16)

    def row_group(it, carry):
        chains = []
        for u in range(NA_UNROLL):
            r = it * NA_UNROLL + u
            rs = jnp.clip(r - kr_n // 2, 0, rows - kr_n)
            shift = rs - r + NA_ROWS - 1
            q0 = pl.multiple_of(r * GRID_W, GRID_W)
            k0 = pl.multiple_of(rs * GRID_W, GRID_W)
            qcat = jnp.concatenate([qm[0, pl.ds(q0, GRID_W), :], qm[1, pl.ds(q0, GRID_W), :]], axis=0)
            s = lax.dot_general(qcat, k_ref[0, pl.ds(k0, kr_n * GRID_W), :], (((1,), (1,)), ((), ())),
                                preferred_element_type=F32)
            chains.append((q0, k0, shift, s))
        probs = []
        for q0, k0, shift, s in chains:
            bias = [jnp.concatenate([bias_ref[h, shift + i] for i in range(0, kr_n, 2)], axis=1) for h in range(2)]
            s = s + jnp.concatenate(bias, axis=0)
            probs.append(jnp.exp(s - jnp.max(s, axis=1, keepdims=True)).astype(BF16))
        for (q0, k0, shift, s), p in zip(chains, probs):
            pv = jnp.dot(p, va[pl.ds(k0, kr_n * GRID_W), :], preferred_element_type=F32)
            o = pv[:, :LANES] / pv[:, LANES:]
            g = g_ref[0, pl.ds(q0, GRID_W), :].astype(F32)
            o = _head_select(o[:GRID_W], o[GRID_W:]) * (g * jax.nn.sigmoid(g))
            o_ref[0, pl.ds(q0, GRID_W), :] = o.astype(o_ref.dtype)
        return carry

    lax.fori_loop(0, rows // NA_UNROLL, row_group, 0)


def _neighbourhood_attention(proj, bias, n_heads):
    _, b, seq, _ = proj.shape
    npair = n_heads // 2
    blk = lambda off: pl.BlockSpec((None, 1, seq, LANES), lambda j, i: (off * npair + j, i, 0, 0))
    return pl.pallas_call(
        _na_kernel,
        grid=(npair, b),
        in_specs=[blk(0), blk(1), blk(2), blk(3),
                  pl.BlockSpec((2,) + bias.shape[1:], lambda j, i: (j, 0, 0, 0))],
        out_specs=pl.BlockSpec((None, 1, seq, LANES), lambda j, i: (j, i, 0, 0)),
        out_shape=jax.ShapeDtypeStruct((npair, b, seq, LANES), BF16),
        scratch_shapes=[pltpu.VMEM((2, seq, LANES), BF16), pltpu.VMEM((seq, 2 * LANES), BF16)],
        compiler_params=_cparams(("parallel", "parallel")),
        name="neighbourhood_attention",
    )(proj, proj, proj, proj, bias)


def _out_kernel(n_y, n_t, *refs):
    n = n_y + n_t
    y_refs = refs[:n]
    w_refs = refs[n:2 * n]
    h_ref, p_ref, wp_ref, wg_ref, gpost_ref, gple_ref, gnext_ref, h_out, u_out = refs[2 * n:]
    t = None
    for idx, (y_ref, w_ref) in enumerate(zip(y_refs, w_refs)):
        if idx < n_y:
            y = jnp.concatenate([y_ref[c] for c in range(y_ref.shape[0])], axis=1)
            part = jnp.dot(y, w_ref[...], preferred_element_type=F32)
        else:
            part = lax.dot_general(y_ref[0], w_ref[...], (((0,), (0,)), ((), ())), preferred_element_type=F32)
        t = part if t is None else t + part
    h = h_ref[...] + _rms(t, gpost_ref[...])
    e = _rms(jnp.dot(p_ref[...].astype(BF16), wp_ref[...], preferred_element_type=F32), gple_ref[...])
    gate = jax.nn.sigmoid(jnp.dot(h.astype(BF16), wg_ref[...], preferred_element_type=F32))
    h = h + gate * e
    h_out[...] = h
    u_out[...] = _rms(h, gnext_ref[...]).astype(u_out.dtype)


def _out_block(ys, ys_t, ws, h, p, layer, w_ple, w_gate, g_post, g_ple, g_next, tm=512):
    m, d = h.shape
    row = lambda a: pl.BlockSpec((tm, a.shape[1]), lambda i: (i, 0))
    slab = lambda a: pl.BlockSpec((a.shape[0], tm, LANES), lambda i: (0, i, 0))
    p_spec = pl.BlockSpec((None, tm, p.shape[2]), lambda i: (layer, i, 0))
    full = lambda a: pl.BlockSpec(a.shape, lambda i: (0, 0))
    vec = lambda g: g.reshape(1, d)

    def col(a):
        tiles = a.shape[2] // tm
        return pl.BlockSpec((1, a.shape[1], tm), lambda i: (i // tiles, 0, i % tiles))

    return pl.pallas_call(
        functools.partial(_out_kernel, len(ys), len(ys_t)),
        grid=(m // tm,),
        in_specs=[slab(y) for y in ys] + [col(y) for y in ys_t] + [full(w) for w in ws]
                 + [row(h), p_spec, full(w_ple), full(w_gate), full(vec(g_post)), full(vec(g_ple)), full(vec(g_next))],
        out_specs=[pl.BlockSpec((tm, d), lambda i: (i, 0)), pl.BlockSpec((tm, d), lambda i: (i, 0))],
        out_shape=[jax.ShapeDtypeStruct((m, d), F32), jax.ShapeDtypeStruct((m, d), BF16)],
        compiler_params=_cparams(("parallel",)),
        name="out_proj_residual_ple",
    )(*ys, *ys_t, *ws, h, p, w_ple, w_gate, vec(g_post), vec(g_ple), vec(g_next))


def kernel(x, p, w_in, w_out, norm_pre, norm_post, hyena_conv_w, hyena_conv_b, hyena_w1, hyena_b1, hyena_w2, hyena_b2, hyena_w3, hyena_b3, hyena_w4, hyena_freq, hyena_skip, rel_bias, na_rpb, ple_proj, ple_norm, ple_gate):
    b, seq, d = x.shape
    depth = w_in.shape[0]
    d_inner = w_out.shape[1]
    hw = d_inner // 2
    m = b * seq
    consts = _fft_consts(seq)
    dil_bias = _dilated_bias(rel_bias)
    h = x.reshape(m, d)
    u = _rmsnorm_bf16(h, norm_pre[0])
    w_all = w_in.astype(BF16)
    p_all = p.reshape(depth, m, -1)
    for i in range(depth):
        j = i // 2
        if i % 2 == 0:
            w_a = w_all[i, :, :4 * hw].T
            ident = jnp.tile(jnp.array([0.0, 1.0, 0.0], F32)[:, None], (1, hw))
            conv_w = jnp.concatenate([hyena_conv_w[j], ident], axis=1).T
            conv_b = jnp.concatenate([hyena_conv_b[j], jnp.zeros((hw,), F32)])[:, None]
            hy = _proj_t(w_a, u.reshape(b, seq, d), conv_w, conv_b)
            halves = _hyena_filters_t(seq, hw, hyena_w1[j], hyena_b1[j], hyena_w2[j], hyena_b2[j],
                                      hyena_w3[j], hyena_b3[j], hyena_w4[j], hyena_freq[j])
            kr, ki = _filter_spectrum(halves, consts)
            skip = hyena_skip[j][:, :, None, None]
            z = _hyena_conv(False, hy, 0, hy, hw, hw, skip[0], kr, ki, 0, consts)
            ya_t = _hyena_conv(True, z, 0, hy, 2 * hw, 3 * hw, skip[1], kr, ki, 1, consts)
            proj_b = _matmul_bf16(u, w_all[i:i + 1, :, 4 * hw:], 0, 0, 4 * hw).reshape(-1, b, seq, LANES)
            yb = _dilated_attention(proj_b, dil_bias, hw // HEAD_DIM)
            ys, ys_t = [yb.reshape(-1, m, LANES)], [ya_t]
            ws = [w_out[i, hw:].astype(BF16), w_out[i, :hw].astype(BF16)]
        else:
            proj = _matmul_bf16(u, w_all, i, 0, 4 * d_inner).reshape(-1, b, seq, LANES)
            yc = _neighbourhood_attention(proj, _na_bias(na_rpb[j], seq // GRID_W), d_inner // HEAD_DIM)
            ys, ys_t = [yc.reshape(-1, m, LANES)], []
            ws = [w_out[i].astype(BF16)]
        g_next = norm_pre[i + 1] if i + 1 < depth else norm_pre[i]
        h, u = _out_block(ys, ys_t, ws, h, p_all, i, ple_proj[i].astype(BF16), ple_gate[i].astype(BF16),
                          norm_post[i], ple_norm[i], g_next)
    return h.reshape(b, seq, d)
```

```python
import functools
import math

import numpy as np
import jax
import jax.numpy as jnp
from jax import lax
from jax.experimental import pallas as pl
from jax.experimental.pallas import tpu as pltpu

F32 = jnp.float32
BF16 = jnp.bfloat16

HEAD_DIM = 64
LANES = 128
HYENA_EMB_DIM = 33
HYENA_FAST_DECAY = 0.3
HYENA_SLOW_DECAY = 1.5
HYENA_TARGET = 1e-2
DILATED_PATTERNS = ((128, 1), (512, 4), (2048, 16))
GRID_W = 64
NA_ROWS = 8
NA_COLS = 16
REL_BUCKETS = 32
REL_MAX_DIST = 1024
RMS_EPS = 1e-6
NEG_INF = -1e30
ATT_QB = 128
ATT_HALF = 64
ATT_KW = ATT_QB + 2 * ATT_HALF
ATT_UNROLL = 8
ATT_STEP = 4
assert all(b[1] == a[1] * ATT_STEP for a, b in zip(DILATED_PATTERNS, DILATED_PATTERNS[1:])) and DILATED_PATTERNS[0][1] == 1
NA_BATCH = 4
NA_UNROLL = 16
VMEM_LIMIT = 56 * 1024 * 1024


def _cparams(sem):
    return pltpu.CompilerParams(dimension_semantics=sem, vmem_limit_bytes=VMEM_LIMIT)


def _rms(x, g):
    return x * lax.rsqrt(jnp.mean(x * x, axis=-1, keepdims=True) + RMS_EPS) * g


def _rmsnorm_kernel(x_ref, g_ref, o_ref):
    o_ref[...] = _rms(x_ref[...], g_ref[...]).astype(o_ref.dtype)


def _rmsnorm_bf16(x2d, g, tm=1024):
    m, d = x2d.shape
    return pl.pallas_call(
        _rmsnorm_kernel,
        grid=(m // tm,),
        in_specs=[pl.BlockSpec((tm, d), lambda i: (i, 0)), pl.BlockSpec((1, d), lambda i: (0, 0))],
        out_specs=pl.BlockSpec((tm, d), lambda i: (i, 0)),
        out_shape=jax.ShapeDtypeStruct((m, d), BF16),
        compiler_params=_cparams(("parallel",)),
        name="rmsnorm",
    )(x2d, g.reshape(1, d))


def _matmul_kernel(a_ref, b_ref, o_ref):
    res = jnp.dot(a_ref[...], b_ref[...], preferred_element_type=F32).astype(o_ref.dtype)
    for c in range(o_ref.shape[0]):
        o_ref[c] = res[:, c * LANES:(c + 1) * LANES]


def _matmul_bf16(a, w, layer, col0, n, tm=1024, tn=1024):
    m, k = a.shape
    return pl.pallas_call(
        _matmul_kernel,
        grid=(m // tm, n // tn),
        in_specs=[pl.BlockSpec((tm, k), lambda i, j: (i, 0)), pl.BlockSpec((None, k, tn), lambda i, j: (layer, 0, col0 // tn + j))],
        out_specs=pl.BlockSpec((tn // LANES, tm, LANES), lambda i, j: (j, i, 0)),
        out_shape=jax.ShapeDtypeStruct((n // LANES, m, LANES), BF16),
        compiler_params=_cparams(("parallel", "parallel")),
        name="in_proj",
    )(a, w)


def _proj_t_kernel(w_ref, u_ref, cw_ref, cb_ref, o_ref):
    p = lax.dot_general(w_ref[...], u_ref[0], (((1,), (1,)), ((), ())), preferred_element_type=F32)
    seq = p.shape[1]
    t = lax.broadcasted_iota(jnp.int32, p.shape, 1)
    prev = jnp.where(t == 0, 0.0, pltpu.roll(p, 1, axis=1))
    nxt = jnp.where(t == seq - 1, 0.0, pltpu.roll(p, seq - 1, axis=1))
    cw = cw_ref[...]
    o_ref[0] = (prev * cw[:, 0:1] + p * cw[:, 1:2] + nxt * cw[:, 2:3] + cb_ref[...]).astype(o_ref.dtype)


def _proj_t(w_t, u, conv_w, conv_b, tc=256):
    c, d = w_t.shape
    b, seq, _ = u.shape
    return pl.pallas_call(
        _proj_t_kernel,
        grid=(b, c // tc),
        in_specs=[
            pl.BlockSpec((tc, d), lambda i, j: (j, 0)),
            pl.BlockSpec((1, seq, d), lambda i, j: (i, 0, 0)),
            pl.BlockSpec((tc, 3), lambda i, j: (j, 0)),
            pl.BlockSpec((tc, 1), lambda i, j: (j, 0)),
        ],
        out_specs=pl.BlockSpec((1, tc, seq), lambda i, j: (i, j, 0)),
        out_shape=jax.ShapeDtypeStruct((b, c, seq), BF16),
        compiler_params=_cparams(("parallel", "parallel")),
        name="in_proj_t_conv3",
    )(w_t, u, conv_w, conv_b)


def _hyena_filter_kernel(z_ref, w1_ref, b1_ref, w2_ref, b2_ref, w3_ref, b3_ref, fq_ref, w4_ref, dl_ref, t_ref,
                         o_ref, a_scr):
    hp = lax.Precision.HIGHEST

    @pl.when((pl.program_id(0) == 0) & (pl.program_id(1) == 0))
    def _():
        for d in range(2):
            a = z_ref[d]
            for w_ref, b_ref in ((w1_ref, b1_ref), (w2_ref, b2_ref), (w3_ref, b3_ref)):
                a = jnp.sin(fq_ref[...] * (jnp.dot(w_ref[...], a, precision=hp, preferred_element_type=F32) + b_ref[...]))
            a_scr[d] = a

    seq = o_ref.shape[-1]
    proj = lambda d: jnp.dot(w4_ref[0, d].astype(BF16), a_scr[d].astype(BF16), preferred_element_type=F32)
    lo = proj(0) * jnp.exp(-t_ref[0] * dl_ref[...])
    hi = proj(1) * jnp.exp(-t_ref[1] * dl_ref[...])
    hi = jnp.where(lax.broadcasted_iota(jnp.int32, (1, seq), 1) == 0, 0.0, hi)
    norm = jnp.sum(jnp.abs(lo), axis=1, keepdims=True) + jnp.sum(jnp.abs(hi), axis=1, keepdims=True)
    o_ref[0, 0] = (lo / norm).astype(o_ref.dtype)
    o_ref[1, 0] = (hi / norm).astype(o_ref.dtype)


def _hyena_filters_t(seq, c, w1, b1, w2, b2, w3, b3, w4, freq, tc=256):
    tc = min(tc, c)
    fo = w2.shape[0]
    emb = -(-HYENA_EMB_DIM // 8) * 8
    t = jnp.linspace(0.0, 1.0, seq, dtype=F32)
    bands = (HYENA_EMB_DIM - 1) // 2
    fr = jnp.linspace(1e-4, bands - 1, bands, dtype=F32)[:, None]
    mirror = (seq - jnp.arange(seq)) % seq
    tables, times = [], []
    for pos, tp in ((jnp.arange(seq), t), (mirror, jnp.roll(t[::-1], 1))):
        wpos = (2.0 * math.pi / seq) * pos.astype(F32)[None, :]
        tp = tp[None, :]
        z = jnp.concatenate([tp, jnp.cos(fr * wpos), -jnp.sin(fr * wpos),
                             jnp.zeros((emb - HYENA_EMB_DIM, seq), F32)], axis=0)
        tables.append(z)
        times.append(tp)
    z = jnp.stack(tables)
    tt = jnp.stack(times)
    w1_t = jnp.pad(w1.astype(F32).T, ((0, 0), (0, emb - HYENA_EMB_DIM)))
    w4_t = jnp.transpose(w4.astype(F32).reshape(fo, 2, 2, c), (1, 2, 3, 0))
    min_decay = math.log(HYENA_TARGET) / HYENA_SLOW_DECAY
    max_decay = math.log(HYENA_TARGET) / HYENA_FAST_DECAY
    deltas = jnp.abs(jnp.linspace(min_decay, max_decay, c, dtype=F32))[:, None]
    col = lambda v: v.astype(F32)[:, None]
    full = lambda a: pl.BlockSpec(a.shape, lambda f, i: (0,) * a.ndim)
    args = (z, w1_t, col(b1), w2.astype(F32).T, col(b2), w3.astype(F32).T, col(b3), col(freq))
    return pl.pallas_call(
        _hyena_filter_kernel,
        grid=(2, c // tc),
        in_specs=[full(a) for a in args]
                 + [pl.BlockSpec((1, 2, tc, fo), lambda f, i: (f, 0, i, 0)),
                    pl.BlockSpec((tc, 1), lambda f, i: (i, 0)), full(tt)],
        out_specs=pl.BlockSpec((2, 1, tc, seq), lambda f, i: (0, f, i, 0)),
        out_shape=jax.ShapeDtypeStruct((2, 2, c, seq), BF16),
        scratch_shapes=[pltpu.VMEM((2, fo, seq), F32)],
        compiler_params=_cparams(("arbitrary", "arbitrary")),
        name="hyena_filter_mlp",
    )(*args, w4_t, deltas, tt)


def _fft_consts(seq):
    n_fft = 2 * seq
    n1 = n_fft // LANES
    nb = n1 // 2 + 1
    kp = -(-nb // 8) * 8
    k1 = jnp.arange(kp, dtype=jnp.int32)
    keep = (k1 < nb)[:, None]
    two_pi = 2.0 * math.pi

    th = ((k1[:, None] * jnp.arange(n1, dtype=jnp.int32)[None, :]) % n1).astype(F32) * (two_pi / n1)
    f1 = jnp.concatenate([jnp.where(keep, jnp.cos(th), 0.0), jnp.where(keep, -jnp.sin(th), 0.0)], axis=0)
    ph = (k1[:, None] * jnp.arange(LANES, dtype=jnp.int32)[None, :]).astype(F32) * (two_pi / n_fft)
    tw = jnp.stack([jnp.cos(ph), -jnp.sin(ph)])
    om = ((jnp.arange(LANES, dtype=jnp.int32)[:, None] * jnp.arange(LANES, dtype=jnp.int32)[None, :]) % LANES
          ).astype(F32) * (two_pi / LANES)
    f2 = jnp.concatenate([jnp.cos(om), -jnp.sin(om)], axis=1)
    f2i = jnp.concatenate([jnp.cos(om), jnp.sin(om)], axis=1)
    weight = jnp.where((k1 == 0) | (k1 == nb - 1), 1.0, 2.0)[:, None]
    g = jnp.concatenate([jnp.where(keep, weight * jnp.cos(th[:, :n1 // 2]), 0.0),
                         jnp.where(keep, -weight * jnp.sin(th[:, :n1 // 2]), 0.0)], axis=0).T
    return dict(f1=f1.astype(BF16), tw=tw, f2=f2.astype(BF16), f2i=f2i.astype(BF16), g=g.astype(BF16), kp=kp)


def _per_channel_matmul(mat_ref, planes, out_scr, ct):
    def body(i, carry):
        c = 2 * i
        pair = jnp.concatenate([planes(c), planes(c + 1)], axis=1)
        res = jnp.dot(mat_ref[...], pair, preferred_element_type=F32)
        out_scr[c] = res[:, :LANES]
        out_scr[c + 1] = res[:, LANES:]
        return carry
    lax.fori_loop(0, ct // 2, body, 0, unroll=16)


def _twiddle_stage2(a, tw_ref, f2_ref):
    ct, kp2, _ = a.shape
    kp = kp2 // 2
    ar, ai = a[:, :kp], a[:, kp:]
    twr, twi = tw_ref[0], tw_ref[1]
    st = jnp.concatenate([ar * twr - ai * twi, ar * twi + ai * twr], axis=1).astype(BF16)
    p = jnp.dot(st.reshape(ct * kp2, LANES), f2_ref[...], preferred_element_type=F32).reshape(ct, kp2, 2 * LANES)
    return p[:, :kp, :LANES] - p[:, kp:, LANES:], p[:, :kp, LANES:] + p[:, kp:, :LANES]


def _stage2_twiddle_inv(yr, yi, tw_ref, f2i_ref):
    ct, kp, _ = yr.shape
    st = jnp.concatenate([yr, yi], axis=1).astype(BF16)
    p = jnp.dot(st.reshape(ct * 2 * kp, LANES), f2i_ref[...], preferred_element_type=F32).reshape(ct, 2 * kp, 2 * LANES)
    tr, ti = p[:, :kp, :LANES] - p[:, kp:, LANES:], p[:, :kp, LANES:] + p[:, kp:, :LANES]
    twr, twi = tw_ref[0], tw_ref[1]
    return jnp.concatenate([tr * twr + ti * twi, ti * twr - tr * twi], axis=1).astype(BF16)


def _filter_spec_kernel(lo_ref, hi_ref, f1_ref, tw_ref, f2_ref, kr_ref, ki_ref, a_scr):
    ct = lo_ref.shape[2]
    planes = lambda c: jnp.concatenate([lo_ref[0, 0, c], hi_ref[0, 0, c]], axis=0)
    _per_channel_matmul(f1_ref, planes, a_scr, ct)
    xr, xi = _twiddle_stage2(a_scr[...], tw_ref, f2_ref)
    n_fft = f1_ref.shape[1] * LANES
    kr_ref[0] = xr * (1.0 / n_fft)
    ki_ref[0] = xi * (1.0 / n_fft)


def _filter_spectrum(halves, consts, ct=64):
    _, nf, c, seq = halves.shape
    ct = min(ct, c)
    kp = consts["kp"]
    h5 = halves.reshape(2, nf, c, seq // LANES, LANES)
    half = lambda s: pl.BlockSpec((1, 1, ct, seq // LANES, LANES), lambda f, i: (s, f, i, 0, 0))
    full = lambda a: pl.BlockSpec(a.shape, lambda f, i: (0,) * a.ndim)
    spec = jax.ShapeDtypeStruct((nf, c, kp, LANES), F32)
    out = pl.BlockSpec((1, ct, kp, LANES), lambda f, i: (f, i, 0, 0))
    return pl.pallas_call(
        _filter_spec_kernel,
        grid=(nf, c // ct),
        in_specs=[half(0), half(1), full(consts["f1"]), full(consts["tw"]), full(consts["f2"])],
        out_specs=[out, out],
        out_shape=[spec, spec],
        scratch_shapes=[pltpu.VMEM((ct, 2 * kp, LANES), F32)],
        compiler_params=_cparams(("parallel", "parallel")),
        name="hyena_filter_spectrum",
    )(h5, h5, consts["f1"], consts["tw"], consts["f2"])


def _hyena_conv_kernel(second, x_ref, m_ref, e_ref, skip_ref, kr_ref, ki_ref, f1_ref, tw_ref, f2_ref, f2i_ref, g_ref,
                       o_ref, a_scr, t_scr, y_scr):
    ct = x_ref.shape[1]
    _per_channel_matmul(f1_ref, lambda c: x_ref[0, c], a_scr, ct)
    xr, xi = _twiddle_stage2(a_scr[...], tw_ref, f2_ref)
    kr, ki = kr_ref[0], ki_ref[0]
    t_scr[...] = _stage2_twiddle_inv(xr * kr - xi * ki, xr * ki + xi * kr, tw_ref, f2i_ref)

    _per_channel_matmul(g_ref, lambda c: t_scr[c], y_scr, ct)

    out = m_ref[0].astype(F32) * (y_scr[...] + x_ref[0].astype(F32) * skip_ref[...])
    if second:
        e = e_ref[0].astype(F32)
        out = out * (e * jax.nn.sigmoid(e))
    o_ref[0] = out.astype(o_ref.dtype)


def _hyena_conv(second, x, x_row0, hy, m_row0, e_row0, skip, kr, ki, filt, consts, ct=64):
    b, _, seq = x.shape
    c = kr.shape[1]
    ct = min(ct, c)
    kp = consts["kp"]
    n1h = seq // LANES
    view = lambda a: a.reshape(a.shape[0], a.shape[1], n1h, LANES)
    rows = lambda blk0: pl.BlockSpec((1, ct, n1h, LANES), lambda j, i: (i, blk0 + j, 0, 0))
    full = lambda a: pl.BlockSpec(a.shape, lambda j, i: (0,) * a.ndim)
    kspec = pl.BlockSpec((1, ct, kp, LANES), lambda j, i: (filt, j, 0, 0))
    mats = [consts["f1"][:, :n1h], consts["tw"], consts["f2"], consts["f2i"], consts["g"]]
    out = pl.pallas_call(
        functools.partial(_hyena_conv_kernel, second),
        grid=(c // ct, b),
        in_specs=[rows(x_row0 // ct), rows(m_row0 // ct), rows(e_row0 // ct),
                  pl.BlockSpec((ct, 1, 1), lambda j, i: (j, 0, 0)),
                  kspec, kspec] + [full(a) for a in mats],
        out_specs=pl.BlockSpec((1, ct, n1h, LANES), lambda j, i: (i, j, 0, 0)),
        out_shape=jax.ShapeDtypeStruct((b, c, n1h, LANES), BF16),
        scratch_shapes=[pltpu.VMEM((ct, 2 * kp, LANES), F32), pltpu.VMEM((ct, 2 * kp, LANES), BF16),
                        pltpu.VMEM((ct, n1h, LANES), F32)],
        compiler_params=_cparams(("parallel", "parallel")),
        name="hyena_conv2" if second else "hyena_conv1",
    )(view(x), view(hy), view(hy), skip, kr, ki, *mats)
    return out.reshape(b, c, seq)


def _t5_bucket(rel):
    half_b = REL_BUCKETS // 2
    max_exact = half_b // 2
    ret = jnp.where(rel > 0, half_b, 0)
    n = jnp.abs(rel)
    nf = jnp.maximum(n, 1).astype(F32)
    large = max_exact + (jnp.log(nf / max_exact) / math.log(REL_MAX_DIST / max_exact)
                         * (half_b - max_exact)).astype(jnp.int32)
    large = jnp.minimum(large, half_b - 1)
    return ret + jnp.where(n < max_exact, n, large)


def _dilated_bias(rel_bias):
    rel = jnp.arange(ATT_KW)[None, :] - ATT_HALF - jnp.arange(ATT_QB)[:, None]
    col = jnp.arange(ATT_KW)
    edge = jnp.stack([col >= ATT_HALF, col >= 0, col < ATT_KW - ATT_HALF])
    out = []
    for _, dil in DILATED_PATTERNS:
        onehot = jax.nn.one_hot(_t5_bucket(rel * dil), REL_BUCKETS, dtype=F32)
        bias = jnp.einsum("qkb,bh->qkh", onehot, rel_bias.astype(F32), precision=lax.Precision.HIGHEST)
        bias = jnp.where((jnp.abs(rel) <= ATT_HALF)[:, :, None], bias, NEG_INF)
        bias = jnp.transpose(bias, (2, 0, 1))[:, None]
        out.append(jnp.where(edge[None, :, None, :], bias, NEG_INF))
    return jnp.stack(out)


def _head_select(first, second):
    lane = lax.broadcasted_iota(jnp.int32, first.shape[:-1] + (LANES,), len(first.shape) - 1)
    return jnp.where(lane < HEAD_DIM, first, second)


def _dilated_kernel(q_ref, k_ref, v_ref, g_ref, bias_ref, o_ref, *scratch):
    stage_a, stage_b = scratch[0:3], scratch[3:6]
    qd, kd, vd = scratch[6:9]
    last = len(DILATED_PATTERNS) - 1
    states = [scratch[9 + 3 * p:12 + 3 * p] for p in range(last)]
    seq = q_ref.shape[1]
    stage_a[0][...] = q_ref[0].astype(F32) * (HEAD_DIM ** -0.5)
    stage_a[1][...] = k_ref[0].astype(F32)
    stage_a[2][...] = v_ref[0].astype(F32)
    zero_halo = jnp.zeros((ATT_HALF, LANES), BF16)
    vd[:, LANES:] = jnp.ones((vd.shape[0], LANES), BF16)

    src, dst = stage_a, stage_b
    for pi, (_, dil) in enumerate(DILATED_PATTERNS):
        n = seq // dil
        nblk = n // ATT_QB
        if pi > 0:
            prev_dil = dil // ATT_STEP
            for r1 in range(prev_dil):
                for r2 in range(ATT_STEP):
                    r = r1 + prev_dil * r2
                    for a_src, a_dst in zip(src, dst):
                        a_dst[r * n:(r + 1) * n, :] = a_src[pl.ds(r1 * n * ATT_STEP + r2, n, stride=ATT_STEP), :]
            src, dst = dst, src
        stride_k = n + 2 * ATT_HALF
        q_all = src[0][...]
        lane_s = lax.broadcasted_iota(jnp.int32, (seq, LANES), 1)
        for h in range(2):
            qd[h] = jnp.where((lane_s < HEAD_DIM) == (h == 0), q_all, 0.0).astype(BF16)
        for r in range(dil):
            base = r * stride_k
            for a_src, a_dst in ((src[1], kd), (src[2], vd.at[:, :LANES])):
                a_dst[base:base + ATT_HALF, :] = zero_halo
                a_dst[base + ATT_HALF:base + ATT_HALF + n, :] = a_src[r * n:(r + 1) * n, :].astype(BF16)
                a_dst[base + ATT_HALF + n:base + stride_k, :] = zero_halo
        if pi == last:
            states.append(dst)
        acc_w, mx_w, den_w = states[pi]

        def block_group(it, carry, pi=pi, nblk=nblk, acc_w=acc_w, mx_w=mx_w, den_w=den_w):
            chains = []
            for u in range(ATT_UNROLL):
                blk = it * ATT_UNROLL + u
                r = blk // nblk
                qb = blk % nblk
                q0 = pl.multiple_of(blk * ATT_QB, ATT_QB)
                k0 = pl.multiple_of(q0 + r * (2 * ATT_HALF), ATT_QB)
                edge = jnp.where(qb == 0, 0, jnp.where(qb == nblk - 1, 2, 1))
                qcat = jnp.concatenate([qd[0, pl.ds(q0, ATT_QB), :], qd[1, pl.ds(q0, ATT_QB), :]], axis=0)
                s = lax.dot_general(qcat, kd[pl.ds(k0, ATT_KW), :], (((1,), (1,)), ((), ())),
                                    preferred_element_type=F32)
                chains.append((q0, k0, edge, s))
            probs = []
            for q0, k0, edge, s in chains:
                s = s + jnp.concatenate([bias_ref[pi, 0, edge], bias_ref[pi, 1, edge]], axis=0)
                m = jnp.max(s, axis=1, keepdims=True)
                probs.append((jnp.exp(s - m).astype(BF16), m))
            for (q0, k0, edge, s), (p, m) in zip(chains, probs):
                pv = jnp.dot(p, vd[pl.ds(k0, ATT_KW), :], preferred_element_type=F32)
                acc_w[pl.ds(q0, ATT_QB), :] = _head_select(pv[:ATT_QB, :LANES], pv[ATT_QB:, :LANES])
                mx_w[pl.ds(q0, ATT_QB), :] = _head_select(m[:ATT_QB], m[ATT_QB:])
                den_w[pl.ds(q0, ATT_QB), :] = _head_select(pv[:ATT_QB, LANES:], pv[ATT_QB:, LANES:])
            return carry

        lax.fori_loop(0, dil * nblk // ATT_UNROLL, block_group, 0)

    for pi in range(len(DILATED_PATTERNS) - 1, 0, -1):
        dil = DILATED_PATTERNS[pi][1]
        n = seq // dil
        prev_dil = dil // ATT_STEP
        acc_c, mx_c, den_c = states[pi - 1]
        acc_f, mx_f, den_f = states[pi]
        for r1 in range(prev_dil):
            for r2 in range(ATT_STEP):
                r = r1 + prev_dil * r2
                rows = pl.ds(r1 * n * ATT_STEP + r2, n, stride=ATT_STEP)
                fine = slice(r * n, (r + 1) * n)
                m_old = mx_c[rows, :]
                m_new = mx_f[fine, :]
                m_all = jnp.maximum(m_old, m_new)
                decay = jnp.exp(-jnp.abs(m_old - m_new))
                a_old = jnp.where(m_old >= m_new, 1.0, decay)
                a_new = jnp.where(m_old >= m_new, decay, 1.0)
                acc_c[rows, :] = acc_c[rows, :] * a_old + acc_f[fine, :] * a_new
                den_c[rows, :] = den_c[rows, :] * a_old + den_f[fine, :] * a_new
                mx_c[rows, :] = m_all

    acc, _, den = states[0]
    g = g_ref[0].astype(F32)
    o_ref[0] = (acc[...] / den[...] * (g * jax.nn.sigmoid(g))).astype(o_ref.dtype)


def _dilated_attention(proj, bias, n_heads):
    _, b, seq, _ = proj.shape
    npair = n_heads // 2
    max_dil = max(d for _, d in DILATED_PATTERNS)
    kd_rows = seq + 2 * ATT_HALF * max_dil
    blk = lambda off: pl.BlockSpec((None, 1, seq, LANES), lambda i, j: (off * npair + j, i, 0, 0))
    state = pltpu.VMEM((seq, LANES), F32)
    return pl.pallas_call(
        _dilated_kernel,
        grid=(b, npair),
        in_specs=[blk(0), blk(1), blk(2), blk(3),
                  pl.BlockSpec((len(DILATED_PATTERNS), 2, 3, ATT_QB, ATT_KW), lambda i, j: (0, j, 0, 0, 0))],
        out_specs=pl.BlockSpec((None, 1, seq, LANES), lambda i, j: (j, i, 0, 0)),
        out_shape=jax.ShapeDtypeStruct((npair, b, seq, LANES), BF16),
        scratch_shapes=[state] * 6
                       + [pltpu.VMEM((2, seq, LANES), BF16), pltpu.VMEM((kd_rows, LANES), BF16),
                          pltpu.VMEM((kd_rows, 2 * LANES), BF16)]
                       + [state] * (3 * (len(DILATED_PATTERNS) - 1)),
        compiler_params=_cparams(("parallel", "parallel")),
        name="dilated_attention",
    )(proj, proj, proj, proj, bias)


def _na_bias(rpb, rows):
    assert min(NA_ROWS, rows) % 2 == 0
    cols = np.arange(GRID_W)
    cs = np.clip(cols - NA_COLS // 2, 0, GRID_W - NA_COLS)
    kc = np.arange(GRID_W)
    inside = (kc[None, :] >= cs[:, None]) & (kc[None, :] < cs[:, None] + NA_COLS)
    col_idx = kc[None, :] - cols[:, None] + NA_COLS - 1
    onehot = (col_idx[:, :, None] == np.arange(2 * NA_COLS - 1)[None, None, :]).astype(np.float32)
    e = jnp.einsum("hrd,cqd->hrcq", rpb.astype(F32), jnp.asarray(onehot), precision=lax.Precision.HIGHEST)
    e = jnp.where(jnp.asarray(inside)[None, None], e, NEG_INF)
    return jnp.concatenate([e[:, :-1], e[:, 1:]], axis=-1)


def _na_kernel(q_ref, k_ref, v_ref, g_ref, bias_ref, o_ref, qm, va):
    seq = q_ref.shape[1]
    rows = seq // GRID_W
    kr_n = min(NA_ROWS, rows)
    lane = lax.broadcasted_iota(jnp.int32, (seq, LANES), 1)
    q = q_ref[0] * (HEAD_DIM ** -0.5)
    for h in range(2):
        qm[h] = jnp.where((lane < HEAD_DIM) == (h == 0), q, jnp.zeros_like(q))
    va[:, :LANES] = v_ref[0]
    va[:, LANES:] = jnp.ones((seq, LANES), BF16)

    def row_group(it, carry):
        def scores(batch):
            out = []
            for u in range(batch * NA_BATCH, (batch + 1) * NA_BATCH):
                r = it * NA_UNROLL + u
                rs = jnp.clip(r - kr_n // 2, 0, rows - kr_n)
                shift = rs - r + NA_ROWS - 1
                q0 = pl.multiple_of(r * GRID_W, GRID_W)
                k0 = pl.multiple_of(rs * GRID_W, GRID_W)
                qcat = jnp.concatenate([qm[0, pl.ds(q0, GRID_W), :], qm[1, pl.ds(q0, GRID_W), :]], axis=0)
                s = lax.dot_general(qcat, k_ref[0, pl.ds(k0, kr_n * GRID_W), :], (((1,), (1,)), ((), ())),
                                    preferred_element_type=F32)
                out.append((q0, k0, shift, s))
            return out

        def finish(chains):
            probs = []
            for q0, k0, shift, s in chains:
                bias = [jnp.concatenate([bias_ref[h, shift + i] for i in range(0, kr_n, 2)], axis=1)
                        for h in range(2)]
                s = s + jnp.concatenate(bias, axis=0)
                probs.append(jnp.exp(s - jnp.max(s, axis=1, keepdims=True)).astype(BF16))
            for (q0, k0, shift, s), p in zip(chains, probs):
                pv = jnp.dot(p, va[pl.ds(k0, kr_n * GRID_W), :], preferred_element_type=F32)
                o = pv[:, :LANES] / pv[:, LANES:]
                g = g_ref[0, pl.ds(q0, GRID_W), :].astype(F32)
                o = _head_select(o[:GRID_W], o[GRID_W:]) * (g * jax.nn.sigmoid(g))
                o_ref[0, pl.ds(q0, GRID_W), :] = o.astype(o_ref.dtype)

        n_batches = NA_UNROLL // NA_BATCH
        pending = scores(0)
        for batch in range(n_batches):
            ahead = scores(batch + 1) if batch + 1 < n_batches else None
            finish(pending)
            pending = ahead
        return carry

    lax.fori_loop(0, rows // NA_UNROLL, row_group, 0)


def _neighbourhood_attention(proj, bias, n_heads):
    _, b, seq, _ = proj.shape
    npair = n_heads // 2
    blk = lambda off: pl.BlockSpec((None, 1, seq, LANES), lambda j, i: (off * npair + j, i, 0, 0))
    return pl.pallas_call(
        _na_kernel,
        grid=(npair, b),
        in_specs=[blk(0), blk(1), blk(2), blk(3),
                  pl.BlockSpec((2,) + bias.shape[1:], lambda j, i: (j, 0, 0, 0))],
        out_specs=pl.BlockSpec((None, 1, seq, LANES), lambda j, i: (j, i, 0, 0)),
        out_shape=jax.ShapeDtypeStruct((npair, b, seq, LANES), BF16),
        scratch_shapes=[pltpu.VMEM((2, seq, LANES), BF16), pltpu.VMEM((seq, 2 * LANES), BF16)],
        compiler_params=_cparams(("parallel", "parallel")),
        name="neighbourhood_attention",
    )(proj, proj, proj, proj, bias)


def _out_kernel(n_y, n_t, *refs):
    n = n_y + n_t
    y_refs = refs[:n]
    w_refs = refs[n:2 * n]
    h_ref, p_ref, wp_ref, wg_ref, gpost_ref, gple_ref, gnext_ref, h_out, u_out = refs[2 * n:]
    t = None
    for idx, (y_ref, w_ref) in enumerate(zip(y_refs, w_refs)):
        if idx < n_y:
            y = jnp.concatenate([y_ref[c] for c in range(y_ref.shape[0])], axis=1)
            part = jnp.dot(y, w_ref[...], preferred_element_type=F32)
        else:
            part = lax.dot_general(y_ref[0], w_ref[...], (((0,), (0,)), ((), ())), preferred_element_type=F32)
        t = part if t is None else t + part
    h = h_ref[...] + _rms(t, gpost_ref[...])
    e = _rms(jnp.dot(p_ref[...].astype(BF16), wp_ref[...], preferred_element_type=F32), gple_ref[...])
    gate = jax.nn.sigmoid(jnp.dot(h.astype(BF16), wg_ref[...], preferred_element_type=F32))
    h = h + gate * e
    h_out[...] = h
    u_out[...] = _rms(h, gnext_ref[...]).astype(u_out.dtype)


def _out_block(ys, ys_t, ws, h, p, layer, w_ple, w_gate, g_post, g_ple, g_next, tm=512):
    m, d = h.shape
    row = lambda a: pl.BlockSpec((tm, a.shape[1]), lambda i: (i, 0))
    slab = lambda a: pl.BlockSpec((a.shape[0], tm, LANES), lambda i: (0, i, 0))
    p_spec = pl.BlockSpec((None, tm, p.shape[2]), lambda i: (layer, i, 0))
    full = lambda a: pl.BlockSpec(a.shape, lambda i: (0, 0))
    vec = lambda g: g.reshape(1, d)

    def col(a):
        tiles = a.shape[2] // tm
        return pl.BlockSpec((1, a.shape[1], tm), lambda i: (i // tiles, 0, i % tiles))

    return pl.pallas_call(
        functools.partial(_out_kernel, len(ys), len(ys_t)),
        grid=(m // tm,),
        in_specs=[slab(y) for y in ys] + [col(y) for y in ys_t] + [full(w) for w in ws]
                 + [row(h), p_spec, full(w_ple), full(w_gate), full(vec(g_post)), full(vec(g_ple)), full(vec(g_next))],
        out_specs=[pl.BlockSpec((tm, d), lambda i: (i, 0)), pl.BlockSpec((tm, d), lambda i: (i, 0))],
        out_shape=[jax.ShapeDtypeStruct((m, d), F32), jax.ShapeDtypeStruct((m, d), BF16)],
        compiler_params=_cparams(("parallel",)),
        name="out_proj_residual_ple",
    )(*ys, *ys_t, *ws, h, p, w_ple, w_gate, vec(g_post), vec(g_ple), vec(g_next))


def kernel(x, p, w_in, w_out, norm_pre, norm_post, hyena_conv_w, hyena_conv_b, hyena_w1, hyena_b1, hyena_w2, hyena_b2, hyena_w3, hyena_b3, hyena_w4, hyena_freq, hyena_skip, rel_bias, na_rpb, ple_proj, ple_norm, ple_gate):
    b, seq, d = x.shape
    depth = w_in.shape[0]
    d_inner = w_out.shape[1]
    hw = d_inner // 2
    m = b * seq
    consts = _fft_consts(seq)
    dil_bias = _dilated_bias(rel_bias)
    h = x.reshape(m, d)
    u = _rmsnorm_bf16(h, norm_pre[0])
    w_all = w_in.astype(BF16)
    p_all = p.reshape(depth, m, -1)
    for i in range(depth):
        j = i // 2
        if i % 2 == 0:
            w_a = w_all[i, :, :4 * hw].T
            ident = jnp.tile(jnp.array([0.0, 1.0, 0.0], F32)[:, None], (1, hw))
            conv_w = jnp.concatenate([hyena_conv_w[j], ident], axis=1).T
            conv_b = jnp.concatenate([hyena_conv_b[j], jnp.zeros((hw,), F32)])[:, None]
            hy = _proj_t(w_a, u.reshape(b, seq, d), conv_w, conv_b)
            halves = _hyena_filters_t(seq, hw, hyena_w1[j], hyena_b1[j], hyena_w2[j], hyena_b2[j],
                                      hyena_w3[j], hyena_b3[j], hyena_w4[j], hyena_freq[j])
            kr, ki = _filter_spectrum(halves, consts)
            skip = hyena_skip[j][:, :, None, None]
            z = _hyena_conv(False, hy, 0, hy, hw, hw, skip[0], kr, ki, 0, consts)
            ya_t = _hyena_conv(True, z, 0, hy, 2 * hw, 3 * hw, skip[1], kr, ki, 1, consts)
            proj_b = _matmul_bf16(u, w_all[i:i + 1, :, 4 * hw:], 0, 0, 4 * hw).reshape(-1, b, seq, LANES)
            yb = _dilated_attention(proj_b, dil_bias, hw // HEAD_DIM)
            ys, ys_t = [yb.reshape(-1, m, LANES)], [ya_t]
            ws = [w_out[i, hw:].astype(BF16), w_out[i, :hw].astype(BF16)]
        else:
            proj = _matmul_bf16(u, w_all, i, 0, 4 * d_inner).reshape(-1, b, seq, LANES)
            yc = _neighbourhood_attention(proj, _na_bias(na_rpb[j], seq // GRID_W), d_inner // HEAD_DIM)
            ys, ys_t = [yc.reshape(-1, m, LANES)], []
            ws = [w_out[i].astype(BF16)]
        g_next = norm_pre[i + 1] if i + 1 < depth else norm_pre[i]
        h, u = _out_block(ys, ys_t, ws, h, p_all, i, ple_proj[i].astype(BF16), ple_gate[i].astype(BF16),
                          norm_post[i], ple_norm[i], g_next)
    return h.reshape(b, seq, d)
```

```python
import functools
import math

import numpy as np
import jax
import jax.numpy as jnp
from jax import lax
from jax.experimental import pallas as pl
from jax.experimental.pallas import tpu as pltpu

F32 = jnp.float32
BF16 = jnp.bfloat16

HEAD_DIM = 64
LANES = 128
HYENA_EMB_DIM = 33
HYENA_FAST_DECAY = 0.3
HYENA_SLOW_DECAY = 1.5
HYENA_TARGET = 1e-2
DILATED_PATTERNS = ((128, 1), (512, 4), (2048, 16))
GRID_W = 64
NA_ROWS = 8
NA_COLS = 16
REL_BUCKETS = 32
REL_MAX_DIST = 1024
RMS_EPS = 1e-6
NEG_INF = -1e30
ATT_QB = 128
ATT_HALF = 64
ATT_KW = ATT_QB + 2 * ATT_HALF
ATT_UNROLL = 8
ATT_STEP = 4
assert all(b[1] == a[1] * ATT_STEP for a, b in zip(DILATED_PATTERNS, DILATED_PATTERNS[1:])) and DILATED_PATTERNS[0][1] == 1
NA_BATCH = 4
NA_UNROLL = 16
VMEM_LIMIT = 56 * 1024 * 1024


def _cparams(sem):
    return pltpu.CompilerParams(dimension_semantics=sem, vmem_limit_bytes=VMEM_LIMIT)


def _rms(x, g):
    return x * lax.rsqrt(jnp.mean(x * x, axis=-1, keepdims=True) + RMS_EPS) * g


def _rmsnorm_kernel(x_ref, g_ref, o_ref):
    o_ref[...] = _rms(x_ref[...], g_ref[...]).astype(o_ref.dtype)


def _rmsnorm_bf16(x2d, g, tm=1024):
    m, d = x2d.shape
    return pl.pallas_call(
        _rmsnorm_kernel,
        grid=(m // tm,),
        in_specs=[pl.BlockSpec((tm, d), lambda i: (i, 0)), pl.BlockSpec((1, d), lambda i: (0, 0))],
        out_specs=pl.BlockSpec((tm, d), lambda i: (i, 0)),
        out_shape=jax.ShapeDtypeStruct((m, d), BF16),
        compiler_params=_cparams(("parallel",)),
        name="rmsnorm",
    )(x2d, g.reshape(1, d))


def _matmul_kernel(a_ref, b_ref, o_ref):
    res = jnp.dot(a_ref[...], b_ref[...], preferred_element_type=F32).astype(o_ref.dtype)
    for c in range(o_ref.shape[0]):
        o_ref[c] = res[:, c * LANES:(c + 1) * LANES]


def _matmul_bf16(a, w, layer, col0, n, tm=1024, tn=1024):
    m, k = a.shape
    return pl.pallas_call(
        _matmul_kernel,
        grid=(m // tm, n // tn),
        in_specs=[pl.BlockSpec((tm, k), lambda i, j: (i, 0)), pl.BlockSpec((None, k, tn), lambda i, j: (layer, 0, col0 // tn + j))],
        out_specs=pl.BlockSpec((tn // LANES, tm, LANES), lambda i, j: (j, i, 0)),
        out_shape=jax.ShapeDtypeStruct((n // LANES, m, LANES), BF16),
        compiler_params=_cparams(("parallel", "parallel")),
        name="in_proj",
    )(a, w)


def _proj_t_kernel(w_ref, u_ref, cw_ref, cb_ref, o_ref):
    p = lax.dot_general(w_ref[...], u_ref[0], (((1,), (1,)), ((), ())), preferred_element_type=F32)
    seq = p.shape[1]
    t = lax.broadcasted_iota(jnp.int32, p.shape, 1)
    prev = jnp.where(t == 0, 0.0, pltpu.roll(p, 1, axis=1))
    nxt = jnp.where(t == seq - 1, 0.0, pltpu.roll(p, seq - 1, axis=1))
    cw = cw_ref[...]
    o_ref[0] = (prev * cw[:, 0:1] + p * cw[:, 1:2] + nxt * cw[:, 2:3] + cb_ref[...]).astype(o_ref.dtype)


def _proj_t(w_t, u, conv_w, conv_b, tc=256):
    c, d = w_t.shape
    b, seq, _ = u.shape
    return pl.pallas_call(
        _proj_t_kernel,
        grid=(b, c // tc),
        in_specs=[
            pl.BlockSpec((tc, d), lambda i, j: (j, 0)),
            pl.BlockSpec((1, seq, d), lambda i, j: (i, 0, 0)),
            pl.BlockSpec((tc, 3), lambda i, j: (j, 0)),
            pl.BlockSpec((tc, 1), lambda i, j: (j, 0)),
        ],
        out_specs=pl.BlockSpec((1, tc, seq), lambda i, j: (i, j, 0)),
        out_shape=jax.ShapeDtypeStruct((b, c, seq), BF16),
        compiler_params=_cparams(("parallel", "parallel")),
        name="in_proj_t_conv3",
    )(w_t, u, conv_w, conv_b)


def _hyena_filter_kernel(z_ref, w1_ref, b1_ref, w2_ref, b2_ref, w3_ref, b3_ref, fq_ref, w4_ref, dl_ref, t_ref,
                         o_ref, a_scr):
    hp = lax.Precision.HIGHEST

    @pl.when((pl.program_id(0) == 0) & (pl.program_id(1) == 0))
    def _():
        for d in range(2):
            a = z_ref[d]
            for w_ref, b_ref in ((w1_ref, b1_ref), (w2_ref, b2_ref), (w3_ref, b3_ref)):
                a = jnp.sin(fq_ref[...] * (jnp.dot(w_ref[...], a, precision=hp, preferred_element_type=F32) + b_ref[...]))
            a_scr[d] = a

    seq = o_ref.shape[-1]
    proj = lambda d: jnp.dot(w4_ref[0, d].astype(BF16), a_scr[d].astype(BF16), preferred_element_type=F32)
    lo = proj(0) * jnp.exp(-t_ref[0] * dl_ref[...])
    hi = proj(1) * jnp.exp(-t_ref[1] * dl_ref[...])
    hi = jnp.where(lax.broadcasted_iota(jnp.int32, (1, seq), 1) == 0, 0.0, hi)
    norm = jnp.sum(jnp.abs(lo), axis=1, keepdims=True) + jnp.sum(jnp.abs(hi), axis=1, keepdims=True)
    o_ref[0, 0] = (lo / norm).astype(o_ref.dtype)
    o_ref[1, 0] = (hi / norm).astype(o_ref.dtype)


def _hyena_filters_t(seq, c, w1, b1, w2, b2, w3, b3, w4, freq, tc=256):
    tc = min(tc, c)
    fo = w2.shape[0]
    emb = -(-HYENA_EMB_DIM // 8) * 8
    t = jnp.linspace(0.0, 1.0, seq, dtype=F32)
    bands = (HYENA_EMB_DIM - 1) // 2
    fr = jnp.linspace(1e-4, bands - 1, bands, dtype=F32)[:, None]
    mirror = (seq - jnp.arange(seq)) % seq
    tables, times = [], []
    for pos, tp in ((jnp.arange(seq), t), (mirror, jnp.roll(t[::-1], 1))):
        wpos = (2.0 * math.pi / seq) * pos.astype(F32)[None, :]
        tp = tp[None, :]
        z = jnp.concatenate([tp, jnp.cos(fr * wpos), -jnp.sin(fr * wpos),
                             jnp.zeros((emb - HYENA_EMB_DIM, seq), F32)], axis=0)
        tables.append(z)
        times.append(tp)
    z = jnp.stack(tables)
    tt = jnp.stack(times)
    w1_t = jnp.pad(w1.astype(F32).T, ((0, 0), (0, emb - HYENA_EMB_DIM)))
    w4_t = jnp.transpose(w4.astype(F32).reshape(fo, 2, 2, c), (1, 2, 3, 0))
    min_decay = math.log(HYENA_TARGET) / HYENA_SLOW_DECAY
    max_decay = math.log(HYENA_TARGET) / HYENA_FAST_DECAY
    deltas = jnp.abs(jnp.linspace(min_decay, max_decay, c, dtype=F32))[:, None]
    col = lambda v: v.astype(F32)[:, None]
    full = lambda a: pl.BlockSpec(a.shape, lambda f, i: (0,) * a.ndim)
    args = (z, w1_t, col(b1), w2.astype(F32).T, col(b2), w3.astype(F32).T, col(b3), col(freq))
    return pl.pallas_call(
        _hyena_filter_kernel,
        grid=(2, c // tc),
        in_specs=[full(a) for a in args]
                 + [pl.BlockSpec((1, 2, tc, fo), lambda f, i: (f, 0, i, 0)),
                    pl.BlockSpec((tc, 1), lambda f, i: (i, 0)), full(tt)],
        out_specs=pl.BlockSpec((2, 1, tc, seq), lambda f, i: (0, f, i, 0)),
        out_shape=jax.ShapeDtypeStruct((2, 2, c, seq), BF16),
        scratch_shapes=[pltpu.VMEM((2, fo, seq), F32)],
        compiler_params=_cparams(("arbitrary", "arbitrary")),
        name="hyena_filter_mlp",
    )(*args, w4_t, deltas, tt)


def _fft_consts(seq):
    n_fft = 2 * seq
    n1 = n_fft // LANES
    nb = n1 // 2 + 1
    kp = -(-nb // 8) * 8
    k1 = jnp.arange(kp, dtype=jnp.int32)
    keep = (k1 < nb)[:, None]
    two_pi = 2.0 * math.pi

    th = ((k1[:, None] * jnp.arange(n1, dtype=jnp.int32)[None, :]) % n1).astype(F32) * (two_pi / n1)
    f1 = jnp.concatenate([jnp.where(keep, jnp.cos(th), 0.0), jnp.where(keep, -jnp.sin(th), 0.0)], axis=0)
    ph = (k1[:, None] * jnp.arange(LANES, dtype=jnp.int32)[None, :]).astype(F32) * (two_pi / n_fft)
    tw = jnp.stack([jnp.cos(ph), -jnp.sin(ph)])
    om = ((jnp.arange(LANES, dtype=jnp.int32)[:, None] * jnp.arange(LANES, dtype=jnp.int32)[None, :]) % LANES
          ).astype(F32) * (two_pi / LANES)
    f2 = jnp.concatenate([jnp.cos(om), -jnp.sin(om)], axis=1)
    f2i = jnp.concatenate([jnp.cos(om), jnp.sin(om)], axis=1)
    weight = jnp.where((k1 == 0) | (k1 == nb - 1), 1.0, 2.0)[:, None]
    g = jnp.concatenate([jnp.where(keep, weight * jnp.cos(th[:, :n1 // 2]), 0.0),
                         jnp.where(keep, -weight * jnp.sin(th[:, :n1 // 2]), 0.0)], axis=0).T
    return dict(f1=f1.astype(BF16), tw=tw, f2=f2.astype(BF16), f2i=f2i.astype(BF16), g=g.astype(BF16), kp=kp)


def _per_channel_matmul(mat_ref, planes, out_scr, ct):
    def body(i, carry):
        c = 2 * i
        pair = jnp.concatenate([planes(c), planes(c + 1)], axis=1)
        res = jnp.dot(mat_ref[...], pair, preferred_element_type=F32)
        out_scr[c] = res[:, :LANES]
        out_scr[c + 1] = res[:, LANES:]
        return carry
    lax.fori_loop(0, ct // 2, body, 0, unroll=16)


def _twiddle_stage2(a, tw_ref, f2_ref):
    ct, kp2, _ = a.shape
    kp = kp2 // 2
    ar, ai = a[:, :kp], a[:, kp:]
    twr, twi = tw_ref[0], tw_ref[1]
    st = jnp.concatenate([ar * twr - ai * twi, ar * twi + ai * twr], axis=1).astype(BF16)
    p = jnp.dot(st.reshape(ct * kp2, LANES), f2_ref[...], preferred_element_type=F32).reshape(ct, kp2, 2 * LANES)
    return p[:, :kp, :LANES] - p[:, kp:, LANES:], p[:, :kp, LANES:] + p[:, kp:, :LANES]


def _stage2_twiddle_inv(yr, yi, tw_ref, f2i_ref):
    ct, kp, _ = yr.shape
    st = jnp.concatenate([yr, yi], axis=1).astype(BF16)
    p = jnp.dot(st.reshape(ct * 2 * kp, LANES), f2i_ref[...], preferred_element_type=F32).reshape(ct, 2 * kp, 2 * LANES)
    tr, ti = p[:, :kp, :LANES] - p[:, kp:, LANES:], p[:, :kp, LANES:] + p[:, kp:, :LANES]
    twr, twi = tw_ref[0], tw_ref[1]
    return jnp.concatenate([tr * twr + ti * twi, ti * twr - tr * twi], axis=1).astype(BF16)


def _filter_spec_kernel(lo_ref, hi_ref, f1_ref, tw_ref, f2_ref, kr_ref, ki_ref, a_scr):
    ct = lo_ref.shape[2]
    planes = lambda c: jnp.concatenate([lo_ref[0, 0, c], hi_ref[0, 0, c]], axis=0)
    _per_channel_matmul(f1_ref, planes, a_scr, ct)
    xr, xi = _twiddle_stage2(a_scr[...], tw_ref, f2_ref)
    n_fft = f1_ref.shape[1] * LANES
    kr_ref[0] = xr * (1.0 / n_fft)
    ki_ref[0] = xi * (1.0 / n_fft)


def _filter_spectrum(halves, consts, ct=64):
    _, nf, c, seq = halves.shape
    ct = min(ct, c)
    kp = consts["kp"]
    h5 = halves.reshape(2, nf, c, seq // LANES, LANES)
    half = lambda s: pl.BlockSpec((1, 1, ct, seq // LANES, LANES), lambda f, i: (s, f, i, 0, 0))
    full = lambda a: pl.BlockSpec(a.shape, lambda f, i: (0,) * a.ndim)
    spec = jax.ShapeDtypeStruct((nf, c, kp, LANES), F32)
    out = pl.BlockSpec((1, ct, kp, LANES), lambda f, i: (f, i, 0, 0))
    return pl.pallas_call(
        _filter_spec_kernel,
        grid=(nf, c // ct),
        in_specs=[half(0), half(1), full(consts["f1"]), full(consts["tw"]), full(consts["f2"])],
        out_specs=[out, out],
        out_shape=[spec, spec],
        scratch_shapes=[pltpu.VMEM((ct, 2 * kp, LANES), F32)],
        compiler_params=_cparams(("parallel", "parallel")),
        name="hyena_filter_spectrum",
    )(h5, h5, consts["f1"], consts["tw"], consts["f2"])


def _hyena_conv_kernel(second, x_ref, m_ref, e_ref, skip_ref, kr_ref, ki_ref, f1_ref, tw_ref, f2_ref, f2i_ref, g_ref,
                       o_ref, a_scr, t_scr, y_scr):
    ct = x_ref.shape[1]
    _per_channel_matmul(f1_ref, lambda c: x_ref[0, c], a_scr, ct)
    xr, xi = _twiddle_stage2(a_scr[...], tw_ref, f2_ref)
    kr, ki = kr_ref[0], ki_ref[0]
    t_scr[...] = _stage2_twiddle_inv(xr * kr - xi * ki, xr * ki + xi * kr, tw_ref, f2i_ref)

    _per_channel_matmul(g_ref, lambda c: t_scr[c], y_scr, ct)

    out = m_ref[0].astype(F32) * (y_scr[...] + x_ref[0].astype(F32) * skip_ref[...])
    if second:
        e = e_ref[0].astype(F32)
        out = out * (e * jax.nn.sigmoid(e))
    o_ref[0] = out.astype(o_ref.dtype)


def _hyena_conv(second, x, x_row0, hy, m_row0, e_row0, skip, kr, ki, filt, consts, ct=64):
    b, _, seq = x.shape
    c = kr.shape[1]
    ct = min(ct, c)
    kp = consts["kp"]
    n1h = seq // LANES
    view = lambda a: a.reshape(a.shape[0], a.shape[1], n1h, LANES)
    rows = lambda blk0: pl.BlockSpec((1, ct, n1h, LANES), lambda j, i: (i, blk0 + j, 0, 0))
    full = lambda a: pl.BlockSpec(a.shape, lambda j, i: (0,) * a.ndim)
    kspec = pl.BlockSpec((1, ct, kp, LANES), lambda j, i: (filt, j, 0, 0))
    mats = [consts["f1"][:, :n1h], consts["tw"], consts["f2"], consts["f2i"], consts["g"]]
    out = pl.pallas_call(
        functools.partial(_hyena_conv_kernel, second),
        grid=(c // ct, b),
        in_specs=[rows(x_row0 // ct), rows(m_row0 // ct), rows(e_row0 // ct),
                  pl.BlockSpec((ct, 1, 1), lambda j, i: (j, 0, 0)),
                  kspec, kspec] + [full(a) for a in mats],
        out_specs=pl.BlockSpec((1, ct, n1h, LANES), lambda j, i: (i, j, 0, 0)),
        out_shape=jax.ShapeDtypeStruct((b, c, n1h, LANES), BF16),
        scratch_shapes=[pltpu.VMEM((ct, 2 * kp, LANES), F32), pltpu.VMEM((ct, 2 * kp, LANES), BF16),
                        pltpu.VMEM((ct, n1h, LANES), F32)],
        compiler_params=_cparams(("parallel", "parallel")),
        name="hyena_conv2" if second else "hyena_conv1",
    )(view(x), view(hy), view(hy), skip, kr, ki, *mats)
    return out.reshape(b, c, seq)


def _t5_bucket(rel):
    half_b = REL_BUCKETS // 2
    max_exact = half_b // 2
    ret = jnp.where(rel > 0, half_b, 0)
    n = jnp.abs(rel)
    nf = jnp.maximum(n, 1).astype(F32)
    large = max_exact + (jnp.log(nf / max_exact) / math.log(REL_MAX_DIST / max_exact)
                         * (half_b - max_exact)).astype(jnp.int32)
    large = jnp.minimum(large, half_b - 1)
    return ret + jnp.where(n < max_exact, n, large)


def _dilated_bias(rel_bias):
    rel = jnp.arange(ATT_KW)[None, :] - ATT_HALF - jnp.arange(ATT_QB)[:, None]
    col = jnp.arange(ATT_KW)
    edge = jnp.stack([col >= ATT_HALF, col >= 0, col < ATT_KW - ATT_HALF])
    out = []
    for _, dil in DILATED_PATTERNS:
        onehot = jax.nn.one_hot(_t5_bucket(rel * dil), REL_BUCKETS, dtype=F32)
        bias = jnp.einsum("qkb,bh->qkh", onehot, rel_bias.astype(F32), precision=lax.Precision.HIGHEST)
        bias = jnp.where((jnp.abs(rel) <= ATT_HALF)[:, :, None], bias, NEG_INF)
        bias = jnp.transpose(bias, (2, 0, 1))[:, None]
        out.append(jnp.where(edge[None, :, None, :], bias, NEG_INF))
    return jnp.stack(out)


def _head_select(first, second):
    lane = lax.broadcasted_iota(jnp.int32, first.shape[:-1] + (LANES,), len(first.shape) - 1)
    return jnp.where(lane < HEAD_DIM, first, second)


def _dilated_kernel(q_ref, k_ref, v_ref, g_ref, bias_ref, o_ref, *scratch):
    stage_a, stage_b = scratch[0:3], scratch[3:6]
    qd, kd, vd = scratch[6:9]
    last = len(DILATED_PATTERNS) - 1
    states = [scratch[9 + 3 * p:12 + 3 * p] for p in range(last)]
    seq = q_ref.shape[1]
    stage_a[0][...] = q_ref[0].astype(F32) * (HEAD_DIM ** -0.5)
    stage_a[1][...] = k_ref[0].astype(F32)
    stage_a[2][...] = v_ref[0].astype(F32)
    zero_halo = jnp.zeros((ATT_HALF, LANES), BF16)
    first_head = lax.broadcasted_iota(jnp.int32, (ATT_QB, LANES), 1) < HEAD_DIM
    vd[:, LANES:] = jnp.ones((vd.shape[0], LANES), BF16)

    src, dst = stage_a, stage_b
    for pi, (_, dil) in enumerate(DILATED_PATTERNS):
        n = seq // dil
        nblk = n // ATT_QB
        if pi > 0:
            prev_dil = dil // ATT_STEP
            for r1 in range(prev_dil):
                for r2 in range(ATT_STEP):
                    r = r1 + prev_dil * r2
                    for a_src, a_dst in zip(src, dst):
                        a_dst[r * n:(r + 1) * n, :] = a_src[pl.ds(r1 * n * ATT_STEP + r2, n, stride=ATT_STEP), :]
            src, dst = dst, src
        stride_k = n + 2 * ATT_HALF
        qd[...] = src[0][...].astype(BF16)
        for r in range(dil):
            base = r * stride_k
            for a_src, a_dst in ((src[1], kd), (src[2], vd.at[:, :LANES])):
                a_dst[base:base + ATT_HALF, :] = zero_halo
                a_dst[base + ATT_HALF:base + ATT_HALF + n, :] = a_src[r * n:(r + 1) * n, :].astype(BF16)
                a_dst[base + ATT_HALF + n:base + stride_k, :] = zero_halo
        if pi == last:
            states.append(dst)
        acc_w, mx_w, den_w = states[pi]

        def block_group(it, carry, pi=pi, nblk=nblk, acc_w=acc_w, mx_w=mx_w, den_w=den_w):
            chains = []
            for u in range(ATT_UNROLL):
                blk = it * ATT_UNROLL + u
                r = blk // nblk
                qb = blk % nblk
                q0 = pl.multiple_of(blk * ATT_QB, ATT_QB)
                k0 = pl.multiple_of(q0 + r * (2 * ATT_HALF), ATT_QB)
                edge = jnp.where(qb == 0, 0, jnp.where(qb == nblk - 1, 2, 1))
                qblk = qd[pl.ds(q0, ATT_QB), :]
                qcat = jnp.concatenate([jnp.where(first_head, qblk, jnp.zeros_like(qblk)),
                                        jnp.where(first_head, jnp.zeros_like(qblk), qblk)], axis=0)
                s = lax.dot_general(qcat, kd[pl.ds(k0, ATT_KW), :], (((1,), (1,)), ((), ())),
                                    preferred_element_type=F32)
                chains.append((q0, k0, edge, s))
            probs = []
            for q0, k0, edge, s in chains:
                s = s + jnp.concatenate([bias_ref[pi, 0, edge], bias_ref[pi, 1, edge]], axis=0)
                m = jnp.max(s, axis=1, keepdims=True)
                probs.append((jnp.exp(s - m).astype(BF16), m))
            for (q0, k0, edge, s), (p, m) in zip(chains, probs):
                pv = jnp.dot(p, vd[pl.ds(k0, ATT_KW), :], preferred_element_type=F32)
                acc_w[pl.ds(q0, ATT_QB), :] = _head_select(pv[:ATT_QB, :LANES], pv[ATT_QB:, :LANES])
                mx_w[pl.ds(q0, ATT_QB), :] = _head_select(m[:ATT_QB], m[ATT_QB:])
                den_w[pl.ds(q0, ATT_QB), :] = _head_select(pv[:ATT_QB, LANES:], pv[ATT_QB:, LANES:])
            return carry

        lax.fori_loop(0, dil * nblk // ATT_UNROLL, block_group, 0)

    for pi in range(len(DILATED_PATTERNS) - 1, 0, -1):
        dil = DILATED_PATTERNS[pi][1]
        n = seq // dil
        prev_dil = dil // ATT_STEP
        acc_c, mx_c, den_c = states[pi - 1]
        acc_f, mx_f, den_f = states[pi]
        for r1 in range(prev_dil):
            for r2 in range(ATT_STEP):
                r = r1 + prev_dil * r2
                rows = pl.ds(r1 * n * ATT_STEP + r2, n, stride=ATT_STEP)
                fine = slice(r * n, (r + 1) * n)
                m_old = mx_c[rows, :]
                m_new = mx_f[fine, :]
                m_all = jnp.maximum(m_old, m_new)
                decay = jnp.exp(-jnp.abs(m_old - m_new))
                a_old = jnp.where(m_old >= m_new, 1.0, decay)
                a_new = jnp.where(m_old >= m_new, decay, 1.0)
                acc_c[rows, :] = acc_c[rows, :] * a_old + acc_f[fine, :] * a_new
                den_c[rows, :] = den_c[rows, :] * a_old + den_f[fine, :] * a_new
                mx_c[rows, :] = m_all

    acc, _, den = states[0]
    g = g_ref[0].astype(F32)
    o_ref[0] = (acc[...] / den[...] * (g * jax.nn.sigmoid(g))).astype(o_ref.dtype)


def _dilated_attention(proj, bias, n_heads):
    _, b, seq, _ = proj.shape
    npair = n_heads // 2
    max_dil = max(d for _, d in DILATED_PATTERNS)
    kd_rows = seq + 2 * ATT_HALF * max_dil
    blk = lambda off: pl.BlockSpec((None, 1, seq, LANES), lambda i, j: (off * npair + j, i, 0, 0))
    state = pltpu.VMEM((seq, LANES), F32)
    return pl.pallas_call(
        _dilated_kernel,
        grid=(b, npair),
        in_specs=[blk(0), blk(1), blk(2), blk(3),
                  pl.BlockSpec((len(DILATED_PATTERNS), 2, 3, ATT_QB, ATT_KW), lambda i, j: (0, j, 0, 0, 0))],
        out_specs=pl.BlockSpec((None, 1, seq, LANES), lambda i, j: (j, i, 0, 0)),
        out_shape=jax.ShapeDtypeStruct((npair, b, seq, LANES), BF16),
        scratch_shapes=[state] * 6
                       + [pltpu.VMEM((seq, LANES), BF16), pltpu.VMEM((kd_rows, LANES), BF16),
                          pltpu.VMEM((kd_rows, 2 * LANES), BF16)]
                       + [state] * (3 * (len(DILATED_PATTERNS) - 1)),
        compiler_params=_cparams(("parallel", "parallel")),
        name="dilated_attention",
    )(proj, proj, proj, proj, bias)


def _na_bias(rpb, rows):
    assert min(NA_ROWS, rows) % 2 == 0
    cols = np.arange(GRID_W)
    cs = np.clip(cols - NA_COLS // 2, 0, GRID_W - NA_COLS)
    kc = np.arange(GRID_W)
    inside = (kc[None, :] >= cs[:, None]) & (kc[None, :] < cs[:, None] + NA_COLS)
    col_idx = kc[None, :] - cols[:, None] + NA_COLS - 1
    onehot = (col_idx[:, :, None] == np.arange(2 * NA_COLS - 1)[None, None, :]).astype(np.float32)
    e = jnp.einsum("hrd,cqd->hrcq", rpb.astype(F32), jnp.asarray(onehot), precision=lax.Precision.HIGHEST)
    e = jnp.where(jnp.asarray(inside)[None, None], e, NEG_INF)
    return jnp.concatenate([e[:, :-1], e[:, 1:]], axis=-1)


def _na_kernel(q_ref, k_ref, v_ref, g_ref, bias_ref, o_ref, qm):
    seq = q_ref.shape[1]
    rows = seq // GRID_W
    kr_n = min(NA_ROWS, rows)
    lane = lax.broadcasted_iota(jnp.int32, (seq, LANES), 1)
    q = q_ref[0] * (HEAD_DIM ** -0.5)
    for h in range(2):
        qm[h] = jnp.where((lane < HEAD_DIM) == (h == 0), q, jnp.zeros_like(q))

    def row_group(it, carry):
        def scores(batch):
            out = []
            for u in range(batch * NA_BATCH, (batch + 1) * NA_BATCH):
                r = it * NA_UNROLL + u
                rs = jnp.clip(r - kr_n // 2, 0, rows - kr_n)
                shift = rs - r + NA_ROWS - 1
                q0 = pl.multiple_of(r * GRID_W, GRID_W)
                k0 = pl.multiple_of(rs * GRID_W, GRID_W)
                qcat = jnp.concatenate([qm[0, pl.ds(q0, GRID_W), :], qm[1, pl.ds(q0, GRID_W), :]], axis=0)
                s = lax.dot_general(qcat, k_ref[0, pl.ds(k0, kr_n * GRID_W), :], (((1,), (1,)), ((), ())),
                                    preferred_element_type=F32)
                out.append((q0, k0, shift, s))
            return out

        def finish(chains):
            probs = []
            for q0, k0, shift, s in chains:
                bias = [jnp.concatenate([bias_ref[h, shift + i] for i in range(0, kr_n, 2)], axis=1)
                        for h in range(2)]
                s = s + jnp.concatenate(bias, axis=0)
                p = jnp.exp(s - jnp.max(s, axis=1, keepdims=True))
                probs.append((p.astype(BF16), jnp.sum(p, axis=1, keepdims=True)))
            for (q0, k0, shift, s), (p, l) in zip(chains, probs):
                o = jnp.dot(p, v_ref[0, pl.ds(k0, kr_n * GRID_W), :], preferred_element_type=F32) / l
                g = g_ref[0, pl.ds(q0, GRID_W), :].astype(F32)
                o = _head_select(o[:GRID_W], o[GRID_W:]) * (g * jax.nn.sigmoid(g))
                o_ref[0, pl.ds(q0, GRID_W), :] = o.astype(o_ref.dtype)

        n_batches = NA_UNROLL // NA_BATCH
        pending = scores(0)
        for batch in range(n_batches):
            ahead = scores(batch + 1) if batch + 1 < n_batches else None
            finish(pending)
            pending = ahead
        return carry

    lax.fori_loop(0, rows // NA_UNROLL, row_group, 0)


def _neighbourhood_attention(proj, bias, n_heads):
    _, b, seq, _ = proj.shape
    npair = n_heads // 2
    blk = lambda off: pl.BlockSpec((None, 1, seq, LANES), lambda j, i: (off * npair + j, i, 0, 0))
    return pl.pallas_call(
        _na_kernel,
        grid=(npair, b),
        in_specs=[blk(0), blk(1), blk(2), blk(3),
                  pl.BlockSpec((2,) + bias.shape[1:], lambda j, i: (j, 0, 0, 0))],
        out_specs=pl.BlockSpec((None, 1, seq, LANES), lambda j, i: (j, i, 0, 0)),
        out_shape=jax.ShapeDtypeStruct((npair, b, seq, LANES), BF16),
        scratch_shapes=[pltpu.VMEM((2, seq, LANES), BF16)],
        compiler_params=_cparams(("parallel", "parallel")),
        name="neighbourhood_attention",
    )(proj, proj, proj, proj, bias)


def _out_kernel(n_y, n_t, *refs):
    n = n_y + n_t
    y_refs = refs[:n]
    w_refs = refs[n:2 * n]
    h_ref, p_ref, wp_ref, wg_ref, gpost_ref, gple_ref, gnext_ref, h_out, u_out = refs[2 * n:]
    t = None
    for idx, (y_ref, w_ref) in enumerate(zip(y_refs, w_refs)):
        if idx < n_y:
            y = jnp.concatenate([y_ref[c] for c in range(y_ref.shape[0])], axis=1)
            part = jnp.dot(y, w_ref[...], preferred_element_type=F32)
        else:
            part = lax.dot_general(y_ref[0], w_ref[...], (((0,), (0,)), ((), ())), preferred_element_type=F32)
        t = part if t is None else t + part
    h = h_ref[...] + _rms(t, gpost_ref[...])
    e = _rms(jnp.dot(p_ref[...].astype(BF16), wp_ref[...], preferred_element_type=F32), gple_ref[...])
    gate = jax.nn.sigmoid(jnp.dot(h.astype(BF16), wg_ref[...], preferred_element_type=F32))
    h = h + gate * e
    h_out[...] = h
    u_out[...] = _rms(h, gnext_ref[...]).astype(u_out.dtype)


def _out_block(ys, ys_t, ws, h, p, layer, w_ple, w_gate, g_post, g_ple, g_next, tm=512):
    m, d = h.shape
    row = lambda a: pl.BlockSpec((tm, a.shape[1]), lambda i: (i, 0))
    slab = lambda a: pl.BlockSpec((a.shape[0], tm, LANES), lambda i: (0, i, 0))
    p_spec = pl.BlockSpec((None, tm, p.shape[2]), lambda i: (layer, i, 0))
    full = lambda a: pl.BlockSpec(a.shape, lambda i: (0, 0))
    vec = lambda g: g.reshape(1, d)

    def col(a):
        tiles = a.shape[2] // tm
        return pl.BlockSpec((1, a.shape[1], tm), lambda i: (i // tiles, 0, i % tiles))

    return pl.pallas_call(
        functools.partial(_out_kernel, len(ys), len(ys_t)),
        grid=(m // tm,),
        in_specs=[slab(y) for y in ys] + [col(y) for y in ys_t] + [full(w) for w in ws]
                 + [row(h), p_spec, full(w_ple), full(w_gate), full(vec(g_post)), full(vec(g_ple)), full(vec(g_next))],
        out_specs=[pl.BlockSpec((tm, d), lambda i: (i, 0)), pl.BlockSpec((tm, d), lambda i: (i, 0))],
        out_shape=[jax.ShapeDtypeStruct((m, d), F32), jax.ShapeDtypeStruct((m, d), BF16)],
        compiler_params=_cparams(("parallel",)),
        name="out_proj_residual_ple",
    )(*ys, *ys_t, *ws, h, p, w_ple, w_gate, vec(g_post), vec(g_ple), vec(g_next))


def kernel(x, p, w_in, w_out, norm_pre, norm_post, hyena_conv_w, hyena_conv_b, hyena_w1, hyena_b1, hyena_w2, hyena_b2, hyena_w3, hyena_b3, hyena_w4, hyena_freq, hyena_skip, rel_bias, na_rpb, ple_proj, ple_norm, ple_gate):
    b, seq, d = x.shape
    depth = w_in.shape[0]
    d_inner = w_out.shape[1]
    hw = d_inner // 2
    m = b * seq
    consts = _fft_consts(seq)
    dil_bias = _dilated_bias(rel_bias)
    h = x.reshape(m, d)
    u = _rmsnorm_bf16(h, norm_pre[0])
    w_all = w_in.astype(BF16)
    p_all = p.reshape(depth, m, -1)
    for i in range(depth):
        j = i // 2
        if i % 2 == 0:
            w_a = w_all[i, :, :4 * hw].T
            ident = jnp.tile(jnp.array([0.0, 1.0, 0.0], F32)[:, None], (1, hw))
            conv_w = jnp.concatenate([hyena_conv_w[j], ident], axis=1).T
            conv_b = jnp.concatenate([hyena_conv_b[j], jnp.zeros((hw,), F32)])[:, None]
            hy = _proj_t(w_a, u.reshape(b, seq, d), conv_w, conv_b)
            halves = _hyena_filters_t(seq, hw, hyena_w1[j], hyena_b1[j], hyena_w2[j], hyena_b2[j],
                                      hyena_w3[j], hyena_b3[j], hyena_w4[j], hyena_freq[j])
            kr, ki = _filter_spectrum(halves, consts)
            skip = hyena_skip[j][:, :, None, None]
            z = _hyena_conv(False, hy, 0, hy, hw, hw, skip[0], kr, ki, 0, consts)
            ya_t = _hyena_conv(True, z, 0, hy, 2 * hw, 3 * hw, skip[1], kr, ki, 1, consts)
            proj_b = _matmul_bf16(u, w_all[i:i + 1, :, 4 * hw:], 0, 0, 4 * hw).reshape(-1, b, seq, LANES)
            yb = _dilated_attention(proj_b, dil_bias, hw // HEAD_DIM)
            ys, ys_t = [yb.reshape(-1, m, LANES)], [ya_t]
            ws = [w_out[i, hw:].astype(BF16), w_out[i, :hw].astype(BF16)]
        else:
            proj = _matmul_bf16(u, w_all, i, 0, 4 * d_inner).reshape(-1, b, seq, LANES)
            yc = _neighbourhood_attention(proj, _na_bias(na_rpb[j], seq // GRID_W), d_inner // HEAD_DIM)
            ys, ys_t = [yc.reshape(-1, m, LANES)], []
            ws = [w_out[i].astype(BF16)]
        g_next = norm_pre[i + 1] if i + 1 < depth else norm_pre[i]
        h, u = _out_block(ys, ys_t, ws, h, p_all, i, ple_proj[i].astype(BF16), ple_gate[i].astype(BF16),
                          norm_post[i], ple_norm[i], g_next)
    return h.reshape(b, seq, d)
```

```python
import functools
import math

import numpy as np
import jax
import jax.numpy as jnp
from jax import lax
from jax.experimental import pallas as pl
from jax.experimental.pallas import tpu as pltpu

F32 = jnp.float32
BF16 = jnp.bfloat16

HEAD_DIM = 64
LANES = 128
HYENA_EMB_DIM = 33
HYENA_FAST_DECAY = 0.3
HYENA_SLOW_DECAY = 1.5
HYENA_TARGET = 1e-2
DILATED_PATTERNS = ((128, 1), (512, 4), (2048, 16))
GRID_W = 64
NA_ROWS = 8
NA_COLS = 16
REL_BUCKETS = 32
REL_MAX_DIST = 1024
RMS_EPS = 1e-6
NEG_INF = -1e30
ATT_QB = 128
ATT_HALF = 64
ATT_KW = ATT_QB + 2 * ATT_HALF
ATT_UNROLL = 8
ATT_STEP = 4
assert all(b[1] == a[1] * ATT_STEP for a, b in zip(DILATED_PATTERNS, DILATED_PATTERNS[1:])) and DILATED_PATTERNS[0][1] == 1
NA_BATCH = 4
NA_UNROLL = 16
VMEM_LIMIT = 56 * 1024 * 1024


def _cparams(sem):
    return pltpu.CompilerParams(dimension_semantics=sem, vmem_limit_bytes=VMEM_LIMIT)


def _rms(x, g):
    return x * lax.rsqrt(jnp.mean(x * x, axis=-1, keepdims=True) + RMS_EPS) * g


def _rmsnorm_kernel(x_ref, g_ref, o_ref):
    o_ref[...] = _rms(x_ref[...], g_ref[...]).astype(o_ref.dtype)


def _rmsnorm_bf16(x2d, g, tm=1024):
    m, d = x2d.shape
    return pl.pallas_call(
        _rmsnorm_kernel,
        grid=(m // tm,),
        in_specs=[pl.BlockSpec((tm, d), lambda i: (i, 0)), pl.BlockSpec((1, d), lambda i: (0, 0))],
        out_specs=pl.BlockSpec((tm, d), lambda i: (i, 0)),
        out_shape=jax.ShapeDtypeStruct((m, d), BF16),
        compiler_params=_cparams(("parallel",)),
        name="rmsnorm",
    )(x2d, g.reshape(1, d))


def _matmul_kernel(a_ref, b_ref, o_ref):
    res = jnp.dot(a_ref[...], b_ref[...], preferred_element_type=F32).astype(o_ref.dtype)
    for c in range(o_ref.shape[0]):
        o_ref[c] = res[:, c * LANES:(c + 1) * LANES]


def _matmul_bf16(a, w, layer, col0, n, tm=1024, tn=1024):
    m, k = a.shape
    return pl.pallas_call(
        _matmul_kernel,
        grid=(m // tm, n // tn),
        in_specs=[pl.BlockSpec((tm, k), lambda i, j: (i, 0)), pl.BlockSpec((None, k, tn), lambda i, j: (layer, 0, col0 // tn + j))],
        out_specs=pl.BlockSpec((tn // LANES, tm, LANES), lambda i, j: (j, i, 0)),
        out_shape=jax.ShapeDtypeStruct((n // LANES, m, LANES), BF16),
        compiler_params=_cparams(("parallel", "parallel")),
        name="in_proj",
    )(a, w)


def _proj_t_kernel(w_ref, u_ref, cw_ref, cb_ref, o_ref):
    p = lax.dot_general(w_ref[...], u_ref[0], (((1,), (1,)), ((), ())), preferred_element_type=F32)
    seq = p.shape[1]
    t = lax.broadcasted_iota(jnp.int32, p.shape, 1)
    prev = jnp.where(t == 0, 0.0, pltpu.roll(p, 1, axis=1))
    nxt = jnp.where(t == seq - 1, 0.0, pltpu.roll(p, seq - 1, axis=1))
    cw = cw_ref[...]
    o_ref[0] = (prev * cw[:, 0:1] + p * cw[:, 1:2] + nxt * cw[:, 2:3] + cb_ref[...]).astype(o_ref.dtype)


def _proj_t(w_t, u, conv_w, conv_b, tc=256):
    c, d = w_t.shape
    b, seq, _ = u.shape
    return pl.pallas_call(
        _proj_t_kernel,
        grid=(b, c // tc),
        in_specs=[
            pl.BlockSpec((tc, d), lambda i, j: (j, 0)),
            pl.BlockSpec((1, seq, d), lambda i, j: (i, 0, 0)),
            pl.BlockSpec((tc, 3), lambda i, j: (j, 0)),
            pl.BlockSpec((tc, 1), lambda i, j: (j, 0)),
        ],
        out_specs=pl.BlockSpec((1, tc, seq), lambda i, j: (i, j, 0)),
        out_shape=jax.ShapeDtypeStruct((b, c, seq), BF16),
        compiler_params=_cparams(("parallel", "parallel")),
        name="in_proj_t_conv3",
    )(w_t, u, conv_w, conv_b)


def _hyena_filter_kernel(z_ref, w1_ref, b1_ref, w2_ref, b2_ref, w3_ref, b3_ref, fq_ref, w4_ref, dl_ref, t_ref,
                         o_ref, a_scr):
    hp = lax.Precision.HIGHEST

    @pl.when((pl.program_id(0) == 0) & (pl.program_id(1) == 0))
    def _():
        for d in range(2):
            a = z_ref[d]
            for w_ref, b_ref in ((w1_ref, b1_ref), (w2_ref, b2_ref), (w3_ref, b3_ref)):
                a = jnp.sin(fq_ref[...] * (jnp.dot(w_ref[...], a, precision=hp, preferred_element_type=F32) + b_ref[...]))
            a_scr[d] = a

    seq = o_ref.shape[-1]
    proj = lambda d: jnp.dot(w4_ref[0, d].astype(BF16), a_scr[d].astype(BF16), preferred_element_type=F32)
    lo = proj(0) * jnp.exp(-t_ref[0] * dl_ref[...])
    hi = proj(1) * jnp.exp(-t_ref[1] * dl_ref[...])
    hi = jnp.where(lax.broadcasted_iota(jnp.int32, (1, seq), 1) == 0, 0.0, hi)
    norm = jnp.sum(jnp.abs(lo), axis=1, keepdims=True) + jnp.sum(jnp.abs(hi), axis=1, keepdims=True)
    o_ref[0, 0] = (lo / norm).astype(o_ref.dtype)
    o_ref[1, 0] = (hi / norm).astype(o_ref.dtype)


def _hyena_filters_t(seq, c, w1, b1, w2, b2, w3, b3, w4, freq, tc=256):
    tc = min(tc, c)
    fo = w2.shape[0]
    emb = -(-HYENA_EMB_DIM // 8) * 8
    t = jnp.linspace(0.0, 1.0, seq, dtype=F32)
    bands = (HYENA_EMB_DIM - 1) // 2
    fr = jnp.linspace(1e-4, bands - 1, bands, dtype=F32)[:, None]
    mirror = (seq - jnp.arange(seq)) % seq
    tables, times = [], []
    for pos, tp in ((jnp.arange(seq), t), (mirror, jnp.roll(t[::-1], 1))):
        wpos = (2.0 * math.pi / seq) * pos.astype(F32)[None, :]
        tp = tp[None, :]
        z = jnp.concatenate([tp, jnp.cos(fr * wpos), -jnp.sin(fr * wpos),
                             jnp.zeros((emb - HYENA_EMB_DIM, seq), F32)], axis=0)
        tables.append(z)
        times.append(tp)
    z = jnp.stack(tables)
    tt = jnp.stack(times)
    w1_t = jnp.pad(w1.astype(F32).T, ((0, 0), (0, emb - HYENA_EMB_DIM)))
    w4_t = jnp.transpose(w4.astype(F32).reshape(fo, 2, 2, c), (1, 2, 3, 0))
    min_decay = math.log(HYENA_TARGET) / HYENA_SLOW_DECAY
    max_decay = math.log(HYENA_TARGET) / HYENA_FAST_DECAY
    deltas = jnp.abs(jnp.linspace(min_decay, max_decay, c, dtype=F32))[:, None]
    col = lambda v: v.astype(F32)[:, None]
    full = lambda a: pl.BlockSpec(a.shape, lambda f, i: (0,) * a.ndim)
    args = (z, w1_t, col(b1), w2.astype(F32).T, col(b2), w3.astype(F32).T, col(b3), col(freq))
    return pl.pallas_call(
        _hyena_filter_kernel,
        grid=(2, c // tc),
        in_specs=[full(a) for a in args]
                 + [pl.BlockSpec((1, 2, tc, fo), lambda f, i: (f, 0, i, 0)),
                    pl.BlockSpec((tc, 1), lambda f, i: (i, 0)), full(tt)],
        out_specs=pl.BlockSpec((2, 1, tc, seq), lambda f, i: (0, f, i, 0)),
        out_shape=jax.ShapeDtypeStruct((2, 2, c, seq), BF16),
        scratch_shapes=[pltpu.VMEM((2, fo, seq), F32)],
        compiler_params=_cparams(("arbitrary", "arbitrary")),
        name="hyena_filter_mlp",
    )(*args, w4_t, deltas, tt)


def _fft_consts(seq):
    n_fft = 2 * seq
    n1 = n_fft // LANES
    nb = n1 // 2 + 1
    kp = -(-nb // 8) * 8
    k1 = jnp.arange(kp, dtype=jnp.int32)
    keep = (k1 < nb)[:, None]
    two_pi = 2.0 * math.pi

    th = ((k1[:, None] * jnp.arange(n1, dtype=jnp.int32)[None, :]) % n1).astype(F32) * (two_pi / n1)
    f1 = jnp.concatenate([jnp.where(keep, jnp.cos(th), 0.0), jnp.where(keep, -jnp.sin(th), 0.0)], axis=0)
    ph = (k1[:, None] * jnp.arange(LANES, dtype=jnp.int32)[None, :]).astype(F32) * (two_pi / n_fft)
    tw = jnp.stack([jnp.cos(ph), -jnp.sin(ph)])
    om = ((jnp.arange(LANES, dtype=jnp.int32)[:, None] * jnp.arange(LANES, dtype=jnp.int32)[None, :]) % LANES
          ).astype(F32) * (two_pi / LANES)
    f2 = jnp.concatenate([jnp.cos(om), -jnp.sin(om)], axis=1)
    f2i = jnp.concatenate([jnp.cos(om), jnp.sin(om)], axis=1)
    weight = jnp.where((k1 == 0) | (k1 == nb - 1), 1.0, 2.0)[:, None]
    g = jnp.concatenate([jnp.where(keep, weight * jnp.cos(th[:, :n1 // 2]), 0.0),
                         jnp.where(keep, -weight * jnp.sin(th[:, :n1 // 2]), 0.0)], axis=0).T
    return dict(f1=f1.astype(BF16), tw=tw, f2=f2.astype(BF16), f2i=f2i.astype(BF16), g=g.astype(BF16), kp=kp)


def _per_channel_matmul(mat_ref, planes, out_scr, ct):
    def body(i, carry):
        c = 2 * i
        pair = jnp.concatenate([planes(c), planes(c + 1)], axis=1)
        res = jnp.dot(mat_ref[...], pair, preferred_element_type=F32)
        out_scr[c] = res[:, :LANES]
        out_scr[c + 1] = res[:, LANES:]
        return carry
    lax.fori_loop(0, ct // 2, body, 0, unroll=16)


def _twiddle_stage2(a, tw_ref, f2_ref):
    ct, kp2, _ = a.shape
    kp = kp2 // 2
    ar, ai = a[:, :kp], a[:, kp:]
    twr, twi = tw_ref[0], tw_ref[1]
    st = jnp.concatenate([ar * twr - ai * twi, ar * twi + ai * twr], axis=1).astype(BF16)
    p = jnp.dot(st.reshape(ct * kp2, LANES), f2_ref[...], preferred_element_type=F32).reshape(ct, kp2, 2 * LANES)
    return p[:, :kp, :LANES] - p[:, kp:, LANES:], p[:, :kp, LANES:] + p[:, kp:, :LANES]


def _stage2_twiddle_inv(yr, yi, tw_ref, f2i_ref):
    ct, kp, _ = yr.shape
    st = jnp.concatenate([yr, yi], axis=1).astype(BF16)
    p = jnp.dot(st.reshape(ct * 2 * kp, LANES), f2i_ref[...], preferred_element_type=F32).reshape(ct, 2 * kp, 2 * LANES)
    tr, ti = p[:, :kp, :LANES] - p[:, kp:, LANES:], p[:, :kp, LANES:] + p[:, kp:, :LANES]
    twr, twi = tw_ref[0], tw_ref[1]
    return jnp.concatenate([tr * twr + ti * twi, ti * twr - tr * twi], axis=1).astype(BF16)


def _filter_spec_kernel(lo_ref, hi_ref, f1_ref, tw_ref, f2_ref, kr_ref, ki_ref, a_scr):
    ct = lo_ref.shape[2]
    planes = lambda c: jnp.concatenate([lo_ref[0, 0, c], hi_ref[0, 0, c]], axis=0)
    _per_channel_matmul(f1_ref, planes, a_scr, ct)
    xr, xi = _twiddle_stage2(a_scr[...], tw_ref, f2_ref)
    n_fft = f1_ref.shape[1] * LANES
    kr_ref[0] = xr * (1.0 / n_fft)
    ki_ref[0] = xi * (1.0 / n_fft)


def _filter_spectrum(halves, consts, ct=64):
    _, nf, c, seq = halves.shape
    ct = min(ct, c)
    kp = consts["kp"]
    h5 = halves.reshape(2, nf, c, seq // LANES, LANES)
    half = lambda s: pl.BlockSpec((1, 1, ct, seq // LANES, LANES), lambda f, i: (s, f, i, 0, 0))
    full = lambda a: pl.BlockSpec(a.shape, lambda f, i: (0,) * a.ndim)
    spec = jax.ShapeDtypeStruct((nf, c, kp, LANES), F32)
    out = pl.BlockSpec((1, ct, kp, LANES), lambda f, i: (f, i, 0, 0))
    return pl.pallas_call(
        _filter_spec_kernel,
        grid=(nf, c // ct),
        in_specs=[half(0), half(1), full(consts["f1"]), full(consts["tw"]), full(consts["f2"])],
        out_specs=[out, out],
        out_shape=[spec, spec],
        scratch_shapes=[pltpu.VMEM((ct, 2 * kp, LANES), F32)],
        compiler_params=_cparams(("parallel", "parallel")),
        name="hyena_filter_spectrum",
    )(h5, h5, consts["f1"], consts["tw"], consts["f2"])


def _hyena_conv_kernel(second, x_ref, m_ref, e_ref, skip_ref, kr_ref, ki_ref, f1_ref, tw_ref, f2_ref, f2i_ref, g_ref,
                       o_ref, a_scr, t_scr, y_scr):
    ct = x_ref.shape[1]
    _per_channel_matmul(f1_ref, lambda c: x_ref[0, c], a_scr, ct)
    xr, xi = _twiddle_stage2(a_scr[...], tw_ref, f2_ref)
    kr, ki = kr_ref[0], ki_ref[0]
    t_scr[...] = _stage2_twiddle_inv(xr * kr - xi * ki, xr * ki + xi * kr, tw_ref, f2i_ref)

    _per_channel_matmul(g_ref, lambda c: t_scr[c], y_scr, ct)

    out = m_ref[0].astype(F32) * (y_scr[...] + x_ref[0].astype(F32) * skip_ref[...])
    if second:
        e = e_ref[0].astype(F32)
        out = out * (e * jax.nn.sigmoid(e))
    o_ref[0] = out.astype(o_ref.dtype)


def _hyena_conv(second, x, x_row0, hy, m_row0, e_row0, skip, kr, ki, filt, consts, ct=64):
    b, _, seq = x.shape
    c = kr.shape[1]
    ct = min(ct, c)
    kp = consts["kp"]
    n1h = seq // LANES
    view = lambda a: a.reshape(a.shape[0], a.shape[1], n1h, LANES)
    rows = lambda blk0: pl.BlockSpec((1, ct, n1h, LANES), lambda j, i: (i, blk0 + j, 0, 0))
    full = lambda a: pl.BlockSpec(a.shape, lambda j, i: (0,) * a.ndim)
    kspec = pl.BlockSpec((1, ct, kp, LANES), lambda j, i: (filt, j, 0, 0))
    mats = [consts["f1"][:, :n1h], consts["tw"], consts["f2"], consts["f2i"], consts["g"]]
    out = pl.pallas_call(
        functools.partial(_hyena_conv_kernel, second),
        grid=(c // ct, b),
        in_specs=[rows(x_row0 // ct), rows(m_row0 // ct), rows(e_row0 // ct),
                  pl.BlockSpec((ct, 1, 1), lambda j, i: (j, 0, 0)),
                  kspec, kspec] + [full(a) for a in mats],
        out_specs=pl.BlockSpec((1, ct, n1h, LANES), lambda j, i: (i, j, 0, 0)),
        out_shape=jax.ShapeDtypeStruct((b, c, n1h, LANES), BF16),
        scratch_shapes=[pltpu.VMEM((ct, 2 * kp, LANES), F32), pltpu.VMEM((ct, 2 * kp, LANES), BF16),
                        pltpu.VMEM((ct, n1h, LANES), F32)],
        compiler_params=_cparams(("parallel", "parallel")),
        name="hyena_conv2" if second else "hyena_conv1",
    )(view(x), view(hy), view(hy), skip, kr, ki, *mats)
    return out.reshape(b, c, seq)


def _t5_bucket(rel):
    half_b = REL_BUCKETS // 2
    max_exact = half_b // 2
    ret = jnp.where(rel > 0, half_b, 0)
    n = jnp.abs(rel)
    nf = jnp.maximum(n, 1).astype(F32)
    large = max_exact + (jnp.log(nf / max_exact) / math.log(REL_MAX_DIST / max_exact)
                         * (half_b - max_exact)).astype(jnp.int32)
    large = jnp.minimum(large, half_b - 1)
    return ret + jnp.where(n < max_exact, n, large)


def _dilated_bias(rel_bias):
    rel = jnp.arange(ATT_KW)[None, :] - ATT_HALF - jnp.arange(ATT_QB)[:, None]
    col = jnp.arange(ATT_KW)
    edge = jnp.stack([col >= ATT_HALF, col >= 0, col < ATT_KW - ATT_HALF])
    out = []
    for _, dil in DILATED_PATTERNS:
        onehot = jax.nn.one_hot(_t5_bucket(rel * dil), REL_BUCKETS, dtype=F32)
        bias = jnp.einsum("qkb,bh->qkh", onehot, rel_bias.astype(F32), precision=lax.Precision.HIGHEST)
        bias = jnp.where((jnp.abs(rel) <= ATT_HALF)[:, :, None], bias, NEG_INF)
        bias = jnp.transpose(bias, (2, 0, 1))[:, None]
        out.append(jnp.where(edge[None, :, None, :], bias, NEG_INF))
    return jnp.stack(out)


def _head_select(first, second):
    lane = lax.broadcasted_iota(jnp.int32, first.shape[:-1] + (LANES,), len(first.shape) - 1)
    return jnp.where(lane < HEAD_DIM, first, second)


def _dilated_kernel(q_ref, k_ref, v_ref, g_ref, bias_ref, o_ref, *scratch):
    stage_a, stage_b = scratch[0:3], scratch[3:6]
    qd, kd, vd = scratch[6:9]
    last = len(DILATED_PATTERNS) - 1
    states = [scratch[9 + 3 * p:12 + 3 * p] for p in range(last)]
    seq = q_ref.shape[1]
    stage_a[0][...] = q_ref[0].astype(F32) * (HEAD_DIM ** -0.5)
    stage_a[1][...] = k_ref[0].astype(F32)
    stage_a[2][...] = v_ref[0].astype(F32)
    zero_halo = jnp.zeros((ATT_HALF, LANES), BF16)
    first_head = lax.broadcasted_iota(jnp.int32, (ATT_QB, LANES), 1) < HEAD_DIM

    src, dst = stage_a, stage_b
    for pi, (_, dil) in enumerate(DILATED_PATTERNS):
        n = seq // dil
        nblk = n // ATT_QB
        if pi > 0:
            prev_dil = dil // ATT_STEP
            for r1 in range(prev_dil):
                for r2 in range(ATT_STEP):
                    r = r1 + prev_dil * r2
                    for a_src, a_dst in zip(src, dst):
                        a_dst[r * n:(r + 1) * n, :] = a_src[pl.ds(r1 * n * ATT_STEP + r2, n, stride=ATT_STEP), :]
            src, dst = dst, src
        stride_k = n + 2 * ATT_HALF
        qd[...] = src[0][...].astype(BF16)
        for r in range(dil):
            base = r * stride_k
            for a_src, a_dst in ((src[1], kd), (src[2], vd)):
                a_dst[base:base + ATT_HALF, :] = zero_halo
                a_dst[base + ATT_HALF:base + ATT_HALF + n, :] = a_src[r * n:(r + 1) * n, :].astype(BF16)
                a_dst[base + ATT_HALF + n:base + stride_k, :] = zero_halo
        if pi == last:
            states.append(dst)
        acc_w, mx_w, den_w = states[pi]

        def block_group(it, carry, pi=pi, nblk=nblk, acc_w=acc_w, mx_w=mx_w, den_w=den_w):
            chains = []
            for u in range(ATT_UNROLL):
                blk = it * ATT_UNROLL + u
                r = blk // nblk
                qb = blk % nblk
                q0 = pl.multiple_of(blk * ATT_QB, ATT_QB)
                k0 = pl.multiple_of(q0 + r * (2 * ATT_HALF), ATT_QB)
                edge = jnp.where(qb == 0, 0, jnp.where(qb == nblk - 1, 2, 1))
                qblk = qd[pl.ds(q0, ATT_QB), :]
                qcat = jnp.concatenate([jnp.where(first_head, qblk, jnp.zeros_like(qblk)),
                                        jnp.where(first_head, jnp.zeros_like(qblk), qblk)], axis=0)
                s = lax.dot_general(qcat, kd[pl.ds(k0, ATT_KW), :], (((1,), (1,)), ((), ())),
                                    preferred_element_type=F32)
                chains.append((q0, k0, edge, s))
            probs = []
            for q0, k0, edge, s in chains:
                s = s + jnp.concatenate([bias_ref[pi, 0, edge], bias_ref[pi, 1, edge]], axis=0)
                m = jnp.max(s, axis=1, keepdims=True)
                p = jnp.exp(s - m)
                probs.append((p.astype(BF16), m, jnp.sum(p, axis=1, keepdims=True)))
            for (q0, k0, edge, s), (p, m, l) in zip(chains, probs):
                pv = jnp.dot(p, vd[pl.ds(k0, ATT_KW), :], preferred_element_type=F32)
                acc_w[pl.ds(q0, ATT_QB), :] = _head_select(pv[:ATT_QB], pv[ATT_QB:])
                mx_w[pl.ds(q0, ATT_QB), :] = _head_select(m[:ATT_QB], m[ATT_QB:])
                den_w[pl.ds(q0, ATT_QB), :] = _head_select(l[:ATT_QB], l[ATT_QB:])
            return carry

        lax.fori_loop(0, dil * nblk // ATT_UNROLL, block_group, 0)

    for pi in range(len(DILATED_PATTERNS) - 1, 0, -1):
        dil = DILATED_PATTERNS[pi][1]
        n = seq // dil
        prev_dil = dil // ATT_STEP
        acc_c, mx_c, den_c = states[pi - 1]
        acc_f, mx_f, den_f = states[pi]
        for r1 in range(prev_dil):
            for r2 in range(ATT_STEP):
                r = r1 + prev_dil * r2
                rows = pl.ds(r1 * n * ATT_STEP + r2, n, stride=ATT_STEP)
                fine = slice(r * n, (r + 1) * n)
                m_old = mx_c[rows, :]
                m_new = mx_f[fine, :]
                m_all = jnp.maximum(m_old, m_new)
                decay = jnp.exp(-jnp.abs(m_old - m_new))
                a_old = jnp.where(m_old >= m_new, 1.0, decay)
                a_new = jnp.where(m_old >= m_new, decay, 1.0)
                acc_c[rows, :] = acc_c[rows, :] * a_old + acc_f[fine, :] * a_new
                den_c[rows, :] = den_c[rows, :] * a_old + den_f[fine, :] * a_new
                mx_c[rows, :] = m_all

    acc, _, den = states[0]
    g = g_ref[0].astype(F32)
    o_ref[0] = (acc[...] / den[...] * (g * jax.nn.sigmoid(g))).astype(o_ref.dtype)


def _dilated_attention(proj, bias, n_heads):
    _, b, seq, _ = proj.shape
    npair = n_heads // 2
    max_dil = max(d for _, d in DILATED_PATTERNS)
    kd_rows = seq + 2 * ATT_HALF * max_dil
    blk = lambda off: pl.BlockSpec((None, 1, seq, LANES), lambda i, j: (off * npair + j, i, 0, 0))
    state = pltpu.VMEM((seq, LANES), F32)
    return pl.pallas_call(
        _dilated_kernel,
        grid=(b, npair),
        in_specs=[blk(0), blk(1), blk(2), blk(3),
                  pl.BlockSpec((len(DILATED_PATTERNS), 2, 3, ATT_QB, ATT_KW), lambda i, j: (0, j, 0, 0, 0))],
        out_specs=pl.BlockSpec((None, 1, seq, LANES), lambda i, j: (j, i, 0, 0)),
        out_shape=jax.ShapeDtypeStruct((npair, b, seq, LANES), BF16),
        scratch_shapes=[state] * 6
                       + [pltpu.VMEM((seq, LANES), BF16), pltpu.VMEM((kd_rows, LANES), BF16),
                          pltpu.VMEM((kd_rows, LANES), BF16)]
                       + [state] * (3 * (len(DILATED_PATTERNS) - 1)),
        compiler_params=_cparams(("parallel", "parallel")),
        name="dilated_attention",
    )(proj, proj, proj, proj, bias)


def _na_bias(rpb, rows):
    assert min(NA_ROWS, rows) % 2 == 0
    cols = np.arange(GRID_W)
    cs = np.clip(cols - NA_COLS // 2, 0, GRID_W - NA_COLS)
    kc = np.arange(GRID_W)
    inside = (kc[None, :] >= cs[:, None]) & (kc[None, :] < cs[:, None] + NA_COLS)
    col_idx = kc[None, :] - cols[:, None] + NA_COLS - 1
    onehot = (col_idx[:, :, None] == np.arange(2 * NA_COLS - 1)[None, None, :]).astype(np.float32)
    e = jnp.einsum("hrd,cqd->hrcq", rpb.astype(F32), jnp.asarray(onehot), precision=lax.Precision.HIGHEST)
    e = jnp.where(jnp.asarray(inside)[None, None], e, NEG_INF)
    return jnp.concatenate([e[:, :-1], e[:, 1:]], axis=-1)


def _na_kernel(q_ref, k_ref, v_ref, g_ref, bias_ref, o_ref, qm):
    seq = q_ref.shape[1]
    rows = seq // GRID_W
    kr_n = min(NA_ROWS, rows)
    lane = lax.broadcasted_iota(jnp.int32, (seq, LANES), 1)
    q = q_ref[0] * (HEAD_DIM ** -0.5)
    for h in range(2):
        qm[h] = jnp.where((lane < HEAD_DIM) == (h == 0), q, jnp.zeros_like(q))

    def row_group(it, carry):
        def scores(batch):
            out = []
            for u in range(batch * NA_BATCH, (batch + 1) * NA_BATCH):
                r = it * NA_UNROLL + u
                rs = jnp.clip(r - kr_n // 2, 0, rows - kr_n)
                shift = rs - r + NA_ROWS - 1
                q0 = pl.multiple_of(r * GRID_W, GRID_W)
                k0 = pl.multiple_of(rs * GRID_W, GRID_W)
                qcat = jnp.concatenate([qm[0, pl.ds(q0, GRID_W), :], qm[1, pl.ds(q0, GRID_W), :]], axis=0)
                s = lax.dot_general(qcat, k_ref[0, pl.ds(k0, kr_n * GRID_W), :], (((1,), (1,)), ((), ())),
                                    preferred_element_type=F32)
                out.append((q0, k0, shift, s))
            return out

        def finish(chains):
            probs = []
            for q0, k0, shift, s in chains:
                bias = [jnp.concatenate([bias_ref[h, shift + i] for i in range(0, kr_n, 2)], axis=1)
                        for h in range(2)]
                s = s + jnp.concatenate(bias, axis=0)
                p = jnp.exp(s - jnp.max(s, axis=1, keepdims=True))
                probs.append((p.astype(BF16), jnp.sum(p, axis=1, keepdims=True)))
            for (q0, k0, shift, s), (p, l) in zip(chains, probs):
                o = jnp.dot(p, v_ref[0, pl.ds(k0, kr_n * GRID_W), :], preferred_element_type=F32) / l
                g = g_ref[0, pl.ds(q0, GRID_W), :].astype(F32)
                o = _head_select(o[:GRID_W], o[GRID_W:]) * (g * jax.nn.sigmoid(g))
                o_ref[0, pl.ds(q0, GRID_W), :] = o.astype(o_ref.dtype)

        n_batches = NA_UNROLL // NA_BATCH
        pending = scores(0)
        for batch in range(n_batches):
            ahead = scores(batch + 1) if batch + 1 < n_batches else None
            finish(pending)
            pending = ahead
        return carry

    lax.fori_loop(0, rows // NA_UNROLL, row_group, 0)


def _neighbourhood_attention(proj, bias, n_heads):
    _, b, seq, _ = proj.shape
    npair = n_heads // 2
    blk = lambda off: pl.BlockSpec((None, 1, seq, LANES), lambda j, i: (off * npair + j, i, 0, 0))
    return pl.pallas_call(
        _na_kernel,
        grid=(npair, b),
        in_specs=[blk(0), blk(1), blk(2), blk(3),
                  pl.BlockSpec((2,) + bias.shape[1:], lambda j, i: (j, 0, 0, 0))],
        out_specs=pl.BlockSpec((None, 1, seq, LANES), lambda j, i: (j, i, 0, 0)),
        out_shape=jax.ShapeDtypeStruct((npair, b, seq, LANES), BF16),
        scratch_shapes=[pltpu.VMEM((2, seq, LANES), BF16)],
        compiler_params=_cparams(("parallel", "parallel")),
        name="neighbourhood_attention",
    )(proj, proj, proj, proj, bias)


def _out_kernel(n_y, n_t, *refs):
    n = n_y + n_t
    y_refs = refs[:n]
    w_refs = refs[n:2 * n]
    h_ref, p_ref, wp_ref, wg_ref, gpost_ref, gple_ref, gnext_ref, h_out, u_out = refs[2 * n:]
    t = None
    for idx, (y_ref, w_ref) in enumerate(zip(y_refs, w_refs)):
        if idx < n_y:
            y = jnp.concatenate([y_ref[c] for c in range(y_ref.shape[0])], axis=1)
            part = jnp.dot(y, w_ref[...], preferred_element_type=F32)
        else:
            part = lax.dot_general(y_ref[0], w_ref[...], (((0,), (0,)), ((), ())), preferred_element_type=F32)
        t = part if t is None else t + part
    h = h_ref[...] + _rms(t, gpost_ref[...])
    e = _rms(jnp.dot(p_ref[...].astype(BF16), wp_ref[...], preferred_element_type=F32), gple_ref[...])
    gate = jax.nn.sigmoid(jnp.dot(h.astype(BF16), wg_ref[...], preferred_element_type=F32))
    h = h + gate * e
    h_out[...] = h
    u_out[...] = _rms(h, gnext_ref[...]).astype(u_out.dtype)


def _out_block(ys, ys_t, ws, h, p, layer, w_ple, w_gate, g_post, g_ple, g_next, tm=512):
    m, d = h.shape
    row = lambda a: pl.BlockSpec((tm, a.shape[1]), lambda i: (i, 0))
    slab = lambda a: pl.BlockSpec((a.shape[0], tm, LANES), lambda i: (0, i, 0))
    p_spec = pl.BlockSpec((None, tm, p.shape[2]), lambda i: (layer, i, 0))
    full = lambda a: pl.BlockSpec(a.shape, lambda i: (0, 0))
    vec = lambda g: g.reshape(1, d)

    def col(a):
        tiles = a.shape[2] // tm
        return pl.BlockSpec((1, a.shape[1], tm), lambda i: (i // tiles, 0, i % tiles))

    return pl.pallas_call(
        functools.partial(_out_kernel, len(ys), len(ys_t)),
        grid=(m // tm,),
        in_specs=[slab(y) for y in ys] + [col(y) for y in ys_t] + [full(w) for w in ws]
                 + [row(h), p_spec, full(w_ple), full(w_gate), full(vec(g_post)), full(vec(g_ple)), full(vec(g_next))],
        out_specs=[pl.BlockSpec((tm, d), lambda i: (i, 0)), pl.BlockSpec((tm, d), lambda i: (i, 0))],
        out_shape=[jax.ShapeDtypeStruct((m, d), F32), jax.ShapeDtypeStruct((m, d), BF16)],
        compiler_params=_cparams(("parallel",)),
        name="out_proj_residual_ple",
    )(*ys, *ys_t, *ws, h, p, w_ple, w_gate, vec(g_post), vec(g_ple), vec(g_next))


def kernel(x, p, w_in, w_out, norm_pre, norm_post, hyena_conv_w, hyena_conv_b, hyena_w1, hyena_b1, hyena_w2, hyena_b2, hyena_w3, hyena_b3, hyena_w4, hyena_freq, hyena_skip, rel_bias, na_rpb, ple_proj, ple_norm, ple_gate):
    b, seq, d = x.shape
    depth = w_in.shape[0]
    d_inner = w_out.shape[1]
    hw = d_inner // 2
    m = b * seq
    consts = _fft_consts(seq)
    dil_bias = _dilated_bias(rel_bias)
    h = x.reshape(m, d)
    u = _rmsnorm_bf16(h, norm_pre[0])
    w_all = w_in.astype(BF16)
    p_all = p.reshape(depth, m, -1)
    for i in range(depth):
        j = i // 2
        if i % 2 == 0:
            w_a = w_all[i, :, :4 * hw].T
            ident = jnp.tile(jnp.array([0.0, 1.0, 0.0], F32)[:, None], (1, hw))
            conv_w = jnp.concatenate([hyena_conv_w[j], ident], axis=1).T
            conv_b = jnp.concatenate([hyena_conv_b[j], jnp.zeros((hw,), F32)])[:, None]
            hy = _proj_t(w_a, u.reshape(b, seq, d), conv_w, conv_b)
            halves = _hyena_filters_t(seq, hw, hyena_w1[j], hyena_b1[j], hyena_w2[j], hyena_b2[j],
                                      hyena_w3[j], hyena_b3[j], hyena_w4[j], hyena_freq[j])
            kr, ki = _filter_spectrum(halves, consts)
            skip = hyena_skip[j][:, :, None, None]
            z = _hyena_conv(False, hy, 0, hy, hw, hw, skip[0], kr, ki, 0, consts)
            ya_t = _hyena_conv(True, z, 0, hy, 2 * hw, 3 * hw, skip[1], kr, ki, 1, consts)
            proj_b = _matmul_bf16(u, w_all[i:i + 1, :, 4 * hw:], 0, 0, 4 * hw).reshape(-1, b, seq, LANES)
            yb = _dilated_attention(proj_b, dil_bias, hw // HEAD_DIM)
            ys, ys_t = [yb.reshape(-1, m, LANES)], [ya_t]
            ws = [w_out[i, hw:].astype(BF16), w_out[i, :hw].astype(BF16)]
        else:
            proj = _matmul_bf16(u, w_all, i, 0, 4 * d_inner).reshape(-1, b, seq, LANES)
            yc = _neighbourhood_attention(proj, _na_bias(na_rpb[j], seq // GRID_W), d_inner // HEAD_DIM)
            ys, ys_t = [yc.reshape(-1, m, LANES)], []
            ws = [w_out[i].astype(BF16)]
        g_next = norm_pre[i + 1] if i + 1 < depth else norm_pre[i]
        h, u = _out_block(ys, ys_t, ws, h, p_all, i, ple_proj[i].astype(BF16), ple_gate[i].astype(BF16),
                          norm_post[i], ple_norm[i], g_next)
    return h.reshape(b, seq, d)
```

```python
import functools
import math

import numpy as np
import jax
import jax.numpy as jnp
from jax import lax
from jax.experimental import pallas as pl
from jax.experimental.pallas import tpu as pltpu

F32 = jnp.float32
BF16 = jnp.bfloat16

HEAD_DIM = 64
LANES = 128
HYENA_EMB_DIM = 33
HYENA_FAST_DECAY = 0.3
HYENA_SLOW_DECAY = 1.5
HYENA_TARGET = 1e-2
DILATED_PATTERNS = ((128, 1), (512, 4), (2048, 16))
GRID_W = 64
NA_ROWS = 8
NA_COLS = 16
REL_BUCKETS = 32
REL_MAX_DIST = 1024
RMS_EPS = 1e-6
NEG_INF = -1e30
ATT_QB = 128
ATT_HALF = 64
ATT_KW = ATT_QB + 2 * ATT_HALF
ATT_BATCH = 2
ATT_UNROLL = 8
ATT_STEP = 4
assert all(b[1] == a[1] * ATT_STEP for a, b in zip(DILATED_PATTERNS, DILATED_PATTERNS[1:])) and DILATED_PATTERNS[0][1] == 1
NA_BATCH = 4
NA_UNROLL = 16
VMEM_LIMIT = 56 * 1024 * 1024


def _cparams(sem):
    return pltpu.CompilerParams(dimension_semantics=sem, vmem_limit_bytes=VMEM_LIMIT)


def _rms(x, g):
    return x * lax.rsqrt(jnp.mean(x * x, axis=-1, keepdims=True) + RMS_EPS) * g


def _rmsnorm_kernel(x_ref, g_ref, o_ref):
    o_ref[...] = _rms(x_ref[...], g_ref[...]).astype(o_ref.dtype)


def _rmsnorm_bf16(x2d, g, tm=1024):
    m, d = x2d.shape
    return pl.pallas_call(
        _rmsnorm_kernel,
        grid=(m // tm,),
        in_specs=[pl.BlockSpec((tm, d), lambda i: (i, 0)), pl.BlockSpec((1, d), lambda i: (0, 0))],
        out_specs=pl.BlockSpec((tm, d), lambda i: (i, 0)),
        out_shape=jax.ShapeDtypeStruct((m, d), BF16),
        compiler_params=_cparams(("parallel",)),
        name="rmsnorm",
    )(x2d, g.reshape(1, d))


def _matmul_kernel(a_ref, b_ref, o_ref):
    res = jnp.dot(a_ref[...], b_ref[...], preferred_element_type=F32).astype(o_ref.dtype)
    for c in range(o_ref.shape[0]):
        o_ref[c] = res[:, c * LANES:(c + 1) * LANES]


def _matmul_bf16(a, w, layer, col0, n, tm=2048, tn=1024):
    m, k = a.shape
    return pl.pallas_call(
        _matmul_kernel,
        grid=(m // tm, n // tn),
        in_specs=[pl.BlockSpec((tm, k), lambda i, j: (i, 0)), pl.BlockSpec((None, k, tn), lambda i, j: (layer, 0, col0 // tn + j))],
        out_specs=pl.BlockSpec((tn // LANES, tm, LANES), lambda i, j: (j, i, 0)),
        out_shape=jax.ShapeDtypeStruct((n // LANES, m, LANES), BF16),
        compiler_params=_cparams(("parallel", "parallel")),
        name="in_proj",
    )(a, w)


def _proj_t_kernel(w_ref, u_ref, cw_ref, cb_ref, o_ref):
    p = lax.dot_general(w_ref[...], u_ref[0], (((1,), (1,)), ((), ())), preferred_element_type=F32)
    seq = p.shape[1]
    t = lax.broadcasted_iota(jnp.int32, p.shape, 1)
    prev = jnp.where(t == 0, 0.0, pltpu.roll(p, 1, axis=1))
    nxt = jnp.where(t == seq - 1, 0.0, pltpu.roll(p, seq - 1, axis=1))
    cw = cw_ref[...]
    o_ref[0] = (prev * cw[:, 0:1] + p * cw[:, 1:2] + nxt * cw[:, 2:3] + cb_ref[...]).astype(o_ref.dtype)


def _proj_t(w_t, u, conv_w, conv_b, tc=256):
    c, d = w_t.shape
    b, seq, _ = u.shape
    return pl.pallas_call(
        _proj_t_kernel,
        grid=(b, c // tc),
        in_specs=[
            pl.BlockSpec((tc, d), lambda i, j: (j, 0)),
            pl.BlockSpec((1, seq, d), lambda i, j: (i, 0, 0)),
            pl.BlockSpec((tc, 3), lambda i, j: (j, 0)),
            pl.BlockSpec((tc, 1), lambda i, j: (j, 0)),
        ],
        out_specs=pl.BlockSpec((1, tc, seq), lambda i, j: (i, j, 0)),
        out_shape=jax.ShapeDtypeStruct((b, c, seq), BF16),
        compiler_params=_cparams(("parallel", "parallel")),
        name="in_proj_t_conv3",
    )(w_t, u, conv_w, conv_b)


def _hyena_filter_kernel(z_ref, w1_ref, b1_ref, w2_ref, b2_ref, w3_ref, b3_ref, fq_ref, w4_ref, dl_ref, t_ref,
                         o_ref, a_scr):
    hp = lax.Precision.HIGHEST

    @pl.when((pl.program_id(0) == 0) & (pl.program_id(1) == 0))
    def _():
        for d in range(2):
            a = z_ref[d]
            for w_ref, b_ref in ((w1_ref, b1_ref), (w2_ref, b2_ref), (w3_ref, b3_ref)):
                a = jnp.sin(fq_ref[...] * (jnp.dot(w_ref[...], a, precision=hp, preferred_element_type=F32) + b_ref[...]))
            a_scr[d] = a

    seq = o_ref.shape[-1]
    proj = lambda d: jnp.dot(w4_ref[0, d].astype(BF16), a_scr[d].astype(BF16), preferred_element_type=F32)
    lo = proj(0) * jnp.exp(-t_ref[0] * dl_ref[...])
    hi = proj(1) * jnp.exp(-t_ref[1] * dl_ref[...])
    hi = jnp.where(lax.broadcasted_iota(jnp.int32, (1, seq), 1) == 0, 0.0, hi)
    norm = jnp.sum(jnp.abs(lo), axis=1, keepdims=True) + jnp.sum(jnp.abs(hi), axis=1, keepdims=True)
    o_ref[0, 0] = (lo / norm).astype(o_ref.dtype)
    o_ref[1, 0] = (hi / norm).astype(o_ref.dtype)


def _hyena_filters_t(seq, c, w1, b1, w2, b2, w3, b3, w4, freq, tc=256):
    tc = min(tc, c)
    fo = w2.shape[0]
    emb = -(-HYENA_EMB_DIM // 8) * 8
    t = jnp.linspace(0.0, 1.0, seq, dtype=F32)
    bands = (HYENA_EMB_DIM - 1) // 2
    fr = jnp.linspace(1e-4, bands - 1, bands, dtype=F32)[:, None]
    mirror = (seq - jnp.arange(seq)) % seq
    tables, times = [], []
    for pos, tp in ((jnp.arange(seq), t), (mirror, jnp.roll(t[::-1], 1))):
        wpos = (2.0 * math.pi / seq) * pos.astype(F32)[None, :]
        tp = tp[None, :]
        z = jnp.concatenate([tp, jnp.cos(fr * wpos), -jnp.sin(fr * wpos),
                             jnp.zeros((emb - HYENA_EMB_DIM, seq), F32)], axis=0)
        tables.append(z)
        times.append(tp)
    z = jnp.stack(tables)
    tt = jnp.stack(times)
    w1_t = jnp.pad(w1.astype(F32).T, ((0, 0), (0, emb - HYENA_EMB_DIM)))
    w4_t = jnp.transpose(w4.astype(F32).reshape(fo, 2, 2, c), (1, 2, 3, 0))
    min_decay = math.log(HYENA_TARGET) / HYENA_SLOW_DECAY
    max_decay = math.log(HYENA_TARGET) / HYENA_FAST_DECAY
    deltas = jnp.abs(jnp.linspace(min_decay, max_decay, c, dtype=F32))[:, None]
    col = lambda v: v.astype(F32)[:, None]
    full = lambda a: pl.BlockSpec(a.shape, lambda f, i: (0,) * a.ndim)
    args = (z, w1_t, col(b1), w2.astype(F32).T, col(b2), w3.astype(F32).T, col(b3), col(freq))
    return pl.pallas_call(
        _hyena_filter_kernel,
        grid=(2, c // tc),
        in_specs=[full(a) for a in args]
                 + [pl.BlockSpec((1, 2, tc, fo), lambda f, i: (f, 0, i, 0)),
                    pl.BlockSpec((tc, 1), lambda f, i: (i, 0)), full(tt)],
        out_specs=pl.BlockSpec((2, 1, tc, seq), lambda f, i: (0, f, i, 0)),
        out_shape=jax.ShapeDtypeStruct((2, 2, c, seq), BF16),
        scratch_shapes=[pltpu.VMEM((2, fo, seq), F32)],
        compiler_params=_cparams(("arbitrary", "arbitrary")),
        name="hyena_filter_mlp",
    )(*args, w4_t, deltas, tt)


def _fft_consts(seq):
    n_fft = 2 * seq
    n1 = n_fft // LANES
    nb = n1 // 2 + 1
    kp = -(-nb // 8) * 8
    k1 = jnp.arange(kp, dtype=jnp.int32)
    keep = (k1 < nb)[:, None]
    two_pi = 2.0 * math.pi

    th = ((k1[:, None] * jnp.arange(n1, dtype=jnp.int32)[None, :]) % n1).astype(F32) * (two_pi / n1)
    f1 = jnp.concatenate([jnp.where(keep, jnp.cos(th), 0.0), jnp.where(keep, -jnp.sin(th), 0.0)], axis=0)
    ph = (k1[:, None] * jnp.arange(LANES, dtype=jnp.int32)[None, :]).astype(F32) * (two_pi / n_fft)
    tw = jnp.stack([jnp.cos(ph), -jnp.sin(ph)])
    om = ((jnp.arange(LANES, dtype=jnp.int32)[:, None] * jnp.arange(LANES, dtype=jnp.int32)[None, :]) % LANES
          ).astype(F32) * (two_pi / LANES)
    f2 = jnp.concatenate([jnp.cos(om), -jnp.sin(om)], axis=1)
    f2i = jnp.concatenate([jnp.cos(om), jnp.sin(om)], axis=1)
    weight = jnp.where((k1 == 0) | (k1 == nb - 1), 1.0, 2.0)[:, None]
    g = jnp.concatenate([jnp.where(keep, weight * jnp.cos(th[:, :n1 // 2]), 0.0),
                         jnp.where(keep, -weight * jnp.sin(th[:, :n1 // 2]), 0.0)], axis=0).T
    return dict(f1=f1.astype(BF16), tw=tw, f2=f2.astype(BF16), f2i=f2i.astype(BF16), g=g.astype(BF16), kp=kp)


def _per_channel_matmul(mat_ref, planes, out_scr, ct):
    def body(i, carry):
        c = 2 * i
        pair = jnp.concatenate([planes(c), planes(c + 1)], axis=1)
        res = jnp.dot(mat_ref[...], pair, preferred_element_type=F32)
        out_scr[c] = res[:, :LANES]
        out_scr[c + 1] = res[:, LANES:]
        return carry
    lax.fori_loop(0, ct // 2, body, 0, unroll=16)


def _twiddle_stage2(a, tw_ref, f2_ref):
    ct, kp2, _ = a.shape
    kp = kp2 // 2
    ar, ai = a[:, :kp], a[:, kp:]
    twr, twi = tw_ref[0], tw_ref[1]
    st = jnp.concatenate([ar * twr - ai * twi, ar * twi + ai * twr], axis=1).astype(BF16)
    p = jnp.dot(st.reshape(ct * kp2, LANES), f2_ref[...], preferred_element_type=F32).reshape(ct, kp2, 2 * LANES)
    return p[:, :kp, :LANES] - p[:, kp:, LANES:], p[:, :kp, LANES:] + p[:, kp:, :LANES]


def _stage2_twiddle_inv(yr, yi, tw_ref, f2i_ref):
    ct, kp, _ = yr.shape
    st = jnp.concatenate([yr, yi], axis=1).astype(BF16)
    p = jnp.dot(st.reshape(ct * 2 * kp, LANES), f2i_ref[...], preferred_element_type=F32).reshape(ct, 2 * kp, 2 * LANES)
    tr, ti = p[:, :kp, :LANES] - p[:, kp:, LANES:], p[:, :kp, LANES:] + p[:, kp:, :LANES]
    twr, twi = tw_ref[0], tw_ref[1]
    return jnp.concatenate([tr * twr + ti * twi, ti * twr - tr * twi], axis=1).astype(BF16)


def _filter_spec_kernel(lo_ref, hi_ref, f1_ref, tw_ref, f2_ref, kr_ref, ki_ref, a_scr):
    ct = lo_ref.shape[2]
    planes = lambda c: jnp.concatenate([lo_ref[0, 0, c], hi_ref[0, 0, c]], axis=0)
    _per_channel_matmul(f1_ref, planes, a_scr, ct)
    xr, xi = _twiddle_stage2(a_scr[...], tw_ref, f2_ref)
    n_fft = f1_ref.shape[1] * LANES
    kr_ref[0] = xr * (1.0 / n_fft)
    ki_ref[0] = xi * (1.0 / n_fft)


def _filter_spectrum(halves, consts, ct=64):
    _, nf, c, seq = halves.shape
    ct = min(ct, c)
    kp = consts["kp"]
    h5 = halves.reshape(2, nf, c, seq // LANES, LANES)
    half = lambda s: pl.BlockSpec((1, 1, ct, seq // LANES, LANES), lambda f, i: (s, f, i, 0, 0))
    full = lambda a: pl.BlockSpec(a.shape, lambda f, i: (0,) * a.ndim)
    spec = jax.ShapeDtypeStruct((nf, c, kp, LANES), F32)
    out = pl.BlockSpec((1, ct, kp, LANES), lambda f, i: (f, i, 0, 0))
    return pl.pallas_call(
        _filter_spec_kernel,
        grid=(nf, c // ct),
        in_specs=[half(0), half(1), full(consts["f1"]), full(consts["tw"]), full(consts["f2"])],
        out_specs=[out, out],
        out_shape=[spec, spec],
        scratch_shapes=[pltpu.VMEM((ct, 2 * kp, LANES), F32)],
        compiler_params=_cparams(("parallel", "parallel")),
        name="hyena_filter_spectrum",
    )(h5, h5, consts["f1"], consts["tw"], consts["f2"])


def _hyena_conv_kernel(second, x_ref, m_ref, e_ref, skip_ref, kr_ref, ki_ref, f1_ref, tw_ref, f2_ref, f2i_ref, g_ref,
                       o_ref, a_scr, t_scr, y_scr):
    ct = x_ref.shape[1]
    _per_channel_matmul(f1_ref, lambda c: x_ref[0, c], a_scr, ct)
    xr, xi = _twiddle_stage2(a_scr[...], tw_ref, f2_ref)
    kr, ki = kr_ref[0], ki_ref[0]
    t_scr[...] = _stage2_twiddle_inv(xr * kr - xi * ki, xr * ki + xi * kr, tw_ref, f2i_ref)

    _per_channel_matmul(g_ref, lambda c: t_scr[c], y_scr, ct)

    out = m_ref[0].astype(F32) * (y_scr[...] + x_ref[0].astype(F32) * skip_ref[...])
    if second:
        e = e_ref[0].astype(F32)
        out = out * (e * jax.nn.sigmoid(e))
    o_ref[0] = out.astype(o_ref.dtype)


def _hyena_conv(second, x, x_row0, hy, m_row0, e_row0, skip, kr, ki, filt, consts, ct=64):
    b, _, seq = x.shape
    c = kr.shape[1]
    ct = min(ct, c)
    kp = consts["kp"]
    n1h = seq // LANES
    view = lambda a: a.reshape(a.shape[0], a.shape[1], n1h, LANES)
    rows = lambda blk0: pl.BlockSpec((1, ct, n1h, LANES), lambda j, i: (i, blk0 + j, 0, 0))
    full = lambda a: pl.BlockSpec(a.shape, lambda j, i: (0,) * a.ndim)
    kspec = pl.BlockSpec((1, ct, kp, LANES), lambda j, i: (filt, j, 0, 0))
    mats = [consts["f1"][:, :n1h], consts["tw"], consts["f2"], consts["f2i"], consts["g"]]
    out = pl.pallas_call(
        functools.partial(_hyena_conv_kernel, second),
        grid=(c // ct, b),
        in_specs=[rows(x_row0 // ct), rows(m_row0 // ct), rows(e_row0 // ct),
                  pl.BlockSpec((ct, 1, 1), lambda j, i: (j, 0, 0)),
                  kspec, kspec] + [full(a) for a in mats],
        out_specs=pl.BlockSpec((1, ct, n1h, LANES), lambda j, i: (i, j, 0, 0)),
        out_shape=jax.ShapeDtypeStruct((b, c, n1h, LANES), BF16),
        scratch_shapes=[pltpu.VMEM((ct, 2 * kp, LANES), F32), pltpu.VMEM((ct, 2 * kp, LANES), BF16),
                        pltpu.VMEM((ct, n1h, LANES), F32)],
        compiler_params=_cparams(("parallel", "parallel")),
        name="hyena_conv2" if second else "hyena_conv1",
    )(view(x), view(hy), view(hy), skip, kr, ki, *mats)
    return out.reshape(b, c, seq)


def _t5_bucket(rel):
    half_b = REL_BUCKETS // 2
    max_exact = half_b // 2
    ret = jnp.where(rel > 0, half_b, 0)
    n = jnp.abs(rel)
    nf = jnp.maximum(n, 1).astype(F32)
    large = max_exact + (jnp.log(nf / max_exact) / math.log(REL_MAX_DIST / max_exact)
                         * (half_b - max_exact)).astype(jnp.int32)
    large = jnp.minimum(large, half_b - 1)
    return ret + jnp.where(n < max_exact, n, large)


def _dilated_bias(rel_bias):
    rel = jnp.arange(ATT_KW)[None, :] - ATT_HALF - jnp.arange(ATT_QB)[:, None]
    col = jnp.arange(ATT_KW)
    edge = jnp.stack([col >= ATT_HALF, col >= 0, col < ATT_KW - ATT_HALF])
    out = []
    for _, dil in DILATED_PATTERNS:
        onehot = jax.nn.one_hot(_t5_bucket(rel * dil), REL_BUCKETS, dtype=F32)
        bias = jnp.einsum("qkb,bh->qkh", onehot, rel_bias.astype(F32), precision=lax.Precision.HIGHEST)
        bias = jnp.where((jnp.abs(rel) <= ATT_HALF)[:, :, None], bias, NEG_INF)
        bias = jnp.transpose(bias, (2, 0, 1))[:, None]
        out.append(jnp.where(edge[None, :, None, :], bias, NEG_INF))
    return jnp.stack(out)


def _head_select(first, second):
    lane = lax.broadcasted_iota(jnp.int32, first.shape[:-1] + (LANES,), len(first.shape) - 1)
    return jnp.where(lane < HEAD_DIM, first, second)


def _dilated_kernel(q_ref, k_ref, v_ref, g_ref, bias_ref, o_ref, *scratch):
    stage_a, stage_b = scratch[0:3], scratch[3:6]
    qd, kd, vd = scratch[6:9]
    last = len(DILATED_PATTERNS) - 1
    states = [scratch[9 + 3 * p:12 + 3 * p] for p in range(last)]
    seq = q_ref.shape[1]
    stage_a[0][...] = q_ref[0].astype(F32) * (HEAD_DIM ** -0.5)
    stage_a[1][...] = k_ref[0].astype(F32)
    stage_a[2][...] = v_ref[0].astype(F32)
    zero_halo = jnp.zeros((ATT_HALF, LANES), BF16)
    first_head = lax.broadcasted_iota(jnp.int32, (ATT_QB, LANES), 1) < HEAD_DIM

    src, dst = stage_a, stage_b
    for pi, (_, dil) in enumerate(DILATED_PATTERNS):
        n = seq // dil
        nblk = n // ATT_QB
        if pi > 0:
            prev_dil = dil // ATT_STEP
            for r1 in range(prev_dil):
                for r2 in range(ATT_STEP):
                    r = r1 + prev_dil * r2
                    for a_src, a_dst in zip(src, dst):
                        a_dst[r * n:(r + 1) * n, :] = a_src[pl.ds(r1 * n * ATT_STEP + r2, n, stride=ATT_STEP), :]
            src, dst = dst, src
        stride_k = n + 2 * ATT_HALF
        qd[...] = src[0][...].astype(BF16)
        for r in range(dil):
            base = r * stride_k
            for a_src, a_dst in ((src[1], kd), (src[2], vd)):
                a_dst[base:base + ATT_HALF, :] = zero_halo
                a_dst[base + ATT_HALF:base + ATT_HALF + n, :] = a_src[r * n:(r + 1) * n, :].astype(BF16)
                a_dst[base + ATT_HALF + n:base + stride_k, :] = zero_halo
        if pi == last:
            states.append(dst)
        acc_w, mx_w, den_w = states[pi]

        def block_group(it, carry, pi=pi, nblk=nblk, acc_w=acc_w, mx_w=mx_w, den_w=den_w):
            def scores(batch):
                out = []
                for u in range(batch * ATT_BATCH, (batch + 1) * ATT_BATCH):
                    blk = it * ATT_UNROLL + u
                    r = blk // nblk
                    qb = blk % nblk
                    q0 = pl.multiple_of(blk * ATT_QB, ATT_QB)
                    k0 = pl.multiple_of(q0 + r * (2 * ATT_HALF), ATT_QB)
                    edge = jnp.where(qb == 0, 0, jnp.where(qb == nblk - 1, 2, 1))
                    qblk = qd[pl.ds(q0, ATT_QB), :]
                    qcat = jnp.concatenate([jnp.where(first_head, qblk, jnp.zeros_like(qblk)),
                                            jnp.where(first_head, jnp.zeros_like(qblk), qblk)], axis=0)
                    s = lax.dot_general(qcat, kd[pl.ds(k0, ATT_KW), :], (((1,), (1,)), ((), ())),
                                        preferred_element_type=F32)
                    out.append((q0, k0, edge, s))
                return out

            def finish(chains):
                probs = []
                for q0, k0, edge, s in chains:
                    s = s + jnp.concatenate([bias_ref[pi, 0, edge], bias_ref[pi, 1, edge]], axis=0)
                    m = jnp.max(s, axis=1, keepdims=True)
                    p = jnp.exp(s - m)
                    probs.append((p.astype(BF16), m, jnp.sum(p, axis=1, keepdims=True)))
                for (q0, k0, edge, s), (p, m, l) in zip(chains, probs):
                    pv = jnp.dot(p, vd[pl.ds(k0, ATT_KW), :], preferred_element_type=F32)
                    acc_w[pl.ds(q0, ATT_QB), :] = _head_select(pv[:ATT_QB], pv[ATT_QB:])
                    mx_w[pl.ds(q0, ATT_QB), :] = _head_select(m[:ATT_QB], m[ATT_QB:])
                    den_w[pl.ds(q0, ATT_QB), :] = _head_select(l[:ATT_QB], l[ATT_QB:])

            n_batches = ATT_UNROLL // ATT_BATCH
            pending = scores(0)
            for batch in range(n_batches):
                ahead = scores(batch + 1) if batch + 1 < n_batches else None
                finish(pending)
                pending = ahead
            return carry

        lax.fori_loop(0, dil * nblk // ATT_UNROLL, block_group, 0)

    for pi in range(len(DILATED_PATTERNS) - 1, 0, -1):
        dil = DILATED_PATTERNS[pi][1]
        n = seq // dil
        prev_dil = dil // ATT_STEP
        acc_c, mx_c, den_c = states[pi - 1]
        acc_f, mx_f, den_f = states[pi]
        for r1 in range(prev_dil):
            for r2 in range(ATT_STEP):
                r = r1 + prev_dil * r2
                rows = pl.ds(r1 * n * ATT_STEP + r2, n, stride=ATT_STEP)
                fine = slice(r * n, (r + 1) * n)
                m_old = mx_c[rows, :]
                m_new = mx_f[fine, :]
                m_all = jnp.maximum(m_old, m_new)
                decay = jnp.exp(-jnp.abs(m_old - m_new))
                a_old = jnp.where(m_old >= m_new, 1.0, decay)
                a_new = jnp.where(m_old >= m_new, decay, 1.0)
                acc_c[rows, :] = acc_c[rows, :] * a_old + acc_f[fine, :] * a_new
                den_c[rows, :] = den_c[rows, :] * a_old + den_f[fine, :] * a_new
                mx_c[rows, :] = m_all

    acc, _, den = states[0]
    g = g_ref[0].astype(F32)
    o_ref[0] = (acc[...] / den[...] * (g * jax.nn.sigmoid(g))).astype(o_ref.dtype)


def _dilated_attention(proj, bias, n_heads):
    _, b, seq, _ = proj.shape
    npair = n_heads // 2
    max_dil = max(d for _, d in DILATED_PATTERNS)
    kd_rows = seq + 2 * ATT_HALF * max_dil
    blk = lambda off: pl.BlockSpec((None, 1, seq, LANES), lambda i, j: (off * npair + j, i, 0, 0))
    state = pltpu.VMEM((seq, LANES), F32)
    return pl.pallas_call(
        _dilated_kernel,
        grid=(b, npair),
        in_specs=[blk(0), blk(1), blk(2), blk(3),
                  pl.BlockSpec((len(DILATED_PATTERNS), 2, 3, ATT_QB, ATT_KW), lambda i, j: (0, j, 0, 0, 0))],
        out_specs=pl.BlockSpec((None, 1, seq, LANES), lambda i, j: (j, i, 0, 0)),
        out_shape=jax.ShapeDtypeStruct((npair, b, seq, LANES), BF16),
        scratch_shapes=[state] * 6
                       + [pltpu.VMEM((seq, LANES), BF16), pltpu.VMEM((kd_rows, LANES), BF16),
                          pltpu.VMEM((kd_rows, LANES), BF16)]
                       + [state] * (3 * (len(DILATED_PATTERNS) - 1)),
        compiler_params=_cparams(("parallel", "parallel")),
        name="dilated_attention",
    )(proj, proj, proj, proj, bias)


def _na_bias(rpb, rows):
    assert min(NA_ROWS, rows) % 2 == 0
    cols = np.arange(GRID_W)
    cs = np.clip(cols - NA_COLS // 2, 0, GRID_W - NA_COLS)
    kc = np.arange(GRID_W)
    inside = (kc[None, :] >= cs[:, None]) & (kc[None, :] < cs[:, None] + NA_COLS)
    col_idx = kc[None, :] - cols[:, None] + NA_COLS - 1
    onehot = (col_idx[:, :, None] == np.arange(2 * NA_COLS - 1)[None, None, :]).astype(np.float32)
    e = jnp.einsum("hrd,cqd->hrcq", rpb.astype(F32), jnp.asarray(onehot), precision=lax.Precision.HIGHEST)
    e = jnp.where(jnp.asarray(inside)[None, None], e, NEG_INF)
    return jnp.concatenate([e[:, :-1], e[:, 1:]], axis=-1)


def _na_kernel(q_ref, k_ref, v_ref, g_ref, bias_ref, o_ref, qm):
    seq = q_ref.shape[1]
    rows = seq // GRID_W
    kr_n = min(NA_ROWS, rows)
    lane = lax.broadcasted_iota(jnp.int32, (seq, LANES), 1)
    q = q_ref[0] * (HEAD_DIM ** -0.5)
    for h in range(2):
        qm[h] = jnp.where((lane < HEAD_DIM) == (h == 0), q, jnp.zeros_like(q))

    def row_group(it, carry):
        def scores(batch):
            out = []
            for u in range(batch * NA_BATCH, (batch + 1) * NA_BATCH):
                r = it * NA_UNROLL + u
                rs = jnp.clip(r - kr_n // 2, 0, rows - kr_n)
                shift = rs - r + NA_ROWS - 1
                q0 = pl.multiple_of(r * GRID_W, GRID_W)
                k0 = pl.multiple_of(rs * GRID_W, GRID_W)
                qcat = jnp.concatenate([qm[0, pl.ds(q0, GRID_W), :], qm[1, pl.ds(q0, GRID_W), :]], axis=0)
                s = lax.dot_general(qcat, k_ref[0, pl.ds(k0, kr_n * GRID_W), :], (((1,), (1,)), ((), ())),
                                    preferred_element_type=F32)
                out.append((q0, k0, shift, s))
            return out

        def finish(chains):
            probs = []
            for q0, k0, shift, s in chains:
                bias = [jnp.concatenate([bias_ref[h, shift + i] for i in range(0, kr_n, 2)], axis=1)
                        for h in range(2)]
                s = s + jnp.concatenate(bias, axis=0)
                p = jnp.exp(s - jnp.max(s, axis=1, keepdims=True))
                probs.append((p.astype(BF16), jnp.sum(p, axis=1, keepdims=True)))
            for (q0, k0, shift, s), (p, l) in zip(chains, probs):
                o = jnp.dot(p, v_ref[0, pl.ds(k0, kr_n * GRID_W), :], preferred_element_type=F32) / l
                g = g_ref[0, pl.ds(q0, GRID_W), :].astype(F32)
                o = _head_select(o[:GRID_W], o[GRID_W:]) * (g * jax.nn.sigmoid(g))
                o_ref[0, pl.ds(q0, GRID_W), :] = o.astype(o_ref.dtype)

        n_batches = NA_UNROLL // NA_BATCH
        pending = scores(0)
        for batch in range(n_batches):
            ahead = scores(batch + 1) if batch + 1 < n_batches else None
            finish(pending)
            pending = ahead
        return carry

    lax.fori_loop(0, rows // NA_UNROLL, row_group, 0)


def _neighbourhood_attention(proj, bias, n_heads):
    _, b, seq, _ = proj.shape
    npair = n_heads // 2
    blk = lambda off: pl.BlockSpec((None, 1, seq, LANES), lambda j, i: (off * npair + j, i, 0, 0))
    return pl.pallas_call(
        _na_kernel,
        grid=(npair, b),
        in_specs=[blk(0), blk(1), blk(2), blk(3),
                  pl.BlockSpec((2,) + bias.shape[1:], lambda j, i: (j, 0, 0, 0))],
        out_specs=pl.BlockSpec((None, 1, seq, LANES), lambda j, i: (j, i, 0, 0)),
        out_shape=jax.ShapeDtypeStruct((npair, b, seq, LANES), BF16),
        scratch_shapes=[pltpu.VMEM((2, seq, LANES), BF16)],
        compiler_params=_cparams(("parallel", "parallel")),
        name="neighbourhood_attention",
    )(proj, proj, proj, proj, bias)


def _out_kernel(n_y, n_t, *refs):
    n = n_y + n_t
    y_refs = refs[:n]
    w_refs = refs[n:2 * n]
    h_ref, p_ref, wp_ref, wg_ref, gpost_ref, gple_ref, gnext_ref, h_out, u_out = refs[2 * n:]
    t = None
    for idx, (y_ref, w_ref) in enumerate(zip(y_refs, w_refs)):
        if idx < n_y:
            y = jnp.concatenate([y_ref[c] for c in range(y_ref.shape[0])], axis=1)
            part = jnp.dot(y, w_ref[...], preferred_element_type=F32)
        else:
            part = lax.dot_general(y_ref[0], w_ref[...], (((0,), (0,)), ((), ())), preferred_element_type=F32)
        t = part if t is None else t + part
    h = h_ref[...] + _rms(t, gpost_ref[...])
    e = _rms(jnp.dot(p_ref[...].astype(BF16), wp_ref[...], preferred_element_type=F32), gple_ref[...])
    gate = jax.nn.sigmoid(jnp.dot(h.astype(BF16), wg_ref[...], preferred_element_type=F32))
    h = h + gate * e
    h_out[...] = h
    u_out[...] = _rms(h, gnext_ref[...]).astype(u_out.dtype)


def _out_block(ys, ys_t, ws, h, p, layer, w_ple, w_gate, g_post, g_ple, g_next, tm=512):
    m, d = h.shape
    row = lambda a: pl.BlockSpec((tm, a.shape[1]), lambda i: (i, 0))
    slab = lambda a: pl.BlockSpec((a.shape[0], tm, LANES), lambda i: (0, i, 0))
    p_spec = pl.BlockSpec((None, tm, p.shape[2]), lambda i: (layer, i, 0))
    full = lambda a: pl.BlockSpec(a.shape, lambda i: (0, 0))
    vec = lambda g: g.reshape(1, d)

    def col(a):
        tiles = a.shape[2] // tm
        return pl.BlockSpec((1, a.shape[1], tm), lambda i: (i // tiles, 0, i % tiles))

    return pl.pallas_call(
        functools.partial(_out_kernel, len(ys), len(ys_t)),
        grid=(m // tm,),
        in_specs=[slab(y) for y in ys] + [col(y) for y in ys_t] + [full(w) for w in ws]
                 + [row(h), p_spec, full(w_ple), full(w_gate), full(vec(g_post)), full(vec(g_ple)), full(vec(g_next))],
        out_specs=[pl.BlockSpec((tm, d), lambda i: (i, 0)), pl.BlockSpec((tm, d), lambda i: (i, 0))],
        out_shape=[jax.ShapeDtypeStruct((m, d), F32), jax.ShapeDtypeStruct((m, d), BF16)],
        compiler_params=_cparams(("parallel",)),
        name="out_proj_residual_ple",
    )(*ys, *ys_t, *ws, h, p, w_ple, w_gate, vec(g_post), vec(g_ple), vec(g_next))


def kernel(x, p, w_in, w_out, norm_pre, norm_post, hyena_conv_w, hyena_conv_b, hyena_w1, hyena_b1, hyena_w2, hyena_b2, hyena_w3, hyena_b3, hyena_w4, hyena_freq, hyena_skip, rel_bias, na_rpb, ple_proj, ple_norm, ple_gate):
    b, seq, d = x.shape
    depth = w_in.shape[0]
    d_inner = w_out.shape[1]
    hw = d_inner // 2
    m = b * seq
    consts = _fft_consts(seq)
    dil_bias = _dilated_bias(rel_bias)
    h = x.reshape(m, d)
    u = _rmsnorm_bf16(h, norm_pre[0])
    w_all = w_in.astype(BF16)
    p_all = p.reshape(depth, m, -1)
    for i in range(depth):
        j = i // 2
        if i % 2 == 0:
            w_a = w_all[i, :, :4 * hw].T
            ident = jnp.tile(jnp.array([0.0, 1.0, 0.0], F32)[:, None], (1, hw))
            conv_w = jnp.concatenate([hyena_conv_w[j], ident], axis=1).T
            conv_b = jnp.concatenate([hyena_conv_b[j], jnp.zeros((hw,), F32)])[:, None]
            hy = _proj_t(w_a, u.reshape(b, seq, d), conv_w, conv_b)
            halves = _hyena_filters_t(seq, hw, hyena_w1[j], hyena_b1[j], hyena_w2[j], hyena_b2[j],
                                      hyena_w3[j], hyena_b3[j], hyena_w4[j], hyena_freq[j])
            kr, ki = _filter_spectrum(halves, consts)
            skip = hyena_skip[j][:, :, None, None]
            z = _hyena_conv(False, hy, 0, hy, hw, hw, skip[0], kr, ki, 0, consts)
            ya_t = _hyena_conv(True, z, 0, hy, 2 * hw, 3 * hw, skip[1], kr, ki, 1, consts)
            proj_b = _matmul_bf16(u, w_all[i:i + 1, :, 4 * hw:], 0, 0, 4 * hw).reshape(-1, b, seq, LANES)
            yb = _dilated_attention(proj_b, dil_bias, hw // HEAD_DIM)
            ys, ys_t = [yb.reshape(-1, m, LANES)], [ya_t]
            ws = [w_out[i, hw:].astype(BF16), w_out[i, :hw].astype(BF16)]
        else:
            proj = _matmul_bf16(u, w_all, i, 0, 4 * d_inner).reshape(-1, b, seq, LANES)
            yc = _neighbourhood_attention(proj, _na_bias(na_rpb[j], seq // GRID_W), d_inner // HEAD_DIM)
            ys, ys_t = [yc.reshape(-1, m, LANES)], []
            ws = [w_out[i].astype(BF16)]
        g_next = norm_pre[i + 1] if i + 1 < depth else norm_pre[i]
        h, u = _out_block(ys, ys_t, ws, h, p_all, i, ple_proj[i].astype(BF16), ple_gate[i].astype(BF16),
                          norm_post[i], ple_norm[i], g_next)
    return h.reshape(b, seq, d)
```

```python
import functools
import math

import numpy as np
import jax
import jax.numpy as jnp
from jax import lax
from jax.experimental import pallas as pl
from jax.experimental.pallas import tpu as pltpu

F32 = jnp.float32
BF16 = jnp.bfloat16

HEAD_DIM = 64
LANES = 128
HYENA_EMB_DIM = 33
HYENA_FAST_DECAY = 0.3
HYENA_SLOW_DECAY = 1.5
HYENA_TARGET = 1e-2
DILATED_PATTERNS = ((128, 1), (512, 4), (2048, 16))
GRID_W = 64
NA_ROWS = 8
NA_COLS = 16
REL_BUCKETS = 32
REL_MAX_DIST = 1024
RMS_EPS = 1e-6
NEG_INF = -1e30
ATT_QB = 128
ATT_HALF = 64
ATT_KW = ATT_QB + 2 * ATT_HALF
ATT_UNROLL = 8
ATT_STEP = 4
assert all(b[1] == a[1] * ATT_STEP for a, b in zip(DILATED_PATTERNS, DILATED_PATTERNS[1:])) and DILATED_PATTERNS[0][1] == 1
NA_BATCH = 4
NA_UNROLL = 16
VMEM_LIMIT = 56 * 1024 * 1024


def _cparams(sem):
    return pltpu.CompilerParams(dimension_semantics=sem, vmem_limit_bytes=VMEM_LIMIT)


def _rms(x, g):
    return x * lax.rsqrt(jnp.mean(x * x, axis=-1, keepdims=True) + RMS_EPS) * g


def _rmsnorm_kernel(x_ref, g_ref, o_ref):
    o_ref[...] = _rms(x_ref[...], g_ref[...]).astype(o_ref.dtype)


def _rmsnorm_bf16(x2d, g, tm=1024):
    m, d = x2d.shape
    return pl.pallas_call(
        _rmsnorm_kernel,
        grid=(m // tm,),
        in_specs=[pl.BlockSpec((tm, d), lambda i: (i, 0)), pl.BlockSpec((1, d), lambda i: (0, 0))],
        out_specs=pl.BlockSpec((tm, d), lambda i: (i, 0)),
        out_shape=jax.ShapeDtypeStruct((m, d), BF16),
        compiler_params=_cparams(("parallel",)),
        name="rmsnorm",
    )(x2d, g.reshape(1, d))


def _matmul_kernel(a_ref, b_ref, o_ref):
    res = jnp.dot(a_ref[...], b_ref[...], preferred_element_type=F32).astype(o_ref.dtype)
    for c in range(o_ref.shape[0]):
        o_ref[c] = res[:, c * LANES:(c + 1) * LANES]


def _matmul_bf16(a, w, layer, col0, n, tm=2048, tn=1024):
    m, k = a.shape
    return pl.pallas_call(
        _matmul_kernel,
        grid=(m // tm, n // tn),
        in_specs=[pl.BlockSpec((tm, k), lambda i, j: (i, 0)), pl.BlockSpec((None, k, tn), lambda i, j: (layer, 0, col0 // tn + j))],
        out_specs=pl.BlockSpec((tn // LANES, tm, LANES), lambda i, j: (j, i, 0)),
        out_shape=jax.ShapeDtypeStruct((n // LANES, m, LANES), BF16),
        compiler_params=_cparams(("parallel", "parallel")),
        name="in_proj",
    )(a, w)


def _proj_t_kernel(w_ref, u_ref, cw_ref, cb_ref, o_ref):
    p = lax.dot_general(w_ref[...], u_ref[0], (((1,), (1,)), ((), ())), preferred_element_type=F32)
    seq = p.shape[1]
    t = lax.broadcasted_iota(jnp.int32, p.shape, 1)
    prev = jnp.where(t == 0, 0.0, pltpu.roll(p, 1, axis=1))
    nxt = jnp.where(t == seq - 1, 0.0, pltpu.roll(p, seq - 1, axis=1))
    cw = cw_ref[...]
    o_ref[0] = (prev * cw[:, 0:1] + p * cw[:, 1:2] + nxt * cw[:, 2:3] + cb_ref[...]).astype(o_ref.dtype)


def _proj_t(w_t, u, conv_w, conv_b, tc=256):
    c, d = w_t.shape
    b, seq, _ = u.shape
    return pl.pallas_call(
        _proj_t_kernel,
        grid=(b, c // tc),
        in_specs=[
            pl.BlockSpec((tc, d), lambda i, j: (j, 0)),
            pl.BlockSpec((1, seq, d), lambda i, j: (i, 0, 0)),
            pl.BlockSpec((tc, 3), lambda i, j: (j, 0)),
            pl.BlockSpec((tc, 1), lambda i, j: (j, 0)),
        ],
        out_specs=pl.BlockSpec((1, tc, seq), lambda i, j: (i, j, 0)),
        out_shape=jax.ShapeDtypeStruct((b, c, seq), BF16),
        compiler_params=_cparams(("parallel", "parallel")),
        name="in_proj_t_conv3",
    )(w_t, u, conv_w, conv_b)


def _hyena_filter_kernel(z_ref, w1_ref, b1_ref, w2_ref, b2_ref, w3_ref, b3_ref, fq_ref, w4_ref, dl_ref, t_ref,
                         o_ref, a_scr):
    hp = lax.Precision.HIGHEST

    @pl.when((pl.program_id(0) == 0) & (pl.program_id(1) == 0))
    def _():
        for d in range(2):
            a = z_ref[d]
            for w_ref, b_ref in ((w1_ref, b1_ref), (w2_ref, b2_ref), (w3_ref, b3_ref)):
                a = jnp.sin(fq_ref[...] * (jnp.dot(w_ref[...], a, precision=hp, preferred_element_type=F32) + b_ref[...]))
            a_scr[d] = a

    seq = o_ref.shape[-1]
    proj = lambda d: jnp.dot(w4_ref[0, d].astype(BF16), a_scr[d].astype(BF16), preferred_element_type=F32)
    lo = proj(0) * jnp.exp(-t_ref[0] * dl_ref[...])
    hi = proj(1) * jnp.exp(-t_ref[1] * dl_ref[...])
    hi = jnp.where(lax.broadcasted_iota(jnp.int32, (1, seq), 1) == 0, 0.0, hi)
    norm = jnp.sum(jnp.abs(lo), axis=1, keepdims=True) + jnp.sum(jnp.abs(hi), axis=1, keepdims=True)
    o_ref[0, 0] = (lo / norm).astype(o_ref.dtype)
    o_ref[1, 0] = (hi / norm).astype(o_ref.dtype)


def _hyena_filters_t(seq, c, w1, b1, w2, b2, w3, b3, w4, freq, tc=256):
    tc = min(tc, c)
    fo = w2.shape[0]
    emb = -(-HYENA_EMB_DIM // 8) * 8
    t = jnp.linspace(0.0, 1.0, seq, dtype=F32)
    bands = (HYENA_EMB_DIM - 1) // 2
    fr = jnp.linspace(1e-4, bands - 1, bands, dtype=F32)[:, None]
    mirror = (seq - jnp.arange(seq)) % seq
    tables, times = [], []
    for pos, tp in ((jnp.arange(seq), t), (mirror, jnp.roll(t[::-1], 1))):
        wpos = (2.0 * math.pi / seq) * pos.astype(F32)[None, :]
        tp = tp[None, :]
        z = jnp.concatenate([tp, jnp.cos(fr * wpos), -jnp.sin(fr * wpos),
                             jnp.zeros((emb - HYENA_EMB_DIM, seq), F32)], axis=0)
        tables.append(z)
        times.append(tp)
    z = jnp.stack(tables)
    tt = jnp.stack(times)
    w1_t = jnp.pad(w1.astype(F32).T, ((0, 0), (0, emb - HYENA_EMB_DIM)))
    w4_t = jnp.transpose(w4.astype(F32).reshape(fo, 2, 2, c), (1, 2, 3, 0))
    min_decay = math.log(HYENA_TARGET) / HYENA_SLOW_DECAY
    max_decay = math.log(HYENA_TARGET) / HYENA_FAST_DECAY
    deltas = jnp.abs(jnp.linspace(min_decay, max_decay, c, dtype=F32))[:, None]
    col = lambda v: v.astype(F32)[:, None]
    full = lambda a: pl.BlockSpec(a.shape, lambda f, i: (0,) * a.ndim)
    args = (z, w1_t, col(b1), w2.astype(F32).T, col(b2), w3.astype(F32).T, col(b3), col(freq))
    return pl.pallas_call(
        _hyena_filter_kernel,
        grid=(2, c // tc),
        in_specs=[full(a) for a in args]
                 + [pl.BlockSpec((1, 2, tc, fo), lambda f, i: (f, 0, i, 0)),
                    pl.BlockSpec((tc, 1), lambda f, i: (i, 0)), full(tt)],
        out_specs=pl.BlockSpec((2, 1, tc, seq), lambda f, i: (0, f, i, 0)),
        out_shape=jax.ShapeDtypeStruct((2, 2, c, seq), BF16),
        scratch_shapes=[pltpu.VMEM((2, fo, seq), F32)],
        compiler_params=_cparams(("arbitrary", "arbitrary")),
        name="hyena_filter_mlp",
    )(*args, w4_t, deltas, tt)


def _fft_consts(seq):
    n_fft = 2 * seq
    n1 = n_fft // LANES
    nb = n1 // 2 + 1
    kp = -(-nb // 8) * 8
    k1 = jnp.arange(kp, dtype=jnp.int32)
    keep = (k1 < nb)[:, None]
    two_pi = 2.0 * math.pi

    th = ((k1[:, None] * jnp.arange(n1, dtype=jnp.int32)[None, :]) % n1).astype(F32) * (two_pi / n1)
    f1 = jnp.concatenate([jnp.where(keep, jnp.cos(th), 0.0), jnp.where(keep, -jnp.sin(th), 0.0)], axis=0)
    ph = (k1[:, None] * jnp.arange(LANES, dtype=jnp.int32)[None, :]).astype(F32) * (two_pi / n_fft)
    tw = jnp.stack([jnp.cos(ph), -jnp.sin(ph)])
    om = ((jnp.arange(LANES, dtype=jnp.int32)[:, None] * jnp.arange(LANES, dtype=jnp.int32)[None, :]) % LANES
          ).astype(F32) * (two_pi / LANES)
    f2 = jnp.concatenate([jnp.cos(om), -jnp.sin(om)], axis=1)
    f2i = jnp.concatenate([jnp.cos(om), jnp.sin(om)], axis=1)
    weight = jnp.where((k1 == 0) | (k1 == nb - 1), 1.0, 2.0)[:, None]
    g = jnp.concatenate([jnp.where(keep, weight * jnp.cos(th[:, :n1 // 2]), 0.0),
                         jnp.where(keep, -weight * jnp.sin(th[:, :n1 // 2]), 0.0)], axis=0).T
    return dict(f1=f1.astype(BF16), tw=tw, f2=f2.astype(BF16), f2i=f2i.astype(BF16), g=g.astype(BF16), kp=kp)


def _per_channel_matmul(mat_ref, planes, out_scr, ct):
    def body(i, carry):
        c = 2 * i
        pair = jnp.concatenate([planes(c), planes(c + 1)], axis=1)
        res = jnp.dot(mat_ref[...], pair, preferred_element_type=F32)
        out_scr[c] = res[:, :LANES]
        out_scr[c + 1] = res[:, LANES:]
        return carry
    lax.fori_loop(0, ct // 2, body, 0, unroll=16)


def _twiddle_stage2(a, tw_ref, f2_ref):
    ct, kp2, _ = a.shape
    kp = kp2 // 2
    ar, ai = a[:, :kp], a[:, kp:]
    twr, twi = tw_ref[0], tw_ref[1]
    st = jnp.concatenate([ar * twr - ai * twi, ar * twi + ai * twr], axis=1).astype(BF16)
    p = jnp.dot(st.reshape(ct * kp2, LANES), f2_ref[...], preferred_element_type=F32).reshape(ct, kp2, 2 * LANES)
    return p[:, :kp, :LANES] - p[:, kp:, LANES:], p[:, :kp, LANES:] + p[:, kp:, :LANES]


def _stage2_twiddle_inv(yr, yi, tw_ref, f2i_ref):
    ct, kp, _ = yr.shape
    st = jnp.concatenate([yr, yi], axis=1).astype(BF16)
    p = jnp.dot(st.reshape(ct * 2 * kp, LANES), f2i_ref[...], preferred_element_type=F32).reshape(ct, 2 * kp, 2 * LANES)
    tr, ti = p[:, :kp, :LANES] - p[:, kp:, LANES:], p[:, :kp, LANES:] + p[:, kp:, :LANES]
    twr, twi = tw_ref[0], tw_ref[1]
    return jnp.concatenate([tr * twr + ti * twi, ti * twr - tr * twi], axis=1).astype(BF16)


def _filter_spec_kernel(lo_ref, hi_ref, f1_ref, tw_ref, f2_ref, kr_ref, ki_ref, a_scr):
    ct = lo_ref.shape[2]
    planes = lambda c: jnp.concatenate([lo_ref[0, 0, c], hi_ref[0, 0, c]], axis=0)
    _per_channel_matmul(f1_ref, planes, a_scr, ct)
    xr, xi = _twiddle_stage2(a_scr[...], tw_ref, f2_ref)
    n_fft = f1_ref.shape[1] * LANES
    kr_ref[0] = xr * (1.0 / n_fft)
    ki_ref[0] = xi * (1.0 / n_fft)


def _filter_spectrum(halves, consts, ct=64):
    _, nf, c, seq = halves.shape
    ct = min(ct, c)
    kp = consts["kp"]
    h5 = halves.reshape(2, nf, c, seq // LANES, LANES)
    half = lambda s: pl.BlockSpec((1, 1, ct, seq // LANES, LANES), lambda f, i: (s, f, i, 0, 0))
    full = lambda a: pl.BlockSpec(a.shape, lambda f, i: (0,) * a.ndim)
    spec = jax.ShapeDtypeStruct((nf, c, kp, LANES), F32)
    out = pl.BlockSpec((1, ct, kp, LANES), lambda f, i: (f, i, 0, 0))
    return pl.pallas_call(
        _filter_spec_kernel,
        grid=(nf, c // ct),
        in_specs=[half(0), half(1), full(consts["f1"]), full(consts["tw"]), full(consts["f2"])],
        out_specs=[out, out],
        out_shape=[spec, spec],
        scratch_shapes=[pltpu.VMEM((ct, 2 * kp, LANES), F32)],
        compiler_params=_cparams(("parallel", "parallel")),
        name="hyena_filter_spectrum",
    )(h5, h5, consts["f1"], consts["tw"], consts["f2"])


def _hyena_conv_kernel(second, x_ref, m_ref, e_ref, skip_ref, kr_ref, ki_ref, f1_ref, tw_ref, f2_ref, f2i_ref, g_ref,
                       o_ref, a_scr, t_scr, y_scr):
    ct = x_ref.shape[1]
    _per_channel_matmul(f1_ref, lambda c: x_ref[0, c], a_scr, ct)
    xr, xi = _twiddle_stage2(a_scr[...], tw_ref, f2_ref)
    kr, ki = kr_ref[0], ki_ref[0]
    t_scr[...] = _stage2_twiddle_inv(xr * kr - xi * ki, xr * ki + xi * kr, tw_ref, f2i_ref)

    _per_channel_matmul(g_ref, lambda c: t_scr[c], y_scr, ct)

    out = m_ref[0].astype(F32) * (y_scr[...] + x_ref[0].astype(F32) * skip_ref[...])
    if second:
        e = e_ref[0].astype(F32)
        out = out * (e * jax.nn.sigmoid(e))
    o_ref[0] = out.astype(o_ref.dtype)


def _hyena_conv(second, x, x_row0, hy, m_row0, e_row0, skip, kr, ki, filt, consts, ct=128):
    b, _, seq = x.shape
    c = kr.shape[1]
    ct = min(ct, c)
    kp = consts["kp"]
    n1h = seq // LANES
    view = lambda a: a.reshape(a.shape[0], a.shape[1], n1h, LANES)
    rows = lambda blk0: pl.BlockSpec((1, ct, n1h, LANES), lambda j, i: (i, blk0 + j, 0, 0))
    full = lambda a: pl.BlockSpec(a.shape, lambda j, i: (0,) * a.ndim)
    kspec = pl.BlockSpec((1, ct, kp, LANES), lambda j, i: (filt, j, 0, 0))
    mats = [consts["f1"][:, :n1h], consts["tw"], consts["f2"], consts["f2i"], consts["g"]]
    out = pl.pallas_call(
        functools.partial(_hyena_conv_kernel, second),
        grid=(c // ct, b),
        in_specs=[rows(x_row0 // ct), rows(m_row0 // ct), rows(e_row0 // ct),
                  pl.BlockSpec((ct, 1, 1), lambda j, i: (j, 0, 0)),
                  kspec, kspec] + [full(a) for a in mats],
        out_specs=pl.BlockSpec((1, ct, n1h, LANES), lambda j, i: (i, j, 0, 0)),
        out_shape=jax.ShapeDtypeStruct((b, c, n1h, LANES), BF16),
        scratch_shapes=[pltpu.VMEM((ct, 2 * kp, LANES), F32), pltpu.VMEM((ct, 2 * kp, LANES), BF16),
                        pltpu.VMEM((ct, n1h, LANES), F32)],
        compiler_params=_cparams(("parallel", "parallel")),
        name="hyena_conv2" if second else "hyena_conv1",
    )(view(x), view(hy), view(hy), skip, kr, ki, *mats)
    return out.reshape(b, c, seq)


def _t5_bucket(rel):
    half_b = REL_BUCKETS // 2
    max_exact = half_b // 2
    ret = jnp.where(rel > 0, half_b, 0)
    n = jnp.abs(rel)
    nf = jnp.maximum(n, 1).astype(F32)
    large = max_exact + (jnp.log(nf / max_exact) / math.log(REL_MAX_DIST / max_exact)
                         * (half_b - max_exact)).astype(jnp.int32)
    large = jnp.minimum(large, half_b - 1)
    return ret + jnp.where(n < max_exact, n, large)


def _dilated_bias(rel_bias):
    rel = jnp.arange(ATT_KW)[None, :] - ATT_HALF - jnp.arange(ATT_QB)[:, None]
    col = jnp.arange(ATT_KW)
    edge = jnp.stack([col >= ATT_HALF, col >= 0, col < ATT_KW - ATT_HALF])
    out = []
    for _, dil in DILATED_PATTERNS:
        onehot = jax.nn.one_hot(_t5_bucket(rel * dil), REL_BUCKETS, dtype=F32)
        bias = jnp.einsum("qkb,bh->qkh", onehot, rel_bias.astype(F32), precision=lax.Precision.HIGHEST)
        bias = jnp.where((jnp.abs(rel) <= ATT_HALF)[:, :, None], bias, NEG_INF)
        bias = jnp.transpose(bias, (2, 0, 1))[:, None]
        out.append(jnp.where(edge[None, :, None, :], bias, NEG_INF))
    return jnp.stack(out)


def _head_select(first, second):
    lane = lax.broadcasted_iota(jnp.int32, first.shape[:-1] + (LANES,), len(first.shape) - 1)
    return jnp.where(lane < HEAD_DIM, first, second)


def _dilated_kernel(q_ref, k_ref, v_ref, g_ref, bias_ref, o_ref, *scratch):
    stage_a, stage_b = scratch[0:3], scratch[3:6]
    qd, kd, vd = scratch[6:9]
    last = len(DILATED_PATTERNS) - 1
    states = [scratch[9 + 3 * p:12 + 3 * p] for p in range(last)]
    seq = q_ref.shape[1]
    stage_a[0][...] = q_ref[0].astype(F32) * (HEAD_DIM ** -0.5)
    stage_a[1][...] = k_ref[0].astype(F32)
    stage_a[2][...] = v_ref[0].astype(F32)
    zero_halo = jnp.zeros((ATT_HALF, LANES), BF16)
    first_head = lax.broadcasted_iota(jnp.int32, (ATT_QB, LANES), 1) < HEAD_DIM

    src, dst = stage_a, stage_b
    for pi, (_, dil) in enumerate(DILATED_PATTERNS):
        n = seq // dil
        nblk = n // ATT_QB
        if pi > 0:
            prev_dil = dil // ATT_STEP
            for r1 in range(prev_dil):
                for r2 in range(ATT_STEP):
                    r = r1 + prev_dil * r2
                    for a_src, a_dst in zip(src, dst):
                        a_dst[r * n:(r + 1) * n, :] = a_src[pl.ds(r1 * n * ATT_STEP + r2, n, stride=ATT_STEP), :]
            src, dst = dst, src
        stride_k = n + 2 * ATT_HALF
        qd[...] = src[0][...].astype(BF16)
        for r in range(dil):
            base = r * stride_k
            for a_src, a_dst in ((src[1], kd), (src[2], vd)):
                a_dst[base:base + ATT_HALF, :] = zero_halo
                a_dst[base + ATT_HALF:base + ATT_HALF + n, :] = a_src[r * n:(r + 1) * n, :].astype(BF16)
                a_dst[base + ATT_HALF + n:base + stride_k, :] = zero_halo
        if pi == last:
            states.append(dst)
        acc_w, mx_w, den_w = states[pi]

        def block_group(it, carry, pi=pi, nblk=nblk, acc_w=acc_w, mx_w=mx_w, den_w=den_w):
            chains = []
            for u in range(ATT_UNROLL):
                blk = it * ATT_UNROLL + u
                r = blk // nblk
                qb = blk % nblk
                q0 = pl.multiple_of(blk * ATT_QB, ATT_QB)
                k0 = pl.multiple_of(q0 + r * (2 * ATT_HALF), ATT_QB)
                edge = jnp.where(qb == 0, 0, jnp.where(qb == nblk - 1, 2, 1))
                qblk = qd[pl.ds(q0, ATT_QB), :]
                qcat = jnp.concatenate([jnp.where(first_head, qblk, jnp.zeros_like(qblk)),
                                        jnp.where(first_head, jnp.zeros_like(qblk), qblk)], axis=0)
                s = lax.dot_general(qcat, kd[pl.ds(k0, ATT_KW), :], (((1,), (1,)), ((), ())),
                                    preferred_element_type=F32)
                chains.append((q0, k0, edge, s))
            probs = []
            for q0, k0, edge, s in chains:
                s = s + jnp.concatenate([bias_ref[pi, 0, edge], bias_ref[pi, 1, edge]], axis=0)
                m = jnp.max(s, axis=1, keepdims=True)
                p = jnp.exp(s - m)
                probs.append((p.astype(BF16), m, jnp.sum(p, axis=1, keepdims=True)))
            for (q0, k0, edge, s), (p, m, l) in zip(chains, probs):
                pv = jnp.dot(p, vd[pl.ds(k0, ATT_KW), :], preferred_element_type=F32)
                acc_w[pl.ds(q0, ATT_QB), :] = _head_select(pv[:ATT_QB], pv[ATT_QB:])
                mx_w[pl.ds(q0, ATT_QB), :] = _head_select(m[:ATT_QB], m[ATT_QB:])
                den_w[pl.ds(q0, ATT_QB), :] = _head_select(l[:ATT_QB], l[ATT_QB:])
            return carry

        lax.fori_loop(0, dil * nblk // ATT_UNROLL, block_group, 0)

    for pi in range(len(DILATED_PATTERNS) - 1, 0, -1):
        dil = DILATED_PATTERNS[pi][1]
        n = seq // dil
        prev_dil = dil // ATT_STEP
        acc_c, mx_c, den_c = states[pi - 1]
        acc_f, mx_f, den_f = states[pi]
        for r1 in range(prev_dil):
            for r2 in range(ATT_STEP):
                r = r1 + prev_dil * r2
                rows = pl.ds(r1 * n * ATT_STEP + r2, n, stride=ATT_STEP)
                fine = slice(r * n, (r + 1) * n)
                m_old = mx_c[rows, :]
                m_new = mx_f[fine, :]
                m_all = jnp.maximum(m_old, m_new)
                decay = jnp.exp(-jnp.abs(m_old - m_new))
                a_old = jnp.where(m_old >= m_new, 1.0, decay)
                a_new = jnp.where(m_old >= m_new, decay, 1.0)
                acc_c[rows, :] = acc_c[rows, :] * a_old + acc_f[fine, :] * a_new
                den_c[rows, :] = den_c[rows, :] * a_old + den_f[fine, :] * a_new
                mx_c[rows, :] = m_all

    acc, _, den = states[0]
    g = g_ref[0].astype(F32)
    o_ref[0] = (acc[...] / den[...] * (g * jax.nn.sigmoid(g))).astype(o_ref.dtype)


def _dilated_attention(proj, bias, n_heads):
    _, b, seq, _ = proj.shape
    npair = n_heads // 2
    max_dil = max(d for _, d in DILATED_PATTERNS)
    kd_rows = seq + 2 * ATT_HALF * max_dil
    blk = lambda off: pl.BlockSpec((None, 1, seq, LANES), lambda i, j: (off * npair + j, i, 0, 0))
    state = pltpu.VMEM((seq, LANES), F32)
    return pl.pallas_call(
        _dilated_kernel,
        grid=(b, npair),
        in_specs=[blk(0), blk(1), blk(2), blk(3),
                  pl.BlockSpec((len(DILATED_PATTERNS), 2, 3, ATT_QB, ATT_KW), lambda i, j: (0, j, 0, 0, 0))],
        out_specs=pl.BlockSpec((None, 1, seq, LANES), lambda i, j: (j, i, 0, 0)),
        out_shape=jax.ShapeDtypeStruct((npair, b, seq, LANES), BF16),
        scratch_shapes=[state] * 6
                       + [pltpu.VMEM((seq, LANES), BF16), pltpu.VMEM((kd_rows, LANES), BF16),
                          pltpu.VMEM((kd_rows, LANES), BF16)]
                       + [state] * (3 * (len(DILATED_PATTERNS) - 1)),
        compiler_params=_cparams(("parallel", "parallel")),
        name="dilated_attention",
    )(proj, proj, proj, proj, bias)


def _na_bias(rpb, rows):
    assert min(NA_ROWS, rows) % 2 == 0
    cols = np.arange(GRID_W)
    cs = np.clip(cols - NA_COLS // 2, 0, GRID_W - NA_COLS)
    kc = np.arange(GRID_W)
    inside = (kc[None, :] >= cs[:, None]) & (kc[None, :] < cs[:, None] + NA_COLS)
    col_idx = kc[None, :] - cols[:, None] + NA_COLS - 1
    onehot = (col_idx[:, :, None] == np.arange(2 * NA_COLS - 1)[None, None, :]).astype(np.float32)
    e = jnp.einsum("hrd,cqd->hrcq", rpb.astype(F32), jnp.asarray(onehot), precision=lax.Precision.HIGHEST)
    e = jnp.where(jnp.asarray(inside)[None, None], e, NEG_INF)
    return jnp.concatenate([e[:, :-1], e[:, 1:]], axis=-1)


def _na_kernel(q_ref, k_ref, v_ref, g_ref, bias_ref, o_ref, qm):
    seq = q_ref.shape[1]
    rows = seq // GRID_W
    kr_n = min(NA_ROWS, rows)
    lane = lax.broadcasted_iota(jnp.int32, (seq, LANES), 1)
    q = q_ref[0] * (HEAD_DIM ** -0.5)
    for h in range(2):
        qm[h] = jnp.where((lane < HEAD_DIM) == (h == 0), q, jnp.zeros_like(q))

    def row_group(it, carry):
        def scores(batch):
            out = []
            for u in range(batch * NA_BATCH, (batch + 1) * NA_BATCH):
                r = it * NA_UNROLL + u
                rs = jnp.clip(r - kr_n // 2, 0, rows - kr_n)
                shift = rs - r + NA_ROWS - 1
                q0 = pl.multiple_of(r * GRID_W, GRID_W)
                k0 = pl.multiple_of(rs * GRID_W, GRID_W)
                qcat = jnp.concatenate([qm[0, pl.ds(q0, GRID_W), :], qm[1, pl.ds(q0, GRID_W), :]], axis=0)
                s = lax.dot_general(qcat, k_ref[0, pl.ds(k0, kr_n * GRID_W), :], (((1,), (1,)), ((), ())),
                                    preferred_element_type=F32)
                out.append((q0, k0, shift, s))
            return out

        def finish(chains):
            probs = []
            for q0, k0, shift, s in chains:
                bias = [jnp.concatenate([bias_ref[h, shift + i] for i in range(0, kr_n, 2)], axis=1)
                        for h in range(2)]
                s = s + jnp.concatenate(bias, axis=0)
                p = jnp.exp(s - jnp.max(s, axis=1, keepdims=True))
                probs.append((p.astype(BF16), jnp.sum(p, axis=1, keepdims=True)))
            for (q0, k0, shift, s), (p, l) in zip(chains, probs):
                o = jnp.dot(p, v_ref[0, pl.ds(k0, kr_n * GRID_W), :], preferred_element_type=F32) / l
                g = g_ref[0, pl.ds(q0, GRID_W), :].astype(F32)
                o = _head_select(o[:GRID_W], o[GRID_W:]) * (g * jax.nn.sigmoid(g))
                o_ref[0, pl.ds(q0, GRID_W), :] = o.astype(o_ref.dtype)

        n_batches = NA_UNROLL // NA_BATCH
        pending = scores(0)
        for batch in range(n_batches):
            ahead = scores(batch + 1) if batch + 1 < n_batches else None
            finish(pending)
            pending = ahead
        return carry

    lax.fori_loop(0, rows // NA_UNROLL, row_group, 0)


def _neighbourhood_attention(proj, bias, n_heads):
    _, b, seq, _ = proj.shape
    npair = n_heads // 2
    blk = lambda off: pl.BlockSpec((None, 1, seq, LANES), lambda j, i: (off * npair + j, i, 0, 0))
    return pl.pallas_call(
        _na_kernel,
        grid=(npair, b),
        in_specs=[blk(0), blk(1), blk(2), blk(3),
                  pl.BlockSpec((2,) + bias.shape[1:], lambda j, i: (j, 0, 0, 0))],
        out_specs=pl.BlockSpec((None, 1, seq, LANES), lambda j, i: (j, i, 0, 0)),
        out_shape=jax.ShapeDtypeStruct((npair, b, seq, LANES), BF16),
        scratch_shapes=[pltpu.VMEM((2, seq, LANES), BF16)],
        compiler_params=_cparams(("parallel", "parallel")),
        name="neighbourhood_attention",
    )(proj, proj, proj, proj, bias)


def _out_kernel(n_y, n_t, *refs):
    n = n_y + n_t
    y_refs = refs[:n]
    w_refs = refs[n:2 * n]
    h_ref, p_ref, wp_ref, wg_ref, gpost_ref, gple_ref, gnext_ref, h_out, u_out = refs[2 * n:]
    t = None
    for idx, (y_ref, w_ref) in enumerate(zip(y_refs, w_refs)):
        if idx < n_y:
            y = jnp.concatenate([y_ref[c] for c in range(y_ref.shape[0])], axis=1)
            part = jnp.dot(y, w_ref[...], preferred_element_type=F32)
        else:
            part = lax.dot_general(y_ref[0], w_ref[...], (((0,), (0,)), ((), ())), preferred_element_type=F32)
        t = part if t is None else t + part
    h = h_ref[...] + _rms(t, gpost_ref[...])
    e = _rms(jnp.dot(p_ref[...].astype(BF16), wp_ref[...], preferred_element_type=F32), gple_ref[...])
    gate = jax.nn.sigmoid(jnp.dot(h.astype(BF16), wg_ref[...], preferred_element_type=F32))
    h = h + gate * e
    h_out[...] = h
    u_out[...] = _rms(h, gnext_ref[...]).astype(u_out.dtype)


def _out_block(ys, ys_t, ws, h, p, layer, w_ple, w_gate, g_post, g_ple, g_next, tm=512):
    m, d = h.shape
    row = lambda a: pl.BlockSpec((tm, a.shape[1]), lambda i: (i, 0))
    slab = lambda a: pl.BlockSpec((a.shape[0], tm, LANES), lambda i: (0, i, 0))
    p_spec = pl.BlockSpec((None, tm, p.shape[2]), lambda i: (layer, i, 0))
    full = lambda a: pl.BlockSpec(a.shape, lambda i: (0, 0))
    vec = lambda g: g.reshape(1, d)

    def col(a):
        tiles = a.shape[2] // tm
        return pl.BlockSpec((1, a.shape[1], tm), lambda i: (i // tiles, 0, i % tiles))

    return pl.pallas_call(
        functools.partial(_out_kernel, len(ys), len(ys_t)),
        grid=(m // tm,),
        in_specs=[slab(y) for y in ys] + [col(y) for y in ys_t] + [full(w) for w in ws]
                 + [row(h), p_spec, full(w_ple), full(w_gate), full(vec(g_post)), full(vec(g_ple)), full(vec(g_next))],
        out_specs=[pl.BlockSpec((tm, d), lambda i: (i, 0)), pl.BlockSpec((tm, d), lambda i: (i, 0))],
        out_shape=[jax.ShapeDtypeStruct((m, d), F32), jax.ShapeDtypeStruct((m, d), BF16)],
        compiler_params=_cparams(("parallel",)),
        name="out_proj_residual_ple",
    )(*ys, *ys_t, *ws, h, p, w_ple, w_gate, vec(g_post), vec(g_ple), vec(g_next))


def kernel(x, p, w_in, w_out, norm_pre, norm_post, hyena_conv_w, hyena_conv_b, hyena_w1, hyena_b1, hyena_w2, hyena_b2, hyena_w3, hyena_b3, hyena_w4, hyena_freq, hyena_skip, rel_bias, na_rpb, ple_proj, ple_norm, ple_gate):
    b, seq, d = x.shape
    depth = w_in.shape[0]
    d_inner = w_out.shape[1]
    hw = d_inner // 2
    m = b * seq
    consts = _fft_consts(seq)
    dil_bias = _dilated_bias(rel_bias)
    h = x.reshape(m, d)
    u = _rmsnorm_bf16(h, norm_pre[0])
    w_all = w_in.astype(BF16)
    p_all = p.reshape(depth, m, -1)
    for i in range(depth):
        j = i // 2
        if i % 2 == 0:
            w_a = w_all[i, :, :4 * hw].T
            ident = jnp.tile(jnp.array([0.0, 1.0, 0.0], F32)[:, None], (1, hw))
            conv_w = jnp.concatenate([hyena_conv_w[j], ident], axis=1).T
            conv_b = jnp.concatenate([hyena_conv_b[j], jnp.zeros((hw,), F32)])[:, None]
            hy = _proj_t(w_a, u.reshape(b, seq, d), conv_w, conv_b)
            halves = _hyena_filters_t(seq, hw, hyena_w1[j], hyena_b1[j], hyena_w2[j], hyena_b2[j],
                                      hyena_w3[j], hyena_b3[j], hyena_w4[j], hyena_freq[j])
            kr, ki = _filter_spectrum(halves, consts)
            skip = hyena_skip[j][:, :, None, None]
            z = _hyena_conv(False, hy, 0, hy, hw, hw, skip[0], kr, ki, 0, consts)
            ya_t = _hyena_conv(True, z, 0, hy, 2 * hw, 3 * hw, skip[1], kr, ki, 1, consts)
            proj_b = _matmul_bf16(u, w_all[i:i + 1, :, 4 * hw:], 0, 0, 4 * hw).reshape(-1, b, seq, LANES)
            yb = _dilated_attention(proj_b, dil_bias, hw // HEAD_DIM)
            ys, ys_t = [yb.reshape(-1, m, LANES)], [ya_t]
            ws = [w_out[i, hw:].astype(BF16), w_out[i, :hw].astype(BF16)]
        else:
            proj = _matmul_bf16(u, w_all, i, 0, 4 * d_inner).reshape(-1, b, seq, LANES)
            yc = _neighbourhood_attention(proj, _na_bias(na_rpb[j], seq // GRID_W), d_inner // HEAD_DIM)
            ys, ys_t = [yc.reshape(-1, m, LANES)], []
            ws = [w_out[i].astype(BF16)]
        g_next = norm_pre[i + 1] if i + 1 < depth else norm_pre[i]
        h, u = _out_block(ys, ys_t, ws, h, p_all, i, ple_proj[i].astype(BF16), ple_gate[i].astype(BF16),
                          norm_post[i], ple_norm[i], g_next)
    return h.reshape(b, seq, d)
```

```python
import functools
import math

import numpy as np
import jax
import jax.numpy as jnp
from jax import lax
from jax.experimental import pallas as pl
from jax.experimental.pallas import tpu as pltpu

F32 = jnp.float32
BF16 = jnp.bfloat16

HEAD_DIM = 64
LANES = 128
HYENA_EMB_DIM = 33
HYENA_FAST_DECAY = 0.3
HYENA_SLOW_DECAY = 1.5
HYENA_TARGET = 1e-2
DILATED_PATTERNS = ((128, 1), (512, 4), (2048, 16))
GRID_W = 64
NA_ROWS = 8
NA_COLS = 16
REL_BUCKETS = 32
REL_MAX_DIST = 1024
RMS_EPS = 1e-6
NEG_INF = -1e30
ATT_QB = 128
ATT_HALF = 64
ATT_KW = ATT_QB + 2 * ATT_HALF
ATT_UNROLL = 8
ATT_STEP = 4
assert all(b[1] == a[1] * ATT_STEP for a, b in zip(DILATED_PATTERNS, DILATED_PATTERNS[1:])) and DILATED_PATTERNS[0][1] == 1
NA_BATCH = 4
NA_UNROLL = 16
VMEM_LIMIT = 56 * 1024 * 1024


def _cparams(sem):
    return pltpu.CompilerParams(dimension_semantics=sem, vmem_limit_bytes=VMEM_LIMIT)


def _rms(x, g):
    return x * lax.rsqrt(jnp.mean(x * x, axis=-1, keepdims=True) + RMS_EPS) * g


def _rmsnorm_kernel(x_ref, g_ref, o_ref):
    o_ref[...] = _rms(x_ref[...], g_ref[...]).astype(o_ref.dtype)


def _rmsnorm_bf16(x2d, g, tm=1024):
    m, d = x2d.shape
    return pl.pallas_call(
        _rmsnorm_kernel,
        grid=(m // tm,),
        in_specs=[pl.BlockSpec((tm, d), lambda i: (i, 0)), pl.BlockSpec((1, d), lambda i: (0, 0))],
        out_specs=pl.BlockSpec((tm, d), lambda i: (i, 0)),
        out_shape=jax.ShapeDtypeStruct((m, d), BF16),
        compiler_params=_cparams(("parallel",)),
        name="rmsnorm",
    )(x2d, g.reshape(1, d))


def _matmul_kernel(a_ref, b_ref, o_ref):
    res = jnp.dot(a_ref[...], b_ref[...], preferred_element_type=F32).astype(o_ref.dtype)
    for c in range(o_ref.shape[0]):
        o_ref[c] = res[:, c * LANES:(c + 1) * LANES]


def _matmul_bf16(a, w, layer, col0, n, tm=2048, tn=1024):
    m, k = a.shape
    return pl.pallas_call(
        _matmul_kernel,
        grid=(m // tm, n // tn),
        in_specs=[pl.BlockSpec((tm, k), lambda i, j: (i, 0)), pl.BlockSpec((None, k, tn), lambda i, j: (layer, 0, col0 // tn + j))],
        out_specs=pl.BlockSpec((tn // LANES, tm, LANES), lambda i, j: (j, i, 0)),
        out_shape=jax.ShapeDtypeStruct((n // LANES, m, LANES), BF16),
        compiler_params=_cparams(("parallel", "parallel")),
        name="in_proj",
    )(a, w)


def _proj_t_kernel(w_ref, u_ref, cw_ref, cb_ref, o_ref):
    p = lax.dot_general(w_ref[...], u_ref[0], (((1,), (1,)), ((), ())), preferred_element_type=F32)
    seq = p.shape[1]
    t = lax.broadcasted_iota(jnp.int32, p.shape, 1)
    prev = jnp.where(t == 0, 0.0, pltpu.roll(p, 1, axis=1))
    nxt = jnp.where(t == seq - 1, 0.0, pltpu.roll(p, seq - 1, axis=1))
    cw = cw_ref[...]
    o_ref[0] = (prev * cw[:, 0:1] + p * cw[:, 1:2] + nxt * cw[:, 2:3] + cb_ref[...]).astype(o_ref.dtype)


def _proj_t(w_t, u, conv_w, conv_b, tc=256):
    c, d = w_t.shape
    b, seq, _ = u.shape
    return pl.pallas_call(
        _proj_t_kernel,
        grid=(b, c // tc),
        in_specs=[
            pl.BlockSpec((tc, d), lambda i, j: (j, 0)),
            pl.BlockSpec((1, seq, d), lambda i, j: (i, 0, 0)),
            pl.BlockSpec((tc, 3), lambda i, j: (j, 0)),
            pl.BlockSpec((tc, 1), lambda i, j: (j, 0)),
        ],
        out_specs=pl.BlockSpec((1, tc, seq), lambda i, j: (i, j, 0)),
        out_shape=jax.ShapeDtypeStruct((b, c, seq), BF16),
        compiler_params=_cparams(("parallel", "parallel")),
        name="in_proj_t_conv3",
    )(w_t, u, conv_w, conv_b)


def _hyena_filter_kernel(z_ref, w1_ref, b1_ref, w2_ref, b2_ref, w3_ref, b3_ref, fq_ref, w4_ref, dl_ref, t_ref,
                         o_ref, a_scr):
    hp = lax.Precision.HIGHEST

    @pl.when((pl.program_id(0) == 0) & (pl.program_id(1) == 0))
    def _():
        for d in range(2):
            a = z_ref[d]
            for w_ref, b_ref in ((w1_ref, b1_ref), (w2_ref, b2_ref), (w3_ref, b3_ref)):
                a = jnp.sin(fq_ref[...] * (jnp.dot(w_ref[...], a, precision=hp, preferred_element_type=F32) + b_ref[...]))
            a_scr[d] = a

    seq = o_ref.shape[-1]
    proj = lambda d: jnp.dot(w4_ref[0, d].astype(BF16), a_scr[d].astype(BF16), preferred_element_type=F32)
    lo = proj(0) * jnp.exp(-t_ref[0] * dl_ref[...])
    hi = proj(1) * jnp.exp(-t_ref[1] * dl_ref[...])
    hi = jnp.where(lax.broadcasted_iota(jnp.int32, (1, seq), 1) == 0, 0.0, hi)
    norm = jnp.sum(jnp.abs(lo), axis=1, keepdims=True) + jnp.sum(jnp.abs(hi), axis=1, keepdims=True)
    o_ref[0, 0] = (lo / norm).astype(o_ref.dtype)
    o_ref[1, 0] = (hi / norm).astype(o_ref.dtype)


def _hyena_filters_t(seq, c, w1, b1, w2, b2, w3, b3, w4, freq, tc=256):
    tc = min(tc, c)
    fo = w2.shape[0]
    emb = -(-HYENA_EMB_DIM // 8) * 8
    t = jnp.linspace(0.0, 1.0, seq, dtype=F32)
    bands = (HYENA_EMB_DIM - 1) // 2
    fr = jnp.linspace(1e-4, bands - 1, bands, dtype=F32)[:, None]
    mirror = (seq - jnp.arange(seq)) % seq
    tables, times = [], []
    for pos, tp in ((jnp.arange(seq), t), (mirror, jnp.roll(t[::-1], 1))):
        wpos = (2.0 * math.pi / seq) * pos.astype(F32)[None, :]
        tp = tp[None, :]
        z = jnp.concatenate([tp, jnp.cos(fr * wpos), -jnp.sin(fr * wpos),
                             jnp.zeros((emb - HYENA_EMB_DIM, seq), F32)], axis=0)
        tables.append(z)
        times.append(tp)
    z = jnp.stack(tables)
    tt = jnp.stack(times)
    w1_t = jnp.pad(w1.astype(F32).T, ((0, 0), (0, emb - HYENA_EMB_DIM)))
    w4_t = jnp.transpose(w4.astype(F32).reshape(fo, 2, 2, c), (1, 2, 3, 0))
    min_decay = math.log(HYENA_TARGET) / HYENA_SLOW_DECAY
    max_decay = math.log(HYENA_TARGET) / HYENA_FAST_DECAY
    deltas = jnp.abs(jnp.linspace(min_decay, max_decay, c, dtype=F32))[:, None]
    col = lambda v: v.astype(F32)[:, None]
    full = lambda a: pl.BlockSpec(a.shape, lambda f, i: (0,) * a.ndim)
    args = (z, w1_t, col(b1), w2.astype(F32).T, col(b2), w3.astype(F32).T, col(b3), col(freq))
    return pl.pallas_call(
        _hyena_filter_kernel,
        grid=(2, c // tc),
        in_specs=[full(a) for a in args]
                 + [pl.BlockSpec((1, 2, tc, fo), lambda f, i: (f, 0, i, 0)),
                    pl.BlockSpec((tc, 1), lambda f, i: (i, 0)), full(tt)],
        out_specs=pl.BlockSpec((2, 1, tc, seq), lambda f, i: (0, f, i, 0)),
        out_shape=jax.ShapeDtypeStruct((2, 2, c, seq), BF16),
        scratch_shapes=[pltpu.VMEM((2, fo, seq), F32)],
        compiler_params=_cparams(("arbitrary", "arbitrary")),
        name="hyena_filter_mlp",
    )(*args, w4_t, deltas, tt)


def _fft_consts(seq):
    n_fft = 2 * seq
    n1 = n_fft // LANES
    nb = n1 // 2 + 1
    kp = -(-nb // 8) * 8
    k1 = jnp.arange(kp, dtype=jnp.int32)
    keep = (k1 < nb)[:, None]
    two_pi = 2.0 * math.pi

    th = ((k1[:, None] * jnp.arange(n1, dtype=jnp.int32)[None, :]) % n1).astype(F32) * (two_pi / n1)
    f1 = jnp.concatenate([jnp.where(keep, jnp.cos(th), 0.0), jnp.where(keep, -jnp.sin(th), 0.0)], axis=0)
    ph = (k1[:, None] * jnp.arange(LANES, dtype=jnp.int32)[None, :]).astype(F32) * (two_pi / n_fft)
    tw = jnp.stack([jnp.cos(ph), -jnp.sin(ph)])
    om = ((jnp.arange(LANES, dtype=jnp.int32)[:, None] * jnp.arange(LANES, dtype=jnp.int32)[None, :]) % LANES
          ).astype(F32) * (two_pi / LANES)
    f2 = jnp.concatenate([jnp.cos(om), -jnp.sin(om)], axis=1)
    f2i = jnp.concatenate([jnp.cos(om), jnp.sin(om)], axis=1)
    weight = jnp.where((k1 == 0) | (k1 == nb - 1), 1.0, 2.0)[:, None]
    g = jnp.concatenate([jnp.where(keep, weight * jnp.cos(th[:, :n1 // 2]), 0.0),
                         jnp.where(keep, -weight * jnp.sin(th[:, :n1 // 2]), 0.0)], axis=0).T
    return dict(f1=f1.astype(BF16), tw=tw, f2=f2.astype(BF16), f2i=f2i.astype(BF16), g=g.astype(BF16), kp=kp)


def _per_channel_matmul(mat_ref, planes, out_scr, ct):
    def body(i, carry):
        c = 2 * i
        pair = jnp.concatenate([planes(c), planes(c + 1)], axis=1)
        res = jnp.dot(mat_ref[...], pair, preferred_element_type=F32)
        out_scr[c] = res[:, :LANES]
        out_scr[c + 1] = res[:, LANES:]
        return carry
    lax.fori_loop(0, ct // 2, body, 0, unroll=16)


def _twiddle_stage2(a, tw_ref, f2_ref):
    ct, kp2, _ = a.shape
    kp = kp2 // 2
    ar, ai = a[:, :kp], a[:, kp:]
    twr, twi = tw_ref[0], tw_ref[1]
    st = jnp.concatenate([ar * twr - ai * twi, ar * twi + ai * twr], axis=1).astype(BF16)
    p = jnp.dot(st.reshape(ct * kp2, LANES), f2_ref[...], preferred_element_type=F32).reshape(ct, kp2, 2 * LANES)
    return p[:, :kp, :LANES] - p[:, kp:, LANES:], p[:, :kp, LANES:] + p[:, kp:, :LANES]


def _stage2_twiddle_inv(yr, yi, tw_ref, f2i_ref):
    ct, kp, _ = yr.shape
    st = jnp.concatenate([yr, yi], axis=1).astype(BF16)
    p = jnp.dot(st.reshape(ct * 2 * kp, LANES), f2i_ref[...], preferred_element_type=F32).reshape(ct, 2 * kp, 2 * LANES)
    tr, ti = p[:, :kp, :LANES] - p[:, kp:, LANES:], p[:, :kp, LANES:] + p[:, kp:, :LANES]
    twr, twi = tw_ref[0], tw_ref[1]
    return jnp.concatenate([tr * twr + ti * twi, ti * twr - tr * twi], axis=1).astype(BF16)


def _filter_spec_kernel(lo_ref, hi_ref, f1_ref, tw_ref, f2_ref, kr_ref, ki_ref, a_scr):
    ct = lo_ref.shape[2]
    planes = lambda c: jnp.concatenate([lo_ref[0, 0, c], hi_ref[0, 0, c]], axis=0)
    _per_channel_matmul(f1_ref, planes, a_scr, ct)
    xr, xi = _twiddle_stage2(a_scr[...], tw_ref, f2_ref)
    n_fft = f1_ref.shape[1] * LANES
    kr_ref[0] = xr * (1.0 / n_fft)
    ki_ref[0] = xi * (1.0 / n_fft)


def _filter_spectrum(halves, consts, ct=128):
    _, nf, c, seq = halves.shape
    ct = min(ct, c)
    kp = consts["kp"]
    h5 = halves.reshape(2, nf, c, seq // LANES, LANES)
    half = lambda s: pl.BlockSpec((1, 1, ct, seq // LANES, LANES), lambda f, i: (s, f, i, 0, 0))
    full = lambda a: pl.BlockSpec(a.shape, lambda f, i: (0,) * a.ndim)
    spec = jax.ShapeDtypeStruct((nf, c, kp, LANES), F32)
    out = pl.BlockSpec((1, ct, kp, LANES), lambda f, i: (f, i, 0, 0))
    return pl.pallas_call(
        _filter_spec_kernel,
        grid=(nf, c // ct),
        in_specs=[half(0), half(1), full(consts["f1"]), full(consts["tw"]), full(consts["f2"])],
        out_specs=[out, out],
        out_shape=[spec, spec],
        scratch_shapes=[pltpu.VMEM((ct, 2 * kp, LANES), F32)],
        compiler_params=_cparams(("parallel", "parallel")),
        name="hyena_filter_spectrum",
    )(h5, h5, consts["f1"], consts["tw"], consts["f2"])


def _hyena_conv_kernel(second, x_ref, m_ref, e_ref, skip_ref, kr_ref, ki_ref, f1_ref, tw_ref, f2_ref, f2i_ref, g_ref,
                       o_ref, a_scr, t_scr, y_scr):
    ct = x_ref.shape[1]
    _per_channel_matmul(f1_ref, lambda c: x_ref[0, c], a_scr, ct)
    xr, xi = _twiddle_stage2(a_scr[...], tw_ref, f2_ref)
    kr, ki = kr_ref[0], ki_ref[0]
    t_scr[...] = _stage2_twiddle_inv(xr * kr - xi * ki, xr * ki + xi * kr, tw_ref, f2i_ref)

    _per_channel_matmul(g_ref, lambda c: t_scr[c], y_scr, ct)

    out = m_ref[0].astype(F32) * (y_scr[...] + x_ref[0].astype(F32) * skip_ref[...])
    if second:
        e = e_ref[0].astype(F32)
        out = out * (e * jax.nn.sigmoid(e))
    o_ref[0] = out.astype(o_ref.dtype)


def _hyena_conv(second, x, x_row0, hy, m_row0, e_row0, skip, kr, ki, filt, consts, ct=128):
    b, _, seq = x.shape
    c = kr.shape[1]
    ct = min(ct, c)
    kp = consts["kp"]
    n1h = seq // LANES
    view = lambda a: a.reshape(a.shape[0], a.shape[1], n1h, LANES)
    rows = lambda blk0: pl.BlockSpec((1, ct, n1h, LANES), lambda j, i: (i, blk0 + j, 0, 0))
    full = lambda a: pl.BlockSpec(a.shape, lambda j, i: (0,) * a.ndim)
    kspec = pl.BlockSpec((1, ct, kp, LANES), lambda j, i: (filt, j, 0, 0))
    mats = [consts["f1"][:, :n1h], consts["tw"], consts["f2"], consts["f2i"], consts["g"]]
    out = pl.pallas_call(
        functools.partial(_hyena_conv_kernel, second),
        grid=(c // ct, b),
        in_specs=[rows(x_row0 // ct), rows(m_row0 // ct), rows(e_row0 // ct),
                  pl.BlockSpec((ct, 1, 1), lambda j, i: (j, 0, 0)),
                  kspec, kspec] + [full(a) for a in mats],
        out_specs=pl.BlockSpec((1, ct, n1h, LANES), lambda j, i: (i, j, 0, 0)),
        out_shape=jax.ShapeDtypeStruct((b, c, n1h, LANES), BF16),
        scratch_shapes=[pltpu.VMEM((ct, 2 * kp, LANES), F32), pltpu.VMEM((ct, 2 * kp, LANES), BF16),
                        pltpu.VMEM((ct, n1h, LANES), F32)],
        compiler_params=_cparams(("parallel", "parallel")),
        name="hyena_conv2" if second else "hyena_conv1",
    )(view(x), view(hy), view(hy), skip, kr, ki, *mats)
    return out.reshape(b, c, seq)


def _t5_bucket(rel):
    half_b = REL_BUCKETS // 2
    max_exact = half_b // 2
    ret = jnp.where(rel > 0, half_b, 0)
    n = jnp.abs(rel)
    nf = jnp.maximum(n, 1).astype(F32)
    large = max_exact + (jnp.log(nf / max_exact) / math.log(REL_MAX_DIST / max_exact)
                         * (half_b - max_exact)).astype(jnp.int32)
    large = jnp.minimum(large, half_b - 1)
    return ret + jnp.where(n < max_exact, n, large)


def _dilated_bias(rel_bias):
    rel = jnp.arange(ATT_KW)[None, :] - ATT_HALF - jnp.arange(ATT_QB)[:, None]
    col = jnp.arange(ATT_KW)
    edge = jnp.stack([col >= ATT_HALF, col >= 0, col < ATT_KW - ATT_HALF])
    out = []
    for _, dil in DILATED_PATTERNS:
        onehot = jax.nn.one_hot(_t5_bucket(rel * dil), REL_BUCKETS, dtype=F32)
        bias = jnp.einsum("qkb,bh->qkh", onehot, rel_bias.astype(F32), precision=lax.Precision.HIGHEST)
        bias = jnp.where((jnp.abs(rel) <= ATT_HALF)[:, :, None], bias, NEG_INF)
        bias = jnp.transpose(bias, (2, 0, 1))[:, None]
        out.append(jnp.where(edge[None, :, None, :], bias, NEG_INF))
    return jnp.stack(out)


def _head_select(first, second):
    lane = lax.broadcasted_iota(jnp.int32, first.shape[:-1] + (LANES,), len(first.shape) - 1)
    return jnp.where(lane < HEAD_DIM, first, second)


def _dilated_kernel(q_ref, k_ref, v_ref, g_ref, bias_ref, o_ref, *scratch):
    stage_a, stage_b = scratch[0:3], scratch[3:6]
    qd, kd, vd = scratch[6:9]
    last = len(DILATED_PATTERNS) - 1
    states = [scratch[9 + 3 * p:12 + 3 * p] for p in range(last)]
    seq = q_ref.shape[1]
    stage_a[0][...] = q_ref[0].astype(F32) * (HEAD_DIM ** -0.5)
    stage_a[1][...] = k_ref[0].astype(F32)
    stage_a[2][...] = v_ref[0].astype(F32)
    zero_halo = jnp.zeros((ATT_HALF, LANES), BF16)
    first_head = lax.broadcasted_iota(jnp.int32, (ATT_QB, LANES), 1) < HEAD_DIM

    src, dst = stage_a, stage_b
    for pi, (_, dil) in enumerate(DILATED_PATTERNS):
        n = seq // dil
        nblk = n // ATT_QB
        if pi > 0:
            prev_dil = dil // ATT_STEP
            for r1 in range(prev_dil):
                for r2 in range(ATT_STEP):
                    r = r1 + prev_dil * r2
                    for a_src, a_dst in zip(src, dst):
                        a_dst[r * n:(r + 1) * n, :] = a_src[pl.ds(r1 * n * ATT_STEP + r2, n, stride=ATT_STEP), :]
            src, dst = dst, src
        stride_k = n + 2 * ATT_HALF
        qd[...] = src[0][...].astype(BF16)
        for r in range(dil):
            base = r * stride_k
            for a_src, a_dst in ((src[1], kd), (src[2], vd)):
                a_dst[base:base + ATT_HALF, :] = zero_halo
                a_dst[base + ATT_HALF:base + ATT_HALF + n, :] = a_src[r * n:(r + 1) * n, :].astype(BF16)
                a_dst[base + ATT_HALF + n:base + stride_k, :] = zero_halo
        if pi == last:
            states.append(dst)
        acc_w, mx_w, den_w = states[pi]

        def block_group(it, carry, pi=pi, nblk=nblk, acc_w=acc_w, mx_w=mx_w, den_w=den_w):
            chains = []
            for u in range(ATT_UNROLL):
                blk = it * ATT_UNROLL + u
                r = blk // nblk
                qb = blk % nblk
                q0 = pl.multiple_of(blk * ATT_QB, ATT_QB)
                k0 = pl.multiple_of(q0 + r * (2 * ATT_HALF), ATT_QB)
                edge = jnp.where(qb == 0, 0, jnp.where(qb == nblk - 1, 2, 1))
                qblk = qd[pl.ds(q0, ATT_QB), :]
                qcat = jnp.concatenate([jnp.where(first_head, qblk, jnp.zeros_like(qblk)),
                                        jnp.where(first_head, jnp.zeros_like(qblk), qblk)], axis=0)
                s = lax.dot_general(qcat, kd[pl.ds(k0, ATT_KW), :], (((1,), (1,)), ((), ())),
                                    preferred_element_type=F32)
                chains.append((q0, k0, edge, s))
            probs = []
            for q0, k0, edge, s in chains:
                s = s + jnp.concatenate([bias_ref[pi, 0, edge], bias_ref[pi, 1, edge]], axis=0)
                m = jnp.max(s, axis=1, keepdims=True)
                p = jnp.exp(s - m)
                probs.append((p.astype(BF16), m, jnp.sum(p, axis=1, keepdims=True)))
            for (q0, k0, edge, s), (p, m, l) in zip(chains, probs):
                pv = jnp.dot(p, vd[pl.ds(k0, ATT_KW), :], preferred_element_type=F32)
                acc_w[pl.ds(q0, ATT_QB), :] = _head_select(pv[:ATT_QB], pv[ATT_QB:])
                mx_w[pl.ds(q0, ATT_QB), :] = _head_select(m[:ATT_QB], m[ATT_QB:])
                den_w[pl.ds(q0, ATT_QB), :] = _head_select(l[:ATT_QB], l[ATT_QB:])
            return carry

        lax.fori_loop(0, dil * nblk // ATT_UNROLL, block_group, 0)

    for pi in range(len(DILATED_PATTERNS) - 1, 0, -1):
        dil = DILATED_PATTERNS[pi][1]
        n = seq // dil
        prev_dil = dil // ATT_STEP
        acc_c, mx_c, den_c = states[pi - 1]
        acc_f, mx_f, den_f = states[pi]
        for r1 in range(prev_dil):
            for r2 in range(ATT_STEP):
                r = r1 + prev_dil * r2
                rows = pl.ds(r1 * n * ATT_STEP + r2, n, stride=ATT_STEP)
                fine = slice(r * n, (r + 1) * n)
                m_old = mx_c[rows, :]
                m_new = mx_f[fine, :]
                m_all = jnp.maximum(m_old, m_new)
                decay = jnp.exp(-jnp.abs(m_old - m_new))
                a_old = jnp.where(m_old >= m_new, 1.0, decay)
                a_new = jnp.where(m_old >= m_new, decay, 1.0)
                acc_c[rows, :] = acc_c[rows, :] * a_old + acc_f[fine, :] * a_new
                den_c[rows, :] = den_c[rows, :] * a_old + den_f[fine, :] * a_new
                mx_c[rows, :] = m_all

    acc, _, den = states[0]
    g = g_ref[0].astype(F32)
    o_ref[0] = (acc[...] / den[...] * (g * jax.nn.sigmoid(g))).astype(o_ref.dtype)


def _dilated_attention(proj, bias, n_heads):
    _, b, seq, _ = proj.shape
    npair = n_heads // 2
    max_dil = max(d for _, d in DILATED_PATTERNS)
    kd_rows = seq + 2 * ATT_HALF * max_dil
    blk = lambda off: pl.BlockSpec((None, 1, seq, LANES), lambda i, j: (off * npair + j, i, 0, 0))
    state = pltpu.VMEM((seq, LANES), F32)
    return pl.pallas_call(
        _dilated_kernel,
        grid=(b, npair),
        in_specs=[blk(0), blk(1), blk(2), blk(3),
                  pl.BlockSpec((len(DILATED_PATTERNS), 2, 3, ATT_QB, ATT_KW), lambda i, j: (0, j, 0, 0, 0))],
        out_specs=pl.BlockSpec((None, 1, seq, LANES), lambda i, j: (j, i, 0, 0)),
        out_shape=jax.ShapeDtypeStruct((npair, b, seq, LANES), BF16),
        scratch_shapes=[state] * 6
                       + [pltpu.VMEM((seq, LANES), BF16), pltpu.VMEM((kd_rows, LANES), BF16),
                          pltpu.VMEM((kd_rows, LANES), BF16)]
                       + [state] * (3 * (len(DILATED_PATTERNS) - 1)),
        compiler_params=_cparams(("parallel", "parallel")),
        name="dilated_attention",
    )(proj, proj, proj, proj, bias)


def _na_bias(rpb, rows):
    assert min(NA_ROWS, rows) % 2 == 0
    cols = np.arange(GRID_W)
    cs = np.clip(cols - NA_COLS // 2, 0, GRID_W - NA_COLS)
    kc = np.arange(GRID_W)
    inside = (kc[None, :] >= cs[:, None]) & (kc[None, :] < cs[:, None] + NA_COLS)
    col_idx = kc[None, :] - cols[:, None] + NA_COLS - 1
    onehot = (col_idx[:, :, None] == np.arange(2 * NA_COLS - 1)[None, None, :]).astype(np.float32)
    e = jnp.einsum("hrd,cqd->hrcq", rpb.astype(F32), jnp.asarray(onehot), precision=lax.Precision.HIGHEST)
    e = jnp.where(jnp.asarray(inside)[None, None], e, NEG_INF)
    return jnp.concatenate([e[:, :-1], e[:, 1:]], axis=-1)


def _na_kernel(q_ref, k_ref, v_ref, g_ref, bias_ref, o_ref, qm):
    seq = q_ref.shape[1]
    rows = seq // GRID_W
    kr_n = min(NA_ROWS, rows)
    lane = lax.broadcasted_iota(jnp.int32, (seq, LANES), 1)
    q = q_ref[0] * (HEAD_DIM ** -0.5)
    for h in range(2):
        qm[h] = jnp.where((lane < HEAD_DIM) == (h == 0), q, jnp.zeros_like(q))

    def row_group(it, carry):
        def scores(batch):
            out = []
            for u in range(batch * NA_BATCH, (batch + 1) * NA_BATCH):
                r = it * NA_UNROLL + u
                rs = jnp.clip(r - kr_n // 2, 0, rows - kr_n)
                shift = rs - r + NA_ROWS - 1
                q0 = pl.multiple_of(r * GRID_W, GRID_W)
                k0 = pl.multiple_of(rs * GRID_W, GRID_W)
                qcat = jnp.concatenate([qm[0, pl.ds(q0, GRID_W), :], qm[1, pl.ds(q0, GRID_W), :]], axis=0)
                s = lax.dot_general(qcat, k_ref[0, pl.ds(k0, kr_n * GRID_W), :], (((1,), (1,)), ((), ())),
                                    preferred_element_type=F32)
                out.append((q0, k0, shift, s))
            return out

        def finish(chains):
            probs = []
            for q0, k0, shift, s in chains:
                bias = [jnp.concatenate([bias_ref[h, shift + i] for i in range(0, kr_n, 2)], axis=1)
                        for h in range(2)]
                s = s + jnp.concatenate(bias, axis=0)
                p = jnp.exp(s - jnp.max(s, axis=1, keepdims=True))
                probs.append((p.astype(BF16), jnp.sum(p, axis=1, keepdims=True)))
            for (q0, k0, shift, s), (p, l) in zip(chains, probs):
                o = jnp.dot(p, v_ref[0, pl.ds(k0, kr_n * GRID_W), :], preferred_element_type=F32) / l
                g = g_ref[0, pl.ds(q0, GRID_W), :].astype(F32)
                o = _head_select(o[:GRID_W], o[GRID_W:]) * (g * jax.nn.sigmoid(g))
                o_ref[0, pl.ds(q0, GRID_W), :] = o.astype(o_ref.dtype)

        n_batches = NA_UNROLL // NA_BATCH
        pending = scores(0)
        for batch in range(n_batches):
            ahead = scores(batch + 1) if batch + 1 < n_batches else None
            finish(pending)
            pending = ahead
        return carry

    lax.fori_loop(0, rows // NA_UNROLL, row_group, 0)


def _neighbourhood_attention(proj, bias, n_heads):
    _, b, seq, _ = proj.shape
    npair = n_heads // 2
    blk = lambda off: pl.BlockSpec((None, 1, seq, LANES), lambda j, i: (off * npair + j, i, 0, 0))
    return pl.pallas_call(
        _na_kernel,
        grid=(npair, b),
        in_specs=[blk(0), blk(1), blk(2), blk(3),
                  pl.BlockSpec((2,) + bias.shape[1:], lambda j, i: (j, 0, 0, 0))],
        out_specs=pl.BlockSpec((None, 1, seq, LANES), lambda j, i: (j, i, 0, 0)),
        out_shape=jax.ShapeDtypeStruct((npair, b, seq, LANES), BF16),
        scratch_shapes=[pltpu.VMEM((2, seq, LANES), BF16)],
        compiler_params=_cparams(("parallel", "parallel")),
        name="neighbourhood_attention",
    )(proj, proj, proj, proj, bias)


def _out_kernel(n_y, n_t, *refs):
    n = n_y + n_t
    y_refs = refs[:n]
    w_refs = refs[n:2 * n]
    h_ref, p_ref, wp_ref, wg_ref, gpost_ref, gple_ref, gnext_ref, h_out, u_out = refs[2 * n:]
    t = None
    for idx, (y_ref, w_ref) in enumerate(zip(y_refs, w_refs)):
        if idx < n_y:
            y = jnp.concatenate([y_ref[c] for c in range(y_ref.shape[0])], axis=1)
            part = jnp.dot(y, w_ref[...], preferred_element_type=F32)
        else:
            part = lax.dot_general(y_ref[0], w_ref[...], (((0,), (0,)), ((), ())), preferred_element_type=F32)
        t = part if t is None else t + part
    h = h_ref[...] + _rms(t, gpost_ref[...])
    e = _rms(jnp.dot(p_ref[...].astype(BF16), wp_ref[...], preferred_element_type=F32), gple_ref[...])
    gate = jax.nn.sigmoid(jnp.dot(h.astype(BF16), wg_ref[...], preferred_element_type=F32))
    h = h + gate * e
    h_out[...] = h
    u_out[...] = _rms(h, gnext_ref[...]).astype(u_out.dtype)


def _out_block(ys, ys_t, ws, h, p, layer, w_ple, w_gate, g_post, g_ple, g_next, tm=1024):
    m, d = h.shape
    row = lambda a: pl.BlockSpec((tm, a.shape[1]), lambda i: (i, 0))
    slab = lambda a: pl.BlockSpec((a.shape[0], tm, LANES), lambda i: (0, i, 0))
    p_spec = pl.BlockSpec((None, tm, p.shape[2]), lambda i: (layer, i, 0))
    full = lambda a: pl.BlockSpec(a.shape, lambda i: (0, 0))
    vec = lambda g: g.reshape(1, d)

    def col(a):
        tiles = a.shape[2] // tm
        return pl.BlockSpec((1, a.shape[1], tm), lambda i: (i // tiles, 0, i % tiles))

    return pl.pallas_call(
        functools.partial(_out_kernel, len(ys), len(ys_t)),
        grid=(m // tm,),
        in_specs=[slab(y) for y in ys] + [col(y) for y in ys_t] + [full(w) for w in ws]
                 + [row(h), p_spec, full(w_ple), full(w_gate), full(vec(g_post)), full(vec(g_ple)), full(vec(g_next))],
        out_specs=[pl.BlockSpec((tm, d), lambda i: (i, 0)), pl.BlockSpec((tm, d), lambda i: (i, 0))],
        out_shape=[jax.ShapeDtypeStruct((m, d), F32), jax.ShapeDtypeStruct((m, d), BF16)],
        compiler_params=_cparams(("parallel",)),
        name="out_proj_residual_ple",
    )(*ys, *ys_t, *ws, h, p, w_ple, w_gate, vec(g_post), vec(g_ple), vec(g_next))


def kernel(x, p, w_in, w_out, norm_pre, norm_post, hyena_conv_w, hyena_conv_b, hyena_w1, hyena_b1, hyena_w2, hyena_b2, hyena_w3, hyena_b3, hyena_w4, hyena_freq, hyena_skip, rel_bias, na_rpb, ple_proj, ple_norm, ple_gate):
    b, seq, d = x.shape
    depth = w_in.shape[0]
    d_inner = w_out.shape[1]
    hw = d_inner // 2
    m = b * seq
    consts = _fft_consts(seq)
    dil_bias = _dilated_bias(rel_bias)
    h = x.reshape(m, d)
    u = _rmsnorm_bf16(h, norm_pre[0])
    w_all = w_in.astype(BF16)
    p_all = p.reshape(depth, m, -1)
    for i in range(depth):
        j = i // 2
        if i % 2 == 0:
            w_a = w_all[i, :, :4 * hw].T
            ident = jnp.tile(jnp.array([0.0, 1.0, 0.0], F32)[:, None], (1, hw))
            conv_w = jnp.concatenate([hyena_conv_w[j], ident], axis=1).T
            conv_b = jnp.concatenate([hyena_conv_b[j], jnp.zeros((hw,), F32)])[:, None]
            hy = _proj_t(w_a, u.reshape(b, seq, d), conv_w, conv_b)
            halves = _hyena_filters_t(seq, hw, hyena_w1[j], hyena_b1[j], hyena_w2[j], hyena_b2[j],
                                      hyena_w3[j], hyena_b3[j], hyena_w4[j], hyena_freq[j])
            kr, ki = _filter_spectrum(halves, consts)
            skip = hyena_skip[j][:, :, None, None]
            z = _hyena_conv(False, hy, 0, hy, hw, hw, skip[0], kr, ki, 0, consts)
            ya_t = _hyena_conv(True, z, 0, hy, 2 * hw, 3 * hw, skip[1], kr, ki, 1, consts)
            proj_b = _matmul_bf16(u, w_all[i:i + 1, :, 4 * hw:], 0, 0, 4 * hw).reshape(-1, b, seq, LANES)
            yb = _dilated_attention(proj_b, dil_bias, hw // HEAD_DIM)
            ys, ys_t = [yb.reshape(-1, m, LANES)], [ya_t]
            ws = [w_out[i, hw:].astype(BF16), w_out[i, :hw].astype(BF16)]
        else:
            proj = _matmul_bf16(u, w_all, i, 0, 4 * d_inner).reshape(-1, b, seq, LANES)
            yc = _neighbourhood_attention(proj, _na_bias(na_rpb[j], seq // GRID_W), d_inner // HEAD_DIM)
            ys, ys_t = [yc.reshape(-1, m, LANES)], []
            ws = [w_out[i].astype(BF16)]
        g_next = norm_pre[i + 1] if i + 1 < depth else norm_pre[i]
        h, u = _out_block(ys, ys_t, ws, h, p_all, i, ple_proj[i].astype(BF16), ple_gate[i].astype(BF16),
                          norm_post[i], ple_norm[i], g_next)
    return h.reshape(b, seq, d)
```

```python
import functools
import math

import numpy as np
import jax
import jax.numpy as jnp
from jax import lax
from jax.experimental import pallas as pl
from jax.experimental.pallas import tpu as pltpu

F32 = jnp.float32
BF16 = jnp.bfloat16

HEAD_DIM = 64
LANES = 128
HYENA_EMB_DIM = 33
HYENA_FAST_DECAY = 0.3
HYENA_SLOW_DECAY = 1.5
HYENA_TARGET = 1e-2
DILATED_PATTERNS = ((128, 1), (512, 4), (2048, 16))
GRID_W = 64
NA_ROWS = 8
NA_COLS = 16
REL_BUCKETS = 32
REL_MAX_DIST = 1024
RMS_EPS = 1e-6
NEG_INF = -1e30
ATT_QB = 128
ATT_HALF = 64
ATT_KW = ATT_QB + 2 * ATT_HALF
ATT_UNROLL = 16
ATT_STEP = 4
assert all(b[1] == a[1] * ATT_STEP for a, b in zip(DILATED_PATTERNS, DILATED_PATTERNS[1:])) and DILATED_PATTERNS[0][1] == 1
NA_BATCH = 4
NA_UNROLL = 32
VMEM_LIMIT = 56 * 1024 * 1024


def _cparams(sem):
    return pltpu.CompilerParams(dimension_semantics=sem, vmem_limit_bytes=VMEM_LIMIT)


def _rms(x, g):
    return x * lax.rsqrt(jnp.mean(x * x, axis=-1, keepdims=True) + RMS_EPS) * g


def _rmsnorm_kernel(x_ref, g_ref, o_ref):
    o_ref[...] = _rms(x_ref[...], g_ref[...]).astype(o_ref.dtype)


def _rmsnorm_bf16(x2d, g, tm=1024):
    m, d = x2d.shape
    return pl.pallas_call(
        _rmsnorm_kernel,
        grid=(m // tm,),
        in_specs=[pl.BlockSpec((tm, d), lambda i: (i, 0)), pl.BlockSpec((1, d), lambda i: (0, 0))],
        out_specs=pl.BlockSpec((tm, d), lambda i: (i, 0)),
        out_shape=jax.ShapeDtypeStruct((m, d), BF16),
        compiler_params=_cparams(("parallel",)),
        name="rmsnorm",
    )(x2d, g.reshape(1, d))


def _matmul_kernel(a_ref, b_ref, o_ref):
    res = jnp.dot(a_ref[...], b_ref[...], preferred_element_type=F32).astype(o_ref.dtype)
    for c in range(o_ref.shape[0]):
        o_ref[c] = res[:, c * LANES:(c + 1) * LANES]


def _matmul_bf16(a, w, layer, col0, n, tm=2048, tn=1024):
    m, k = a.shape
    return pl.pallas_call(
        _matmul_kernel,
        grid=(m // tm, n // tn),
        in_specs=[pl.BlockSpec((tm, k), lambda i, j: (i, 0)), pl.BlockSpec((None, k, tn), lambda i, j: (layer, 0, col0 // tn + j))],
        out_specs=pl.BlockSpec((tn // LANES, tm, LANES), lambda i, j: (j, i, 0)),
        out_shape=jax.ShapeDtypeStruct((n // LANES, m, LANES), BF16),
        compiler_params=_cparams(("parallel", "parallel")),
        name="in_proj",
    )(a, w)


def _proj_t_kernel(w_ref, u_ref, cw_ref, cb_ref, o_ref):
    p = lax.dot_general(w_ref[...], u_ref[0], (((1,), (1,)), ((), ())), preferred_element_type=F32)
    seq = p.shape[1]
    t = lax.broadcasted_iota(jnp.int32, p.shape, 1)
    prev = jnp.where(t == 0, 0.0, pltpu.roll(p, 1, axis=1))
    nxt = jnp.where(t == seq - 1, 0.0, pltpu.roll(p, seq - 1, axis=1))
    cw = cw_ref[...]
    o_ref[0] = (prev * cw[:, 0:1] + p * cw[:, 1:2] + nxt * cw[:, 2:3] + cb_ref[...]).astype(o_ref.dtype)


def _proj_t(w_t, u, conv_w, conv_b, tc=256):
    c, d = w_t.shape
    b, seq, _ = u.shape
    return pl.pallas_call(
        _proj_t_kernel,
        grid=(b, c // tc),
        in_specs=[
            pl.BlockSpec((tc, d), lambda i, j: (j, 0)),
            pl.BlockSpec((1, seq, d), lambda i, j: (i, 0, 0)),
            pl.BlockSpec((tc, 3), lambda i, j: (j, 0)),
            pl.BlockSpec((tc, 1), lambda i, j: (j, 0)),
        ],
        out_specs=pl.BlockSpec((1, tc, seq), lambda i, j: (i, j, 0)),
        out_shape=jax.ShapeDtypeStruct((b, c, seq), BF16),
        compiler_params=_cparams(("parallel", "parallel")),
        name="in_proj_t_conv3",
    )(w_t, u, conv_w, conv_b)


def _hyena_filter_kernel(z_ref, w1_ref, b1_ref, w2_ref, b2_ref, w3_ref, b3_ref, fq_ref, w4_ref, dl_ref, t_ref,
                         o_ref, a_scr):
    hp = lax.Precision.HIGHEST

    @pl.when((pl.program_id(0) == 0) & (pl.program_id(1) == 0))
    def _():
        for d in range(2):
            a = z_ref[d]
            for w_ref, b_ref in ((w1_ref, b1_ref), (w2_ref, b2_ref), (w3_ref, b3_ref)):
                a = jnp.sin(fq_ref[...] * (jnp.dot(w_ref[...], a, precision=hp, preferred_element_type=F32) + b_ref[...]))
            a_scr[d] = a

    seq = o_ref.shape[-1]
    proj = lambda d: jnp.dot(w4_ref[0, d].astype(BF16), a_scr[d].astype(BF16), preferred_element_type=F32)
    lo = proj(0) * jnp.exp(-t_ref[0] * dl_ref[...])
    hi = proj(1) * jnp.exp(-t_ref[1] * dl_ref[...])
    hi = jnp.where(lax.broadcasted_iota(jnp.int32, (1, seq), 1) == 0, 0.0, hi)
    norm = jnp.sum(jnp.abs(lo), axis=1, keepdims=True) + jnp.sum(jnp.abs(hi), axis=1, keepdims=True)
    o_ref[0, 0] = (lo / norm).astype(o_ref.dtype)
    o_ref[1, 0] = (hi / norm).astype(o_ref.dtype)


def _hyena_filters_t(seq, c, w1, b1, w2, b2, w3, b3, w4, freq, tc=256):
    tc = min(tc, c)
    fo = w2.shape[0]
    emb = -(-HYENA_EMB_DIM // 8) * 8
    t = jnp.linspace(0.0, 1.0, seq, dtype=F32)
    bands = (HYENA_EMB_DIM - 1) // 2
    fr = jnp.linspace(1e-4, bands - 1, bands, dtype=F32)[:, None]
    mirror = (seq - jnp.arange(seq)) % seq
    tables, times = [], []
    for pos, tp in ((jnp.arange(seq), t), (mirror, jnp.roll(t[::-1], 1))):
        wpos = (2.0 * math.pi / seq) * pos.astype(F32)[None, :]
        tp = tp[None, :]
        z = jnp.concatenate([tp, jnp.cos(fr * wpos), -jnp.sin(fr * wpos),
                             jnp.zeros((emb - HYENA_EMB_DIM, seq), F32)], axis=0)
        tables.append(z)
        times.append(tp)
    z = jnp.stack(tables)
    tt = jnp.stack(times)
    w1_t = jnp.pad(w1.astype(F32).T, ((0, 0), (0, emb - HYENA_EMB_DIM)))
    w4_t = jnp.transpose(w4.astype(F32).reshape(fo, 2, 2, c), (1, 2, 3, 0))
    min_decay = math.log(HYENA_TARGET) / HYENA_SLOW_DECAY
    max_decay = math.log(HYENA_TARGET) / HYENA_FAST_DECAY
    deltas = jnp.abs(jnp.linspace(min_decay, max_decay, c, dtype=F32))[:, None]
    col = lambda v: v.astype(F32)[:, None]
    full = lambda a: pl.BlockSpec(a.shape, lambda f, i: (0,) * a.ndim)
    args = (z, w1_t, col(b1), w2.astype(F32).T, col(b2), w3.astype(F32).T, col(b3), col(freq))
    return pl.pallas_call(
        _hyena_filter_kernel,
        grid=(2, c // tc),
        in_specs=[full(a) for a in args]
                 + [pl.BlockSpec((1, 2, tc, fo), lambda f, i: (f, 0, i, 0)),
                    pl.BlockSpec((tc, 1), lambda f, i: (i, 0)), full(tt)],
        out_specs=pl.BlockSpec((2, 1, tc, seq), lambda f, i: (0, f, i, 0)),
        out_shape=jax.ShapeDtypeStruct((2, 2, c, seq), BF16),
        scratch_shapes=[pltpu.VMEM((2, fo, seq), F32)],
        compiler_params=_cparams(("arbitrary", "arbitrary")),
        name="hyena_filter_mlp",
    )(*args, w4_t, deltas, tt)


def _fft_consts(seq):
    n_fft = 2 * seq
    n1 = n_fft // LANES
    nb = n1 // 2 + 1
    kp = -(-nb // 8) * 8
    k1 = jnp.arange(kp, dtype=jnp.int32)
    keep = (k1 < nb)[:, None]
    two_pi = 2.0 * math.pi

    th = ((k1[:, None] * jnp.arange(n1, dtype=jnp.int32)[None, :]) % n1).astype(F32) * (two_pi / n1)
    f1 = jnp.concatenate([jnp.where(keep, jnp.cos(th), 0.0), jnp.where(keep, -jnp.sin(th), 0.0)], axis=0)
    ph = (k1[:, None] * jnp.arange(LANES, dtype=jnp.int32)[None, :]).astype(F32) * (two_pi / n_fft)
    tw = jnp.stack([jnp.cos(ph), -jnp.sin(ph)])
    om = ((jnp.arange(LANES, dtype=jnp.int32)[:, None] * jnp.arange(LANES, dtype=jnp.int32)[None, :]) % LANES
          ).astype(F32) * (two_pi / LANES)
    f2 = jnp.concatenate([jnp.cos(om), -jnp.sin(om)], axis=1)
    f2i = jnp.concatenate([jnp.cos(om), jnp.sin(om)], axis=1)
    weight = jnp.where((k1 == 0) | (k1 == nb - 1), 1.0, 2.0)[:, None]
    g = jnp.concatenate([jnp.where(keep, weight * jnp.cos(th[:, :n1 // 2]), 0.0),
                         jnp.where(keep, -weight * jnp.sin(th[:, :n1 // 2]), 0.0)], axis=0).T
    return dict(f1=f1.astype(BF16), tw=tw, f2=f2.astype(BF16), f2i=f2i.astype(BF16), g=g.astype(BF16), kp=kp)


def _per_channel_matmul(mat_ref, planes, out_scr, ct):
    def body(i, carry):
        c = 2 * i
        pair = jnp.concatenate([planes(c), planes(c + 1)], axis=1)
        res = jnp.dot(mat_ref[...], pair, preferred_element_type=F32)
        out_scr[c] = res[:, :LANES]
        out_scr[c + 1] = res[:, LANES:]
        return carry
    lax.fori_loop(0, ct // 2, body, 0, unroll=16)


def _twiddle_stage2(a, tw_ref, f2_ref):
    ct, kp2, _ = a.shape
    kp = kp2 // 2
    ar, ai = a[:, :kp], a[:, kp:]
    twr, twi = tw_ref[0], tw_ref[1]
    st = jnp.concatenate([ar * twr - ai * twi, ar * twi + ai * twr], axis=1).astype(BF16)
    p = jnp.dot(st.reshape(ct * kp2, LANES), f2_ref[...], preferred_element_type=F32).reshape(ct, kp2, 2 * LANES)
    return p[:, :kp, :LANES] - p[:, kp:, LANES:], p[:, :kp, LANES:] + p[:, kp:, :LANES]


def _stage2_twiddle_inv(yr, yi, tw_ref, f2i_ref):
    ct, kp, _ = yr.shape
    st = jnp.concatenate([yr, yi], axis=1).astype(BF16)
    p = jnp.dot(st.reshape(ct * 2 * kp, LANES), f2i_ref[...], preferred_element_type=F32).reshape(ct, 2 * kp, 2 * LANES)
    tr, ti = p[:, :kp, :LANES] - p[:, kp:, LANES:], p[:, :kp, LANES:] + p[:, kp:, :LANES]
    twr, twi = tw_ref[0], tw_ref[1]
    return jnp.concatenate([tr * twr + ti * twi, ti * twr - tr * twi], axis=1).astype(BF16)


def _filter_spec_kernel(lo_ref, hi_ref, f1_ref, tw_ref, f2_ref, kr_ref, ki_ref, a_scr):
    ct = lo_ref.shape[2]
    planes = lambda c: jnp.concatenate([lo_ref[0, 0, c], hi_ref[0, 0, c]], axis=0)
    _per_channel_matmul(f1_ref, planes, a_scr, ct)
    xr, xi = _twiddle_stage2(a_scr[...], tw_ref, f2_ref)
    n_fft = f1_ref.shape[1] * LANES
    kr_ref[0] = xr * (1.0 / n_fft)
    ki_ref[0] = xi * (1.0 / n_fft)


def _filter_spectrum(halves, consts, ct=128):
    _, nf, c, seq = halves.shape
    ct = min(ct, c)
    kp = consts["kp"]
    h5 = halves.reshape(2, nf, c, seq // LANES, LANES)
    half = lambda s: pl.BlockSpec((1, 1, ct, seq // LANES, LANES), lambda f, i: (s, f, i, 0, 0))
    full = lambda a: pl.BlockSpec(a.shape, lambda f, i: (0,) * a.ndim)
    spec = jax.ShapeDtypeStruct((nf, c, kp, LANES), F32)
    out = pl.BlockSpec((1, ct, kp, LANES), lambda f, i: (f, i, 0, 0))
    return pl.pallas_call(
        _filter_spec_kernel,
        grid=(nf, c // ct),
        in_specs=[half(0), half(1), full(consts["f1"]), full(consts["tw"]), full(consts["f2"])],
        out_specs=[out, out],
        out_shape=[spec, spec],
        scratch_shapes=[pltpu.VMEM((ct, 2 * kp, LANES), F32)],
        compiler_params=_cparams(("parallel", "parallel")),
        name="hyena_filter_spectrum",
    )(h5, h5, consts["f1"], consts["tw"], consts["f2"])


def _hyena_conv_kernel(second, x_ref, m_ref, e_ref, skip_ref, kr_ref, ki_ref, f1_ref, tw_ref, f2_ref, f2i_ref, g_ref,
                       o_ref, a_scr, t_scr, y_scr):
    ct = x_ref.shape[1]
    _per_channel_matmul(f1_ref, lambda c: x_ref[0, c], a_scr, ct)
    xr, xi = _twiddle_stage2(a_scr[...], tw_ref, f2_ref)
    kr, ki = kr_ref[0], ki_ref[0]
    t_scr[...] = _stage2_twiddle_inv(xr * kr - xi * ki, xr * ki + xi * kr, tw_ref, f2i_ref)

    _per_channel_matmul(g_ref, lambda c: t_scr[c], y_scr, ct)

    out = m_ref[0].astype(F32) * (y_scr[...] + x_ref[0].astype(F32) * skip_ref[...])
    if second:
        e = e_ref[0].astype(F32)
        out = out * (e * jax.nn.sigmoid(e))
    o_ref[0] = out.astype(o_ref.dtype)


def _hyena_conv(second, x, x_row0, hy, m_row0, e_row0, skip, kr, ki, filt, consts, ct=128):
    b, _, seq = x.shape
    c = kr.shape[1]
    ct = min(ct, c)
    kp = consts["kp"]
    n1h = seq // LANES
    view = lambda a: a.reshape(a.shape[0], a.shape[1], n1h, LANES)
    rows = lambda blk0: pl.BlockSpec((1, ct, n1h, LANES), lambda j, i: (i, blk0 + j, 0, 0))
    full = lambda a: pl.BlockSpec(a.shape, lambda j, i: (0,) * a.ndim)
    kspec = pl.BlockSpec((1, ct, kp, LANES), lambda j, i: (filt, j, 0, 0))
    mats = [consts["f1"][:, :n1h], consts["tw"], consts["f2"], consts["f2i"], consts["g"]]
    out = pl.pallas_call(
        functools.partial(_hyena_conv_kernel, second),
        grid=(c // ct, b),
        in_specs=[rows(x_row0 // ct), rows(m_row0 // ct), rows(e_row0 // ct),
                  pl.BlockSpec((ct, 1, 1), lambda j, i: (j, 0, 0)),
                  kspec, kspec] + [full(a) for a in mats],
        out_specs=pl.BlockSpec((1, ct, n1h, LANES), lambda j, i: (i, j, 0, 0)),
        out_shape=jax.ShapeDtypeStruct((b, c, n1h, LANES), BF16),
        scratch_shapes=[pltpu.VMEM((ct, 2 * kp, LANES), F32), pltpu.VMEM((ct, 2 * kp, LANES), BF16),
                        pltpu.VMEM((ct, n1h, LANES), F32)],
        compiler_params=_cparams(("parallel", "parallel")),
        name="hyena_conv2" if second else "hyena_conv1",
    )(view(x), view(hy), view(hy), skip, kr, ki, *mats)
    return out.reshape(b, c, seq)


def _t5_bucket(rel):
    half_b = REL_BUCKETS // 2
    max_exact = half_b // 2
    ret = jnp.where(rel > 0, half_b, 0)
    n = jnp.abs(rel)
    nf = jnp.maximum(n, 1).astype(F32)
    large = max_exact + (jnp.log(nf / max_exact) / math.log(REL_MAX_DIST / max_exact)
                         * (half_b - max_exact)).astype(jnp.int32)
    large = jnp.minimum(large, half_b - 1)
    return ret + jnp.where(n < max_exact, n, large)


def _dilated_bias(rel_bias):
    rel = jnp.arange(ATT_KW)[None, :] - ATT_HALF - jnp.arange(ATT_QB)[:, None]
    col = jnp.arange(ATT_KW)
    edge = jnp.stack([col >= ATT_HALF, col >= 0, col < ATT_KW - ATT_HALF])
    out = []
    for _, dil in DILATED_PATTERNS:
        onehot = jax.nn.one_hot(_t5_bucket(rel * dil), REL_BUCKETS, dtype=F32)
        bias = jnp.einsum("qkb,bh->qkh", onehot, rel_bias.astype(F32), precision=lax.Precision.HIGHEST)
        bias = jnp.where((jnp.abs(rel) <= ATT_HALF)[:, :, None], bias, NEG_INF)
        bias = jnp.transpose(bias, (2, 0, 1))[:, None]
        out.append(jnp.where(edge[None, :, None, :], bias, NEG_INF))
    return jnp.stack(out)


def _head_select(first, second):
    lane = lax.broadcasted_iota(jnp.int32, first.shape[:-1] + (LANES,), len(first.shape) - 1)
    return jnp.where(lane < HEAD_DIM, first, second)


def _dilated_kernel(q_ref, k_ref, v_ref, g_ref, bias_ref, o_ref, *scratch):
    stage_a, stage_b = scratch[0:3], scratch[3:6]
    qd, kd, vd = scratch[6:9]
    last = len(DILATED_PATTERNS) - 1
    states = [scratch[9 + 3 * p:12 + 3 * p] for p in range(last)]
    seq = q_ref.shape[1]
    stage_a[0][...] = q_ref[0].astype(F32) * (HEAD_DIM ** -0.5)
    stage_a[1][...] = k_ref[0].astype(F32)
    stage_a[2][...] = v_ref[0].astype(F32)
    zero_halo = jnp.zeros((ATT_HALF, LANES), BF16)
    first_head = lax.broadcasted_iota(jnp.int32, (ATT_QB, LANES), 1) < HEAD_DIM

    src, dst = stage_a, stage_b
    for pi, (_, dil) in enumerate(DILATED_PATTERNS):
        n = seq // dil
        nblk = n // ATT_QB
        if pi > 0:
            prev_dil = dil // ATT_STEP
            for r1 in range(prev_dil):
                for r2 in range(ATT_STEP):
                    r = r1 + prev_dil * r2
                    for a_src, a_dst in zip(src, dst):
                        a_dst[r * n:(r + 1) * n, :] = a_src[pl.ds(r1 * n * ATT_STEP + r2, n, stride=ATT_STEP), :]
            src, dst = dst, src
        stride_k = n + 2 * ATT_HALF
        qd[...] = src[0][...].astype(BF16)
        for r in range(dil):
            base = r * stride_k
            for a_src, a_dst in ((src[1], kd), (src[2], vd)):
                a_dst[base:base + ATT_HALF, :] = zero_halo
                a_dst[base + ATT_HALF:base + ATT_HALF + n, :] = a_src[r * n:(r + 1) * n, :].astype(BF16)
                a_dst[base + ATT_HALF + n:base + stride_k, :] = zero_halo
        if pi == last:
            states.append(dst)
        acc_w, mx_w, den_w = states[pi]

        def block_group(it, carry, pi=pi, nblk=nblk, acc_w=acc_w, mx_w=mx_w, den_w=den_w):
            chains = []
            for u in range(ATT_UNROLL):
                blk = it * ATT_UNROLL + u
                r = blk // nblk
                qb = blk % nblk
                q0 = pl.multiple_of(blk * ATT_QB, ATT_QB)
                k0 = pl.multiple_of(q0 + r * (2 * ATT_HALF), ATT_QB)
                edge = jnp.where(qb == 0, 0, jnp.where(qb == nblk - 1, 2, 1))
                qblk = qd[pl.ds(q0, ATT_QB), :]
                qcat = jnp.concatenate([jnp.where(first_head, qblk, jnp.zeros_like(qblk)),
                                        jnp.where(first_head, jnp.zeros_like(qblk), qblk)], axis=0)
                s = lax.dot_general(qcat, kd[pl.ds(k0, ATT_KW), :], (((1,), (1,)), ((), ())),
                                    preferred_element_type=F32)
                chains.append((q0, k0, edge, s))
            probs = []
            for q0, k0, edge, s in chains:
                s = s + jnp.concatenate([bias_ref[pi, 0, edge], bias_ref[pi, 1, edge]], axis=0)
                m = jnp.max(s, axis=1, keepdims=True)
                p = jnp.exp(s - m)
                probs.append((p.astype(BF16), m, jnp.sum(p, axis=1, keepdims=True)))
            for (q0, k0, edge, s), (p, m, l) in zip(chains, probs):
                pv = jnp.dot(p, vd[pl.ds(k0, ATT_KW), :], preferred_element_type=F32)
                acc_w[pl.ds(q0, ATT_QB), :] = _head_select(pv[:ATT_QB], pv[ATT_QB:])
                mx_w[pl.ds(q0, ATT_QB), :] = _head_select(m[:ATT_QB], m[ATT_QB:])
                den_w[pl.ds(q0, ATT_QB), :] = _head_select(l[:ATT_QB], l[ATT_QB:])
            return carry

        lax.fori_loop(0, dil * nblk // ATT_UNROLL, block_group, 0)

    for pi in range(len(DILATED_PATTERNS) - 1, 0, -1):
        dil = DILATED_PATTERNS[pi][1]
        n = seq // dil
        prev_dil = dil // ATT_STEP
        acc_c, mx_c, den_c = states[pi - 1]
        acc_f, mx_f, den_f = states[pi]
        for r1 in range(prev_dil):
            for r2 in range(ATT_STEP):
                r = r1 + prev_dil * r2
                rows = pl.ds(r1 * n * ATT_STEP + r2, n, stride=ATT_STEP)
                fine = slice(r * n, (r + 1) * n)
                m_old = mx_c[rows, :]
                m_new = mx_f[fine, :]
                m_all = jnp.maximum(m_old, m_new)
                decay = jnp.exp(-jnp.abs(m_old - m_new))
                a_old = jnp.where(m_old >= m_new, 1.0, decay)
                a_new = jnp.where(m_old >= m_new, decay, 1.0)
                acc_c[rows, :] = acc_c[rows, :] * a_old + acc_f[fine, :] * a_new
                den_c[rows, :] = den_c[rows, :] * a_old + den_f[fine, :] * a_new
                mx_c[rows, :] = m_all

    acc, _, den = states[0]
    g = g_ref[0].astype(F32)
    o_ref[0] = (acc[...] / den[...] * (g * jax.nn.sigmoid(g))).astype(o_ref.dtype)


def _dilated_attention(proj, bias, n_heads):
    _, b, seq, _ = proj.shape
    npair = n_heads // 2
    max_dil = max(d for _, d in DILATED_PATTERNS)
    kd_rows = seq + 2 * ATT_HALF * max_dil
    blk = lambda off: pl.BlockSpec((None, 1, seq, LANES), lambda i, j: (off * npair + j, i, 0, 0))
    state = pltpu.VMEM((seq, LANES), F32)
    return pl.pallas_call(
        _dilated_kernel,
        grid=(b, npair),
        in_specs=[blk(0), blk(1), blk(2), blk(3),
                  pl.BlockSpec((len(DILATED_PATTERNS), 2, 3, ATT_QB, ATT_KW), lambda i, j: (0, j, 0, 0, 0))],
        out_specs=pl.BlockSpec((None, 1, seq, LANES), lambda i, j: (j, i, 0, 0)),
        out_shape=jax.ShapeDtypeStruct((npair, b, seq, LANES), BF16),
        scratch_shapes=[state] * 6
                       + [pltpu.VMEM((seq, LANES), BF16), pltpu.VMEM((kd_rows, LANES), BF16),
                          pltpu.VMEM((kd_rows, LANES), BF16)]
                       + [state] * (3 * (len(DILATED_PATTERNS) - 1)),
        compiler_params=_cparams(("parallel", "parallel")),
        name="dilated_attention",
    )(proj, proj, proj, proj, bias)


def _na_bias(rpb, rows):
    assert min(NA_ROWS, rows) % 2 == 0
    cols = np.arange(GRID_W)
    cs = np.clip(cols - NA_COLS // 2, 0, GRID_W - NA_COLS)
    kc = np.arange(GRID_W)
    inside = (kc[None, :] >= cs[:, None]) & (kc[None, :] < cs[:, None] + NA_COLS)
    col_idx = kc[None, :] - cols[:, None] + NA_COLS - 1
    onehot = (col_idx[:, :, None] == np.arange(2 * NA_COLS - 1)[None, None, :]).astype(np.float32)
    e = jnp.einsum("hrd,cqd->hrcq", rpb.astype(F32), jnp.asarray(onehot), precision=lax.Precision.HIGHEST)
    e = jnp.where(jnp.asarray(inside)[None, None], e, NEG_INF)
    return jnp.concatenate([e[:, :-1], e[:, 1:]], axis=-1)


def _na_kernel(q_ref, k_ref, v_ref, g_ref, bias_ref, o_ref, qm):
    seq = q_ref.shape[1]
    rows = seq // GRID_W
    kr_n = min(NA_ROWS, rows)
    lane = lax.broadcasted_iota(jnp.int32, (seq, LANES), 1)
    q = q_ref[0] * (HEAD_DIM ** -0.5)
    for h in range(2):
        qm[h] = jnp.where((lane < HEAD_DIM) == (h == 0), q, jnp.zeros_like(q))

    def row_group(it, carry):
        def scores(batch):
            out = []
            for u in range(batch * NA_BATCH, (batch + 1) * NA_BATCH):
                r = it * NA_UNROLL + u
                rs = jnp.clip(r - kr_n // 2, 0, rows - kr_n)
                shift = rs - r + NA_ROWS - 1
                q0 = pl.multiple_of(r * GRID_W, GRID_W)
                k0 = pl.multiple_of(rs * GRID_W, GRID_W)
                qcat = jnp.concatenate([qm[0, pl.ds(q0, GRID_W), :], qm[1, pl.ds(q0, GRID_W), :]], axis=0)
                s = lax.dot_general(qcat, k_ref[0, pl.ds(k0, kr_n * GRID_W), :], (((1,), (1,)), ((), ())),
                                    preferred_element_type=F32)
                out.append((q0, k0, shift, s))
            return out

        def finish(chains):
            probs = []
            for q0, k0, shift, s in chains:
                bias = [jnp.concatenate([bias_ref[h, shift + i] for i in range(0, kr_n, 2)], axis=1)
                        for h in range(2)]
                s = s + jnp.concatenate(bias, axis=0)
                p = jnp.exp(s - jnp.max(s, axis=1, keepdims=True))
                probs.append((p.astype(BF16), jnp.sum(p, axis=1, keepdims=True)))
            for (q0, k0, shift, s), (p, l) in zip(chains, probs):
                o = jnp.dot(p, v_ref[0, pl.ds(k0, kr_n * GRID_W), :], preferred_element_type=F32) / l
                g = g_ref[0, pl.ds(q0, GRID_W), :].astype(F32)
                o = _head_select(o[:GRID_W], o[GRID_W:]) * (g * jax.nn.sigmoid(g))
                o_ref[0, pl.ds(q0, GRID_W), :] = o.astype(o_ref.dtype)

        n_batches = NA_UNROLL // NA_BATCH
        pending = scores(0)
        for batch in range(n_batches):
            ahead = scores(batch + 1) if batch + 1 < n_batches else None
            finish(pending)
            pending = ahead
        return carry

    lax.fori_loop(0, rows // NA_UNROLL, row_group, 0)


def _neighbourhood_attention(proj, bias, n_heads):
    _, b, seq, _ = proj.shape
    npair = n_heads // 2
    blk = lambda off: pl.BlockSpec((None, 1, seq, LANES), lambda j, i: (off * npair + j, i, 0, 0))
    return pl.pallas_call(
        _na_kernel,
        grid=(npair, b),
        in_specs=[blk(0), blk(1), blk(2), blk(3),
                  pl.BlockSpec((2,) + bias.shape[1:], lambda j, i: (j, 0, 0, 0))],
        out_specs=pl.BlockSpec((None, 1, seq, LANES), lambda j, i: (j, i, 0, 0)),
        out_shape=jax.ShapeDtypeStruct((npair, b, seq, LANES), BF16),
        scratch_shapes=[pltpu.VMEM((2, seq, LANES), BF16)],
        compiler_params=_cparams(("parallel", "parallel")),
        name="neighbourhood_attention",
    )(proj, proj, proj, proj, bias)


def _out_kernel(n_y, n_t, *refs):
    n = n_y + n_t
    y_refs = refs[:n]
    w_refs = refs[n:2 * n]
    h_ref, p_ref, wp_ref, wg_ref, gpost_ref, gple_ref, gnext_ref, h_out, u_out = refs[2 * n:]
    t = None
    for idx, (y_ref, w_ref) in enumerate(zip(y_refs, w_refs)):
        if idx < n_y:
            y = jnp.concatenate([y_ref[c] for c in range(y_ref.shape[0])], axis=1)
            part = jnp.dot(y, w_ref[...], preferred_element_type=F32)
        else:
            part = lax.dot_general(y_ref[0], w_ref[...], (((0,), (0,)), ((), ())), preferred_element_type=F32)
        t = part if t is None else t + part
    h = h_ref[...] + _rms(t, gpost_ref[...])
    e = _rms(jnp.dot(p_ref[...].astype(BF16), wp_ref[...], preferred_element_type=F32), gple_ref[...])
    gate = jax.nn.sigmoid(jnp.dot(h.astype(BF16), wg_ref[...], preferred_element_type=F32))
    h = h + gate * e
    h_out[...] = h
    u_out[...] = _rms(h, gnext_ref[...]).astype(u_out.dtype)


def _out_block(ys, ys_t, ws, h, p, layer, w_ple, w_gate, g_post, g_ple, g_next, tm=1024):
    m, d = h.shape
    row = lambda a: pl.BlockSpec((tm, a.shape[1]), lambda i: (i, 0))
    slab = lambda a: pl.BlockSpec((a.shape[0], tm, LANES), lambda i: (0, i, 0))
    p_spec = pl.BlockSpec((None, tm, p.shape[2]), lambda i: (layer, i, 0))
    full = lambda a: pl.BlockSpec(a.shape, lambda i: (0, 0))
    vec = lambda g: g.reshape(1, d)

    def col(a):
        tiles = a.shape[2] // tm
        return pl.BlockSpec((1, a.shape[1], tm), lambda i: (i // tiles, 0, i % tiles))

    return pl.pallas_call(
        functools.partial(_out_kernel, len(ys), len(ys_t)),
        grid=(m // tm,),
        in_specs=[slab(y) for y in ys] + [col(y) for y in ys_t] + [full(w) for w in ws]
                 + [row(h), p_spec, full(w_ple), full(w_gate), full(vec(g_post)), full(vec(g_ple)), full(vec(g_next))],
        out_specs=[pl.BlockSpec((tm, d), lambda i: (i, 0)), pl.BlockSpec((tm, d), lambda i: (i, 0))],
        out_shape=[jax.ShapeDtypeStruct((m, d), F32), jax.ShapeDtypeStruct((m, d), BF16)],
        compiler_params=_cparams(("parallel",)),
        name="out_proj_residual_ple",
    )(*ys, *ys_t, *ws, h, p, w_ple, w_gate, vec(g_post), vec(g_ple), vec(g_next))


def kernel(x, p, w_in, w_out, norm_pre, norm_post, hyena_conv_w, hyena_conv_b, hyena_w1, hyena_b1, hyena_w2, hyena_b2, hyena_w3, hyena_b3, hyena_w4, hyena_freq, hyena_skip, rel_bias, na_rpb, ple_proj, ple_norm, ple_gate):
    b, seq, d = x.shape
    depth = w_in.shape[0]
    d_inner = w_out.shape[1]
    hw = d_inner // 2
    m = b * seq
    consts = _fft_consts(seq)
    dil_bias = _dilated_bias(rel_bias)
    h = x.reshape(m, d)
    u = _rmsnorm_bf16(h, norm_pre[0])
    w_all = w_in.astype(BF16)
    p_all = p.reshape(depth, m, -1)
    for i in range(depth):
        j = i // 2
        if i % 2 == 0:
            w_a = w_all[i, :, :4 * hw].T
            ident = jnp.tile(jnp.array([0.0, 1.0, 0.0], F32)[:, None], (1, hw))
            conv_w = jnp.concatenate([hyena_conv_w[j], ident], axis=1).T
            conv_b = jnp.concatenate([hyena_conv_b[j], jnp.zeros((hw,), F32)])[:, None]
            hy = _proj_t(w_a, u.reshape(b, seq, d), conv_w, conv_b)
            halves = _hyena_filters_t(seq, hw, hyena_w1[j], hyena_b1[j], hyena_w2[j], hyena_b2[j],
                                      hyena_w3[j], hyena_b3[j], hyena_w4[j], hyena_freq[j])
            kr, ki = _filter_spectrum(halves, consts)
            skip = hyena_skip[j][:, :, None, None]
            z = _hyena_conv(False, hy, 0, hy, hw, hw, skip[0], kr, ki, 0, consts)
            ya_t = _hyena_conv(True, z, 0, hy, 2 * hw, 3 * hw, skip[1], kr, ki, 1, consts)
            proj_b = _matmul_bf16(u, w_all[i:i + 1, :, 4 * hw:], 0, 0, 4 * hw).reshape(-1, b, seq, LANES)
            yb = _dilated_attention(proj_b, dil_bias, hw // HEAD_DIM)
            ys, ys_t = [yb.reshape(-1, m, LANES)], [ya_t]
            ws = [w_out[i, hw:].astype(BF16), w_out[i, :hw].astype(BF16)]
        else:
            proj = _matmul_bf16(u, w_all, i, 0, 4 * d_inner).reshape(-1, b, seq, LANES)
            yc = _neighbourhood_attention(proj, _na_bias(na_rpb[j], seq // GRID_W), d_inner // HEAD_DIM)
            ys, ys_t = [yc.reshape(-1, m, LANES)], []
            ws = [w_out[i].astype(BF16)]
        g_next = norm_pre[i + 1] if i + 1 < depth else norm_pre[i]
        h, u = _out_block(ys, ys_t, ws, h, p_all, i, ple_proj[i].astype(BF16), ple_gate[i].astype(BF16),
                          norm_post[i], ple_norm[i], g_next)
    return h.reshape(b, seq, d)
```

```python
import functools
import math

import numpy as np
import jax
import jax.numpy as jnp
from jax import lax
from jax.experimental import pallas as pl
from jax.experimental.pallas import tpu as pltpu

F32 = jnp.float32
BF16 = jnp.bfloat16

HEAD_DIM = 64
LANES = 128
HYENA_EMB_DIM = 33
HYENA_FAST_DECAY = 0.3
HYENA_SLOW_DECAY = 1.5
HYENA_TARGET = 1e-2
DILATED_PATTERNS = ((128, 1), (512, 4), (2048, 16))
GRID_W = 64
NA_ROWS = 8
NA_COLS = 16
REL_BUCKETS = 32
REL_MAX_DIST = 1024
RMS_EPS = 1e-6
NEG_INF = -1e30
ATT_QB = 128
ATT_HALF = 64
ATT_KW = ATT_QB + 2 * ATT_HALF
ATT_UNROLL = 8
ATT_STEP = 4
assert all(b[1] == a[1] * ATT_STEP for a, b in zip(DILATED_PATTERNS, DILATED_PATTERNS[1:])) and DILATED_PATTERNS[0][1] == 1
NA_BATCH = 8
NA_UNROLL = 32
VMEM_LIMIT = 56 * 1024 * 1024


def _cparams(sem):
    return pltpu.CompilerParams(dimension_semantics=sem, vmem_limit_bytes=VMEM_LIMIT)


def _rms(x, g):
    return x * lax.rsqrt(jnp.mean(x * x, axis=-1, keepdims=True) + RMS_EPS) * g


def _rmsnorm_kernel(x_ref, g_ref, o_ref):
    o_ref[...] = _rms(x_ref[...], g_ref[...]).astype(o_ref.dtype)


def _rmsnorm_bf16(x2d, g, tm=1024):
    m, d = x2d.shape
    return pl.pallas_call(
        _rmsnorm_kernel,
        grid=(m // tm,),
        in_specs=[pl.BlockSpec((tm, d), lambda i: (i, 0)), pl.BlockSpec((1, d), lambda i: (0, 0))],
        out_specs=pl.BlockSpec((tm, d), lambda i: (i, 0)),
        out_shape=jax.ShapeDtypeStruct((m, d), BF16),
        compiler_params=_cparams(("parallel",)),
        name="rmsnorm",
    )(x2d, g.reshape(1, d))


def _matmul_kernel(a_ref, b_ref, o_ref):
    res = jnp.dot(a_ref[...], b_ref[...], preferred_element_type=F32).astype(o_ref.dtype)
    for c in range(o_ref.shape[0]):
        o_ref[c] = res[:, c * LANES:(c + 1) * LANES]


def _matmul_bf16(a, w, layer, col0, n, tm=2048, tn=1024):
    m, k = a.shape
    return pl.pallas_call(
        _matmul_kernel,
        grid=(m // tm, n // tn),
        in_specs=[pl.BlockSpec((tm, k), lambda i, j: (i, 0)), pl.BlockSpec((None, k, tn), lambda i, j: (layer, 0, col0 // tn + j))],
        out_specs=pl.BlockSpec((tn // LANES, tm, LANES), lambda i, j: (j, i, 0)),
        out_shape=jax.ShapeDtypeStruct((n // LANES, m, LANES), BF16),
        compiler_params=_cparams(("parallel", "parallel")),
        name="in_proj",
    )(a, w)


def _proj_t_kernel(w_ref, u_ref, cw_ref, cb_ref, o_ref):
    p = lax.dot_general(w_ref[...], u_ref[0], (((1,), (1,)), ((), ())), preferred_element_type=F32)
    seq = p.shape[1]
    t = lax.broadcasted_iota(jnp.int32, p.shape, 1)
    prev = jnp.where(t == 0, 0.0, pltpu.roll(p, 1, axis=1))
    nxt = jnp.where(t == seq - 1, 0.0, pltpu.roll(p, seq - 1, axis=1))
    cw = cw_ref[...]
    o_ref[0] = (prev * cw[:, 0:1] + p * cw[:, 1:2] + nxt * cw[:, 2:3] + cb_ref[...]).astype(o_ref.dtype)


def _proj_t(w_t, u, conv_w, conv_b, tc=256):
    c, d = w_t.shape
    b, seq, _ = u.shape
    return pl.pallas_call(
        _proj_t_kernel,
        grid=(b, c // tc),
        in_specs=[
            pl.BlockSpec((tc, d), lambda i, j: (j, 0)),
            pl.BlockSpec((1, seq, d), lambda i, j: (i, 0, 0)),
            pl.BlockSpec((tc, 3), lambda i, j: (j, 0)),
            pl.BlockSpec((tc, 1), lambda i, j: (j, 0)),
        ],
        out_specs=pl.BlockSpec((1, tc, seq), lambda i, j: (i, j, 0)),
        out_shape=jax.ShapeDtypeStruct((b, c, seq), BF16),
        compiler_params=_cparams(("parallel", "parallel")),
        name="in_proj_t_conv3",
    )(w_t, u, conv_w, conv_b)


def _hyena_filter_kernel(z_ref, w1_ref, b1_ref, w2_ref, b2_ref, w3_ref, b3_ref, fq_ref, w4_ref, dl_ref, t_ref,
                         o_ref, a_scr):
    hp = lax.Precision.HIGHEST

    @pl.when((pl.program_id(0) == 0) & (pl.program_id(1) == 0))
    def _():
        for d in range(2):
            a = z_ref[d]
            for w_ref, b_ref in ((w1_ref, b1_ref), (w2_ref, b2_ref), (w3_ref, b3_ref)):
                a = jnp.sin(fq_ref[...] * (jnp.dot(w_ref[...], a, precision=hp, preferred_element_type=F32) + b_ref[...]))
            a_scr[d] = a

    seq = o_ref.shape[-1]
    proj = lambda d: jnp.dot(w4_ref[0, d].astype(BF16), a_scr[d].astype(BF16), preferred_element_type=F32)
    lo = proj(0) * jnp.exp(-t_ref[0] * dl_ref[...])
    hi = proj(1) * jnp.exp(-t_ref[1] * dl_ref[...])
    hi = jnp.where(lax.broadcasted_iota(jnp.int32, (1, seq), 1) == 0, 0.0, hi)
    norm = jnp.sum(jnp.abs(lo), axis=1, keepdims=True) + jnp.sum(jnp.abs(hi), axis=1, keepdims=True)
    o_ref[0, 0] = (lo / norm).astype(o_ref.dtype)
    o_ref[1, 0] = (hi / norm).astype(o_ref.dtype)


def _hyena_filters_t(seq, c, w1, b1, w2, b2, w3, b3, w4, freq, tc=256):
    tc = min(tc, c)
    fo = w2.shape[0]
    emb = -(-HYENA_EMB_DIM // 8) * 8
    t = jnp.linspace(0.0, 1.0, seq, dtype=F32)
    bands = (HYENA_EMB_DIM - 1) // 2
    fr = jnp.linspace(1e-4, bands - 1, bands, dtype=F32)[:, None]
    mirror = (seq - jnp.arange(seq)) % seq
    tables, times = [], []
    for pos, tp in ((jnp.arange(seq), t), (mirror, jnp.roll(t[::-1], 1))):
        wpos = (2.0 * math.pi / seq) * pos.astype(F32)[None, :]
        tp = tp[None, :]
        z = jnp.concatenate([tp, jnp.cos(fr * wpos), -jnp.sin(fr * wpos),
                             jnp.zeros((emb - HYENA_EMB_DIM, seq), F32)], axis=0)
        tables.append(z)
        times.append(tp)
    z = jnp.stack(tables)
    tt = jnp.stack(times)
    w1_t = jnp.pad(w1.astype(F32).T, ((0, 0), (0, emb - HYENA_EMB_DIM)))
    w4_t = jnp.transpose(w4.astype(F32).reshape(fo, 2, 2, c), (1, 2, 3, 0))
    min_decay = math.log(HYENA_TARGET) / HYENA_SLOW_DECAY
    max_decay = math.log(HYENA_TARGET) / HYENA_FAST_DECAY
    deltas = jnp.abs(jnp.linspace(min_decay, max_decay, c, dtype=F32))[:, None]
    col = lambda v: v.astype(F32)[:, None]
    full = lambda a: pl.BlockSpec(a.shape, lambda f, i: (0,) * a.ndim)
    args = (z, w1_t, col(b1), w2.astype(F32).T, col(b2), w3.astype(F32).T, col(b3), col(freq))
    return pl.pallas_call(
        _hyena_filter_kernel,
        grid=(2, c // tc),
        in_specs=[full(a) for a in args]
                 + [pl.BlockSpec((1, 2, tc, fo), lambda f, i: (f, 0, i, 0)),
                    pl.BlockSpec((tc, 1), lambda f, i: (i, 0)), full(tt)],
        out_specs=pl.BlockSpec((2, 1, tc, seq), lambda f, i: (0, f, i, 0)),
        out_shape=jax.ShapeDtypeStruct((2, 2, c, seq), BF16),
        scratch_shapes=[pltpu.VMEM((2, fo, seq), F32)],
        compiler_params=_cparams(("arbitrary", "arbitrary")),
        name="hyena_filter_mlp",
    )(*args, w4_t, deltas, tt)


def _fft_consts(seq):
    n_fft = 2 * seq
    n1 = n_fft // LANES
    nb = n1 // 2 + 1
    kp = -(-nb // 8) * 8
    k1 = jnp.arange(kp, dtype=jnp.int32)
    keep = (k1 < nb)[:, None]
    two_pi = 2.0 * math.pi

    th = ((k1[:, None] * jnp.arange(n1, dtype=jnp.int32)[None, :]) % n1).astype(F32) * (two_pi / n1)
    f1 = jnp.concatenate([jnp.where(keep, jnp.cos(th), 0.0), jnp.where(keep, -jnp.sin(th), 0.0)], axis=0)
    ph = (k1[:, None] * jnp.arange(LANES, dtype=jnp.int32)[None, :]).astype(F32) * (two_pi / n_fft)
    tw = jnp.stack([jnp.cos(ph), -jnp.sin(ph)])
    om = ((jnp.arange(LANES, dtype=jnp.int32)[:, None] * jnp.arange(LANES, dtype=jnp.int32)[None, :]) % LANES
          ).astype(F32) * (two_pi / LANES)
    f2 = jnp.concatenate([jnp.cos(om), -jnp.sin(om)], axis=1)
    f2i = jnp.concatenate([jnp.cos(om), jnp.sin(om)], axis=1)
    weight = jnp.where((k1 == 0) | (k1 == nb - 1), 1.0, 2.0)[:, None]
    g = jnp.concatenate([jnp.where(keep, weight * jnp.cos(th[:, :n1 // 2]), 0.0),
                         jnp.where(keep, -weight * jnp.sin(th[:, :n1 // 2]), 0.0)], axis=0).T
    return dict(f1=f1.astype(BF16), tw=tw, f2=f2.astype(BF16), f2i=f2i.astype(BF16), g=g.astype(BF16), kp=kp)


def _per_channel_matmul(mat_ref, planes, out_scr, ct):
    def body(i, carry):
        c = 2 * i
        pair = jnp.concatenate([planes(c), planes(c + 1)], axis=1)
        res = jnp.dot(mat_ref[...], pair, preferred_element_type=F32)
        out_scr[c] = res[:, :LANES]
        out_scr[c + 1] = res[:, LANES:]
        return carry
    lax.fori_loop(0, ct // 2, body, 0, unroll=16)


def _twiddle_stage2(a, tw_ref, f2_ref):
    ct, kp2, _ = a.shape
    kp = kp2 // 2
    ar, ai = a[:, :kp], a[:, kp:]
    twr, twi = tw_ref[0], tw_ref[1]
    st = jnp.concatenate([ar * twr - ai * twi, ar * twi + ai * twr], axis=1).astype(BF16)
    p = jnp.dot(st.reshape(ct * kp2, LANES), f2_ref[...], preferred_element_type=F32).reshape(ct, kp2, 2 * LANES)
    return p[:, :kp, :LANES] - p[:, kp:, LANES:], p[:, :kp, LANES:] + p[:, kp:, :LANES]


def _stage2_twiddle_inv(yr, yi, tw_ref, f2i_ref):
    ct, kp, _ = yr.shape
    st = jnp.concatenate([yr, yi], axis=1).astype(BF16)
    p = jnp.dot(st.reshape(ct * 2 * kp, LANES), f2i_ref[...], preferred_element_type=F32).reshape(ct, 2 * kp, 2 * LANES)
    tr, ti = p[:, :kp, :LANES] - p[:, kp:, LANES:], p[:, :kp, LANES:] + p[:, kp:, :LANES]
    twr, twi = tw_ref[0], tw_ref[1]
    return jnp.concatenate([tr * twr + ti * twi, ti * twr - tr * twi], axis=1).astype(BF16)


def _filter_spec_kernel(lo_ref, hi_ref, f1_ref, tw_ref, f2_ref, kr_ref, ki_ref, a_scr):
    ct = lo_ref.shape[2]
    planes = lambda c: jnp.concatenate([lo_ref[0, 0, c], hi_ref[0, 0, c]], axis=0)
    _per_channel_matmul(f1_ref, planes, a_scr, ct)
    xr, xi = _twiddle_stage2(a_scr[...], tw_ref, f2_ref)
    n_fft = f1_ref.shape[1] * LANES
    kr_ref[0] = xr * (1.0 / n_fft)
    ki_ref[0] = xi * (1.0 / n_fft)


def _filter_spectrum(halves, consts, ct=128):
    _, nf, c, seq = halves.shape
    ct = min(ct, c)
    kp = consts["kp"]
    h5 = halves.reshape(2, nf, c, seq // LANES, LANES)
    half = lambda s: pl.BlockSpec((1, 1, ct, seq // LANES, LANES), lambda f, i: (s, f, i, 0, 0))
    full = lambda a: pl.BlockSpec(a.shape, lambda f, i: (0,) * a.ndim)
    spec = jax.ShapeDtypeStruct((nf, c, kp, LANES), F32)
    out = pl.BlockSpec((1, ct, kp, LANES), lambda f, i: (f, i, 0, 0))
    return pl.pallas_call(
        _filter_spec_kernel,
        grid=(nf, c // ct),
        in_specs=[half(0), half(1), full(consts["f1"]), full(consts["tw"]), full(consts["f2"])],
        out_specs=[out, out],
        out_shape=[spec, spec],
        scratch_shapes=[pltpu.VMEM((ct, 2 * kp, LANES), F32)],
        compiler_params=_cparams(("parallel", "parallel")),
        name="hyena_filter_spectrum",
    )(h5, h5, consts["f1"], consts["tw"], consts["f2"])


def _hyena_conv_kernel(second, x_ref, m_ref, e_ref, skip_ref, kr_ref, ki_ref, f1_ref, tw_ref, f2_ref, f2i_ref, g_ref,
                       o_ref, a_scr, t_scr, y_scr):
    ct = x_ref.shape[1]
    _per_channel_matmul(f1_ref, lambda c: x_ref[0, c], a_scr, ct)
    xr, xi = _twiddle_stage2(a_scr[...], tw_ref, f2_ref)
    kr, ki = kr_ref[0], ki_ref[0]
    t_scr[...] = _stage2_twiddle_inv(xr * kr - xi * ki, xr * ki + xi * kr, tw_ref, f2i_ref)

    _per_channel_matmul(g_ref, lambda c: t_scr[c], y_scr, ct)

    out = m_ref[0].astype(F32) * (y_scr[...] + x_ref[0].astype(F32) * skip_ref[...])
    if second:
        e = e_ref[0].astype(F32)
        out = out * (e * jax.nn.sigmoid(e))
    o_ref[0] = out.astype(o_ref.dtype)


def _hyena_conv(second, x, x_row0, hy, m_row0, e_row0, skip, kr, ki, filt, consts, ct=128):
    b, _, seq = x.shape
    c = kr.shape[1]
    ct = min(ct, c)
    kp = consts["kp"]
    n1h = seq // LANES
    view = lambda a: a.reshape(a.shape[0], a.shape[1], n1h, LANES)
    rows = lambda blk0: pl.BlockSpec((1, ct, n1h, LANES), lambda j, i: (i, blk0 + j, 0, 0))
    full = lambda a: pl.BlockSpec(a.shape, lambda j, i: (0,) * a.ndim)
    kspec = pl.BlockSpec((1, ct, kp, LANES), lambda j, i: (filt, j, 0, 0))
    mats = [consts["f1"][:, :n1h], consts["tw"], consts["f2"], consts["f2i"], consts["g"]]
    out = pl.pallas_call(
        functools.partial(_hyena_conv_kernel, second),
        grid=(c // ct, b),
        in_specs=[rows(x_row0 // ct), rows(m_row0 // ct), rows(e_row0 // ct),
                  pl.BlockSpec((ct, 1, 1), lambda j, i: (j, 0, 0)),
                  kspec, kspec] + [full(a) for a in mats],
        out_specs=pl.BlockSpec((1, ct, n1h, LANES), lambda j, i: (i, j, 0, 0)),
        out_shape=jax.ShapeDtypeStruct((b, c, n1h, LANES), BF16),
        scratch_shapes=[pltpu.VMEM((ct, 2 * kp, LANES), F32), pltpu.VMEM((ct, 2 * kp, LANES), BF16),
                        pltpu.VMEM((ct, n1h, LANES), F32)],
        compiler_params=_cparams(("parallel", "parallel")),
        name="hyena_conv2" if second else "hyena_conv1",
    )(view(x), view(hy), view(hy), skip, kr, ki, *mats)
    return out.reshape(b, c, seq)


def _t5_bucket(rel):
    half_b = REL_BUCKETS // 2
    max_exact = half_b // 2
    ret = jnp.where(rel > 0, half_b, 0)
    n = jnp.abs(rel)
    nf = jnp.maximum(n, 1).astype(F32)
    large = max_exact + (jnp.log(nf / max_exact) / math.log(REL_MAX_DIST / max_exact)
                         * (half_b - max_exact)).astype(jnp.int32)
    large = jnp.minimum(large, half_b - 1)
    return ret + jnp.where(n < max_exact, n, large)


def _dilated_bias(rel_bias):
    rel = jnp.arange(ATT_KW)[None, :] - ATT_HALF - jnp.arange(ATT_QB)[:, None]
    col = jnp.arange(ATT_KW)
    edge = jnp.stack([col >= ATT_HALF, col >= 0, col < ATT_KW - ATT_HALF])
    out = []
    for _, dil in DILATED_PATTERNS:
        onehot = jax.nn.one_hot(_t5_bucket(rel * dil), REL_BUCKETS, dtype=F32)
        bias = jnp.einsum("qkb,bh->qkh", onehot, rel_bias.astype(F32), precision=lax.Precision.HIGHEST)
        bias = jnp.where((jnp.abs(rel) <= ATT_HALF)[:, :, None], bias, NEG_INF)
        bias = jnp.transpose(bias, (2, 0, 1))[:, None]
        out.append(jnp.where(edge[None, :, None, :], bias, NEG_INF))
    return jnp.stack(out)


def _head_select(first, second):
    lane = lax.broadcasted_iota(jnp.int32, first.shape[:-1] + (LANES,), len(first.shape) - 1)
    return jnp.where(lane < HEAD_DIM, first, second)


def _dilated_kernel(q_ref, k_ref, v_ref, g_ref, bias_ref, o_ref, *scratch):
    stage_a, stage_b = scratch[0:3], scratch[3:6]
    qd, kd, vd = scratch[6:9]
    last = len(DILATED_PATTERNS) - 1
    states = [scratch[9 + 3 * p:12 + 3 * p] for p in range(last)]
    seq = q_ref.shape[1]
    stage_a[0][...] = q_ref[0].astype(F32) * (HEAD_DIM ** -0.5)
    stage_a[1][...] = k_ref[0].astype(F32)
    stage_a[2][...] = v_ref[0].astype(F32)
    zero_halo = jnp.zeros((ATT_HALF, LANES), BF16)
    first_head = lax.broadcasted_iota(jnp.int32, (ATT_QB, LANES), 1) < HEAD_DIM

    src, dst = stage_a, stage_b
    for pi, (_, dil) in enumerate(DILATED_PATTERNS):
        n = seq // dil
        nblk = n // ATT_QB
        if pi > 0:
            prev_dil = dil // ATT_STEP
            for r1 in range(prev_dil):
                for r2 in range(ATT_STEP):
                    r = r1 + prev_dil * r2
                    for a_src, a_dst in zip(src, dst):
                        a_dst[r * n:(r + 1) * n, :] = a_src[pl.ds(r1 * n * ATT_STEP + r2, n, stride=ATT_STEP), :]
            src, dst = dst, src
        stride_k = n + 2 * ATT_HALF
        qd[...] = src[0][...].astype(BF16)
        for r in range(dil):
            base = r * stride_k
            for a_src, a_dst in ((src[1], kd), (src[2], vd)):
                a_dst[base:base + ATT_HALF, :] = zero_halo
                a_dst[base + ATT_HALF:base + ATT_HALF + n, :] = a_src[r * n:(r + 1) * n, :].astype(BF16)
                a_dst[base + ATT_HALF + n:base + stride_k, :] = zero_halo
        if pi == last:
            states.append(dst)
        acc_w, mx_w, den_w = states[pi]

        def block_group(it, carry, pi=pi, nblk=nblk, acc_w=acc_w, mx_w=mx_w, den_w=den_w):
            chains = []
            for u in range(ATT_UNROLL):
                blk = it * ATT_UNROLL + u
                r = blk // nblk
                qb = blk % nblk
                q0 = pl.multiple_of(blk * ATT_QB, ATT_QB)
                k0 = pl.multiple_of(q0 + r * (2 * ATT_HALF), ATT_QB)
                edge = jnp.where(qb == 0, 0, jnp.where(qb == nblk - 1, 2, 1))
                qblk = qd[pl.ds(q0, ATT_QB), :]
                qcat = jnp.concatenate([jnp.where(first_head, qblk, jnp.zeros_like(qblk)),
                                        jnp.where(first_head, jnp.zeros_like(qblk), qblk)], axis=0)
                s = lax.dot_general(qcat, kd[pl.ds(k0, ATT_KW), :], (((1,), (1,)), ((), ())),
                                    preferred_element_type=F32)
                chains.append((q0, k0, edge, s))
            probs = []
            for q0, k0, edge, s in chains:
                s = s + jnp.concatenate([bias_ref[pi, 0, edge], bias_ref[pi, 1, edge]], axis=0)
                m = jnp.max(s, axis=1, keepdims=True)
                p = jnp.exp(s - m)
                probs.append((p.astype(BF16), m, jnp.sum(p, axis=1, keepdims=True)))
            for (q0, k0, edge, s), (p, m, l) in zip(chains, probs):
                pv = jnp.dot(p, vd[pl.ds(k0, ATT_KW), :], preferred_element_type=F32)
                acc_w[pl.ds(q0, ATT_QB), :] = _head_select(pv[:ATT_QB], pv[ATT_QB:])
                mx_w[pl.ds(q0, ATT_QB), :] = _head_select(m[:ATT_QB], m[ATT_QB:])
                den_w[pl.ds(q0, ATT_QB), :] = _head_select(l[:ATT_QB], l[ATT_QB:])
            return carry

        lax.fori_loop(0, dil * nblk // ATT_UNROLL, block_group, 0)

    for pi in range(len(DILATED_PATTERNS) - 1, 0, -1):
        dil = DILATED_PATTERNS[pi][1]
        n = seq // dil
        prev_dil = dil // ATT_STEP
        acc_c, mx_c, den_c = states[pi - 1]
        acc_f, mx_f, den_f = states[pi]
        for r1 in range(prev_dil):
            for r2 in range(ATT_STEP):
                r = r1 + prev_dil * r2
                rows = pl.ds(r1 * n * ATT_STEP + r2, n, stride=ATT_STEP)
                fine = slice(r * n, (r + 1) * n)
                m_old = mx_c[rows, :]
                m_new = mx_f[fine, :]
                m_all = jnp.maximum(m_old, m_new)
                decay = jnp.exp(-jnp.abs(m_old - m_new))
                a_old = jnp.where(m_old >= m_new, 1.0, decay)
                a_new = jnp.where(m_old >= m_new, decay, 1.0)
                acc_c[rows, :] = acc_c[rows, :] * a_old + acc_f[fine, :] * a_new
                den_c[rows, :] = den_c[rows, :] * a_old + den_f[fine, :] * a_new
                mx_c[rows, :] = m_all

    acc, _, den = states[0]
    g = g_ref[0].astype(F32)
    o_ref[0] = (acc[...] / den[...] * (g * jax.nn.sigmoid(g))).astype(o_ref.dtype)


def _dilated_attention(proj, bias, n_heads):
    _, b, seq, _ = proj.shape
    npair = n_heads // 2
    max_dil = max(d for _, d in DILATED_PATTERNS)
    kd_rows = seq + 2 * ATT_HALF * max_dil
    blk = lambda off: pl.BlockSpec((None, 1, seq, LANES), lambda i, j: (off * npair + j, i, 0, 0))
    state = pltpu.VMEM((seq, LANES), F32)
    return pl.pallas_call(
        _dilated_kernel,
        grid=(b, npair),
        in_specs=[blk(0), blk(1), blk(2), blk(3),
                  pl.BlockSpec((len(DILATED_PATTERNS), 2, 3, ATT_QB, ATT_KW), lambda i, j: (0, j, 0, 0, 0))],
        out_specs=pl.BlockSpec((None, 1, seq, LANES), lambda i, j: (j, i, 0, 0)),
        out_shape=jax.ShapeDtypeStruct((npair, b, seq, LANES), BF16),
        scratch_shapes=[state] * 6
                       + [pltpu.VMEM((seq, LANES), BF16), pltpu.VMEM((kd_rows, LANES), BF16),
                          pltpu.VMEM((kd_rows, LANES), BF16)]
                       + [state] * (3 * (len(DILATED_PATTERNS) - 1)),
        compiler_params=_cparams(("parallel", "parallel")),
        name="dilated_attention",
    )(proj, proj, proj, proj, bias)


def _na_bias(rpb, rows):
    assert min(NA_ROWS, rows) % 2 == 0
    cols = np.arange(GRID_W)
    cs = np.clip(cols - NA_COLS // 2, 0, GRID_W - NA_COLS)
    kc = np.arange(GRID_W)
    inside = (kc[None, :] >= cs[:, None]) & (kc[None, :] < cs[:, None] + NA_COLS)
    col_idx = kc[None, :] - cols[:, None] + NA_COLS - 1
    onehot = (col_idx[:, :, None] == np.arange(2 * NA_COLS - 1)[None, None, :]).astype(np.float32)
    e = jnp.einsum("hrd,cqd->hrcq", rpb.astype(F32), jnp.asarray(onehot), precision=lax.Precision.HIGHEST)
    e = jnp.where(jnp.asarray(inside)[None, None], e, NEG_INF)
    return jnp.concatenate([e[:, :-1], e[:, 1:]], axis=-1)


def _na_kernel(q_ref, k_ref, v_ref, g_ref, bias_ref, o_ref, qm):
    seq = q_ref.shape[1]
    rows = seq // GRID_W
    kr_n = min(NA_ROWS, rows)
    lane = lax.broadcasted_iota(jnp.int32, (seq, LANES), 1)
    q = q_ref[0] * (HEAD_DIM ** -0.5)
    for h in range(2):
        qm[h] = jnp.where((lane < HEAD_DIM) == (h == 0), q, jnp.zeros_like(q))

    def row_group(it, carry):
        def scores(batch):
            out = []
            for u in range(batch * NA_BATCH, (batch + 1) * NA_BATCH):
                r = it * NA_UNROLL + u
                rs = jnp.clip(r - kr_n // 2, 0, rows - kr_n)
                shift = rs - r + NA_ROWS - 1
                q0 = pl.multiple_of(r * GRID_W, GRID_W)
                k0 = pl.multiple_of(rs * GRID_W, GRID_W)
                qcat = jnp.concatenate([qm[0, pl.ds(q0, GRID_W), :], qm[1, pl.ds(q0, GRID_W), :]], axis=0)
                s = lax.dot_general(qcat, k_ref[0, pl.ds(k0, kr_n * GRID_W), :], (((1,), (1,)), ((), ())),
                                    preferred_element_type=F32)
                out.append((q0, k0, shift, s))
            return out

        def finish(chains):
            probs = []
            for q0, k0, shift, s in chains:
                bias = [jnp.concatenate([bias_ref[h, shift + i] for i in range(0, kr_n, 2)], axis=1)
                        for h in range(2)]
                s = s + jnp.concatenate(bias, axis=0)
                p = jnp.exp(s - jnp.max(s, axis=1, keepdims=True))
                probs.append((p.astype(BF16), jnp.sum(p, axis=1, keepdims=True)))
            for (q0, k0, shift, s), (p, l) in zip(chains, probs):
                o = jnp.dot(p, v_ref[0, pl.ds(k0, kr_n * GRID_W), :], preferred_element_type=F32) / l
                g = g_ref[0, pl.ds(q0, GRID_W), :].astype(F32)
                o = _head_select(o[:GRID_W], o[GRID_W:]) * (g * jax.nn.sigmoid(g))
                o_ref[0, pl.ds(q0, GRID_W), :] = o.astype(o_ref.dtype)

        n_batches = NA_UNROLL // NA_BATCH
        pending = scores(0)
        for batch in range(n_batches):
            ahead = scores(batch + 1) if batch + 1 < n_batches else None
            finish(pending)
            pending = ahead
        return carry

    lax.fori_loop(0, rows // NA_UNROLL, row_group, 0)


def _neighbourhood_attention(proj, bias, n_heads):
    _, b, seq, _ = proj.shape
    npair = n_heads // 2
    blk = lambda off: pl.BlockSpec((None, 1, seq, LANES), lambda j, i: (off * npair + j, i, 0, 0))
    return pl.pallas_call(
        _na_kernel,
        grid=(npair, b),
        in_specs=[blk(0), blk(1), blk(2), blk(3),
                  pl.BlockSpec((2,) + bias.shape[1:], lambda j, i: (j, 0, 0, 0))],
        out_specs=pl.BlockSpec((None, 1, seq, LANES), lambda j, i: (j, i, 0, 0)),
        out_shape=jax.ShapeDtypeStruct((npair, b, seq, LANES), BF16),
        scratch_shapes=[pltpu.VMEM((2, seq, LANES), BF16)],
        compiler_params=_cparams(("parallel", "parallel")),
        name="neighbourhood_attention",
    )(proj, proj, proj, proj, bias)


def _out_kernel(n_y, n_t, *refs):
    n = n_y + n_t
    y_refs = refs[:n]
    w_refs = refs[n:2 * n]
    h_ref, p_ref, wp_ref, wg_ref, gpost_ref, gple_ref, gnext_ref, h_out, u_out = refs[2 * n:]
    t = None
    for idx, (y_ref, w_ref) in enumerate(zip(y_refs, w_refs)):
        if idx < n_y:
            y = jnp.concatenate([y_ref[c] for c in range(y_ref.shape[0])], axis=1)
            part = jnp.dot(y, w_ref[...], preferred_element_type=F32)
        else:
            part = lax.dot_general(y_ref[0], w_ref[...], (((0,), (0,)), ((), ())), preferred_element_type=F32)
        t = part if t is None else t + part
    h = h_ref[...] + _rms(t, gpost_ref[...])
    e = _rms(jnp.dot(p_ref[...].astype(BF16), wp_ref[...], preferred_element_type=F32), gple_ref[...])
    gate = jax.nn.sigmoid(jnp.dot(h.astype(BF16), wg_ref[...], preferred_element_type=F32))
    h = h + gate * e
    h_out[...] = h
    u_out[...] = _rms(h, gnext_ref[...]).astype(u_out.dtype)


def _out_block(ys, ys_t, ws, h, p, layer, w_ple, w_gate, g_post, g_ple, g_next, tm=1024):
    m, d = h.shape
    row = lambda a: pl.BlockSpec((tm, a.shape[1]), lambda i: (i, 0))
    slab = lambda a: pl.BlockSpec((a.shape[0], tm, LANES), lambda i: (0, i, 0))
    p_spec = pl.BlockSpec((None, tm, p.shape[2]), lambda i: (layer, i, 0))
    full = lambda a: pl.BlockSpec(a.shape, lambda i: (0, 0))
    vec = lambda g: g.reshape(1, d)

    def col(a):
        tiles = a.shape[2] // tm
        return pl.BlockSpec((1, a.shape[1], tm), lambda i: (i // tiles, 0, i % tiles))

    return pl.pallas_call(
        functools.partial(_out_kernel, len(ys), len(ys_t)),
        grid=(m // tm,),
        in_specs=[slab(y) for y in ys] + [col(y) for y in ys_t] + [full(w) for w in ws]
                 + [row(h), p_spec, full(w_ple), full(w_gate), full(vec(g_post)), full(vec(g_ple)), full(vec(g_next))],
        out_specs=[pl.BlockSpec((tm, d), lambda i: (i, 0)), pl.BlockSpec((tm, d), lambda i: (i, 0))],
        out_shape=[jax.ShapeDtypeStruct((m, d), F32), jax.ShapeDtypeStruct((m, d), BF16)],
        compiler_params=_cparams(("parallel",)),
        name="out_proj_residual_ple",
    )(*ys, *ys_t, *ws, h, p, w_ple, w_gate, vec(g_post), vec(g_ple), vec(g_next))


def kernel(x, p, w_in, w_out, norm_pre, norm_post, hyena_conv_w, hyena_conv_b, hyena_w1, hyena_b1, hyena_w2, hyena_b2, hyena_w3, hyena_b3, hyena_w4, hyena_freq, hyena_skip, rel_bias, na_rpb, ple_proj, ple_norm, ple_gate):
    b, seq, d = x.shape
    depth = w_in.shape[0]
    d_inner = w_out.shape[1]
    hw = d_inner // 2
    m = b * seq
    consts = _fft_consts(seq)
    dil_bias = _dilated_bias(rel_bias)
    h = x.reshape(m, d)
    u = _rmsnorm_bf16(h, norm_pre[0])
    w_all = w_in.astype(BF16)
    p_all = p.reshape(depth, m, -1)
    for i in range(depth):
        j = i // 2
        if i % 2 == 0:
            w_a = w_all[i, :, :4 * hw].T
            ident = jnp.tile(jnp.array([0.0, 1.0, 0.0], F32)[:, None], (1, hw))
            conv_w = jnp.concatenate([hyena_conv_w[j], ident], axis=1).T
            conv_b = jnp.concatenate([hyena_conv_b[j], jnp.zeros((hw,), F32)])[:, None]
            hy = _proj_t(w_a, u.reshape(b, seq, d), conv_w, conv_b)
            halves = _hyena_filters_t(seq, hw, hyena_w1[j], hyena_b1[j], hyena_w2[j], hyena_b2[j],
                                      hyena_w3[j], hyena_b3[j], hyena_w4[j], hyena_freq[j])
            kr, ki = _filter_spectrum(halves, consts)
            skip = hyena_skip[j][:, :, None, None]
            z = _hyena_conv(False, hy, 0, hy, hw, hw, skip[0], kr, ki, 0, consts)
            ya_t = _hyena_conv(True, z, 0, hy, 2 * hw, 3 * hw, skip[1], kr, ki, 1, consts)
            proj_b = _matmul_bf16(u, w_all[i:i + 1, :, 4 * hw:], 0, 0, 4 * hw).reshape(-1, b, seq, LANES)
            yb = _dilated_attention(proj_b, dil_bias, hw // HEAD_DIM)
            ys, ys_t = [yb.reshape(-1, m, LANES)], [ya_t]
            ws = [w_out[i, hw:].astype(BF16), w_out[i, :hw].astype(BF16)]
        else:
            proj = _matmul_bf16(u, w_all, i, 0, 4 * d_inner).reshape(-1, b, seq, LANES)
            yc = _neighbourhood_attention(proj, _na_bias(na_rpb[j], seq // GRID_W), d_inner // HEAD_DIM)
            ys, ys_t = [yc.reshape(-1, m, LANES)], []
            ws = [w_out[i].astype(BF16)]
        g_next = norm_pre[i + 1] if i + 1 < depth else norm_pre[i]
        h, u = _out_block(ys, ys_t, ws, h, p_all, i, ple_proj[i].astype(BF16), ple_gate[i].astype(BF16),
                          norm_post[i], ple_norm[i], g_next)
    return h.reshape(b, seq, d)
```

```python
import functools
import math

import numpy as np
import jax
import jax.numpy as jnp
from jax import lax
from jax.experimental import pallas as pl
from jax.experimental.pallas import tpu as pltpu

F32 = jnp.float32
BF16 = jnp.bfloat16

HEAD_DIM = 64
LANES = 128
HYENA_EMB_DIM = 33
HYENA_FAST_DECAY = 0.3
HYENA_SLOW_DECAY = 1.5
HYENA_TARGET = 1e-2
DILATED_PATTERNS = ((128, 1), (512, 4), (2048, 16))
GRID_W = 64
NA_ROWS = 8
NA_COLS = 16
REL_BUCKETS = 32
REL_MAX_DIST = 1024
RMS_EPS = 1e-6
NEG_INF = -1e30
ATT_QB = 128
ATT_HALF = 64
ATT_KW = ATT_QB + 2 * ATT_HALF
ATT_UNROLL = 8
ATT_STEP = 4
assert all(b[1] == a[1] * ATT_STEP for a, b in zip(DILATED_PATTERNS, DILATED_PATTERNS[1:])) and DILATED_PATTERNS[0][1] == 1
NA_BATCH = 8
NA_UNROLL = 32
VMEM_LIMIT = 56 * 1024 * 1024


def _cparams(sem):
    return pltpu.CompilerParams(dimension_semantics=sem, vmem_limit_bytes=VMEM_LIMIT)


def _rms(x, g):
    return x * lax.rsqrt(jnp.mean(x * x, axis=-1, keepdims=True) + RMS_EPS) * g


def _rmsnorm_kernel(x_ref, g_ref, o_ref):
    o_ref[...] = _rms(x_ref[...], g_ref[...]).astype(o_ref.dtype)


def _rmsnorm_bf16(x2d, g, tm=1024):
    m, d = x2d.shape
    return pl.pallas_call(
        _rmsnorm_kernel,
        grid=(m // tm,),
        in_specs=[pl.BlockSpec((tm, d), lambda i: (i, 0)), pl.BlockSpec((1, d), lambda i: (0, 0))],
        out_specs=pl.BlockSpec((tm, d), lambda i: (i, 0)),
        out_shape=jax.ShapeDtypeStruct((m, d), BF16),
        compiler_params=_cparams(("parallel",)),
        name="rmsnorm",
    )(x2d, g.reshape(1, d))


def _matmul_kernel(a_ref, b_ref, o_ref):
    res = jnp.dot(a_ref[...], b_ref[...], preferred_element_type=F32).astype(o_ref.dtype)
    for c in range(o_ref.shape[0]):
        o_ref[c] = res[:, c * LANES:(c + 1) * LANES]


def _matmul_bf16(a, w, layer, col0, n, tm=2048, tn=1024):
    m, k = a.shape
    return pl.pallas_call(
        _matmul_kernel,
        grid=(m // tm, n // tn),
        in_specs=[pl.BlockSpec((tm, k), lambda i, j: (i, 0)), pl.BlockSpec((None, k, tn), lambda i, j: (layer, 0, col0 // tn + j))],
        out_specs=pl.BlockSpec((tn // LANES, tm, LANES), lambda i, j: (j, i, 0)),
        out_shape=jax.ShapeDtypeStruct((n // LANES, m, LANES), BF16),
        compiler_params=_cparams(("parallel", "parallel")),
        name="in_proj",
    )(a, w)


def _proj_t_kernel(w_ref, u_ref, cw_ref, cb_ref, o_ref):
    p = lax.dot_general(w_ref[...], u_ref[0], (((1,), (1,)), ((), ())), preferred_element_type=F32)
    seq = p.shape[1]
    t = lax.broadcasted_iota(jnp.int32, p.shape, 1)
    prev = jnp.where(t == 0, 0.0, pltpu.roll(p, 1, axis=1))
    nxt = jnp.where(t == seq - 1, 0.0, pltpu.roll(p, seq - 1, axis=1))
    cw = cw_ref[...]
    o_ref[0] = (prev * cw[:, 0:1] + p * cw[:, 1:2] + nxt * cw[:, 2:3] + cb_ref[...]).astype(o_ref.dtype)


def _proj_t(w_t, u, conv_w, conv_b, tc=256):
    c, d = w_t.shape
    b, seq, _ = u.shape
    return pl.pallas_call(
        _proj_t_kernel,
        grid=(b, c // tc),
        in_specs=[
            pl.BlockSpec((tc, d), lambda i, j: (j, 0)),
            pl.BlockSpec((1, seq, d), lambda i, j: (i, 0, 0)),
            pl.BlockSpec((tc, 3), lambda i, j: (j, 0)),
            pl.BlockSpec((tc, 1), lambda i, j: (j, 0)),
        ],
        out_specs=pl.BlockSpec((1, tc, seq), lambda i, j: (i, j, 0)),
        out_shape=jax.ShapeDtypeStruct((b, c, seq), BF16),
        compiler_params=_cparams(("parallel", "parallel")),
        name="in_proj_t_conv3",
    )(w_t, u, conv_w, conv_b)


def _hyena_filter_kernel(z_ref, w1_ref, b1_ref, w2_ref, b2_ref, w3_ref, b3_ref, fq_ref, w4_ref, dl_ref, t_ref,
                         o_ref, a_scr):
    hp = lax.Precision.HIGHEST

    @pl.when((pl.program_id(0) == 0) & (pl.program_id(1) == 0))
    def _():
        for d in range(2):
            a = z_ref[d]
            for w_ref, b_ref in ((w1_ref, b1_ref), (w2_ref, b2_ref), (w3_ref, b3_ref)):
                a = jnp.sin(fq_ref[...] * (jnp.dot(w_ref[...], a, precision=hp, preferred_element_type=F32) + b_ref[...]))
            a_scr[d] = a

    seq = o_ref.shape[-1]
    proj = lambda d: jnp.dot(w4_ref[0, d].astype(BF16), a_scr[d].astype(BF16), preferred_element_type=F32)
    lo = proj(0) * jnp.exp(-t_ref[0] * dl_ref[...])
    hi = proj(1) * jnp.exp(-t_ref[1] * dl_ref[...])
    hi = jnp.where(lax.broadcasted_iota(jnp.int32, (1, seq), 1) == 0, 0.0, hi)
    norm = jnp.sum(jnp.abs(lo), axis=1, keepdims=True) + jnp.sum(jnp.abs(hi), axis=1, keepdims=True)
    o_ref[0, 0] = (lo / norm).astype(o_ref.dtype)
    o_ref[1, 0] = (hi / norm).astype(o_ref.dtype)


def _hyena_filters_t(seq, c, w1, b1, w2, b2, w3, b3, w4, freq, tc=256):
    tc = min(tc, c)
    fo = w2.shape[0]
    emb = -(-HYENA_EMB_DIM // 8) * 8
    t = jnp.linspace(0.0, 1.0, seq, dtype=F32)
    bands = (HYENA_EMB_DIM - 1) // 2
    fr = jnp.linspace(1e-4, bands - 1, bands, dtype=F32)[:, None]
    mirror = (seq - jnp.arange(seq)) % seq
    tables, times = [], []
    for pos, tp in ((jnp.arange(seq), t), (mirror, jnp.roll(t[::-1], 1))):
        wpos = (2.0 * math.pi / seq) * pos.astype(F32)[None, :]
        tp = tp[None, :]
        z = jnp.concatenate([tp, jnp.cos(fr * wpos), -jnp.sin(fr * wpos),
                             jnp.zeros((emb - HYENA_EMB_DIM, seq), F32)], axis=0)
        tables.append(z)
        times.append(tp)
    z = jnp.stack(tables)
    tt = jnp.stack(times)
    w1_t = jnp.pad(w1.astype(F32).T, ((0, 0), (0, emb - HYENA_EMB_DIM)))
    w4_t = jnp.transpose(w4.astype(F32).reshape(fo, 2, 2, c), (1, 2, 3, 0))
    min_decay = math.log(HYENA_TARGET) / HYENA_SLOW_DECAY
    max_decay = math.log(HYENA_TARGET) / HYENA_FAST_DECAY
    deltas = jnp.abs(jnp.linspace(min_decay, max_decay, c, dtype=F32))[:, None]
    col = lambda v: v.astype(F32)[:, None]
    full = lambda a: pl.BlockSpec(a.shape, lambda f, i: (0,) * a.ndim)
    args = (z, w1_t, col(b1), w2.astype(F32).T, col(b2), w3.astype(F32).T, col(b3), col(freq))
    return pl.pallas_call(
        _hyena_filter_kernel,
        grid=(2, c // tc),
        in_specs=[full(a) for a in args]
                 + [pl.BlockSpec((1, 2, tc, fo), lambda f, i: (f, 0, i, 0)),
                    pl.BlockSpec((tc, 1), lambda f, i: (i, 0)), full(tt)],
        out_specs=pl.BlockSpec((2, 1, tc, seq), lambda f, i: (0, f, i, 0)),
        out_shape=jax.ShapeDtypeStruct((2, 2, c, seq), BF16),
        scratch_shapes=[pltpu.VMEM((2, fo, seq), F32)],
        compiler_params=_cparams(("arbitrary", "arbitrary")),
        name="hyena_filter_mlp",
    )(*args, w4_t, deltas, tt)


def _fft_consts(seq):
    n_fft = 2 * seq
    n1 = n_fft // LANES
    nb = n1 // 2 + 1
    kp = -(-nb // 8) * 8
    k1 = jnp.arange(kp, dtype=jnp.int32)
    keep = (k1 < nb)[:, None]
    two_pi = 2.0 * math.pi

    th = ((k1[:, None] * jnp.arange(n1, dtype=jnp.int32)[None, :]) % n1).astype(F32) * (two_pi / n1)
    f1 = jnp.concatenate([jnp.where(keep, jnp.cos(th), 0.0), jnp.where(keep, -jnp.sin(th), 0.0)], axis=0)
    ph = (k1[:, None] * jnp.arange(LANES, dtype=jnp.int32)[None, :]).astype(F32) * (two_pi / n_fft)
    tw = jnp.stack([jnp.cos(ph), -jnp.sin(ph)])
    om = ((jnp.arange(LANES, dtype=jnp.int32)[:, None] * jnp.arange(LANES, dtype=jnp.int32)[None, :]) % LANES
          ).astype(F32) * (two_pi / LANES)
    cplx = lambda c, s: jnp.concatenate([jnp.concatenate([c, s], axis=1), jnp.concatenate([-s, c], axis=1)], axis=0)
    f2 = cplx(jnp.cos(om), -jnp.sin(om))
    f2i = cplx(jnp.cos(om), jnp.sin(om))
    weight = jnp.where((k1 == 0) | (k1 == nb - 1), 1.0, 2.0)[:, None]
    g = jnp.concatenate([jnp.where(keep, weight * jnp.cos(th[:, :n1 // 2]), 0.0),
                         jnp.where(keep, -weight * jnp.sin(th[:, :n1 // 2]), 0.0)], axis=0).T
    return dict(f1=f1.astype(BF16), tw=tw, f2=f2.astype(BF16), f2i=f2i.astype(BF16), g=g.astype(BF16), kp=kp)


def _per_channel_matmul(mat_ref, planes, out_scr, ct):
    def body(i, carry):
        c = 2 * i
        pair = jnp.concatenate([planes(c), planes(c + 1)], axis=1)
        res = jnp.dot(mat_ref[...], pair, preferred_element_type=F32)
        out_scr[c] = res[:, :LANES]
        out_scr[c + 1] = res[:, LANES:]
        return carry
    lax.fori_loop(0, ct // 2, body, 0, unroll=16)


def _twiddle_stage2(a, tw_ref, f2_ref):
    ct, kp2, _ = a.shape
    kp = kp2 // 2
    ar, ai = a[:, :kp], a[:, kp:]
    twr, twi = tw_ref[0], tw_ref[1]
    st = jnp.concatenate([ar * twr - ai * twi, ar * twi + ai * twr], axis=2).astype(BF16)
    p = jnp.dot(st.reshape(ct * kp, 2 * LANES), f2_ref[...], preferred_element_type=F32).reshape(ct, kp, 2 * LANES)
    return p[:, :, :LANES], p[:, :, LANES:]


def _stage2_twiddle_inv(yr, yi, tw_ref, f2i_ref):
    ct, kp, _ = yr.shape
    st = jnp.concatenate([yr, yi], axis=2).astype(BF16)
    p = jnp.dot(st.reshape(ct * kp, 2 * LANES), f2i_ref[...], preferred_element_type=F32).reshape(ct, kp, 2 * LANES)
    tr, ti = p[:, :, :LANES], p[:, :, LANES:]
    twr, twi = tw_ref[0], tw_ref[1]
    return jnp.concatenate([tr * twr + ti * twi, ti * twr - tr * twi], axis=1).astype(BF16)


def _filter_spec_kernel(lo_ref, hi_ref, f1_ref, tw_ref, f2_ref, kr_ref, ki_ref, a_scr):
    ct = lo_ref.shape[2]
    planes = lambda c: jnp.concatenate([lo_ref[0, 0, c], hi_ref[0, 0, c]], axis=0)
    _per_channel_matmul(f1_ref, planes, a_scr, ct)
    xr, xi = _twiddle_stage2(a_scr[...], tw_ref, f2_ref)
    n_fft = f1_ref.shape[1] * LANES
    kr_ref[0] = xr * (1.0 / n_fft)
    ki_ref[0] = xi * (1.0 / n_fft)


def _filter_spectrum(halves, consts, ct=128):
    _, nf, c, seq = halves.shape
    ct = min(ct, c)
    kp = consts["kp"]
    h5 = halves.reshape(2, nf, c, seq // LANES, LANES)
    half = lambda s: pl.BlockSpec((1, 1, ct, seq // LANES, LANES), lambda f, i: (s, f, i, 0, 0))
    full = lambda a: pl.BlockSpec(a.shape, lambda f, i: (0,) * a.ndim)
    spec = jax.ShapeDtypeStruct((nf, c, kp, LANES), F32)
    out = pl.BlockSpec((1, ct, kp, LANES), lambda f, i: (f, i, 0, 0))
    return pl.pallas_call(
        _filter_spec_kernel,
        grid=(nf, c // ct),
        in_specs=[half(0), half(1), full(consts["f1"]), full(consts["tw"]), full(consts["f2"])],
        out_specs=[out, out],
        out_shape=[spec, spec],
        scratch_shapes=[pltpu.VMEM((ct, 2 * kp, LANES), F32)],
        compiler_params=_cparams(("parallel", "parallel")),
        name="hyena_filter_spectrum",
    )(h5, h5, consts["f1"], consts["tw"], consts["f2"])


def _hyena_conv_kernel(second, x_ref, m_ref, e_ref, skip_ref, kr_ref, ki_ref, f1_ref, tw_ref, f2_ref, f2i_ref, g_ref,
                       o_ref, a_scr, t_scr, y_scr):
    ct = x_ref.shape[1]
    _per_channel_matmul(f1_ref, lambda c: x_ref[0, c], a_scr, ct)
    xr, xi = _twiddle_stage2(a_scr[...], tw_ref, f2_ref)
    kr, ki = kr_ref[0], ki_ref[0]
    t_scr[...] = _stage2_twiddle_inv(xr * kr - xi * ki, xr * ki + xi * kr, tw_ref, f2i_ref)

    _per_channel_matmul(g_ref, lambda c: t_scr[c], y_scr, ct)

    out = m_ref[0].astype(F32) * (y_scr[...] + x_ref[0].astype(F32) * skip_ref[...])
    if second:
        e = e_ref[0].astype(F32)
        out = out * (e * jax.nn.sigmoid(e))
    o_ref[0] = out.astype(o_ref.dtype)


def _hyena_conv(second, x, x_row0, hy, m_row0, e_row0, skip, kr, ki, filt, consts, ct=128):
    b, _, seq = x.shape
    c = kr.shape[1]
    ct = min(ct, c)
    kp = consts["kp"]
    n1h = seq // LANES
    view = lambda a: a.reshape(a.shape[0], a.shape[1], n1h, LANES)
    rows = lambda blk0: pl.BlockSpec((1, ct, n1h, LANES), lambda j, i: (i, blk0 + j, 0, 0))
    full = lambda a: pl.BlockSpec(a.shape, lambda j, i: (0,) * a.ndim)
    kspec = pl.BlockSpec((1, ct, kp, LANES), lambda j, i: (filt, j, 0, 0))
    mats = [consts["f1"][:, :n1h], consts["tw"], consts["f2"], consts["f2i"], consts["g"]]
    out = pl.pallas_call(
        functools.partial(_hyena_conv_kernel, second),
        grid=(c // ct, b),
        in_specs=[rows(x_row0 // ct), rows(m_row0 // ct), rows(e_row0 // ct),
                  pl.BlockSpec((ct, 1, 1), lambda j, i: (j, 0, 0)),
                  kspec, kspec] + [full(a) for a in mats],
        out_specs=pl.BlockSpec((1, ct, n1h, LANES), lambda j, i: (i, j, 0, 0)),
        out_shape=jax.ShapeDtypeStruct((b, c, n1h, LANES), BF16),
        scratch_shapes=[pltpu.VMEM((ct, 2 * kp, LANES), F32), pltpu.VMEM((ct, 2 * kp, LANES), BF16),
                        pltpu.VMEM((ct, n1h, LANES), F32)],
        compiler_params=_cparams(("parallel", "parallel")),
        name="hyena_conv2" if second else "hyena_conv1",
    )(view(x), view(hy), view(hy), skip, kr, ki, *mats)
    return out.reshape(b, c, seq)


def _t5_bucket(rel):
    half_b = REL_BUCKETS // 2
    max_exact = half_b // 2
    ret = jnp.where(rel > 0, half_b, 0)
    n = jnp.abs(rel)
    nf = jnp.maximum(n, 1).astype(F32)
    large = max_exact + (jnp.log(nf / max_exact) / math.log(REL_MAX_DIST / max_exact)
                         * (half_b - max_exact)).astype(jnp.int32)
    large = jnp.minimum(large, half_b - 1)
    return ret + jnp.where(n < max_exact, n, large)


def _dilated_bias(rel_bias):
    rel = jnp.arange(ATT_KW)[None, :] - ATT_HALF - jnp.arange(ATT_QB)[:, None]
    col = jnp.arange(ATT_KW)
    edge = jnp.stack([col >= ATT_HALF, col >= 0, col < ATT_KW - ATT_HALF])
    out = []
    for _, dil in DILATED_PATTERNS:
        onehot = jax.nn.one_hot(_t5_bucket(rel * dil), REL_BUCKETS, dtype=F32)
        bias = jnp.einsum("qkb,bh->qkh", onehot, rel_bias.astype(F32), precision=lax.Precision.HIGHEST)
        bias = jnp.where((jnp.abs(rel) <= ATT_HALF)[:, :, None], bias, NEG_INF)
        bias = jnp.transpose(bias, (2, 0, 1))[:, None]
        out.append(jnp.where(edge[None, :, None, :], bias, NEG_INF))
    return jnp.stack(out)


def _head_select(first, second):
    lane = lax.broadcasted_iota(jnp.int32, first.shape[:-1] + (LANES,), len(first.shape) - 1)
    return jnp.where(lane < HEAD_DIM, first, second)


def _dilated_kernel(q_ref, k_ref, v_ref, g_ref, bias_ref, o_ref, *scratch):
    stage_a, stage_b = scratch[0:3], scratch[3:6]
    qd, kd, vd = scratch[6:9]
    last = len(DILATED_PATTERNS) - 1
    states = [scratch[9 + 3 * p:12 + 3 * p] for p in range(last)]
    seq = q_ref.shape[1]
    stage_a[0][...] = q_ref[0].astype(F32) * (HEAD_DIM ** -0.5)
    stage_a[1][...] = k_ref[0].astype(F32)
    stage_a[2][...] = v_ref[0].astype(F32)
    zero_halo = jnp.zeros((ATT_HALF, LANES), BF16)
    first_head = lax.broadcasted_iota(jnp.int32, (ATT_QB, LANES), 1) < HEAD_DIM

    src, dst = stage_a, stage_b
    for pi, (_, dil) in enumerate(DILATED_PATTERNS):
        n = seq // dil
        nblk = n // ATT_QB
        if pi > 0:
            prev_dil = dil // ATT_STEP
            for r1 in range(prev_dil):
                for r2 in range(ATT_STEP):
                    r = r1 + prev_dil * r2
                    for a_src, a_dst in zip(src, dst):
                        a_dst[r * n:(r + 1) * n, :] = a_src[pl.ds(r1 * n * ATT_STEP + r2, n, stride=ATT_STEP), :]
            src, dst = dst, src
        stride_k = n + 2 * ATT_HALF
        qd[...] = src[0][...].astype(BF16)
        for r in range(dil):
            base = r * stride_k
            for a_src, a_dst in ((src[1], kd), (src[2], vd)):
                a_dst[base:base + ATT_HALF, :] = zero_halo
                a_dst[base + ATT_HALF:base + ATT_HALF + n, :] = a_src[r * n:(r + 1) * n, :].astype(BF16)
                a_dst[base + ATT_HALF + n:base + stride_k, :] = zero_halo
        if pi == last:
            states.append(dst)
        acc_w, mx_w, den_w = states[pi]

        def block_group(it, carry, pi=pi, nblk=nblk, acc_w=acc_w, mx_w=mx_w, den_w=den_w):
            chains = []
            for u in range(ATT_UNROLL):
                blk = it * ATT_UNROLL + u
                r = blk // nblk
                qb = blk % nblk
                q0 = pl.multiple_of(blk * ATT_QB, ATT_QB)
                k0 = pl.multiple_of(q0 + r * (2 * ATT_HALF), ATT_QB)
                edge = jnp.where(qb == 0, 0, jnp.where(qb == nblk - 1, 2, 1))
                qblk = qd[pl.ds(q0, ATT_QB), :]
                qcat = jnp.concatenate([jnp.where(first_head, qblk, jnp.zeros_like(qblk)),
                                        jnp.where(first_head, jnp.zeros_like(qblk), qblk)], axis=0)
                s = lax.dot_general(qcat, kd[pl.ds(k0, ATT_KW), :], (((1,), (1,)), ((), ())),
                                    preferred_element_type=F32)
                chains.append((q0, k0, edge, s))
            probs = []
            for q0, k0, edge, s in chains:
                s = s + jnp.concatenate([bias_ref[pi, 0, edge], bias_ref[pi, 1, edge]], axis=0)
                m = jnp.max(s, axis=1, keepdims=True)
                p = jnp.exp(s - m)
                probs.append((p.astype(BF16), m, jnp.sum(p, axis=1, keepdims=True)))
            for (q0, k0, edge, s), (p, m, l) in zip(chains, probs):
                pv = jnp.dot(p, vd[pl.ds(k0, ATT_KW), :], preferred_element_type=F32)
                acc_w[pl.ds(q0, ATT_QB), :] = _head_select(pv[:ATT_QB], pv[ATT_QB:])
                mx_w[pl.ds(q0, ATT_QB), :] = _head_select(m[:ATT_QB], m[ATT_QB:])
                den_w[pl.ds(q0, ATT_QB), :] = _head_select(l[:ATT_QB], l[ATT_QB:])
            return carry

        lax.fori_loop(0, dil * nblk // ATT_UNROLL, block_group, 0)

    for pi in range(len(DILATED_PATTERNS) - 1, 0, -1):
        dil = DILATED_PATTERNS[pi][1]
        n = seq // dil
        prev_dil = dil // ATT_STEP
        acc_c, mx_c, den_c = states[pi - 1]
        acc_f, mx_f, den_f = states[pi]
        for r1 in range(prev_dil):
            for r2 in range(ATT_STEP):
                r = r1 + prev_dil * r2
                rows = pl.ds(r1 * n * ATT_STEP + r2, n, stride=ATT_STEP)
                fine = slice(r * n, (r + 1) * n)
                m_old = mx_c[rows, :]
                m_new = mx_f[fine, :]
                m_all = jnp.maximum(m_old, m_new)
                decay = jnp.exp(-jnp.abs(m_old - m_new))
                a_old = jnp.where(m_old >= m_new, 1.0, decay)
                a_new = jnp.where(m_old >= m_new, decay, 1.0)
                acc_c[rows, :] = acc_c[rows, :] * a_old + acc_f[fine, :] * a_new
                den_c[rows, :] = den_c[rows, :] * a_old + den_f[fine, :] * a_new
                mx_c[rows, :] = m_all

    acc, _, den = states[0]
    g = g_ref[0].astype(F32)
    o_ref[0] = (acc[...] / den[...] * (g * jax.nn.sigmoid(g))).astype(o_ref.dtype)


def _dilated_attention(proj, bias, n_heads):
    _, b, seq, _ = proj.shape
    npair = n_heads // 2
    max_dil = max(d for _, d in DILATED_PATTERNS)
    kd_rows = seq + 2 * ATT_HALF * max_dil
    blk = lambda off: pl.BlockSpec((None, 1, seq, LANES), lambda i, j: (off * npair + j, i, 0, 0))
    state = pltpu.VMEM((seq, LANES), F32)
    return pl.pallas_call(
        _dilated_kernel,
        grid=(b, npair),
        in_specs=[blk(0), blk(1), blk(2), blk(3),
                  pl.BlockSpec((len(DILATED_PATTERNS), 2, 3, ATT_QB, ATT_KW), lambda i, j: (0, j, 0, 0, 0))],
        out_specs=pl.BlockSpec((None, 1, seq, LANES), lambda i, j: (j, i, 0, 0)),
        out_shape=jax.ShapeDtypeStruct((npair, b, seq, LANES), BF16),
        scratch_shapes=[state] * 6
                       + [pltpu.VMEM((seq, LANES), BF16), pltpu.VMEM((kd_rows, LANES), BF16),
                          pltpu.VMEM((kd_rows, LANES), BF16)]
                       + [state] * (3 * (len(DILATED_PATTERNS) - 1)),
        compiler_params=_cparams(("parallel", "parallel")),
        name="dilated_attention",
    )(proj, proj, proj, proj, bias)


def _na_bias(rpb, rows):
    assert min(NA_ROWS, rows) % 2 == 0
    cols = np.arange(GRID_W)
    cs = np.clip(cols - NA_COLS // 2, 0, GRID_W - NA_COLS)
    kc = np.arange(GRID_W)
    inside = (kc[None, :] >= cs[:, None]) & (kc[None, :] < cs[:, None] + NA_COLS)
    col_idx = kc[None, :] - cols[:, None] + NA_COLS - 1
    onehot = (col_idx[:, :, None] == np.arange(2 * NA_COLS - 1)[None, None, :]).astype(np.float32)
    e = jnp.einsum("hrd,cqd->hrcq", rpb.astype(F32), jnp.asarray(onehot), precision=lax.Precision.HIGHEST)
    e = jnp.where(jnp.asarray(inside)[None, None], e, NEG_INF)
    return jnp.concatenate([e[:, :-1], e[:, 1:]], axis=-1)


def _na_kernel(q_ref, k_ref, v_ref, g_ref, bias_ref, o_ref, qm):
    seq = q_ref.shape[1]
    rows = seq // GRID_W
    kr_n = min(NA_ROWS, rows)
    lane = lax.broadcasted_iota(jnp.int32, (seq, LANES), 1)
    q = q_ref[0] * (HEAD_DIM ** -0.5)
    for h in range(2):
        qm[h] = jnp.where((lane < HEAD_DIM) == (h == 0), q, jnp.zeros_like(q))

    def row_group(it, carry):
        def scores(batch):
            out = []
            for u in range(batch * NA_BATCH, (batch + 1) * NA_BATCH):
                r = it * NA_UNROLL + u
                rs = jnp.clip(r - kr_n // 2, 0, rows - kr_n)
                shift = rs - r + NA_ROWS - 1
                q0 = pl.multiple_of(r * GRID_W, GRID_W)
                k0 = pl.multiple_of(rs * GRID_W, GRID_W)
                qcat = jnp.concatenate([qm[0, pl.ds(q0, GRID_W), :], qm[1, pl.ds(q0, GRID_W), :]], axis=0)
                s = lax.dot_general(qcat, k_ref[0, pl.ds(k0, kr_n * GRID_W), :], (((1,), (1,)), ((), ())),
                                    preferred_element_type=F32)
                out.append((q0, k0, shift, s))
            return out

        def finish(chains):
            probs = []
            for q0, k0, shift, s in chains:
                bias = [jnp.concatenate([bias_ref[h, shift + i] for i in range(0, kr_n, 2)], axis=1)
                        for h in range(2)]
                s = s + jnp.concatenate(bias, axis=0)
                p = jnp.exp(s - jnp.max(s, axis=1, keepdims=True))
                probs.append((p.astype(BF16), jnp.sum(p, axis=1, keepdims=True)))
            for (q0, k0, shift, s), (p, l) in zip(chains, probs):
                o = jnp.dot(p, v_ref[0, pl.ds(k0, kr_n * GRID_W), :], preferred_element_type=F32) / l
                g = g_ref[0, pl.ds(q0, GRID_W), :].astype(F32)
                o = _head_select(o[:GRID_W], o[GRID_W:]) * (g * jax.nn.sigmoid(g))
                o_ref[0, pl.ds(q0, GRID_W), :] = o.astype(o_ref.dtype)

        n_batches = NA_UNROLL // NA_BATCH
        pending = scores(0)
        for batch in range(n_batches):
            ahead = scores(batch + 1) if batch + 1 < n_batches else None
            finish(pending)
            pending = ahead
        return carry

    lax.fori_loop(0, rows // NA_UNROLL, row_group, 0)


def _neighbourhood_attention(proj, bias, n_heads):
    _, b, seq, _ = proj.shape
    npair = n_heads // 2
    blk = lambda off: pl.BlockSpec((None, 1, seq, LANES), lambda j, i: (off * npair + j, i, 0, 0))
    return pl.pallas_call(
        _na_kernel,
        grid=(npair, b),
        in_specs=[blk(0), blk(1), blk(2), blk(3),
                  pl.BlockSpec((2,) + bias.shape[1:], lambda j, i: (j, 0, 0, 0))],
        out_specs=pl.BlockSpec((None, 1, seq, LANES), lambda j, i: (j, i, 0, 0)),
        out_shape=jax.ShapeDtypeStruct((npair, b, seq, LANES), BF16),
        scratch_shapes=[pltpu.VMEM((2, seq, LANES), BF16)],
        compiler_params=_cparams(("parallel", "parallel")),
        name="neighbourhood_attention",
    )(proj, proj, proj, proj, bias)


def _out_kernel(n_y, n_t, *refs):
    n = n_y + n_t
    y_refs = refs[:n]
    w_refs = refs[n:2 * n]
    h_ref, p_ref, wp_ref, wg_ref, gpost_ref, gple_ref, gnext_ref, h_out, u_out = refs[2 * n:]
    t = None
    for idx, (y_ref, w_ref) in enumerate(zip(y_refs, w_refs)):
        if idx < n_y:
            y = jnp.concatenate([y_ref[c] for c in range(y_ref.shape[0])], axis=1)
            part = jnp.dot(y, w_ref[...], preferred_element_type=F32)
        else:
            part = lax.dot_general(y_ref[0], w_ref[...], (((0,), (0,)), ((), ())), preferred_element_type=F32)
        t = part if t is None else t + part
    h = h_ref[...] + _rms(t, gpost_ref[...])
    e = _rms(jnp.dot(p_ref[...].astype(BF16), wp_ref[...], preferred_element_type=F32), gple_ref[...])
    gate = jax.nn.sigmoid(jnp.dot(h.astype(BF16), wg_ref[...], preferred_element_type=F32))
    h = h + gate * e
    h_out[...] = h
    u_out[...] = _rms(h, gnext_ref[...]).astype(u_out.dtype)


def _out_block(ys, ys_t, ws, h, p, layer, w_ple, w_gate, g_post, g_ple, g_next, tm=1024):
    m, d = h.shape
    row = lambda a: pl.BlockSpec((tm, a.shape[1]), lambda i: (i, 0))
    slab = lambda a: pl.BlockSpec((a.shape[0], tm, LANES), lambda i: (0, i, 0))
    p_spec = pl.BlockSpec((None, tm, p.shape[2]), lambda i: (layer, i, 0))
    full = lambda a: pl.BlockSpec(a.shape, lambda i: (0, 0))
    vec = lambda g: g.reshape(1, d)

    def col(a):
        tiles = a.shape[2] // tm
        return pl.BlockSpec((1, a.shape[1], tm), lambda i: (i // tiles, 0, i % tiles))

    return pl.pallas_call(
        functools.partial(_out_kernel, len(ys), len(ys_t)),
        grid=(m // tm,),
        in_specs=[slab(y) for y in ys] + [col(y) for y in ys_t] + [full(w) for w in ws]
                 + [row(h), p_spec, full(w_ple), full(w_gate), full(vec(g_post)), full(vec(g_ple)), full(vec(g_next))],
        out_specs=[pl.BlockSpec((tm, d), lambda i: (i, 0)), pl.BlockSpec((tm, d), lambda i: (i, 0))],
        out_shape=[jax.ShapeDtypeStruct((m, d), F32), jax.ShapeDtypeStruct((m, d), BF16)],
        compiler_params=_cparams(("parallel",)),
        name="out_proj_residual_ple",
    )(*ys, *ys_t, *ws, h, p, w_ple, w_gate, vec(g_post), vec(g_ple), vec(g_next))


def kernel(x, p, w_in, w_out, norm_pre, norm_post, hyena_conv_w, hyena_conv_b, hyena_w1, hyena_b1, hyena_w2, hyena_b2, hyena_w3, hyena_b3, hyena_w4, hyena_freq, hyena_skip, rel_bias, na_rpb, ple_proj, ple_norm, ple_gate):
    b, seq, d = x.shape
    depth = w_in.shape[0]
    d_inner = w_out.shape[1]
    hw = d_inner // 2
    m = b * seq
    consts = _fft_consts(seq)
    dil_bias = _dilated_bias(rel_bias)
    h = x.reshape(m, d)
    u = _rmsnorm_bf16(h, norm_pre[0])
    w_all = w_in.astype(BF16)
    p_all = p.reshape(depth, m, -1)
    for i in range(depth):
        j = i // 2
        if i % 2 == 0:
            w_a = w_all[i, :, :4 * hw].T
            ident = jnp.tile(jnp.array([0.0, 1.0, 0.0], F32)[:, None], (1, hw))
            conv_w = jnp.concatenate([hyena_conv_w[j], ident], axis=1).T
            conv_b = jnp.concatenate([hyena_conv_b[j], jnp.zeros((hw,), F32)])[:, None]
            hy = _proj_t(w_a, u.reshape(b, seq, d), conv_w, conv_b)
            halves = _hyena_filters_t(seq, hw, hyena_w1[j], hyena_b1[j], hyena_w2[j], hyena_b2[j],
                                      hyena_w3[j], hyena_b3[j], hyena_w4[j], hyena_freq[j])
            kr, ki = _filter_spectrum(halves, consts)
            skip = hyena_skip[j][:, :, None, None]
            z = _hyena_conv(False, hy, 0, hy, hw, hw, skip[0], kr, ki, 0, consts)
            ya_t = _hyena_conv(True, z, 0, hy, 2 * hw, 3 * hw, skip[1], kr, ki, 1, consts)
            proj_b = _matmul_bf16(u, w_all[i:i + 1, :, 4 * hw:], 0, 0, 4 * hw).reshape(-1, b, seq, LANES)
            yb = _dilated_attention(proj_b, dil_bias, hw // HEAD_DIM)
            ys, ys_t = [yb.reshape(-1, m, LANES)], [ya_t]
            ws = [w_out[i, hw:].astype(BF16), w_out[i, :hw].astype(BF16)]
        else:
            proj = _matmul_bf16(u, w_all, i, 0, 4 * d_inner).reshape(-1, b, seq, LANES)
            yc = _neighbourhood_attention(proj, _na_bias(na_rpb[j], seq // GRID_W), d_inner // HEAD_DIM)
            ys, ys_t = [yc.reshape(-1, m, LANES)], []
            ws = [w_out[i].astype(BF16)]
        g_next = norm_pre[i + 1] if i + 1 < depth else norm_pre[i]
        h, u = _out_block(ys, ys_t, ws, h, p_all, i, ple_proj[i].astype(BF16), ple_gate[i].astype(BF16),
                          norm_post[i], ple_norm[i], g_next)
    return h.reshape(b, seq, d)
```

```python
import functools
import math

import numpy as np
import jax
import jax.numpy as jnp
from jax import lax
from jax.experimental import pallas as pl
from jax.experimental.pallas import tpu as pltpu

F32 = jnp.float32
BF16 = jnp.bfloat16

HEAD_DIM = 64
LANES = 128
HYENA_EMB_DIM = 33
HYENA_FAST_DECAY = 0.3
HYENA_SLOW_DECAY = 1.5
HYENA_TARGET = 1e-2
DILATED_PATTERNS = ((128, 1), (512, 4), (2048, 16))
GRID_W = 64
NA_ROWS = 8
NA_COLS = 16
REL_BUCKETS = 32
REL_MAX_DIST = 1024
RMS_EPS = 1e-6
NEG_INF = -1e30
ATT_QB = 128
ATT_HALF = 64
ATT_KW = ATT_QB + 2 * ATT_HALF
ATT_UNROLL = 8
ATT_STEP = 4
assert all(b[1] == a[1] * ATT_STEP for a, b in zip(DILATED_PATTERNS, DILATED_PATTERNS[1:])) and DILATED_PATTERNS[0][1] == 1
NA_BATCH = 8
NA_UNROLL = 32
VMEM_LIMIT = 56 * 1024 * 1024


def _cparams(sem):
    return pltpu.CompilerParams(dimension_semantics=sem, vmem_limit_bytes=VMEM_LIMIT)


def _rms(x, g):
    return x * lax.rsqrt(jnp.mean(x * x, axis=-1, keepdims=True) + RMS_EPS) * g


def _rmsnorm_kernel(x_ref, g_ref, o_ref):
    o_ref[...] = _rms(x_ref[...], g_ref[...]).astype(o_ref.dtype)


def _rmsnorm_bf16(x2d, g, tm=1024):
    m, d = x2d.shape
    return pl.pallas_call(
        _rmsnorm_kernel,
        grid=(m // tm,),
        in_specs=[pl.BlockSpec((tm, d), lambda i: (i, 0)), pl.BlockSpec((1, d), lambda i: (0, 0))],
        out_specs=pl.BlockSpec((tm, d), lambda i: (i, 0)),
        out_shape=jax.ShapeDtypeStruct((m, d), BF16),
        compiler_params=_cparams(("parallel",)),
        name="rmsnorm",
    )(x2d, g.reshape(1, d))


def _matmul_kernel(a_ref, b_ref, o_ref):
    res = jnp.dot(a_ref[...], b_ref[...], preferred_element_type=F32).astype(o_ref.dtype)
    for c in range(o_ref.shape[0]):
        o_ref[c] = res[:, c * LANES:(c + 1) * LANES]


def _matmul_bf16(a, w, layer, col0, n, tm=2048, tn=1024):
    m, k = a.shape
    return pl.pallas_call(
        _matmul_kernel,
        grid=(m // tm, n // tn),
        in_specs=[pl.BlockSpec((tm, k), lambda i, j: (i, 0)), pl.BlockSpec((None, k, tn), lambda i, j: (layer, 0, col0 // tn + j))],
        out_specs=pl.BlockSpec((tn // LANES, tm, LANES), lambda i, j: (j, i, 0)),
        out_shape=jax.ShapeDtypeStruct((n // LANES, m, LANES), BF16),
        compiler_params=_cparams(("parallel", "parallel")),
        name="in_proj",
    )(a, w)


def _proj_t_kernel(w_ref, u_ref, cw_ref, cb_ref, o_ref):
    p = lax.dot_general(w_ref[...], u_ref[0], (((1,), (1,)), ((), ())), preferred_element_type=F32)
    seq = p.shape[1]
    t = lax.broadcasted_iota(jnp.int32, p.shape, 1)
    prev = jnp.where(t == 0, 0.0, pltpu.roll(p, 1, axis=1))
    nxt = jnp.where(t == seq - 1, 0.0, pltpu.roll(p, seq - 1, axis=1))
    cw = cw_ref[...]
    o_ref[0] = (prev * cw[:, 0:1] + p * cw[:, 1:2] + nxt * cw[:, 2:3] + cb_ref[...]).astype(o_ref.dtype)


def _proj_t(w_t, u, conv_w, conv_b, tc=256):
    c, d = w_t.shape
    b, seq, _ = u.shape
    return pl.pallas_call(
        _proj_t_kernel,
        grid=(b, c // tc),
        in_specs=[
            pl.BlockSpec((tc, d), lambda i, j: (j, 0)),
            pl.BlockSpec((1, seq, d), lambda i, j: (i, 0, 0)),
            pl.BlockSpec((tc, 3), lambda i, j: (j, 0)),
            pl.BlockSpec((tc, 1), lambda i, j: (j, 0)),
        ],
        out_specs=pl.BlockSpec((1, tc, seq), lambda i, j: (i, j, 0)),
        out_shape=jax.ShapeDtypeStruct((b, c, seq), BF16),
        compiler_params=_cparams(("parallel", "parallel")),
        name="in_proj_t_conv3",
    )(w_t, u, conv_w, conv_b)


def _hyena_filter_kernel(z_ref, w1_ref, b1_ref, w2_ref, b2_ref, w3_ref, b3_ref, fq_ref, w4_ref, dl_ref, t_ref,
                         o_ref, a_scr):
    hp = lax.Precision.HIGHEST

    @pl.when((pl.program_id(0) == 0) & (pl.program_id(1) == 0))
    def _():
        for d in range(2):
            a = z_ref[d]
            for w_ref, b_ref in ((w1_ref, b1_ref), (w2_ref, b2_ref), (w3_ref, b3_ref)):
                a = jnp.sin(fq_ref[...] * (jnp.dot(w_ref[...], a, precision=hp, preferred_element_type=F32) + b_ref[...]))
            a_scr[d] = a

    seq = o_ref.shape[-1]
    proj = lambda d: jnp.dot(w4_ref[0, d].astype(BF16), a_scr[d].astype(BF16), preferred_element_type=F32)
    lo = proj(0) * jnp.exp(-t_ref[0] * dl_ref[...])
    hi = proj(1) * jnp.exp(-t_ref[1] * dl_ref[...])
    hi = jnp.where(lax.broadcasted_iota(jnp.int32, (1, seq), 1) == 0, 0.0, hi)
    norm = jnp.sum(jnp.abs(lo), axis=1, keepdims=True) + jnp.sum(jnp.abs(hi), axis=1, keepdims=True)
    o_ref[0, 0] = (lo / norm).astype(o_ref.dtype)
    o_ref[1, 0] = (hi / norm).astype(o_ref.dtype)


def _hyena_filters_t(seq, c, w1, b1, w2, b2, w3, b3, w4, freq, tc=256):
    tc = min(tc, c)
    fo = w2.shape[0]
    emb = -(-HYENA_EMB_DIM // 8) * 8
    t = jnp.linspace(0.0, 1.0, seq, dtype=F32)
    bands = (HYENA_EMB_DIM - 1) // 2
    fr = jnp.linspace(1e-4, bands - 1, bands, dtype=F32)[:, None]
    mirror = (seq - jnp.arange(seq)) % seq
    tables, times = [], []
    for pos, tp in ((jnp.arange(seq), t), (mirror, jnp.roll(t[::-1], 1))):
        wpos = (2.0 * math.pi / seq) * pos.astype(F32)[None, :]
        tp = tp[None, :]
        z = jnp.concatenate([tp, jnp.cos(fr * wpos), -jnp.sin(fr * wpos),
                             jnp.zeros((emb - HYENA_EMB_DIM, seq), F32)], axis=0)
        tables.append(z)
        times.append(tp)
    z = jnp.stack(tables)
    tt = jnp.stack(times)
    w1_t = jnp.pad(w1.astype(F32).T, ((0, 0), (0, emb - HYENA_EMB_DIM)))
    w4_t = jnp.transpose(w4.astype(F32).reshape(fo, 2, 2, c), (1, 2, 3, 0))
    min_decay = math.log(HYENA_TARGET) / HYENA_SLOW_DECAY
    max_decay = math.log(HYENA_TARGET) / HYENA_FAST_DECAY
    deltas = jnp.abs(jnp.linspace(min_decay, max_decay, c, dtype=F32))[:, None]
    col = lambda v: v.astype(F32)[:, None]
    full = lambda a: pl.BlockSpec(a.shape, lambda f, i: (0,) * a.ndim)
    args = (z, w1_t, col(b1), w2.astype(F32).T, col(b2), w3.astype(F32).T, col(b3), col(freq))
    return pl.pallas_call(
        _hyena_filter_kernel,
        grid=(2, c // tc),
        in_specs=[full(a) for a in args]
                 + [pl.BlockSpec((1, 2, tc, fo), lambda f, i: (f, 0, i, 0)),
                    pl.BlockSpec((tc, 1), lambda f, i: (i, 0)), full(tt)],
        out_specs=pl.BlockSpec((2, 1, tc, seq), lambda f, i: (0, f, i, 0)),
        out_shape=jax.ShapeDtypeStruct((2, 2, c, seq), BF16),
        scratch_shapes=[pltpu.VMEM((2, fo, seq), F32)],
        compiler_params=_cparams(("arbitrary", "arbitrary")),
        name="hyena_filter_mlp",
    )(*args, w4_t, deltas, tt)


def _fft_consts(seq):
    n_fft = 2 * seq
    n1 = n_fft // LANES
    nb = n1 // 2 + 1
    kp = -(-nb // 16) * 16
    k1 = jnp.arange(kp, dtype=jnp.int32)
    keep = (k1 < nb)[:, None]
    two_pi = 2.0 * math.pi

    th = ((k1[:, None] * jnp.arange(n1, dtype=jnp.int32)[None, :]) % n1).astype(F32) * (two_pi / n1)
    f1 = jnp.concatenate([jnp.where(keep, jnp.cos(th), 0.0), jnp.where(keep, -jnp.sin(th), 0.0)], axis=0)
    ph = (k1[:, None] * jnp.arange(LANES, dtype=jnp.int32)[None, :]).astype(F32) * (two_pi / n_fft)
    tw = jnp.stack([jnp.cos(ph), -jnp.sin(ph)])
    om = ((jnp.arange(LANES, dtype=jnp.int32)[:, None] * jnp.arange(LANES, dtype=jnp.int32)[None, :]) % LANES
          ).astype(F32) * (two_pi / LANES)
    cplx = lambda c, s: jnp.concatenate([jnp.concatenate([c, s], axis=1), jnp.concatenate([-s, c], axis=1)], axis=0)
    f2 = cplx(jnp.cos(om), -jnp.sin(om))
    f2i = cplx(jnp.cos(om), jnp.sin(om))
    weight = jnp.where((k1 == 0) | (k1 == nb - 1), 1.0, 2.0)[:, None]
    g = jnp.concatenate([jnp.where(keep, weight * jnp.cos(th[:, :n1 // 2]), 0.0),
                         jnp.where(keep, -weight * jnp.sin(th[:, :n1 // 2]), 0.0)], axis=0).T
    return dict(f1=f1.astype(BF16), tw=tw.astype(BF16), f2=f2.astype(BF16), f2i=f2i.astype(BF16), g=g.astype(BF16), kp=kp)


def _per_channel_matmul(mat_ref, planes, out_scr, ct):
    def body(i, carry):
        c = 2 * i
        pair = jnp.concatenate([planes(c), planes(c + 1)], axis=1)
        res = jnp.dot(mat_ref[...], pair, preferred_element_type=F32)
        out_scr[c] = res[:, :LANES].astype(out_scr.dtype)
        out_scr[c + 1] = res[:, LANES:].astype(out_scr.dtype)
        return carry
    lax.fori_loop(0, ct // 2, body, 0, unroll=16)


def _twiddle_stage2(a, tw_ref, f2_ref):
    ct, kp2, _ = a.shape
    kp = kp2 // 2
    ar, ai = a[:, :kp], a[:, kp:]
    twr, twi = tw_ref[0], tw_ref[1]
    st = jnp.concatenate([ar * twr - ai * twi, ar * twi + ai * twr], axis=2)
    p = jnp.dot(st.reshape(ct * kp, 2 * LANES), f2_ref[...], preferred_element_type=F32).reshape(ct, kp, 2 * LANES)
    return p[:, :, :LANES].astype(BF16), p[:, :, LANES:].astype(BF16)


def _stage2_twiddle_inv(yr, yi, tw_ref, f2i_ref):
    ct, kp, _ = yr.shape
    st = jnp.concatenate([yr, yi], axis=2)
    p = jnp.dot(st.reshape(ct * kp, 2 * LANES), f2i_ref[...], preferred_element_type=F32).reshape(ct, kp, 2 * LANES)
    tr, ti = p[:, :, :LANES].astype(BF16), p[:, :, LANES:].astype(BF16)
    twr, twi = tw_ref[0], tw_ref[1]
    return jnp.concatenate([tr * twr + ti * twi, ti * twr - tr * twi], axis=1)


def _filter_spec_kernel(lo_ref, hi_ref, f1_ref, tw_ref, f2_ref, kr_ref, ki_ref, a_scr):
    ct = lo_ref.shape[2]
    planes = lambda c: jnp.concatenate([lo_ref[0, 0, c], hi_ref[0, 0, c]], axis=0)
    _per_channel_matmul(f1_ref, planes, a_scr, ct)
    xr, xi = _twiddle_stage2(a_scr[...], tw_ref, f2_ref)
    n_fft = f1_ref.shape[1] * LANES
    scale = 1.0 / n_fft
    kr_ref[0] = xr * scale
    ki_ref[0] = xi * scale


def _filter_spectrum(halves, consts, ct=128):
    _, nf, c, seq = halves.shape
    ct = min(ct, c)
    kp = consts["kp"]
    h5 = halves.reshape(2, nf, c, seq // LANES, LANES)
    half = lambda s: pl.BlockSpec((1, 1, ct, seq // LANES, LANES), lambda f, i: (s, f, i, 0, 0))
    full = lambda a: pl.BlockSpec(a.shape, lambda f, i: (0,) * a.ndim)
    spec = jax.ShapeDtypeStruct((nf, c, kp, LANES), BF16)
    out = pl.BlockSpec((1, ct, kp, LANES), lambda f, i: (f, i, 0, 0))
    return pl.pallas_call(
        _filter_spec_kernel,
        grid=(nf, c // ct),
        in_specs=[half(0), half(1), full(consts["f1"]), full(consts["tw"]), full(consts["f2"])],
        out_specs=[out, out],
        out_shape=[spec, spec],
        scratch_shapes=[pltpu.VMEM((ct, 2 * kp, LANES), BF16)],
        compiler_params=_cparams(("parallel", "parallel")),
        name="hyena_filter_spectrum",
    )(h5, h5, consts["f1"], consts["tw"], consts["f2"])


def _hyena_conv_kernel(second, x_ref, m_ref, e_ref, skip_ref, kr_ref, ki_ref, f1_ref, tw_ref, f2_ref, f2i_ref, g_ref,
                       o_ref, a_scr, t_scr, y_scr):
    ct = x_ref.shape[1]
    _per_channel_matmul(f1_ref, lambda c: x_ref[0, c], a_scr, ct)
    xr, xi = _twiddle_stage2(a_scr[...], tw_ref, f2_ref)
    kr, ki = kr_ref[0], ki_ref[0]
    t_scr[...] = _stage2_twiddle_inv(xr * kr - xi * ki, xr * ki + xi * kr, tw_ref, f2i_ref)

    _per_channel_matmul(g_ref, lambda c: t_scr[c], y_scr, ct)

    out = m_ref[0].astype(F32) * (y_scr[...] + x_ref[0].astype(F32) * skip_ref[...])
    if second:
        e = e_ref[0].astype(F32)
        out = out * (e * jax.nn.sigmoid(e))
    o_ref[0] = out.astype(o_ref.dtype)


def _hyena_conv(second, x, x_row0, hy, m_row0, e_row0, skip, kr, ki, filt, consts, ct=128):
    b, _, seq = x.shape
    c = kr.shape[1]
    ct = min(ct, c)
    kp = consts["kp"]
    n1h = seq // LANES
    view = lambda a: a.reshape(a.shape[0], a.shape[1], n1h, LANES)
    rows = lambda blk0: pl.BlockSpec((1, ct, n1h, LANES), lambda j, i: (i, blk0 + j, 0, 0))
    full = lambda a: pl.BlockSpec(a.shape, lambda j, i: (0,) * a.ndim)
    kspec = pl.BlockSpec((1, ct, kp, LANES), lambda j, i: (filt, j, 0, 0))
    mats = [consts["f1"][:, :n1h], consts["tw"], consts["f2"], consts["f2i"], consts["g"]]
    out = pl.pallas_call(
        functools.partial(_hyena_conv_kernel, second),
        grid=(c // ct, b),
        in_specs=[rows(x_row0 // ct), rows(m_row0 // ct), rows(e_row0 // ct),
                  pl.BlockSpec((ct, 1, 1), lambda j, i: (j, 0, 0)),
                  kspec, kspec] + [full(a) for a in mats],
        out_specs=pl.BlockSpec((1, ct, n1h, LANES), lambda j, i: (i, j, 0, 0)),
        out_shape=jax.ShapeDtypeStruct((b, c, n1h, LANES), BF16),
        scratch_shapes=[pltpu.VMEM((ct, 2 * kp, LANES), BF16), pltpu.VMEM((ct, 2 * kp, LANES), BF16),
                        pltpu.VMEM((ct, n1h, LANES), F32)],
        compiler_params=_cparams(("parallel", "parallel")),
        name="hyena_conv2" if second else "hyena_conv1",
    )(view(x), view(hy), view(hy), skip, kr, ki, *mats)
    return out.reshape(b, c, seq)


def _t5_bucket(rel):
    half_b = REL_BUCKETS // 2
    max_exact = half_b // 2
    ret = jnp.where(rel > 0, half_b, 0)
    n = jnp.abs(rel)
    nf = jnp.maximum(n, 1).astype(F32)
    large = max_exact + (jnp.log(nf / max_exact) / math.log(REL_MAX_DIST / max_exact)
                         * (half_b - max_exact)).astype(jnp.int32)
    large = jnp.minimum(large, half_b - 1)
    return ret + jnp.where(n < max_exact, n, large)


def _dilated_bias(rel_bias):
    rel = jnp.arange(ATT_KW)[None, :] - ATT_HALF - jnp.arange(ATT_QB)[:, None]
    col = jnp.arange(ATT_KW)
    edge = jnp.stack([col >= ATT_HALF, col >= 0, col < ATT_KW - ATT_HALF])
    out = []
    for _, dil in DILATED_PATTERNS:
        onehot = jax.nn.one_hot(_t5_bucket(rel * dil), REL_BUCKETS, dtype=F32)
        bias = jnp.einsum("qkb,bh->qkh", onehot, rel_bias.astype(F32), precision=lax.Precision.HIGHEST)
        bias = jnp.where((jnp.abs(rel) <= ATT_HALF)[:, :, None], bias, NEG_INF)
        bias = jnp.transpose(bias, (2, 0, 1))[:, None]
        out.append(jnp.where(edge[None, :, None, :], bias, NEG_INF))
    return jnp.stack(out)


def _head_select(first, second):
    lane = lax.broadcasted_iota(jnp.int32, first.shape[:-1] + (LANES,), len(first.shape) - 1)
    return jnp.where(lane < HEAD_DIM, first, second)


def _dilated_kernel(q_ref, k_ref, v_ref, g_ref, bias_ref, o_ref, *scratch):
    stage_a, stage_b = scratch[0:3], scratch[3:6]
    qd, kd, vd = scratch[6:9]
    last = len(DILATED_PATTERNS) - 1
    states = [scratch[9 + 3 * p:12 + 3 * p] for p in range(last)]
    seq = q_ref.shape[1]
    stage_a[0][...] = q_ref[0].astype(F32) * (HEAD_DIM ** -0.5)
    stage_a[1][...] = k_ref[0].astype(F32)
    stage_a[2][...] = v_ref[0].astype(F32)
    zero_halo = jnp.zeros((ATT_HALF, LANES), BF16)
    first_head = lax.broadcasted_iota(jnp.int32, (ATT_QB, LANES), 1) < HEAD_DIM

    src, dst = stage_a, stage_b
    for pi, (_, dil) in enumerate(DILATED_PATTERNS):
        n = seq // dil
        nblk = n // ATT_QB
        if pi > 0:
            prev_dil = dil // ATT_STEP
            for r1 in range(prev_dil):
                for r2 in range(ATT_STEP):
                    r = r1 + prev_dil * r2
                    for a_src, a_dst in zip(src, dst):
                        a_dst[r * n:(r + 1) * n, :] = a_src[pl.ds(r1 * n * ATT_STEP + r2, n, stride=ATT_STEP), :]
            src, dst = dst, src
        stride_k = n + 2 * ATT_HALF
        qd[...] = src[0][...].astype(BF16)
        for r in range(dil):
            base = r * stride_k
            for a_src, a_dst in ((src[1], kd), (src[2], vd)):
                a_dst[base:base + ATT_HALF, :] = zero_halo
                a_dst[base + ATT_HALF:base + ATT_HALF + n, :] = a_src[r * n:(r + 1) * n, :].astype(BF16)
                a_dst[base + ATT_HALF + n:base + stride_k, :] = zero_halo
        if pi == last:
            states.append(dst)
        acc_w, mx_w, den_w = states[pi]

        def block_group(it, carry, pi=pi, nblk=nblk, acc_w=acc_w, mx_w=mx_w, den_w=den_w):
            chains = []
            for u in range(ATT_UNROLL):
                blk = it * ATT_UNROLL + u
                r = blk // nblk
                qb = blk % nblk
                q0 = pl.multiple_of(blk * ATT_QB, ATT_QB)
                k0 = pl.multiple_of(q0 + r * (2 * ATT_HALF), ATT_QB)
                edge = jnp.where(qb == 0, 0, jnp.where(qb == nblk - 1, 2, 1))
                qblk = qd[pl.ds(q0, ATT_QB), :]
                qcat = jnp.concatenate([jnp.where(first_head, qblk, jnp.zeros_like(qblk)),
                                        jnp.where(first_head, jnp.zeros_like(qblk), qblk)], axis=0)
                s = lax.dot_general(qcat, kd[pl.ds(k0, ATT_KW), :], (((1,), (1,)), ((), ())),
                                    preferred_element_type=F32)
                chains.append((q0, k0, edge, s))
            probs = []
            for q0, k0, edge, s in chains:
                s = s + jnp.concatenate([bias_ref[pi, 0, edge], bias_ref[pi, 1, edge]], axis=0)
                m = jnp.max(s, axis=1, keepdims=True)
                p = jnp.exp(s - m)
                probs.append((p.astype(BF16), m, jnp.sum(p, axis=1, keepdims=True)))
            for (q0, k0, edge, s), (p, m, l) in zip(chains, probs):
                pv = jnp.dot(p, vd[pl.ds(k0, ATT_KW), :], preferred_element_type=F32)
                acc_w[pl.ds(q0, ATT_QB), :] = _head_select(pv[:ATT_QB], pv[ATT_QB:])
                mx_w[pl.ds(q0, ATT_QB), :] = _head_select(m[:ATT_QB], m[ATT_QB:])
                den_w[pl.ds(q0, ATT_QB), :] = _head_select(l[:ATT_QB], l[ATT_QB:])
            return carry

        lax.fori_loop(0, dil * nblk // ATT_UNROLL, block_group, 0)

    for pi in range(len(DILATED_PATTERNS) - 1, 0, -1):
        dil = DILATED_PATTERNS[pi][1]
        n = seq // dil
        prev_dil = dil // ATT_STEP
        acc_c, mx_c, den_c = states[pi - 1]
        acc_f, mx_f, den_f = states[pi]
        for r1 in range(prev_dil):
            for r2 in range(ATT_STEP):
                r = r1 + prev_dil * r2
                rows = pl.ds(r1 * n * ATT_STEP + r2, n, stride=ATT_STEP)
                fine = slice(r * n, (r + 1) * n)
                m_old = mx_c[rows, :]
                m_new = mx_f[fine, :]
                m_all = jnp.maximum(m_old, m_new)
                decay = jnp.exp(-jnp.abs(m_old - m_new))
                a_old = jnp.where(m_old >= m_new, 1.0, decay)
                a_new = jnp.where(m_old >= m_new, decay, 1.0)
                acc_c[rows, :] = acc_c[rows, :] * a_old + acc_f[fine, :] * a_new
                den_c[rows, :] = den_c[rows, :] * a_old + den_f[fine, :] * a_new
                mx_c[rows, :] = m_all

    acc, _, den = states[0]
    g = g_ref[0].astype(F32)
    o_ref[0] = (acc[...] / den[...] * (g * jax.nn.sigmoid(g))).astype(o_ref.dtype)


def _dilated_attention(proj, bias, n_heads):
    _, b, seq, _ = proj.shape
    npair = n_heads // 2
    max_dil = max(d for _, d in DILATED_PATTERNS)
    kd_rows = seq + 2 * ATT_HALF * max_dil
    blk = lambda off: pl.BlockSpec((None, 1, seq, LANES), lambda i, j: (off * npair + j, i, 0, 0))
    state = pltpu.VMEM((seq, LANES), F32)
    return pl.pallas_call(
        _dilated_kernel,
        grid=(b, npair),
        in_specs=[blk(0), blk(1), blk(2), blk(3),
                  pl.BlockSpec((len(DILATED_PATTERNS), 2, 3, ATT_QB, ATT_KW), lambda i, j: (0, j, 0, 0, 0))],
        out_specs=pl.BlockSpec((None, 1, seq, LANES), lambda i, j: (j, i, 0, 0)),
        out_shape=jax.ShapeDtypeStruct((npair, b, seq, LANES), BF16),
        scratch_shapes=[state] * 6
                       + [pltpu.VMEM((seq, LANES), BF16), pltpu.VMEM((kd_rows, LANES), BF16),
                          pltpu.VMEM((kd_rows, LANES), BF16)]
                       + [state] * (3 * (len(DILATED_PATTERNS) - 1)),
        compiler_params=_cparams(("parallel", "parallel")),
        name="dilated_attention",
    )(proj, proj, proj, proj, bias)


def _na_bias(rpb, rows):
    assert min(NA_ROWS, rows) % 2 == 0
    cols = np.arange(GRID_W)
    cs = np.clip(cols - NA_COLS // 2, 0, GRID_W - NA_COLS)
    kc = np.arange(GRID_W)
    inside = (kc[None, :] >= cs[:, None]) & (kc[None, :] < cs[:, None] + NA_COLS)
    col_idx = kc[None, :] - cols[:, None] + NA_COLS - 1
    onehot = (col_idx[:, :, None] == np.arange(2 * NA_COLS - 1)[None, None, :]).astype(np.float32)
    e = jnp.einsum("hrd,cqd->hrcq", rpb.astype(F32), jnp.asarray(onehot), precision=lax.Precision.HIGHEST)
    e = jnp.where(jnp.asarray(inside)[None, None], e, NEG_INF)
    return jnp.concatenate([e[:, :-1], e[:, 1:]], axis=-1)


def _na_kernel(q_ref, k_ref, v_ref, g_ref, bias_ref, o_ref, qm):
    seq = q_ref.shape[1]
    rows = seq // GRID_W
    kr_n = min(NA_ROWS, rows)
    lane = lax.broadcasted_iota(jnp.int32, (seq, LANES), 1)
    q = q_ref[0] * (HEAD_DIM ** -0.5)
    for h in range(2):
        qm[h] = jnp.where((lane < HEAD_DIM) == (h == 0), q, jnp.zeros_like(q))

    def row_group(it, carry):
        def scores(batch):
            out = []
            for u in range(batch * NA_BATCH, (batch + 1) * NA_BATCH):
                r = it * NA_UNROLL + u
                rs = jnp.clip(r - kr_n // 2, 0, rows - kr_n)
                shift = rs - r + NA_ROWS - 1
                q0 = pl.multiple_of(r * GRID_W, GRID_W)
                k0 = pl.multiple_of(rs * GRID_W, GRID_W)
                qcat = jnp.concatenate([qm[0, pl.ds(q0, GRID_W), :], qm[1, pl.ds(q0, GRID_W), :]], axis=0)
                s = lax.dot_general(qcat, k_ref[0, pl.ds(k0, kr_n * GRID_W), :], (((1,), (1,)), ((), ())),
                                    preferred_element_type=F32)
                out.append((q0, k0, shift, s))
            return out

        def finish(chains):
            probs = []
            for q0, k0, shift, s in chains:
                bias = [jnp.concatenate([bias_ref[h, shift + i] for i in range(0, kr_n, 2)], axis=1)
                        for h in range(2)]
                s = s + jnp.concatenate(bias, axis=0)
                p = jnp.exp(s - jnp.max(s, axis=1, keepdims=True))
                probs.append((p.astype(BF16), jnp.sum(p, axis=1, keepdims=True)))
            for (q0, k0, shift, s), (p, l) in zip(chains, probs):
                o = jnp.dot(p, v_ref[0, pl.ds(k0, kr_n * GRID_W), :], preferred_element_type=F32) / l
                g = g_ref[0, pl.ds(q0, GRID_W), :].astype(F32)
                o = _head_select(o[:GRID_W], o[GRID_W:]) * (g * jax.nn.sigmoid(g))
                o_ref[0, pl.ds(q0, GRID_W), :] = o.astype(o_ref.dtype)

        n_batches = NA_UNROLL // NA_BATCH
        pending = scores(0)
        for batch in range(n_batches):
            ahead = scores(batch + 1) if batch + 1 < n_batches else None
            finish(pending)
            pending = ahead
        return carry

    lax.fori_loop(0, rows // NA_UNROLL, row_group, 0)


def _neighbourhood_attention(proj, bias, n_heads):
    _, b, seq, _ = proj.shape
    npair = n_heads // 2
    blk = lambda off: pl.BlockSpec((None, 1, seq, LANES), lambda j, i: (off * npair + j, i, 0, 0))
    return pl.pallas_call(
        _na_kernel,
        grid=(npair, b),
        in_specs=[blk(0), blk(1), blk(2), blk(3),
                  pl.BlockSpec((2,) + bias.shape[1:], lambda j, i: (j, 0, 0, 0))],
        out_specs=pl.BlockSpec((None, 1, seq, LANES), lambda j, i: (j, i, 0, 0)),
        out_shape=jax.ShapeDtypeStruct((npair, b, seq, LANES), BF16),
        scratch_shapes=[pltpu.VMEM((2, seq, LANES), BF16)],
        compiler_params=_cparams(("parallel", "parallel")),
        name="neighbourhood_attention",
    )(proj, proj, proj, proj, bias)


def _out_kernel(n_y, n_t, *refs):
    n = n_y + n_t
    y_refs = refs[:n]
    w_refs = refs[n:2 * n]
    h_ref, p_ref, wp_ref, wg_ref, gpost_ref, gple_ref, gnext_ref, h_out, u_out = refs[2 * n:]
    t = None
    for idx, (y_ref, w_ref) in enumerate(zip(y_refs, w_refs)):
        if idx < n_y:
            y = jnp.concatenate([y_ref[c] for c in range(y_ref.shape[0])], axis=1)
            part = jnp.dot(y, w_ref[...], preferred_element_type=F32)
        else:
            part = lax.dot_general(y_ref[0], w_ref[...], (((0,), (0,)), ((), ())), preferred_element_type=F32)
        t = part if t is None else t + part
    h = h_ref[...] + _rms(t, gpost_ref[...])
    e = _rms(jnp.dot(p_ref[...].astype(BF16), wp_ref[...], preferred_element_type=F32), gple_ref[...])
    gate = jax.nn.sigmoid(jnp.dot(h.astype(BF16), wg_ref[...], preferred_element_type=F32))
    h = h + gate * e
    h_out[...] = h
    u_out[...] = _rms(h, gnext_ref[...]).astype(u_out.dtype)


def _out_block(ys, ys_t, ws, h, p, layer, w_ple, w_gate, g_post, g_ple, g_next, tm=1024):
    m, d = h.shape
    row = lambda a: pl.BlockSpec((tm, a.shape[1]), lambda i: (i, 0))
    slab = lambda a: pl.BlockSpec((a.shape[0], tm, LANES), lambda i: (0, i, 0))
    p_spec = pl.BlockSpec((None, tm, p.shape[2]), lambda i: (layer, i, 0))
    full = lambda a: pl.BlockSpec(a.shape, lambda i: (0, 0))
    vec = lambda g: g.reshape(1, d)

    def col(a):
        tiles = a.shape[2] // tm
        return pl.BlockSpec((1, a.shape[1], tm), lambda i: (i // tiles, 0, i % tiles))

    return pl.pallas_call(
        functools.partial(_out_kernel, len(ys), len(ys_t)),
        grid=(m // tm,),
        in_specs=[slab(y) for y in ys] + [col(y) for y in ys_t] + [full(w) for w in ws]
                 + [row(h), p_spec, full(w_ple), full(w_gate), full(vec(g_post)), full(vec(g_ple)), full(vec(g_next))],
        out_specs=[pl.BlockSpec((tm, d), lambda i: (i, 0)), pl.BlockSpec((tm, d), lambda i: (i, 0))],
        out_shape=[jax.ShapeDtypeStruct((m, d), F32), jax.ShapeDtypeStruct((m, d), BF16)],
        compiler_params=_cparams(("parallel",)),
        name="out_proj_residual_ple",
    )(*ys, *ys_t, *ws, h, p, w_ple, w_gate, vec(g_post), vec(g_ple), vec(g_next))


def kernel(x, p, w_in, w_out, norm_pre, norm_post, hyena_conv_w, hyena_conv_b, hyena_w1, hyena_b1, hyena_w2, hyena_b2, hyena_w3, hyena_b3, hyena_w4, hyena_freq, hyena_skip, rel_bias, na_rpb, ple_proj, ple_norm, ple_gate):
    b, seq, d = x.shape
    depth = w_in.shape[0]
    d_inner = w_out.shape[1]
    hw = d_inner // 2
    m = b * seq
    consts = _fft_consts(seq)
    dil_bias = _dilated_bias(rel_bias)
    h = x.reshape(m, d)
    u = _rmsnorm_bf16(h, norm_pre[0])
    w_all = w_in.astype(BF16)
    p_all = p.reshape(depth, m, -1)
    for i in range(depth):
        j = i // 2
        if i % 2 == 0:
            w_a = w_all[i, :, :4 * hw].T
            ident = jnp.tile(jnp.array([0.0, 1.0, 0.0], F32)[:, None], (1, hw))
            conv_w = jnp.concatenate([hyena_conv_w[j], ident], axis=1).T
            conv_b = jnp.concatenate([hyena_conv_b[j], jnp.zeros((hw,), F32)])[:, None]
            hy = _proj_t(w_a, u.reshape(b, seq, d), conv_w, conv_b)
            halves = _hyena_filters_t(seq, hw, hyena_w1[j], hyena_b1[j], hyena_w2[j], hyena_b2[j],
                                      hyena_w3[j], hyena_b3[j], hyena_w4[j], hyena_freq[j])
            kr, ki = _filter_spectrum(halves, consts)
            skip = hyena_skip[j][:, :, None, None]
            z = _hyena_conv(False, hy, 0, hy, hw, hw, skip[0], kr, ki, 0, consts)
            ya_t = _hyena_conv(True, z, 0, hy, 2 * hw, 3 * hw, skip[1], kr, ki, 1, consts)
            proj_b = _matmul_bf16(u, w_all[i:i + 1, :, 4 * hw:], 0, 0, 4 * hw).reshape(-1, b, seq, LANES)
            yb = _dilated_attention(proj_b, dil_bias, hw // HEAD_DIM)
            ys, ys_t = [yb.reshape(-1, m, LANES)], [ya_t]
            ws = [w_out[i, hw:].astype(BF16), w_out[i, :hw].astype(BF16)]
        else:
            proj = _matmul_bf16(u, w_all, i, 0, 4 * d_inner).reshape(-1, b, seq, LANES)
            yc = _neighbourhood_attention(proj, _na_bias(na_rpb[j], seq // GRID_W), d_inner // HEAD_DIM)
            ys, ys_t = [yc.reshape(-1, m, LANES)], []
            ws = [w_out[i].astype(BF16)]
        g_next = norm_pre[i + 1] if i + 1 < depth else norm_pre[i]
        h, u = _out_block(ys, ys_t, ws, h, p_all, i, ple_proj[i].astype(BF16), ple_gate[i].astype(BF16),
                          norm_post[i], ple_norm[i], g_next)
    return h.reshape(b, seq, d)
```

```python
import functools
import math

import numpy as np
import jax
import jax.numpy as jnp
from jax import lax
from jax.experimental import pallas as pl
from jax.experimental.pallas import tpu as pltpu

F32 = jnp.float32
BF16 = jnp.bfloat16

HEAD_DIM = 64
LANES = 128
HYENA_EMB_DIM = 33
HYENA_FAST_DECAY = 0.3
HYENA_SLOW_DECAY = 1.5
HYENA_TARGET = 1e-2
DILATED_PATTERNS = ((128, 1), (512, 4), (2048, 16))
GRID_W = 64
NA_ROWS = 8
NA_COLS = 16
REL_BUCKETS = 32
REL_MAX_DIST = 1024
RMS_EPS = 1e-6
NEG_INF = -1e30
ATT_QB = 128
ATT_HALF = 64
ATT_KW = ATT_QB + 2 * ATT_HALF
ATT_UNROLL = 8
ATT_STEP = 4
assert all(b[1] == a[1] * ATT_STEP for a, b in zip(DILATED_PATTERNS, DILATED_PATTERNS[1:])) and DILATED_PATTERNS[0][1] == 1
NA_BATCH = 8
NA_UNROLL = 32
VMEM_LIMIT = 56 * 1024 * 1024


def _cparams(sem):
    return pltpu.CompilerParams(dimension_semantics=sem, vmem_limit_bytes=VMEM_LIMIT)


def _rms(x, g):
    return x * lax.rsqrt(jnp.mean(x * x, axis=-1, keepdims=True) + RMS_EPS) * g


def _rmsnorm_kernel(x_ref, g_ref, o_ref):
    o_ref[...] = _rms(x_ref[...], g_ref[...]).astype(o_ref.dtype)


def _rmsnorm_bf16(x2d, g, tm=1024):
    m, d = x2d.shape
    return pl.pallas_call(
        _rmsnorm_kernel,
        grid=(m // tm,),
        in_specs=[pl.BlockSpec((tm, d), lambda i: (i, 0)), pl.BlockSpec((1, d), lambda i: (0, 0))],
        out_specs=pl.BlockSpec((tm, d), lambda i: (i, 0)),
        out_shape=jax.ShapeDtypeStruct((m, d), BF16),
        compiler_params=_cparams(("parallel",)),
        name="rmsnorm",
    )(x2d, g.reshape(1, d))


def _matmul_kernel(a_ref, b_ref, o_ref):
    res = jnp.dot(a_ref[...], b_ref[...], preferred_element_type=F32).astype(o_ref.dtype)
    for c in range(o_ref.shape[0]):
        o_ref[c] = res[:, c * LANES:(c + 1) * LANES]


def _matmul_bf16(a, w, layer, col0, n, tm=2048, tn=1024):
    m, k = a.shape
    return pl.pallas_call(
        _matmul_kernel,
        grid=(m // tm, n // tn),
        in_specs=[pl.BlockSpec((tm, k), lambda i, j: (i, 0)), pl.BlockSpec((None, k, tn), lambda i, j: (layer, 0, col0 // tn + j))],
        out_specs=pl.BlockSpec((tn // LANES, tm, LANES), lambda i, j: (j, i, 0)),
        out_shape=jax.ShapeDtypeStruct((n // LANES, m, LANES), BF16),
        compiler_params=_cparams(("parallel", "parallel")),
        name="in_proj",
    )(a, w)


def _proj_t_kernel(w_ref, u_ref, cw_ref, cb_ref, o_ref):
    p = lax.dot_general(w_ref[...], u_ref[0], (((1,), (1,)), ((), ())), preferred_element_type=F32)
    seq = p.shape[1]
    t = lax.broadcasted_iota(jnp.int32, p.shape, 1)
    prev = jnp.where(t == 0, 0.0, pltpu.roll(p, 1, axis=1))
    nxt = jnp.where(t == seq - 1, 0.0, pltpu.roll(p, seq - 1, axis=1))
    cw = cw_ref[...]
    o_ref[0] = (prev * cw[:, 0:1] + p * cw[:, 1:2] + nxt * cw[:, 2:3] + cb_ref[...]).astype(o_ref.dtype)


def _proj_t(w_t, u, conv_w, conv_b, tc=256):
    c, d = w_t.shape
    b, seq, _ = u.shape
    return pl.pallas_call(
        _proj_t_kernel,
        grid=(b, c // tc),
        in_specs=[
            pl.BlockSpec((tc, d), lambda i, j: (j, 0)),
            pl.BlockSpec((1, seq, d), lambda i, j: (i, 0, 0)),
            pl.BlockSpec((tc, 3), lambda i, j: (j, 0)),
            pl.BlockSpec((tc, 1), lambda i, j: (j, 0)),
        ],
        out_specs=pl.BlockSpec((1, tc, seq), lambda i, j: (i, j, 0)),
        out_shape=jax.ShapeDtypeStruct((b, c, seq), BF16),
        compiler_params=_cparams(("parallel", "parallel")),
        name="in_proj_t_conv3",
    )(w_t, u, conv_w, conv_b)


def _hyena_filter_kernel(z_ref, w1_ref, b1_ref, w2_ref, b2_ref, w3_ref, b3_ref, fq_ref, w4_ref, dl_ref, t_ref,
                         o_ref, a_scr):
    hp = lax.Precision.HIGHEST

    @pl.when((pl.program_id(0) == 0) & (pl.program_id(1) == 0))
    def _():
        for d in range(2):
            a = z_ref[d]
            for w_ref, b_ref in ((w1_ref, b1_ref), (w2_ref, b2_ref), (w3_ref, b3_ref)):
                a = jnp.sin(fq_ref[...] * (jnp.dot(w_ref[...], a, precision=hp, preferred_element_type=F32) + b_ref[...]))
            a_scr[d] = a

    seq = o_ref.shape[-1]
    proj = lambda d: jnp.dot(w4_ref[0, d].astype(BF16), a_scr[d].astype(BF16), preferred_element_type=F32)
    lo = proj(0) * jnp.exp(-t_ref[0] * dl_ref[...])
    hi = proj(1) * jnp.exp(-t_ref[1] * dl_ref[...])
    hi = jnp.where(lax.broadcasted_iota(jnp.int32, (1, seq), 1) == 0, 0.0, hi)
    norm = jnp.sum(jnp.abs(lo), axis=1, keepdims=True) + jnp.sum(jnp.abs(hi), axis=1, keepdims=True)
    o_ref[0, 0] = (lo / norm).astype(o_ref.dtype)
    o_ref[1, 0] = (hi / norm).astype(o_ref.dtype)


def _hyena_filters_t(seq, c, w1, b1, w2, b2, w3, b3, w4, freq, tc=256):
    tc = min(tc, c)
    fo = w2.shape[0]
    emb = -(-HYENA_EMB_DIM // 8) * 8
    t = jnp.linspace(0.0, 1.0, seq, dtype=F32)
    bands = (HYENA_EMB_DIM - 1) // 2
    fr = jnp.linspace(1e-4, bands - 1, bands, dtype=F32)[:, None]
    mirror = (seq - jnp.arange(seq)) % seq
    tables, times = [], []
    for pos, tp in ((jnp.arange(seq), t), (mirror, jnp.roll(t[::-1], 1))):
        wpos = (2.0 * math.pi / seq) * pos.astype(F32)[None, :]
        tp = tp[None, :]
        z = jnp.concatenate([tp, jnp.cos(fr * wpos), -jnp.sin(fr * wpos),
                             jnp.zeros((emb - HYENA_EMB_DIM, seq), F32)], axis=0)
        tables.append(z)
        times.append(tp)
    z = jnp.stack(tables)
    tt = jnp.stack(times)
    w1_t = jnp.pad(w1.astype(F32).T, ((0, 0), (0, emb - HYENA_EMB_DIM)))
    w4_t = jnp.transpose(w4.astype(F32).reshape(fo, 2, 2, c), (1, 2, 3, 0))
    min_decay = math.log(HYENA_TARGET) / HYENA_SLOW_DECAY
    max_decay = math.log(HYENA_TARGET) / HYENA_FAST_DECAY
    deltas = jnp.abs(jnp.linspace(min_decay, max_decay, c, dtype=F32))[:, None]
    col = lambda v: v.astype(F32)[:, None]
    full = lambda a: pl.BlockSpec(a.shape, lambda f, i: (0,) * a.ndim)
    args = (z, w1_t, col(b1), w2.astype(F32).T, col(b2), w3.astype(F32).T, col(b3), col(freq))
    return pl.pallas_call(
        _hyena_filter_kernel,
        grid=(2, c // tc),
        in_specs=[full(a) for a in args]
                 + [pl.BlockSpec((1, 2, tc, fo), lambda f, i: (f, 0, i, 0)),
                    pl.BlockSpec((tc, 1), lambda f, i: (i, 0)), full(tt)],
        out_specs=pl.BlockSpec((2, 1, tc, seq), lambda f, i: (0, f, i, 0)),
        out_shape=jax.ShapeDtypeStruct((2, 2, c, seq), BF16),
        scratch_shapes=[pltpu.VMEM((2, fo, seq), F32)],
        compiler_params=_cparams(("arbitrary", "arbitrary")),
        name="hyena_filter_mlp",
    )(*args, w4_t, deltas, tt)


def _fft_consts(seq):
    n_fft = 2 * seq
    n1 = n_fft // LANES
    nb = n1 // 2 + 1
    kp = -(-nb // 16) * 16
    k1 = jnp.arange(kp, dtype=jnp.int32)
    keep = (k1 < nb)[:, None]
    two_pi = 2.0 * math.pi

    th = ((k1[:, None] * jnp.arange(n1, dtype=jnp.int32)[None, :]) % n1).astype(F32) * (two_pi / n1)
    f1 = jnp.concatenate([jnp.where(keep, jnp.cos(th), 0.0), jnp.where(keep, -jnp.sin(th), 0.0)], axis=0)
    ph = (k1[:, None] * jnp.arange(LANES, dtype=jnp.int32)[None, :]).astype(F32) * (two_pi / n_fft)
    tw = jnp.stack([jnp.cos(ph), -jnp.sin(ph)])
    om = ((jnp.arange(LANES, dtype=jnp.int32)[:, None] * jnp.arange(LANES, dtype=jnp.int32)[None, :]) % LANES
          ).astype(F32) * (two_pi / LANES)
    cplx = lambda c, s: jnp.concatenate([jnp.concatenate([c, s], axis=1), jnp.concatenate([-s, c], axis=1)], axis=0)
    f2 = cplx(jnp.cos(om), -jnp.sin(om))
    f2i = cplx(jnp.cos(om), jnp.sin(om))
    weight = jnp.where((k1 == 0) | (k1 == nb - 1), 1.0, 2.0)[:, None]
    g = jnp.concatenate([jnp.where(keep, weight * jnp.cos(th[:, :n1 // 2]), 0.0),
                         jnp.where(keep, -weight * jnp.sin(th[:, :n1 // 2]), 0.0)], axis=0).T
    return dict(f1=f1.astype(BF16), tw=tw.astype(BF16), f2=f2.astype(BF16), f2i=f2i.astype(BF16), g=g.astype(BF16), kp=kp)


def _per_channel_matmul(mat_ref, planes, out_scr, ct):
    def body(i, carry):
        c = 2 * i
        pair = jnp.concatenate([planes(c), planes(c + 1)], axis=1)
        res = jnp.dot(mat_ref[...], pair, preferred_element_type=F32)
        out_scr[c] = res[:, :LANES].astype(out_scr.dtype)
        out_scr[c + 1] = res[:, LANES:].astype(out_scr.dtype)
        return carry
    lax.fori_loop(0, ct // 2, body, 0, unroll=32)


def _twiddle_stage2(a, tw_ref, f2_ref):
    ct, kp2, _ = a.shape
    kp = kp2 // 2
    ar, ai = a[:, :kp], a[:, kp:]
    twr, twi = tw_ref[0], tw_ref[1]
    st = jnp.concatenate([ar * twr - ai * twi, ar * twi + ai * twr], axis=2)
    p = jnp.dot(st.reshape(ct * kp, 2 * LANES), f2_ref[...], preferred_element_type=F32).reshape(ct, kp, 2 * LANES)
    return p[:, :, :LANES].astype(BF16), p[:, :, LANES:].astype(BF16)


def _stage2_twiddle_inv(yr, yi, tw_ref, f2i_ref):
    ct, kp, _ = yr.shape
    st = jnp.concatenate([yr, yi], axis=2)
    p = jnp.dot(st.reshape(ct * kp, 2 * LANES), f2i_ref[...], preferred_element_type=F32).reshape(ct, kp, 2 * LANES)
    tr, ti = p[:, :, :LANES].astype(BF16), p[:, :, LANES:].astype(BF16)
    twr, twi = tw_ref[0], tw_ref[1]
    return jnp.concatenate([tr * twr + ti * twi, ti * twr - tr * twi], axis=1)


def _filter_spec_kernel(lo_ref, hi_ref, f1_ref, tw_ref, f2_ref, kr_ref, ki_ref, a_scr):
    ct = lo_ref.shape[2]
    planes = lambda c: jnp.concatenate([lo_ref[0, 0, c], hi_ref[0, 0, c]], axis=0)
    _per_channel_matmul(f1_ref, planes, a_scr, ct)
    xr, xi = _twiddle_stage2(a_scr[...], tw_ref, f2_ref)
    n_fft = f1_ref.shape[1] * LANES
    scale = 1.0 / n_fft
    kr_ref[0] = xr * scale
    ki_ref[0] = xi * scale


def _filter_spectrum(halves, consts, ct=128):
    _, nf, c, seq = halves.shape
    ct = min(ct, c)
    kp = consts["kp"]
    h5 = halves.reshape(2, nf, c, seq // LANES, LANES)
    half = lambda s: pl.BlockSpec((1, 1, ct, seq // LANES, LANES), lambda f, i: (s, f, i, 0, 0))
    full = lambda a: pl.BlockSpec(a.shape, lambda f, i: (0,) * a.ndim)
    spec = jax.ShapeDtypeStruct((nf, c, kp, LANES), BF16)
    out = pl.BlockSpec((1, ct, kp, LANES), lambda f, i: (f, i, 0, 0))
    return pl.pallas_call(
        _filter_spec_kernel,
        grid=(nf, c // ct),
        in_specs=[half(0), half(1), full(consts["f1"]), full(consts["tw"]), full(consts["f2"])],
        out_specs=[out, out],
        out_shape=[spec, spec],
        scratch_shapes=[pltpu.VMEM((ct, 2 * kp, LANES), BF16)],
        compiler_params=_cparams(("parallel", "parallel")),
        name="hyena_filter_spectrum",
    )(h5, h5, consts["f1"], consts["tw"], consts["f2"])


def _hyena_conv_kernel(second, x_ref, m_ref, e_ref, skip_ref, kr_ref, ki_ref, f1_ref, tw_ref, f2_ref, f2i_ref, g_ref,
                       o_ref, a_scr, t_scr, y_scr):
    ct = x_ref.shape[1]
    _per_channel_matmul(f1_ref, lambda c: x_ref[0, c], a_scr, ct)
    xr, xi = _twiddle_stage2(a_scr[...], tw_ref, f2_ref)
    kr, ki = kr_ref[0], ki_ref[0]
    t_scr[...] = _stage2_twiddle_inv(xr * kr - xi * ki, xr * ki + xi * kr, tw_ref, f2i_ref)

    _per_channel_matmul(g_ref, lambda c: t_scr[c], y_scr, ct)

    out = m_ref[0].astype(F32) * (y_scr[...] + x_ref[0].astype(F32) * skip_ref[...])
    if second:
        e = e_ref[0].astype(F32)
        out = out * (e * jax.nn.sigmoid(e))
    o_ref[0] = out.astype(o_ref.dtype)


def _hyena_conv(second, x, x_row0, hy, m_row0, e_row0, skip, kr, ki, filt, consts, ct=128):
    b, _, seq = x.shape
    c = kr.shape[1]
    ct = min(ct, c)
    kp = consts["kp"]
    n1h = seq // LANES
    view = lambda a: a.reshape(a.shape[0], a.shape[1], n1h, LANES)
    rows = lambda blk0: pl.BlockSpec((1, ct, n1h, LANES), lambda j, i: (i, blk0 + j, 0, 0))
    full = lambda a: pl.BlockSpec(a.shape, lambda j, i: (0,) * a.ndim)
    kspec = pl.BlockSpec((1, ct, kp, LANES), lambda j, i: (filt, j, 0, 0))
    mats = [consts["f1"][:, :n1h], consts["tw"], consts["f2"], consts["f2i"], consts["g"]]
    out = pl.pallas_call(
        functools.partial(_hyena_conv_kernel, second),
        grid=(c // ct, b),
        in_specs=[rows(x_row0 // ct), rows(m_row0 // ct), rows(e_row0 // ct),
                  pl.BlockSpec((ct, 1, 1), lambda j, i: (j, 0, 0)),
                  kspec, kspec] + [full(a) for a in mats],
        out_specs=pl.BlockSpec((1, ct, n1h, LANES), lambda j, i: (i, j, 0, 0)),
        out_shape=jax.ShapeDtypeStruct((b, c, n1h, LANES), BF16),
        scratch_shapes=[pltpu.VMEM((ct, 2 * kp, LANES), BF16), pltpu.VMEM((ct, 2 * kp, LANES), BF16),
                        pltpu.VMEM((ct, n1h, LANES), F32)],
        compiler_params=_cparams(("parallel", "parallel")),
        name="hyena_conv2" if second else "hyena_conv1",
    )(view(x), view(hy), view(hy), skip, kr, ki, *mats)
    return out.reshape(b, c, seq)


def _t5_bucket(rel):
    half_b = REL_BUCKETS // 2
    max_exact = half_b // 2
    ret = jnp.where(rel > 0, half_b, 0)
    n = jnp.abs(rel)
    nf = jnp.maximum(n, 1).astype(F32)
    large = max_exact + (jnp.log(nf / max_exact) / math.log(REL_MAX_DIST / max_exact)
                         * (half_b - max_exact)).astype(jnp.int32)
    large = jnp.minimum(large, half_b - 1)
    return ret + jnp.where(n < max_exact, n, large)


def _dilated_bias(rel_bias):
    rel = jnp.arange(ATT_KW)[None, :] - ATT_HALF - jnp.arange(ATT_QB)[:, None]
    col = jnp.arange(ATT_KW)
    edge = jnp.stack([col >= ATT_HALF, col >= 0, col < ATT_KW - ATT_HALF])
    out = []
    for _, dil in DILATED_PATTERNS:
        onehot = jax.nn.one_hot(_t5_bucket(rel * dil), REL_BUCKETS, dtype=F32)
        bias = jnp.einsum("qkb,bh->qkh", onehot, rel_bias.astype(F32), precision=lax.Precision.HIGHEST)
        bias = jnp.where((jnp.abs(rel) <= ATT_HALF)[:, :, None], bias, NEG_INF)
        bias = jnp.transpose(bias, (2, 0, 1))[:, None]
        out.append(jnp.where(edge[None, :, None, :], bias, NEG_INF))
    return jnp.stack(out)


def _head_select(first, second):
    lane = lax.broadcasted_iota(jnp.int32, first.shape[:-1] + (LANES,), len(first.shape) - 1)
    return jnp.where(lane < HEAD_DIM, first, second)


def _dilated_kernel(q_ref, k_ref, v_ref, g_ref, bias_ref, o_ref, *scratch):
    stage_a, stage_b = scratch[0:3], scratch[3:6]
    qd, kd, vd = scratch[6:9]
    last = len(DILATED_PATTERNS) - 1
    states = [scratch[9 + 3 * p:12 + 3 * p] for p in range(last)]
    seq = q_ref.shape[1]
    stage_a[0][...] = q_ref[0].astype(F32) * (HEAD_DIM ** -0.5)
    stage_a[1][...] = k_ref[0].astype(F32)
    stage_a[2][...] = v_ref[0].astype(F32)
    zero_halo = jnp.zeros((ATT_HALF, LANES), BF16)
    first_head = lax.broadcasted_iota(jnp.int32, (ATT_QB, LANES), 1) < HEAD_DIM

    src, dst = stage_a, stage_b
    for pi, (_, dil) in enumerate(DILATED_PATTERNS):
        n = seq // dil
        nblk = n // ATT_QB
        if pi > 0:
            prev_dil = dil // ATT_STEP
            for r1 in range(prev_dil):
                for r2 in range(ATT_STEP):
                    r = r1 + prev_dil * r2
                    for a_src, a_dst in zip(src, dst):
                        a_dst[r * n:(r + 1) * n, :] = a_src[pl.ds(r1 * n * ATT_STEP + r2, n, stride=ATT_STEP), :]
            src, dst = dst, src
        stride_k = n + 2 * ATT_HALF
        qd[...] = src[0][...].astype(BF16)
        for r in range(dil):
            base = r * stride_k
            for a_src, a_dst in ((src[1], kd), (src[2], vd)):
                a_dst[base:base + ATT_HALF, :] = zero_halo
                a_dst[base + ATT_HALF:base + ATT_HALF + n, :] = a_src[r * n:(r + 1) * n, :].astype(BF16)
                a_dst[base + ATT_HALF + n:base + stride_k, :] = zero_halo
        if pi == last:
            states.append(dst)
        acc_w, mx_w, den_w = states[pi]

        def block_group(it, carry, pi=pi, nblk=nblk, acc_w=acc_w, mx_w=mx_w, den_w=den_w):
            chains = []
            for u in range(ATT_UNROLL):
                blk = it * ATT_UNROLL + u
                r = blk // nblk
                qb = blk % nblk
                q0 = pl.multiple_of(blk * ATT_QB, ATT_QB)
                k0 = pl.multiple_of(q0 + r * (2 * ATT_HALF), ATT_QB)
                edge = jnp.where(qb == 0, 0, jnp.where(qb == nblk - 1, 2, 1))
                qblk = qd[pl.ds(q0, ATT_QB), :]
                qcat = jnp.concatenate([jnp.where(first_head, qblk, jnp.zeros_like(qblk)),
                                        jnp.where(first_head, jnp.zeros_like(qblk), qblk)], axis=0)
                s = lax.dot_general(qcat, kd[pl.ds(k0, ATT_KW), :], (((1,), (1,)), ((), ())),
                                    preferred_element_type=F32)
                chains.append((q0, k0, edge, s))
            probs = []
            for q0, k0, edge, s in chains:
                s = s + jnp.concatenate([bias_ref[pi, 0, edge], bias_ref[pi, 1, edge]], axis=0)
                m = jnp.max(s, axis=1, keepdims=True)
                p = jnp.exp(s - m)
                probs.append((p.astype(BF16), m, jnp.sum(p, axis=1, keepdims=True)))
            for (q0, k0, edge, s), (p, m, l) in zip(chains, probs):
                pv = jnp.dot(p, vd[pl.ds(k0, ATT_KW), :], preferred_element_type=F32)
                acc_w[pl.ds(q0, ATT_QB), :] = _head_select(pv[:ATT_QB], pv[ATT_QB:])
                mx_w[pl.ds(q0, ATT_QB), :] = _head_select(m[:ATT_QB], m[ATT_QB:])
                den_w[pl.ds(q0, ATT_QB), :] = _head_select(l[:ATT_QB], l[ATT_QB:])
            return carry

        lax.fori_loop(0, dil * nblk // ATT_UNROLL, block_group, 0)

    for pi in range(len(DILATED_PATTERNS) - 1, 0, -1):
        dil = DILATED_PATTERNS[pi][1]
        n = seq // dil
        prev_dil = dil // ATT_STEP
        acc_c, mx_c, den_c = states[pi - 1]
        acc_f, mx_f, den_f = states[pi]
        for r1 in range(prev_dil):
            for r2 in range(ATT_STEP):
                r = r1 + prev_dil * r2
                rows = pl.ds(r1 * n * ATT_STEP + r2, n, stride=ATT_STEP)
                fine = slice(r * n, (r + 1) * n)
                m_old = mx_c[rows, :]
                m_new = mx_f[fine, :]
                m_all = jnp.maximum(m_old, m_new)
                decay = jnp.exp(-jnp.abs(m_old - m_new))
                a_old = jnp.where(m_old >= m_new, 1.0, decay)
                a_new = jnp.where(m_old >= m_new, decay, 1.0)
                acc_c[rows, :] = acc_c[rows, :] * a_old + acc_f[fine, :] * a_new
                den_c[rows, :] = den_c[rows, :] * a_old + den_f[fine, :] * a_new
                mx_c[rows, :] = m_all

    acc, _, den = states[0]
    g = g_ref[0].astype(F32)
    o_ref[0] = (acc[...] / den[...] * (g * jax.nn.sigmoid(g))).astype(o_ref.dtype)


def _dilated_attention(proj, bias, n_heads):
    _, b, seq, _ = proj.shape
    npair = n_heads // 2
    max_dil = max(d for _, d in DILATED_PATTERNS)
    kd_rows = seq + 2 * ATT_HALF * max_dil
    blk = lambda off: pl.BlockSpec((None, 1, seq, LANES), lambda i, j: (off * npair + j, i, 0, 0))
    state = pltpu.VMEM((seq, LANES), F32)
    return pl.pallas_call(
        _dilated_kernel,
        grid=(b, npair),
        in_specs=[blk(0), blk(1), blk(2), blk(3),
                  pl.BlockSpec((len(DILATED_PATTERNS), 2, 3, ATT_QB, ATT_KW), lambda i, j: (0, j, 0, 0, 0))],
        out_specs=pl.BlockSpec((None, 1, seq, LANES), lambda i, j: (j, i, 0, 0)),
        out_shape=jax.ShapeDtypeStruct((npair, b, seq, LANES), BF16),
        scratch_shapes=[state] * 6
                       + [pltpu.VMEM((seq, LANES), BF16), pltpu.VMEM((kd_rows, LANES), BF16),
                          pltpu.VMEM((kd_rows, LANES), BF16)]
                       + [state] * (3 * (len(DILATED_PATTERNS) - 1)),
        compiler_params=_cparams(("parallel", "parallel")),
        name="dilated_attention",
    )(proj, proj, proj, proj, bias)


def _na_bias(rpb, rows):
    assert min(NA_ROWS, rows) % 2 == 0
    cols = np.arange(GRID_W)
    cs = np.clip(cols - NA_COLS // 2, 0, GRID_W - NA_COLS)
    kc = np.arange(GRID_W)
    inside = (kc[None, :] >= cs[:, None]) & (kc[None, :] < cs[:, None] + NA_COLS)
    col_idx = kc[None, :] - cols[:, None] + NA_COLS - 1
    onehot = (col_idx[:, :, None] == np.arange(2 * NA_COLS - 1)[None, None, :]).astype(np.float32)
    e = jnp.einsum("hrd,cqd->hrcq", rpb.astype(F32), jnp.asarray(onehot), precision=lax.Precision.HIGHEST)
    e = jnp.where(jnp.asarray(inside)[None, None], e, NEG_INF)
    return jnp.concatenate([e[:, :-1], e[:, 1:]], axis=-1)


def _na_kernel(q_ref, k_ref, v_ref, g_ref, bias_ref, o_ref):
    seq = q_ref.shape[1]
    rows = seq // GRID_W
    kr_n = min(NA_ROWS, rows)
    first_head = lax.broadcasted_iota(jnp.int32, (GRID_W, LANES), 1) < HEAD_DIM

    def row_group(it, carry):
        def scores(batch):
            out = []
            for u in range(batch * NA_BATCH, (batch + 1) * NA_BATCH):
                r = it * NA_UNROLL + u
                rs = jnp.clip(r - kr_n // 2, 0, rows - kr_n)
                shift = rs - r + NA_ROWS - 1
                q0 = pl.multiple_of(r * GRID_W, GRID_W)
                k0 = pl.multiple_of(rs * GRID_W, GRID_W)
                qrow = q_ref[0, pl.ds(q0, GRID_W), :] * (HEAD_DIM ** -0.5)
                qcat = jnp.concatenate([jnp.where(first_head, qrow, jnp.zeros_like(qrow)),
                                        jnp.where(first_head, jnp.zeros_like(qrow), qrow)], axis=0)
                s = lax.dot_general(qcat, k_ref[0, pl.ds(k0, kr_n * GRID_W), :], (((1,), (1,)), ((), ())),
                                    preferred_element_type=F32)
                out.append((q0, k0, shift, s))
            return out

        def finish(chains):
            probs = []
            for q0, k0, shift, s in chains:
                bias = [jnp.concatenate([bias_ref[h, shift + i] for i in range(0, kr_n, 2)], axis=1)
                        for h in range(2)]
                s = s + jnp.concatenate(bias, axis=0)
                p = jnp.exp(s - jnp.max(s, axis=1, keepdims=True))
                probs.append((p.astype(BF16), jnp.sum(p, axis=1, keepdims=True)))
            for (q0, k0, shift, s), (p, l) in zip(chains, probs):
                o = jnp.dot(p, v_ref[0, pl.ds(k0, kr_n * GRID_W), :], preferred_element_type=F32) / l
                g = g_ref[0, pl.ds(q0, GRID_W), :].astype(F32)
                o = _head_select(o[:GRID_W], o[GRID_W:]) * (g * jax.nn.sigmoid(g))
                o_ref[0, pl.ds(q0, GRID_W), :] = o.astype(o_ref.dtype)

        n_batches = NA_UNROLL // NA_BATCH
        pending = scores(0)
        for batch in range(n_batches):
            ahead = scores(batch + 1) if batch + 1 < n_batches else None
            finish(pending)
            pending = ahead
        return carry

    lax.fori_loop(0, rows // NA_UNROLL, row_group, 0)


def _neighbourhood_attention(proj, bias, n_heads):
    _, b, seq, _ = proj.shape
    npair = n_heads // 2
    blk = lambda off: pl.BlockSpec((None, 1, seq, LANES), lambda j, i: (off * npair + j, i, 0, 0))
    return pl.pallas_call(
        _na_kernel,
        grid=(npair, b),
        in_specs=[blk(0), blk(1), blk(2), blk(3),
                  pl.BlockSpec((2,) + bias.shape[1:], lambda j, i: (j, 0, 0, 0))],
        out_specs=pl.BlockSpec((None, 1, seq, LANES), lambda j, i: (j, i, 0, 0)),
        out_shape=jax.ShapeDtypeStruct((npair, b, seq, LANES), BF16),
        compiler_params=_cparams(("parallel", "parallel")),
        name="neighbourhood_attention",
    )(proj, proj, proj, proj, bias)


def _out_kernel(n_y, n_t, *refs):
    n = n_y + n_t
    y_refs = refs[:n]
    w_refs = refs[n:2 * n]
    h_ref, p_ref, wp_ref, wg_ref, gpost_ref, gple_ref, gnext_ref, h_out, u_out = refs[2 * n:]
    t = None
    for idx, (y_ref, w_ref) in enumerate(zip(y_refs, w_refs)):
        if idx < n_y:
            y = jnp.concatenate([y_ref[c] for c in range(y_ref.shape[0])], axis=1)
            part = jnp.dot(y, w_ref[...], preferred_element_type=F32)
        else:
            part = lax.dot_general(y_ref[0], w_ref[...], (((0,), (0,)), ((), ())), preferred_element_type=F32)
        t = part if t is None else t + part
    h = h_ref[...] + _rms(t, gpost_ref[...])
    e = _rms(jnp.dot(p_ref[...].astype(BF16), wp_ref[...], preferred_element_type=F32), gple_ref[...])
    gate = jax.nn.sigmoid(jnp.dot(h.astype(BF16), wg_ref[...], preferred_element_type=F32))
    h = h + gate * e
    h_out[...] = h
    u_out[...] = _rms(h, gnext_ref[...]).astype(u_out.dtype)


def _out_block(ys, ys_t, ws, h, p, layer, w_ple, w_gate, g_post, g_ple, g_next, tm=1024):
    m, d = h.shape
    row = lambda a: pl.BlockSpec((tm, a.shape[1]), lambda i: (i, 0))
    slab = lambda a: pl.BlockSpec((a.shape[0], tm, LANES), lambda i: (0, i, 0))
    p_spec = pl.BlockSpec((None, tm, p.shape[2]), lambda i: (layer, i, 0))
    full = lambda a: pl.BlockSpec(a.shape, lambda i: (0, 0))
    vec = lambda g: g.reshape(1, d)

    def col(a):
        tiles = a.shape[2] // tm
        return pl.BlockSpec((1, a.shape[1], tm), lambda i: (i // tiles, 0, i % tiles))

    return pl.pallas_call(
        functools.partial(_out_kernel, len(ys), len(ys_t)),
        grid=(m // tm,),
        in_specs=[slab(y) for y in ys] + [col(y) for y in ys_t] + [full(w) for w in ws]
                 + [row(h), p_spec, full(w_ple), full(w_gate), full(vec(g_post)), full(vec(g_ple)), full(vec(g_next))],
        out_specs=[pl.BlockSpec((tm, d), lambda i: (i, 0)), pl.BlockSpec((tm, d), lambda i: (i, 0))],
        out_shape=[jax.ShapeDtypeStruct((m, d), F32), jax.ShapeDtypeStruct((m, d), BF16)],
        compiler_params=_cparams(("parallel",)),
        name="out_proj_residual_ple",
    )(*ys, *ys_t, *ws, h, p, w_ple, w_gate, vec(g_post), vec(g_ple), vec(g_next))


def kernel(x, p, w_in, w_out, norm_pre, norm_post, hyena_conv_w, hyena_conv_b, hyena_w1, hyena_b1, hyena_w2, hyena_b2, hyena_w3, hyena_b3, hyena_w4, hyena_freq, hyena_skip, rel_bias, na_rpb, ple_proj, ple_norm, ple_gate):
    b, seq, d = x.shape
    depth = w_in.shape[0]
    d_inner = w_out.shape[1]
    hw = d_inner // 2
    m = b * seq
    consts = _fft_consts(seq)
    dil_bias = _dilated_bias(rel_bias)
    h = x.reshape(m, d)
    u = _rmsnorm_bf16(h, norm_pre[0])
    w_all = w_in.astype(BF16)
    p_all = p.reshape(depth, m, -1)
    for i in range(depth):
        j = i // 2
        if i % 2 == 0:
            w_a = w_all[i, :, :4 * hw].T
            ident = jnp.tile(jnp.array([0.0, 1.0, 0.0], F32)[:, None], (1, hw))
            conv_w = jnp.concatenate([hyena_conv_w[j], ident], axis=1).T
            conv_b = jnp.concatenate([hyena_conv_b[j], jnp.zeros((hw,), F32)])[:, None]
            hy = _proj_t(w_a, u.reshape(b, seq, d), conv_w, conv_b)
            halves = _hyena_filters_t(seq, hw, hyena_w1[j], hyena_b1[j], hyena_w2[j], hyena_b2[j],
                                      hyena_w3[j], hyena_b3[j], hyena_w4[j], hyena_freq[j])
            kr, ki = _filter_spectrum(halves, consts)
            skip = hyena_skip[j][:, :, None, None]
            z = _hyena_conv(False, hy, 0, hy, hw, hw, skip[0], kr, ki, 0, consts)
            ya_t = _hyena_conv(True, z, 0, hy, 2 * hw, 3 * hw, skip[1], kr, ki, 1, consts)
            proj_b = _matmul_bf16(u, w_all[i:i + 1, :, 4 * hw:], 0, 0, 4 * hw).reshape(-1, b, seq, LANES)
            yb = _dilated_attention(proj_b, dil_bias, hw // HEAD_DIM)
            ys, ys_t = [yb.reshape(-1, m, LANES)], [ya_t]
            ws = [w_out[i, hw:].astype(BF16), w_out[i, :hw].astype(BF16)]
        else:
            proj = _matmul_bf16(u, w_all, i, 0, 4 * d_inner).reshape(-1, b, seq, LANES)
            yc = _neighbourhood_attention(proj, _na_bias(na_rpb[j], seq // GRID_W), d_inner // HEAD_DIM)
            ys, ys_t = [yc.reshape(-1, m, LANES)], []
            ws = [w_out[i].astype(BF16)]
        g_next = norm_pre[i + 1] if i + 1 < depth else norm_pre[i]
        h, u = _out_block(ys, ys_t, ws, h, p_all, i, ple_proj[i].astype(BF16), ple_gate[i].astype(BF16),
                          norm_post[i], ple_norm[i], g_next)
    return h.reshape(b, seq, d)
```

```python
import functools
import math

import numpy as np
import jax
import jax.numpy as jnp
from jax import lax
from jax.experimental import pallas as pl
from jax.experimental.pallas import tpu as pltpu

F32 = jnp.float32
BF16 = jnp.bfloat16

HEAD_DIM = 64
LANES = 128
HYENA_EMB_DIM = 33
HYENA_FAST_DECAY = 0.3
HYENA_SLOW_DECAY = 1.5
HYENA_TARGET = 1e-2
DILATED_PATTERNS = ((128, 1), (512, 4), (2048, 16))
GRID_W = 64
NA_ROWS = 8
NA_COLS = 16
REL_BUCKETS = 32
REL_MAX_DIST = 1024
RMS_EPS = 1e-6
NEG_INF = -1e30
ATT_QB = 128
ATT_HALF = 64
ATT_KW = ATT_QB + 2 * ATT_HALF
ATT_UNROLL = 8
ATT_STEP = 4
assert all(b[1] == a[1] * ATT_STEP for a, b in zip(DILATED_PATTERNS, DILATED_PATTERNS[1:])) and DILATED_PATTERNS[0][1] == 1
NA_BATCH = 8
NA_UNROLL = 64
VMEM_LIMIT = 56 * 1024 * 1024


def _cparams(sem):
    return pltpu.CompilerParams(dimension_semantics=sem, vmem_limit_bytes=VMEM_LIMIT)


def _rms(x, g):
    return x * lax.rsqrt(jnp.mean(x * x, axis=-1, keepdims=True) + RMS_EPS) * g


def _rmsnorm_kernel(x_ref, g_ref, o_ref):
    o_ref[...] = _rms(x_ref[...], g_ref[...]).astype(o_ref.dtype)


def _rmsnorm_bf16(x2d, g, tm=1024):
    m, d = x2d.shape
    return pl.pallas_call(
        _rmsnorm_kernel,
        grid=(m // tm,),
        in_specs=[pl.BlockSpec((tm, d), lambda i: (i, 0)), pl.BlockSpec((1, d), lambda i: (0, 0))],
        out_specs=pl.BlockSpec((tm, d), lambda i: (i, 0)),
        out_shape=jax.ShapeDtypeStruct((m, d), BF16),
        compiler_params=_cparams(("parallel",)),
        name="rmsnorm",
    )(x2d, g.reshape(1, d))


def _matmul_kernel(a_ref, b_ref, o_ref):
    res = jnp.dot(a_ref[...], b_ref[...], preferred_element_type=F32).astype(o_ref.dtype)
    for c in range(o_ref.shape[0]):
        o_ref[c] = res[:, c * LANES:(c + 1) * LANES]


def _matmul_bf16(a, w, layer, col0, n, tm=2048, tn=1024):
    m, k = a.shape
    return pl.pallas_call(
        _matmul_kernel,
        grid=(m // tm, n // tn),
        in_specs=[pl.BlockSpec((tm, k), lambda i, j: (i, 0)), pl.BlockSpec((None, k, tn), lambda i, j: (layer, 0, col0 // tn + j))],
        out_specs=pl.BlockSpec((tn // LANES, tm, LANES), lambda i, j: (j, i, 0)),
        out_shape=jax.ShapeDtypeStruct((n // LANES, m, LANES), BF16),
        compiler_params=_cparams(("parallel", "parallel")),
        name="in_proj",
    )(a, w)


def _proj_t_kernel(w_ref, u_ref, cw_ref, cb_ref, o_ref):
    p = lax.dot_general(w_ref[...], u_ref[0], (((1,), (1,)), ((), ())), preferred_element_type=F32)
    seq = p.shape[1]
    t = lax.broadcasted_iota(jnp.int32, p.shape, 1)
    prev = jnp.where(t == 0, 0.0, pltpu.roll(p, 1, axis=1))
    nxt = jnp.where(t == seq - 1, 0.0, pltpu.roll(p, seq - 1, axis=1))
    cw = cw_ref[...]
    o_ref[0] = (prev * cw[:, 0:1] + p * cw[:, 1:2] + nxt * cw[:, 2:3] + cb_ref[...]).astype(o_ref.dtype)


def _proj_t(w_t, u, conv_w, conv_b, tc=256):
    c, d = w_t.shape
    b, seq, _ = u.shape
    return pl.pallas_call(
        _proj_t_kernel,
        grid=(b, c // tc),
        in_specs=[
            pl.BlockSpec((tc, d), lambda i, j: (j, 0)),
            pl.BlockSpec((1, seq, d), lambda i, j: (i, 0, 0)),
            pl.BlockSpec((tc, 3), lambda i, j: (j, 0)),
            pl.BlockSpec((tc, 1), lambda i, j: (j, 0)),
        ],
        out_specs=pl.BlockSpec((1, tc, seq), lambda i, j: (i, j, 0)),
        out_shape=jax.ShapeDtypeStruct((b, c, seq), BF16),
        compiler_params=_cparams(("parallel", "parallel")),
        name="in_proj_t_conv3",
    )(w_t, u, conv_w, conv_b)


def _hyena_filter_kernel(z_ref, w1_ref, b1_ref, w2_ref, b2_ref, w3_ref, b3_ref, fq_ref, w4_ref, dl_ref, t_ref,
                         o_ref, a_scr):
    hp = lax.Precision.HIGHEST

    @pl.when((pl.program_id(0) == 0) & (pl.program_id(1) == 0))
    def _():
        for d in range(2):
            a = z_ref[d]
            for w_ref, b_ref in ((w1_ref, b1_ref), (w2_ref, b2_ref), (w3_ref, b3_ref)):
                a = jnp.sin(fq_ref[...] * (jnp.dot(w_ref[...], a, precision=hp, preferred_element_type=F32) + b_ref[...]))
            a_scr[d] = a

    seq = o_ref.shape[-1]
    proj = lambda d: jnp.dot(w4_ref[0, d].astype(BF16), a_scr[d].astype(BF16), preferred_element_type=F32)
    lo = proj(0) * jnp.exp(-t_ref[0] * dl_ref[...])
    hi = proj(1) * jnp.exp(-t_ref[1] * dl_ref[...])
    hi = jnp.where(lax.broadcasted_iota(jnp.int32, (1, seq), 1) == 0, 0.0, hi)
    norm = jnp.sum(jnp.abs(lo), axis=1, keepdims=True) + jnp.sum(jnp.abs(hi), axis=1, keepdims=True)
    o_ref[0, 0] = (lo / norm).astype(o_ref.dtype)
    o_ref[1, 0] = (hi / norm).astype(o_ref.dtype)


def _hyena_filters_t(seq, c, w1, b1, w2, b2, w3, b3, w4, freq, tc=256):
    tc = min(tc, c)
    fo = w2.shape[0]
    emb = -(-HYENA_EMB_DIM // 8) * 8
    t = jnp.linspace(0.0, 1.0, seq, dtype=F32)
    bands = (HYENA_EMB_DIM - 1) // 2
    fr = jnp.linspace(1e-4, bands - 1, bands, dtype=F32)[:, None]
    mirror = (seq - jnp.arange(seq)) % seq
    tables, times = [], []
    for pos, tp in ((jnp.arange(seq), t), (mirror, jnp.roll(t[::-1], 1))):
        wpos = (2.0 * math.pi / seq) * pos.astype(F32)[None, :]
        tp = tp[None, :]
        z = jnp.concatenate([tp, jnp.cos(fr * wpos), -jnp.sin(fr * wpos),
                             jnp.zeros((emb - HYENA_EMB_DIM, seq), F32)], axis=0)
        tables.append(z)
        times.append(tp)
    z = jnp.stack(tables)
    tt = jnp.stack(times)
    w1_t = jnp.pad(w1.astype(F32).T, ((0, 0), (0, emb - HYENA_EMB_DIM)))
    w4_t = jnp.transpose(w4.astype(F32).reshape(fo, 2, 2, c), (1, 2, 3, 0))
    min_decay = math.log(HYENA_TARGET) / HYENA_SLOW_DECAY
    max_decay = math.log(HYENA_TARGET) / HYENA_FAST_DECAY
    deltas = jnp.abs(jnp.linspace(min_decay, max_decay, c, dtype=F32))[:, None]
    col = lambda v: v.astype(F32)[:, None]
    full = lambda a: pl.BlockSpec(a.shape, lambda f, i: (0,) * a.ndim)
    args = (z, w1_t, col(b1), w2.astype(F32).T, col(b2), w3.astype(F32).T, col(b3), col(freq))
    return pl.pallas_call(
        _hyena_filter_kernel,
        grid=(2, c // tc),
        in_specs=[full(a) for a in args]
                 + [pl.BlockSpec((1, 2, tc, fo), lambda f, i: (f, 0, i, 0)),
                    pl.BlockSpec((tc, 1), lambda f, i: (i, 0)), full(tt)],
        out_specs=pl.BlockSpec((2, 1, tc, seq), lambda f, i: (0, f, i, 0)),
        out_shape=jax.ShapeDtypeStruct((2, 2, c, seq), BF16),
        scratch_shapes=[pltpu.VMEM((2, fo, seq), F32)],
        compiler_params=_cparams(("arbitrary", "arbitrary")),
        name="hyena_filter_mlp",
    )(*args, w4_t, deltas, tt)


def _fft_consts(seq):
    n_fft = 2 * seq
    n1 = n_fft // LANES
    nb = n1 // 2 + 1
    kp = -(-nb // 16) * 16
    k1 = jnp.arange(kp, dtype=jnp.int32)
    keep = (k1 < nb)[:, None]
    two_pi = 2.0 * math.pi

    th = ((k1[:, None] * jnp.arange(n1, dtype=jnp.int32)[None, :]) % n1).astype(F32) * (two_pi / n1)
    f1 = jnp.concatenate([jnp.where(keep, jnp.cos(th), 0.0), jnp.where(keep, -jnp.sin(th), 0.0)], axis=0)
    ph = (k1[:, None] * jnp.arange(LANES, dtype=jnp.int32)[None, :]).astype(F32) * (two_pi / n_fft)
    tw = jnp.stack([jnp.cos(ph), -jnp.sin(ph)])
    om = ((jnp.arange(LANES, dtype=jnp.int32)[:, None] * jnp.arange(LANES, dtype=jnp.int32)[None, :]) % LANES
          ).astype(F32) * (two_pi / LANES)
    cplx = lambda c, s: jnp.concatenate([jnp.concatenate([c, s], axis=1), jnp.concatenate([-s, c], axis=1)], axis=0)
    f2 = cplx(jnp.cos(om), -jnp.sin(om))
    f2i = cplx(jnp.cos(om), jnp.sin(om))
    weight = jnp.where((k1 == 0) | (k1 == nb - 1), 1.0, 2.0)[:, None]
    g = jnp.concatenate([jnp.where(keep, weight * jnp.cos(th[:, :n1 // 2]), 0.0),
                         jnp.where(keep, -weight * jnp.sin(th[:, :n1 // 2]), 0.0)], axis=0).T
    return dict(f1=f1.astype(BF16), tw=tw.astype(BF16), f2=f2.astype(BF16), f2i=f2i.astype(BF16), g=g.astype(BF16), kp=kp)


def _per_channel_matmul(mat_ref, planes, out_scr, ct):
    def body(i, carry):
        c = 2 * i
        pair = jnp.concatenate([planes(c), planes(c + 1)], axis=1)
        res = jnp.dot(mat_ref[...], pair, preferred_element_type=F32)
        out_scr[c] = res[:, :LANES].astype(out_scr.dtype)
        out_scr[c + 1] = res[:, LANES:].astype(out_scr.dtype)
        return carry
    lax.fori_loop(0, ct // 2, body, 0, unroll=64)


def _twiddle_stage2(a, tw_ref, f2_ref):
    ct, kp2, _ = a.shape
    kp = kp2 // 2
    ar, ai = a[:, :kp], a[:, kp:]
    twr, twi = tw_ref[0], tw_ref[1]
    st = jnp.concatenate([ar * twr - ai * twi, ar * twi + ai * twr], axis=2)
    p = jnp.dot(st.reshape(ct * kp, 2 * LANES), f2_ref[...], preferred_element_type=F32).reshape(ct, kp, 2 * LANES)
    return p[:, :, :LANES].astype(BF16), p[:, :, LANES:].astype(BF16)


def _stage2_twiddle_inv(yr, yi, tw_ref, f2i_ref):
    ct, kp, _ = yr.shape
    st = jnp.concatenate([yr, yi], axis=2)
    p = jnp.dot(st.reshape(ct * kp, 2 * LANES), f2i_ref[...], preferred_element_type=F32).reshape(ct, kp, 2 * LANES)
    tr, ti = p[:, :, :LANES].astype(BF16), p[:, :, LANES:].astype(BF16)
    twr, twi = tw_ref[0], tw_ref[1]
    return jnp.concatenate([tr * twr + ti * twi, ti * twr - tr * twi], axis=1)


def _filter_spec_kernel(lo_ref, hi_ref, f1_ref, tw_ref, f2_ref, kr_ref, ki_ref, a_scr):
    ct = lo_ref.shape[2]
    planes = lambda c: jnp.concatenate([lo_ref[0, 0, c], hi_ref[0, 0, c]], axis=0)
    _per_channel_matmul(f1_ref, planes, a_scr, ct)
    xr, xi = _twiddle_stage2(a_scr[...], tw_ref, f2_ref)
    n_fft = f1_ref.shape[1] * LANES
    scale = 1.0 / n_fft
    kr_ref[0] = xr * scale
    ki_ref[0] = xi * scale


def _filter_spectrum(halves, consts, ct=128):
    _, nf, c, seq = halves.shape
    ct = min(ct, c)
    kp = consts["kp"]
    h5 = halves.reshape(2, nf, c, seq // LANES, LANES)
    half = lambda s: pl.BlockSpec((1, 1, ct, seq // LANES, LANES), lambda f, i: (s, f, i, 0, 0))
    full = lambda a: pl.BlockSpec(a.shape, lambda f, i: (0,) * a.ndim)
    spec = jax.ShapeDtypeStruct((nf, c, kp, LANES), BF16)
    out = pl.BlockSpec((1, ct, kp, LANES), lambda f, i: (f, i, 0, 0))
    return pl.pallas_call(
        _filter_spec_kernel,
        grid=(nf, c // ct),
        in_specs=[half(0), half(1), full(consts["f1"]), full(consts["tw"]), full(consts["f2"])],
        out_specs=[out, out],
        out_shape=[spec, spec],
        scratch_shapes=[pltpu.VMEM((ct, 2 * kp, LANES), BF16)],
        compiler_params=_cparams(("parallel", "parallel")),
        name="hyena_filter_spectrum",
    )(h5, h5, consts["f1"], consts["tw"], consts["f2"])


def _hyena_conv_kernel(second, x_ref, m_ref, e_ref, skip_ref, kr_ref, ki_ref, f1_ref, tw_ref, f2_ref, f2i_ref, g_ref,
                       o_ref, a_scr, t_scr, y_scr):
    ct = x_ref.shape[1]
    _per_channel_matmul(f1_ref, lambda c: x_ref[0, c], a_scr, ct)
    xr, xi = _twiddle_stage2(a_scr[...], tw_ref, f2_ref)
    kr, ki = kr_ref[0], ki_ref[0]
    t_scr[...] = _stage2_twiddle_inv(xr * kr - xi * ki, xr * ki + xi * kr, tw_ref, f2i_ref)

    _per_channel_matmul(g_ref, lambda c: t_scr[c], y_scr, ct)

    out = m_ref[0].astype(F32) * (y_scr[...] + x_ref[0].astype(F32) * skip_ref[...])
    if second:
        e = e_ref[0].astype(F32)
        out = out * (e * jax.nn.sigmoid(e))
    o_ref[0] = out.astype(o_ref.dtype)


def _hyena_conv(second, x, x_row0, hy, m_row0, e_row0, skip, kr, ki, filt, consts, ct=128):
    b, _, seq = x.shape
    c = kr.shape[1]
    ct = min(ct, c)
    kp = consts["kp"]
    n1h = seq // LANES
    view = lambda a: a.reshape(a.shape[0], a.shape[1], n1h, LANES)
    rows = lambda blk0: pl.BlockSpec((1, ct, n1h, LANES), lambda j, i: (i, blk0 + j, 0, 0))
    full = lambda a: pl.BlockSpec(a.shape, lambda j, i: (0,) * a.ndim)
    kspec = pl.BlockSpec((1, ct, kp, LANES), lambda j, i: (filt, j, 0, 0))
    mats = [consts["f1"][:, :n1h], consts["tw"], consts["f2"], consts["f2i"], consts["g"]]
    out = pl.pallas_call(
        functools.partial(_hyena_conv_kernel, second),
        grid=(c // ct, b),
        in_specs=[rows(x_row0 // ct), rows(m_row0 // ct), rows(e_row0 // ct),
                  pl.BlockSpec((ct, 1, 1), lambda j, i: (j, 0, 0)),
                  kspec, kspec] + [full(a) for a in mats],
        out_specs=pl.BlockSpec((1, ct, n1h, LANES), lambda j, i: (i, j, 0, 0)),
        out_shape=jax.ShapeDtypeStruct((b, c, n1h, LANES), BF16),
        scratch_shapes=[pltpu.VMEM((ct, 2 * kp, LANES), BF16), pltpu.VMEM((ct, 2 * kp, LANES), BF16),
                        pltpu.VMEM((ct, n1h, LANES), F32)],
        compiler_params=_cparams(("parallel", "parallel")),
        name="hyena_conv2" if second else "hyena_conv1",
    )(view(x), view(hy), view(hy), skip, kr, ki, *mats)
    return out.reshape(b, c, seq)


def _t5_bucket(rel):
    half_b = REL_BUCKETS // 2
    max_exact = half_b // 2
    ret = jnp.where(rel > 0, half_b, 0)
    n = jnp.abs(rel)
    nf = jnp.maximum(n, 1).astype(F32)
    large = max_exact + (jnp.log(nf / max_exact) / math.log(REL_MAX_DIST / max_exact)
                         * (half_b - max_exact)).astype(jnp.int32)
    large = jnp.minimum(large, half_b - 1)
    return ret + jnp.where(n < max_exact, n, large)


def _dilated_bias(rel_bias):
    rel = jnp.arange(ATT_KW)[None, :] - ATT_HALF - jnp.arange(ATT_QB)[:, None]
    col = jnp.arange(ATT_KW)
    edge = jnp.stack([col >= ATT_HALF, col >= 0, col < ATT_KW - ATT_HALF])
    out = []
    for _, dil in DILATED_PATTERNS:
        onehot = jax.nn.one_hot(_t5_bucket(rel * dil), REL_BUCKETS, dtype=F32)
        bias = jnp.einsum("qkb,bh->qkh", onehot, rel_bias.astype(F32), precision=lax.Precision.HIGHEST)
        bias = jnp.where((jnp.abs(rel) <= ATT_HALF)[:, :, None], bias, NEG_INF)
        bias = jnp.transpose(bias, (2, 0, 1))[:, None]
        out.append(jnp.where(edge[None, :, None, :], bias, NEG_INF))
    return jnp.stack(out)


def _head_select(first, second):
    lane = lax.broadcasted_iota(jnp.int32, first.shape[:-1] + (LANES,), len(first.shape) - 1)
    return jnp.where(lane < HEAD_DIM, first, second)


def _dilated_kernel(q_ref, k_ref, v_ref, g_ref, bias_ref, o_ref, *scratch):
    stage_a, stage_b = scratch[0:3], scratch[3:6]
    qd, kd, vd = scratch[6:9]
    last = len(DILATED_PATTERNS) - 1
    states = [scratch[9 + 3 * p:12 + 3 * p] for p in range(last)]
    seq = q_ref.shape[1]
    stage_a[0][...] = q_ref[0].astype(F32) * (HEAD_DIM ** -0.5)
    stage_a[1][...] = k_ref[0].astype(F32)
    stage_a[2][...] = v_ref[0].astype(F32)
    zero_halo = jnp.zeros((ATT_HALF, LANES), BF16)
    first_head = lax.broadcasted_iota(jnp.int32, (ATT_QB, LANES), 1) < HEAD_DIM

    src, dst = stage_a, stage_b
    for pi, (_, dil) in enumerate(DILATED_PATTERNS):
        n = seq // dil
        nblk = n // ATT_QB
        if pi > 0:
            prev_dil = dil // ATT_STEP
            for r1 in range(prev_dil):
                for r2 in range(ATT_STEP):
                    r = r1 + prev_dil * r2
                    for a_src, a_dst in zip(src, dst):
                        a_dst[r * n:(r + 1) * n, :] = a_src[pl.ds(r1 * n * ATT_STEP + r2, n, stride=ATT_STEP), :]
            src, dst = dst, src
        stride_k = n + 2 * ATT_HALF
        qd[...] = src[0][...].astype(BF16)
        for r in range(dil):
            base = r * stride_k
            for a_src, a_dst in ((src[1], kd), (src[2], vd)):
                a_dst[base:base + ATT_HALF, :] = zero_halo
                a_dst[base + ATT_HALF:base + ATT_HALF + n, :] = a_src[r * n:(r + 1) * n, :].astype(BF16)
                a_dst[base + ATT_HALF + n:base + stride_k, :] = zero_halo
        if pi == last:
            states.append(dst)
        acc_w, mx_w, den_w = states[pi]

        def block_group(it, carry, pi=pi, nblk=nblk, acc_w=acc_w, mx_w=mx_w, den_w=den_w):
            chains = []
            for u in range(ATT_UNROLL):
                blk = it * ATT_UNROLL + u
                r = blk // nblk
                qb = blk % nblk
                q0 = pl.multiple_of(blk * ATT_QB, ATT_QB)
                k0 = pl.multiple_of(q0 + r * (2 * ATT_HALF), ATT_QB)
                edge = jnp.where(qb == 0, 0, jnp.where(qb == nblk - 1, 2, 1))
                qblk = qd[pl.ds(q0, ATT_QB), :]
                qcat = jnp.concatenate([jnp.where(first_head, qblk, jnp.zeros_like(qblk)),
                                        jnp.where(first_head, jnp.zeros_like(qblk), qblk)], axis=0)
                s = lax.dot_general(qcat, kd[pl.ds(k0, ATT_KW), :], (((1,), (1,)), ((), ())),
                                    preferred_element_type=F32)
                chains.append((q0, k0, edge, s))
            probs = []
            for q0, k0, edge, s in chains:
                s = s + jnp.concatenate([bias_ref[pi, 0, edge], bias_ref[pi, 1, edge]], axis=0)
                m = jnp.max(s, axis=1, keepdims=True)
                p = jnp.exp(s - m)
                probs.append((p.astype(BF16), m, jnp.sum(p, axis=1, keepdims=True)))
            for (q0, k0, edge, s), (p, m, l) in zip(chains, probs):
                pv = jnp.dot(p, vd[pl.ds(k0, ATT_KW), :], preferred_element_type=F32)
                acc_w[pl.ds(q0, ATT_QB), :] = _head_select(pv[:ATT_QB], pv[ATT_QB:])
                mx_w[pl.ds(q0, ATT_QB), :] = _head_select(m[:ATT_QB], m[ATT_QB:])
                den_w[pl.ds(q0, ATT_QB), :] = _head_select(l[:ATT_QB], l[ATT_QB:])
            return carry

        lax.fori_loop(0, dil * nblk // ATT_UNROLL, block_group, 0)

    for pi in range(len(DILATED_PATTERNS) - 1, 0, -1):
        dil = DILATED_PATTERNS[pi][1]
        n = seq // dil
        prev_dil = dil // ATT_STEP
        acc_c, mx_c, den_c = states[pi - 1]
        acc_f, mx_f, den_f = states[pi]
        for r1 in range(prev_dil):
            for r2 in range(ATT_STEP):
                r = r1 + prev_dil * r2
                rows = pl.ds(r1 * n * ATT_STEP + r2, n, stride=ATT_STEP)
                fine = slice(r * n, (r + 1) * n)
                m_old = mx_c[rows, :]
                m_new = mx_f[fine, :]
                m_all = jnp.maximum(m_old, m_new)
                decay = jnp.exp(-jnp.abs(m_old - m_new))
                a_old = jnp.where(m_old >= m_new, 1.0, decay)
                a_new = jnp.where(m_old >= m_new, decay, 1.0)
                acc_c[rows, :] = acc_c[rows, :] * a_old + acc_f[fine, :] * a_new
                den_c[rows, :] = den_c[rows, :] * a_old + den_f[fine, :] * a_new
                mx_c[rows, :] = m_all

    acc, _, den = states[0]
    g = g_ref[0].astype(F32)
    o_ref[0] = (acc[...] / den[...] * (g * jax.nn.sigmoid(g))).astype(o_ref.dtype)


def _dilated_attention(proj, bias, n_heads):
    _, b, seq, _ = proj.shape
    npair = n_heads // 2
    max_dil = max(d for _, d in DILATED_PATTERNS)
    kd_rows = seq + 2 * ATT_HALF * max_dil
    blk = lambda off: pl.BlockSpec((None, 1, seq, LANES), lambda i, j: (off * npair + j, i, 0, 0))
    state = pltpu.VMEM((seq, LANES), F32)
    return pl.pallas_call(
        _dilated_kernel,
        grid=(b, npair),
        in_specs=[blk(0), blk(1), blk(2), blk(3),
                  pl.BlockSpec((len(DILATED_PATTERNS), 2, 3, ATT_QB, ATT_KW), lambda i, j: (0, j, 0, 0, 0))],
        out_specs=pl.BlockSpec((None, 1, seq, LANES), lambda i, j: (j, i, 0, 0)),
        out_shape=jax.ShapeDtypeStruct((npair, b, seq, LANES), BF16),
        scratch_shapes=[state] * 6
                       + [pltpu.VMEM((seq, LANES), BF16), pltpu.VMEM((kd_rows, LANES), BF16),
                          pltpu.VMEM((kd_rows, LANES), BF16)]
                       + [state] * (3 * (len(DILATED_PATTERNS) - 1)),
        compiler_params=_cparams(("parallel", "parallel")),
        name="dilated_attention",
    )(proj, proj, proj, proj, bias)


def _na_bias(rpb, rows):
    assert min(NA_ROWS, rows) % 2 == 0
    cols = np.arange(GRID_W)
    cs = np.clip(cols - NA_COLS // 2, 0, GRID_W - NA_COLS)
    kc = np.arange(GRID_W)
    inside = (kc[None, :] >= cs[:, None]) & (kc[None, :] < cs[:, None] + NA_COLS)
    col_idx = kc[None, :] - cols[:, None] + NA_COLS - 1
    onehot = (col_idx[:, :, None] == np.arange(2 * NA_COLS - 1)[None, None, :]).astype(np.float32)
    e = jnp.einsum("hrd,cqd->hrcq", rpb.astype(F32), jnp.asarray(onehot), precision=lax.Precision.HIGHEST)
    e = jnp.where(jnp.asarray(inside)[None, None], e, NEG_INF)
    return jnp.concatenate([e[:, :-1], e[:, 1:]], axis=-1)


def _na_kernel(q_ref, k_ref, v_ref, g_ref, bias_ref, o_ref):
    seq = q_ref.shape[1]
    rows = seq // GRID_W
    kr_n = min(NA_ROWS, rows)
    first_head = lax.broadcasted_iota(jnp.int32, (GRID_W, LANES), 1) < HEAD_DIM

    def row_group(it, carry):
        def scores(batch):
            out = []
            for u in range(batch * NA_BATCH, (batch + 1) * NA_BATCH):
                r = it * NA_UNROLL + u
                rs = jnp.clip(r - kr_n // 2, 0, rows - kr_n)
                shift = rs - r + NA_ROWS - 1
                q0 = pl.multiple_of(r * GRID_W, GRID_W)
                k0 = pl.multiple_of(rs * GRID_W, GRID_W)
                qrow = q_ref[0, pl.ds(q0, GRID_W), :] * (HEAD_DIM ** -0.5)
                qcat = jnp.concatenate([jnp.where(first_head, qrow, jnp.zeros_like(qrow)),
                                        jnp.where(first_head, jnp.zeros_like(qrow), qrow)], axis=0)
                s = lax.dot_general(qcat, k_ref[0, pl.ds(k0, kr_n * GRID_W), :], (((1,), (1,)), ((), ())),
                                    preferred_element_type=F32)
                out.append((q0, k0, shift, s))
            return out

        def finish(chains):
            probs = []
            for q0, k0, shift, s in chains:
                bias = [jnp.concatenate([bias_ref[h, shift + i] for i in range(0, kr_n, 2)], axis=1)
                        for h in range(2)]
                s = s + jnp.concatenate(bias, axis=0)
                p = jnp.exp(s - jnp.max(s, axis=1, keepdims=True))
                probs.append((p.astype(BF16), jnp.sum(p, axis=1, keepdims=True)))
            for (q0, k0, shift, s), (p, l) in zip(chains, probs):
                o = jnp.dot(p, v_ref[0, pl.ds(k0, kr_n * GRID_W), :], preferred_element_type=F32) / l
                g = g_ref[0, pl.ds(q0, GRID_W), :].astype(F32)
                o = _head_select(o[:GRID_W], o[GRID_W:]) * (g * jax.nn.sigmoid(g))
                o_ref[0, pl.ds(q0, GRID_W), :] = o.astype(o_ref.dtype)

        n_batches = NA_UNROLL // NA_BATCH
        pending = scores(0)
        for batch in range(n_batches):
            ahead = scores(batch + 1) if batch + 1 < n_batches else None
            finish(pending)
            pending = ahead
        return carry

    lax.fori_loop(0, rows // NA_UNROLL, row_group, 0)


def _neighbourhood_attention(proj, bias, n_heads):
    _, b, seq, _ = proj.shape
    npair = n_heads // 2
    blk = lambda off: pl.BlockSpec((None, 1, seq, LANES), lambda j, i: (off * npair + j, i, 0, 0))
    return pl.pallas_call(
        _na_kernel,
        grid=(npair, b),
        in_specs=[blk(0), blk(1), blk(2), blk(3),
                  pl.BlockSpec((2,) + bias.shape[1:], lambda j, i: (j, 0, 0, 0))],
        out_specs=pl.BlockSpec((None, 1, seq, LANES), lambda j, i: (j, i, 0, 0)),
        out_shape=jax.ShapeDtypeStruct((npair, b, seq, LANES), BF16),
        compiler_params=_cparams(("parallel", "parallel")),
        name="neighbourhood_attention",
    )(proj, proj, proj, proj, bias)


def _out_kernel(n_y, n_t, *refs):
    n = n_y + n_t
    y_refs = refs[:n]
    w_refs = refs[n:2 * n]
    h_ref, p_ref, wp_ref, wg_ref, gpost_ref, gple_ref, gnext_ref, h_out, u_out = refs[2 * n:]
    t = None
    for idx, (y_ref, w_ref) in enumerate(zip(y_refs, w_refs)):
        if idx < n_y:
            y = jnp.concatenate([y_ref[c] for c in range(y_ref.shape[0])], axis=1)
            part = jnp.dot(y, w_ref[...], preferred_element_type=F32)
        else:
            part = lax.dot_general(y_ref[0], w_ref[...], (((0,), (0,)), ((), ())), preferred_element_type=F32)
        t = part if t is None else t + part
    h = h_ref[...] + _rms(t, gpost_ref[...])
    e = _rms(jnp.dot(p_ref[...].astype(BF16), wp_ref[...], preferred_element_type=F32), gple_ref[...])
    gate = jax.nn.sigmoid(jnp.dot(h.astype(BF16), wg_ref[...], preferred_element_type=F32))
    h = h + gate * e
    h_out[...] = h
    u_out[...] = _rms(h, gnext_ref[...]).astype(u_out.dtype)


def _out_block(ys, ys_t, ws, h, p, layer, w_ple, w_gate, g_post, g_ple, g_next, tm=1024):
    m, d = h.shape
    row = lambda a: pl.BlockSpec((tm, a.shape[1]), lambda i: (i, 0))
    slab = lambda a: pl.BlockSpec((a.shape[0], tm, LANES), lambda i: (0, i, 0))
    p_spec = pl.BlockSpec((None, tm, p.shape[2]), lambda i: (layer, i, 0))
    full = lambda a: pl.BlockSpec(a.shape, lambda i: (0, 0))
    vec = lambda g: g.reshape(1, d)

    def col(a):
        tiles = a.shape[2] // tm
        return pl.BlockSpec((1, a.shape[1], tm), lambda i: (i // tiles, 0, i % tiles))

    return pl.pallas_call(
        functools.partial(_out_kernel, len(ys), len(ys_t)),
        grid=(m // tm,),
        in_specs=[slab(y) for y in ys] + [col(y) for y in ys_t] + [full(w) for w in ws]
                 + [row(h), p_spec, full(w_ple), full(w_gate), full(vec(g_post)), full(vec(g_ple)), full(vec(g_next))],
        out_specs=[pl.BlockSpec((tm, d), lambda i: (i, 0)), pl.BlockSpec((tm, d), lambda i: (i, 0))],
        out_shape=[jax.ShapeDtypeStruct((m, d), F32), jax.ShapeDtypeStruct((m, d), BF16)],
        compiler_params=_cparams(("parallel",)),
        name="out_proj_residual_ple",
    )(*ys, *ys_t, *ws, h, p, w_ple, w_gate, vec(g_post), vec(g_ple), vec(g_next))


def kernel(x, p, w_in, w_out, norm_pre, norm_post, hyena_conv_w, hyena_conv_b, hyena_w1, hyena_b1, hyena_w2, hyena_b2, hyena_w3, hyena_b3, hyena_w4, hyena_freq, hyena_skip, rel_bias, na_rpb, ple_proj, ple_norm, ple_gate):
    b, seq, d = x.shape
    depth = w_in.shape[0]
    d_inner = w_out.shape[1]
    hw = d_inner // 2
    m = b * seq
    consts = _fft_consts(seq)
    dil_bias = _dilated_bias(rel_bias)
    h = x.reshape(m, d)
    u = _rmsnorm_bf16(h, norm_pre[0])
    w_all = w_in.astype(BF16)
    p_all = p.reshape(depth, m, -1)
    for i in range(depth):
        j = i // 2
        if i % 2 == 0:
            w_a = w_all[i, :, :4 * hw].T
            ident = jnp.tile(jnp.array([0.0, 1.0, 0.0], F32)[:, None], (1, hw))
            conv_w = jnp.concatenate([hyena_conv_w[j], ident], axis=1).T
            conv_b = jnp.concatenate([hyena_conv_b[j], jnp.zeros((hw,), F32)])[:, None]
            hy = _proj_t(w_a, u.reshape(b, seq, d), conv_w, conv_b)
            halves = _hyena_filters_t(seq, hw, hyena_w1[j], hyena_b1[j], hyena_w2[j], hyena_b2[j],
                                      hyena_w3[j], hyena_b3[j], hyena_w4[j], hyena_freq[j])
            kr, ki = _filter_spectrum(halves, consts)
            skip = hyena_skip[j][:, :, None, None]
            z = _hyena_conv(False, hy, 0, hy, hw, hw, skip[0], kr, ki, 0, consts)
            ya_t = _hyena_conv(True, z, 0, hy, 2 * hw, 3 * hw, skip[1], kr, ki, 1, consts)
            proj_b = _matmul_bf16(u, w_all[i:i + 1, :, 4 * hw:], 0, 0, 4 * hw).reshape(-1, b, seq, LANES)
            yb = _dilated_attention(proj_b, dil_bias, hw // HEAD_DIM)
            ys, ys_t = [yb.reshape(-1, m, LANES)], [ya_t]
            ws = [w_out[i, hw:].astype(BF16), w_out[i, :hw].astype(BF16)]
        else:
            proj = _matmul_bf16(u, w_all, i, 0, 4 * d_inner).reshape(-1, b, seq, LANES)
            yc = _neighbourhood_attention(proj, _na_bias(na_rpb[j], seq // GRID_W), d_inner // HEAD_DIM)
            ys, ys_t = [yc.reshape(-1, m, LANES)], []
            ws = [w_out[i].astype(BF16)]
        g_next = norm_pre[i + 1] if i + 1 < depth else norm_pre[i]
        h, u = _out_block(ys, ys_t, ws, h, p_all, i, ple_proj[i].astype(BF16), ple_gate[i].astype(BF16),
                          norm_post[i], ple_norm[i], g_next)
    return h.reshape(b, seq, d)
```

```python
import functools
import math

import numpy as np
import jax
import jax.numpy as jnp
from jax import lax
from jax.experimental import pallas as pl
from jax.experimental.pallas import tpu as pltpu

F32 = jnp.float32
BF16 = jnp.bfloat16

HEAD_DIM = 64
LANES = 128
HYENA_EMB_DIM = 33
HYENA_FAST_DECAY = 0.3
HYENA_SLOW_DECAY = 1.5
HYENA_TARGET = 1e-2
DILATED_PATTERNS = ((128, 1), (512, 4), (2048, 16))
GRID_W = 64
NA_ROWS = 8
NA_COLS = 16
REL_BUCKETS = 32
REL_MAX_DIST = 1024
RMS_EPS = 1e-6
NEG_INF = -1e30
ATT_QB = 128
ATT_HALF = 64
ATT_KW = ATT_QB + 2 * ATT_HALF
ATT_UNROLL = 8
ATT_STEP = 4
assert all(b[1] == a[1] * ATT_STEP for a, b in zip(DILATED_PATTERNS, DILATED_PATTERNS[1:])) and DILATED_PATTERNS[0][1] == 1
NA_BATCH = 8
NA_UNROLL = 64
VMEM_LIMIT = 56 * 1024 * 1024


def _cparams(sem):
    return pltpu.CompilerParams(dimension_semantics=sem, vmem_limit_bytes=VMEM_LIMIT)


def _rms(x, g):
    return x * lax.rsqrt(jnp.mean(x * x, axis=-1, keepdims=True) + RMS_EPS) * g


def _rmsnorm_kernel(x_ref, g_ref, o_ref):
    o_ref[...] = _rms(x_ref[...], g_ref[...]).astype(o_ref.dtype)


def _rmsnorm_bf16(x2d, g, tm=1024):
    m, d = x2d.shape
    return pl.pallas_call(
        _rmsnorm_kernel,
        grid=(m // tm,),
        in_specs=[pl.BlockSpec((tm, d), lambda i: (i, 0)), pl.BlockSpec((1, d), lambda i: (0, 0))],
        out_specs=pl.BlockSpec((tm, d), lambda i: (i, 0)),
        out_shape=jax.ShapeDtypeStruct((m, d), BF16),
        compiler_params=_cparams(("parallel",)),
        name="rmsnorm",
    )(x2d, g.reshape(1, d))


def _matmul_kernel(a_ref, b_ref, o_ref):
    res = jnp.dot(a_ref[...], b_ref[...], preferred_element_type=F32).astype(o_ref.dtype)
    for c in range(o_ref.shape[0]):
        o_ref[c] = res[:, c * LANES:(c + 1) * LANES]


def _matmul_bf16(a, w, layer, col0, n, tm=2048, tn=1024):
    m, k = a.shape
    return pl.pallas_call(
        _matmul_kernel,
        grid=(m // tm, n // tn),
        in_specs=[pl.BlockSpec((tm, k), lambda i, j: (i, 0)), pl.BlockSpec((None, k, tn), lambda i, j: (layer, 0, col0 // tn + j))],
        out_specs=pl.BlockSpec((tn // LANES, tm, LANES), lambda i, j: (j, i, 0)),
        out_shape=jax.ShapeDtypeStruct((n // LANES, m, LANES), BF16),
        compiler_params=_cparams(("parallel", "parallel")),
        name="in_proj",
    )(a, w)


def _proj_t_kernel(w_ref, u_ref, cw_ref, cb_ref, o_ref):
    p = lax.dot_general(w_ref[...], u_ref[0], (((1,), (1,)), ((), ())), preferred_element_type=F32)
    seq = p.shape[1]
    t = lax.broadcasted_iota(jnp.int32, p.shape, 1)
    prev = jnp.where(t == 0, 0.0, pltpu.roll(p, 1, axis=1))
    nxt = jnp.where(t == seq - 1, 0.0, pltpu.roll(p, seq - 1, axis=1))
    cw = cw_ref[...]
    o_ref[0] = (prev * cw[:, 0:1] + p * cw[:, 1:2] + nxt * cw[:, 2:3] + cb_ref[...]).astype(o_ref.dtype)


def _proj_t(w_t, u, conv_w, conv_b, tc=256):
    c, d = w_t.shape
    b, seq, _ = u.shape
    return pl.pallas_call(
        _proj_t_kernel,
        grid=(b, c // tc),
        in_specs=[
            pl.BlockSpec((tc, d), lambda i, j: (j, 0)),
            pl.BlockSpec((1, seq, d), lambda i, j: (i, 0, 0)),
            pl.BlockSpec((tc, 3), lambda i, j: (j, 0)),
            pl.BlockSpec((tc, 1), lambda i, j: (j, 0)),
        ],
        out_specs=pl.BlockSpec((1, tc, seq), lambda i, j: (i, j, 0)),
        out_shape=jax.ShapeDtypeStruct((b, c, seq), BF16),
        compiler_params=_cparams(("parallel", "parallel")),
        name="in_proj_t_conv3",
    )(w_t, u, conv_w, conv_b)


def _hyena_filter_kernel(z_ref, w1_ref, b1_ref, w2_ref, b2_ref, w3_ref, b3_ref, fq_ref, w4_ref, dl_ref, t_ref,
                         o_ref, a_scr):
    hp = lax.Precision.HIGHEST

    @pl.when((pl.program_id(0) == 0) & (pl.program_id(1) == 0))
    def _():
        for d in range(2):
            a = z_ref[d]
            for w_ref, b_ref in ((w1_ref, b1_ref), (w2_ref, b2_ref), (w3_ref, b3_ref)):
                a = jnp.sin(fq_ref[...] * (jnp.dot(w_ref[...], a, precision=hp, preferred_element_type=F32) + b_ref[...]))
            a_scr[d] = a

    seq = o_ref.shape[-1]
    proj = lambda d: jnp.dot(w4_ref[0, d].astype(BF16), a_scr[d].astype(BF16), preferred_element_type=F32)
    lo = proj(0) * jnp.exp(-t_ref[0] * dl_ref[...])
    hi = proj(1) * jnp.exp(-t_ref[1] * dl_ref[...])
    hi = jnp.where(lax.broadcasted_iota(jnp.int32, (1, seq), 1) == 0, 0.0, hi)
    norm = jnp.sum(jnp.abs(lo), axis=1, keepdims=True) + jnp.sum(jnp.abs(hi), axis=1, keepdims=True)
    o_ref[0, 0] = (lo / norm).astype(o_ref.dtype)
    o_ref[1, 0] = (hi / norm).astype(o_ref.dtype)


def _hyena_filters_t(seq, c, w1, b1, w2, b2, w3, b3, w4, freq, tc=256):
    tc = min(tc, c)
    fo = w2.shape[0]
    emb = -(-HYENA_EMB_DIM // 8) * 8
    t = jnp.linspace(0.0, 1.0, seq, dtype=F32)
    bands = (HYENA_EMB_DIM - 1) // 2
    fr = jnp.linspace(1e-4, bands - 1, bands, dtype=F32)[:, None]
    mirror = (seq - jnp.arange(seq)) % seq
    tables, times = [], []
    for pos, tp in ((jnp.arange(seq), t), (mirror, jnp.roll(t[::-1], 1))):
        wpos = (2.0 * math.pi / seq) * pos.astype(F32)[None, :]
        tp = tp[None, :]
        z = jnp.concatenate([tp, jnp.cos(fr * wpos), -jnp.sin(fr * wpos),
                             jnp.zeros((emb - HYENA_EMB_DIM, seq), F32)], axis=0)
        tables.append(z)
        times.append(tp)
    z = jnp.stack(tables)
    tt = jnp.stack(times)
    w1_t = jnp.pad(w1.astype(F32).T, ((0, 0), (0, emb - HYENA_EMB_DIM)))
    w4_t = jnp.transpose(w4.astype(F32).reshape(fo, 2, 2, c), (1, 2, 3, 0))
    min_decay = math.log(HYENA_TARGET) / HYENA_SLOW_DECAY
    max_decay = math.log(HYENA_TARGET) / HYENA_FAST_DECAY
    deltas = jnp.abs(jnp.linspace(min_decay, max_decay, c, dtype=F32))[:, None]
    col = lambda v: v.astype(F32)[:, None]
    full = lambda a: pl.BlockSpec(a.shape, lambda f, i: (0,) * a.ndim)
    args = (z, w1_t, col(b1), w2.astype(F32).T, col(b2), w3.astype(F32).T, col(b3), col(freq))
    return pl.pallas_call(
        _hyena_filter_kernel,
        grid=(2, c // tc),
        in_specs=[full(a) for a in args]
                 + [pl.BlockSpec((1, 2, tc, fo), lambda f, i: (f, 0, i, 0)),
                    pl.BlockSpec((tc, 1), lambda f, i: (i, 0)), full(tt)],
        out_specs=pl.BlockSpec((2, 1, tc, seq), lambda f, i: (0, f, i, 0)),
        out_shape=jax.ShapeDtypeStruct((2, 2, c, seq), BF16),
        scratch_shapes=[pltpu.VMEM((2, fo, seq), F32)],
        compiler_params=_cparams(("arbitrary", "arbitrary")),
        name="hyena_filter_mlp",
    )(*args, w4_t, deltas, tt)


def _fft_consts(seq):
    n_fft = 2 * seq
    n1 = n_fft // LANES
    nb = n1 // 2 + 1
    kp = -(-nb // 16) * 16
    k1 = jnp.arange(kp, dtype=jnp.int32)
    keep = (k1 < nb)[:, None]
    two_pi = 2.0 * math.pi

    th = ((k1[:, None] * jnp.arange(n1, dtype=jnp.int32)[None, :]) % n1).astype(F32) * (two_pi / n1)
    f1 = jnp.concatenate([jnp.where(keep, jnp.cos(th), 0.0), jnp.where(keep, -jnp.sin(th), 0.0)], axis=0)
    ph = (k1[:, None] * jnp.arange(LANES, dtype=jnp.int32)[None, :]).astype(F32) * (two_pi / n_fft)
    tw = jnp.stack([jnp.cos(ph), -jnp.sin(ph)])
    om = ((jnp.arange(LANES, dtype=jnp.int32)[:, None] * jnp.arange(LANES, dtype=jnp.int32)[None, :]) % LANES
          ).astype(F32) * (two_pi / LANES)
    cplx = lambda c, s: jnp.concatenate([jnp.concatenate([c, s], axis=1), jnp.concatenate([-s, c], axis=1)], axis=0)
    f2 = cplx(jnp.cos(om), -jnp.sin(om))
    f2i = cplx(jnp.cos(om), jnp.sin(om))
    weight = jnp.where((k1 == 0) | (k1 == nb - 1), 1.0, 2.0)[:, None]
    g = jnp.concatenate([jnp.where(keep, weight * jnp.cos(th[:, :n1 // 2]), 0.0),
                         jnp.where(keep, -weight * jnp.sin(th[:, :n1 // 2]), 0.0)], axis=0).T
    return dict(f1=f1.astype(BF16), tw=tw.astype(BF16), f2=f2.astype(BF16), f2i=f2i.astype(BF16), g=g.astype(BF16), kp=kp)


def _per_channel_matmul(mat_ref, planes, out_scr, ct):
    for c in range(0, ct, 2):
        pair = jnp.concatenate([planes(c), planes(c + 1)], axis=1)
        res = jnp.dot(mat_ref[...], pair, preferred_element_type=F32)
        out_scr[c] = res[:, :LANES].astype(out_scr.dtype)
        out_scr[c + 1] = res[:, LANES:].astype(out_scr.dtype)


def _twiddle_stage2(a, tw_ref, f2_ref):
    ct, kp2, _ = a.shape
    kp = kp2 // 2
    ar, ai = a[:, :kp], a[:, kp:]
    twr, twi = tw_ref[0], tw_ref[1]
    st = jnp.concatenate([ar * twr - ai * twi, ar * twi + ai * twr], axis=2)
    p = jnp.dot(st.reshape(ct * kp, 2 * LANES), f2_ref[...], preferred_element_type=F32).reshape(ct, kp, 2 * LANES)
    return p[:, :, :LANES].astype(BF16), p[:, :, LANES:].astype(BF16)


def _stage2_twiddle_inv(yr, yi, tw_ref, f2i_ref):
    ct, kp, _ = yr.shape
    st = jnp.concatenate([yr, yi], axis=2)
    p = jnp.dot(st.reshape(ct * kp, 2 * LANES), f2i_ref[...], preferred_element_type=F32).reshape(ct, kp, 2 * LANES)
    tr, ti = p[:, :, :LANES].astype(BF16), p[:, :, LANES:].astype(BF16)
    twr, twi = tw_ref[0], tw_ref[1]
    return jnp.concatenate([tr * twr + ti * twi, ti * twr - tr * twi], axis=1)


def _filter_spec_kernel(lo_ref, hi_ref, f1_ref, tw_ref, f2_ref, kr_ref, ki_ref, a_scr):
    ct = lo_ref.shape[2]
    planes = lambda c: jnp.concatenate([lo_ref[0, 0, c], hi_ref[0, 0, c]], axis=0)
    _per_channel_matmul(f1_ref, planes, a_scr, ct)
    xr, xi = _twiddle_stage2(a_scr[...], tw_ref, f2_ref)
    n_fft = f1_ref.shape[1] * LANES
    scale = 1.0 / n_fft
    kr_ref[0] = xr * scale
    ki_ref[0] = xi * scale


def _filter_spectrum(halves, consts, ct=128):
    _, nf, c, seq = halves.shape
    ct = min(ct, c)
    kp = consts["kp"]
    h5 = halves.reshape(2, nf, c, seq // LANES, LANES)
    half = lambda s: pl.BlockSpec((1, 1, ct, seq // LANES, LANES), lambda f, i: (s, f, i, 0, 0))
    full = lambda a: pl.BlockSpec(a.shape, lambda f, i: (0,) * a.ndim)
    spec = jax.ShapeDtypeStruct((nf, c, kp, LANES), BF16)
    out = pl.BlockSpec((1, ct, kp, LANES), lambda f, i: (f, i, 0, 0))
    return pl.pallas_call(
        _filter_spec_kernel,
        grid=(nf, c // ct),
        in_specs=[half(0), half(1), full(consts["f1"]), full(consts["tw"]), full(consts["f2"])],
        out_specs=[out, out],
        out_shape=[spec, spec],
        scratch_shapes=[pltpu.VMEM((ct, 2 * kp, LANES), BF16)],
        compiler_params=_cparams(("parallel", "parallel")),
        name="hyena_filter_spectrum",
    )(h5, h5, consts["f1"], consts["tw"], consts["f2"])


def _hyena_conv_kernel(second, x_ref, m_ref, e_ref, skip_ref, kr_ref, ki_ref, f1_ref, tw_ref, f2_ref, f2i_ref, g_ref,
                       o_ref, a_scr, t_scr, y_scr):
    ct = x_ref.shape[1]
    _per_channel_matmul(f1_ref, lambda c: x_ref[0, c], a_scr, ct)
    xr, xi = _twiddle_stage2(a_scr[...], tw_ref, f2_ref)
    kr, ki = kr_ref[0], ki_ref[0]
    t_scr[...] = _stage2_twiddle_inv(xr * kr - xi * ki, xr * ki + xi * kr, tw_ref, f2i_ref)

    _per_channel_matmul(g_ref, lambda c: t_scr[c], y_scr, ct)

    out = m_ref[0].astype(F32) * (y_scr[...] + x_ref[0].astype(F32) * skip_ref[...])
    if second:
        e = e_ref[0].astype(F32)
        out = out * (e * jax.nn.sigmoid(e))
    o_ref[0] = out.astype(o_ref.dtype)


def _hyena_conv(second, x, x_row0, hy, m_row0, e_row0, skip, kr, ki, filt, consts, ct=128):
    b, _, seq = x.shape
    c = kr.shape[1]
    ct = min(ct, c)
    kp = consts["kp"]
    n1h = seq // LANES
    view = lambda a: a.reshape(a.shape[0], a.shape[1], n1h, LANES)
    rows = lambda blk0: pl.BlockSpec((1, ct, n1h, LANES), lambda j, i: (i, blk0 + j, 0, 0))
    full = lambda a: pl.BlockSpec(a.shape, lambda j, i: (0,) * a.ndim)
    kspec = pl.BlockSpec((1, ct, kp, LANES), lambda j, i: (filt, j, 0, 0))
    mats = [consts["f1"][:, :n1h], consts["tw"], consts["f2"], consts["f2i"], consts["g"]]
    out = pl.pallas_call(
        functools.partial(_hyena_conv_kernel, second),
        grid=(c // ct, b),
        in_specs=[rows(x_row0 // ct), rows(m_row0 // ct), rows(e_row0 // ct),
                  pl.BlockSpec((ct, 1, 1), lambda j, i: (j, 0, 0)),
                  kspec, kspec] + [full(a) for a in mats],
        out_specs=pl.BlockSpec((1, ct, n1h, LANES), lambda j, i: (i, j, 0, 0)),
        out_shape=jax.ShapeDtypeStruct((b, c, n1h, LANES), BF16),
        scratch_shapes=[pltpu.VMEM((ct, 2 * kp, LANES), BF16), pltpu.VMEM((ct, 2 * kp, LANES), BF16),
                        pltpu.VMEM((ct, n1h, LANES), F32)],
        compiler_params=_cparams(("parallel", "parallel")),
        name="hyena_conv2" if second else "hyena_conv1",
    )(view(x), view(hy), view(hy), skip, kr, ki, *mats)
    return out.reshape(b, c, seq)


def _t5_bucket(rel):
    half_b = REL_BUCKETS // 2
    max_exact = half_b // 2
    ret = jnp.where(rel > 0, half_b, 0)
    n = jnp.abs(rel)
    nf = jnp.maximum(n, 1).astype(F32)
    large = max_exact + (jnp.log(nf / max_exact) / math.log(REL_MAX_DIST / max_exact)
                         * (half_b - max_exact)).astype(jnp.int32)
    large = jnp.minimum(large, half_b - 1)
    return ret + jnp.where(n < max_exact, n, large)


def _dilated_bias(rel_bias):
    rel = jnp.arange(ATT_KW)[None, :] - ATT_HALF - jnp.arange(ATT_QB)[:, None]
    col = jnp.arange(ATT_KW)
    edge = jnp.stack([col >= ATT_HALF, col >= 0, col < ATT_KW - ATT_HALF])
    out = []
    for _, dil in DILATED_PATTERNS:
        onehot = jax.nn.one_hot(_t5_bucket(rel * dil), REL_BUCKETS, dtype=F32)
        bias = jnp.einsum("qkb,bh->qkh", onehot, rel_bias.astype(F32), precision=lax.Precision.HIGHEST)
        bias = jnp.where((jnp.abs(rel) <= ATT_HALF)[:, :, None], bias, NEG_INF)
        bias = jnp.transpose(bias, (2, 0, 1))[:, None]
        out.append(jnp.where(edge[None, :, None, :], bias, NEG_INF))
    return jnp.stack(out)


def _head_select(first, second):
    lane = lax.broadcasted_iota(jnp.int32, first.shape[:-1] + (LANES,), len(first.shape) - 1)
    return jnp.where(lane < HEAD_DIM, first, second)


def _dilated_kernel(q_ref, k_ref, v_ref, g_ref, bias_ref, o_ref, *scratch):
    stage_a, stage_b = scratch[0:3], scratch[3:6]
    qd, kd, vd = scratch[6:9]
    last = len(DILATED_PATTERNS) - 1
    states = [scratch[9 + 3 * p:12 + 3 * p] for p in range(last)]
    seq = q_ref.shape[1]
    stage_a[0][...] = q_ref[0].astype(F32) * (HEAD_DIM ** -0.5)
    stage_a[1][...] = k_ref[0].astype(F32)
    stage_a[2][...] = v_ref[0].astype(F32)
    zero_halo = jnp.zeros((ATT_HALF, LANES), BF16)
    first_head = lax.broadcasted_iota(jnp.int32, (ATT_QB, LANES), 1) < HEAD_DIM

    src, dst = stage_a, stage_b
    for pi, (_, dil) in enumerate(DILATED_PATTERNS):
        n = seq // dil
        nblk = n // ATT_QB
        if pi > 0:
            prev_dil = dil // ATT_STEP
            for r1 in range(prev_dil):
                for r2 in range(ATT_STEP):
                    r = r1 + prev_dil * r2
                    for a_src, a_dst in zip(src, dst):
                        a_dst[r * n:(r + 1) * n, :] = a_src[pl.ds(r1 * n * ATT_STEP + r2, n, stride=ATT_STEP), :]
            src, dst = dst, src
        stride_k = n + 2 * ATT_HALF
        qd[...] = src[0][...].astype(BF16)
        for r in range(dil):
            base = r * stride_k
            for a_src, a_dst in ((src[1], kd), (src[2], vd)):
                a_dst[base:base + ATT_HALF, :] = zero_halo
                a_dst[base + ATT_HALF:base + ATT_HALF + n, :] = a_src[r * n:(r + 1) * n, :].astype(BF16)
                a_dst[base + ATT_HALF + n:base + stride_k, :] = zero_halo
        if pi == last:
            states.append(dst)
        acc_w, mx_w, den_w = states[pi]

        def block_group(it, pi=pi, nblk=nblk, acc_w=acc_w, mx_w=mx_w, den_w=den_w):
            chains = []
            for u in range(ATT_UNROLL):
                blk = it * ATT_UNROLL + u
                r = blk // nblk
                qb = blk % nblk
                q0 = blk * ATT_QB
                k0 = q0 + r * (2 * ATT_HALF)
                edge = 0 if qb == 0 else (2 if qb == nblk - 1 else 1)
                qblk = qd[pl.ds(q0, ATT_QB), :]
                qcat = jnp.concatenate([jnp.where(first_head, qblk, jnp.zeros_like(qblk)),
                                        jnp.where(first_head, jnp.zeros_like(qblk), qblk)], axis=0)
                s = lax.dot_general(qcat, kd[pl.ds(k0, ATT_KW), :], (((1,), (1,)), ((), ())),
                                    preferred_element_type=F32)
                chains.append((q0, k0, edge, s))
            probs = []
            for q0, k0, edge, s in chains:
                s = s + jnp.concatenate([bias_ref[pi, 0, edge], bias_ref[pi, 1, edge]], axis=0)
                m = jnp.max(s, axis=1, keepdims=True)
                p = jnp.exp(s - m)
                probs.append((p.astype(BF16), m, jnp.sum(p, axis=1, keepdims=True)))
            for (q0, k0, edge, s), (p, m, l) in zip(chains, probs):
                pv = jnp.dot(p, vd[pl.ds(k0, ATT_KW), :], preferred_element_type=F32)
                acc_w[pl.ds(q0, ATT_QB), :] = _head_select(pv[:ATT_QB], pv[ATT_QB:])
                mx_w[pl.ds(q0, ATT_QB), :] = _head_select(m[:ATT_QB], m[ATT_QB:])
                den_w[pl.ds(q0, ATT_QB), :] = _head_select(l[:ATT_QB], l[ATT_QB:])

        for it in range(dil * nblk // ATT_UNROLL):
            block_group(it)

    for pi in range(len(DILATED_PATTERNS) - 1, 0, -1):
        dil = DILATED_PATTERNS[pi][1]
        n = seq // dil
        prev_dil = dil // ATT_STEP
        acc_c, mx_c, den_c = states[pi - 1]
        acc_f, mx_f, den_f = states[pi]
        for r1 in range(prev_dil):
            for r2 in range(ATT_STEP):
                r = r1 + prev_dil * r2
                rows = pl.ds(r1 * n * ATT_STEP + r2, n, stride=ATT_STEP)
                fine = slice(r * n, (r + 1) * n)
                m_old = mx_c[rows, :]
                m_new = mx_f[fine, :]
                m_all = jnp.maximum(m_old, m_new)
                decay = jnp.exp(-jnp.abs(m_old - m_new))
                a_old = jnp.where(m_old >= m_new, 1.0, decay)
                a_new = jnp.where(m_old >= m_new, decay, 1.0)
                acc_c[rows, :] = acc_c[rows, :] * a_old + acc_f[fine, :] * a_new
                den_c[rows, :] = den_c[rows, :] * a_old + den_f[fine, :] * a_new
                mx_c[rows, :] = m_all

    acc, _, den = states[0]
    g = g_ref[0].astype(F32)
    o_ref[0] = (acc[...] / den[...] * (g * jax.nn.sigmoid(g))).astype(o_ref.dtype)


def _dilated_attention(proj, bias, n_heads):
    _, b, seq, _ = proj.shape
    npair = n_heads // 2
    max_dil = max(d for _, d in DILATED_PATTERNS)
    kd_rows = seq + 2 * ATT_HALF * max_dil
    blk = lambda off: pl.BlockSpec((None, 1, seq, LANES), lambda i, j: (off * npair + j, i, 0, 0))
    state = pltpu.VMEM((seq, LANES), F32)
    return pl.pallas_call(
        _dilated_kernel,
        grid=(b, npair),
        in_specs=[blk(0), blk(1), blk(2), blk(3),
                  pl.BlockSpec((len(DILATED_PATTERNS), 2, 3, ATT_QB, ATT_KW), lambda i, j: (0, j, 0, 0, 0))],
        out_specs=pl.BlockSpec((None, 1, seq, LANES), lambda i, j: (j, i, 0, 0)),
        out_shape=jax.ShapeDtypeStruct((npair, b, seq, LANES), BF16),
        scratch_shapes=[state] * 6
                       + [pltpu.VMEM((seq, LANES), BF16), pltpu.VMEM((kd_rows, LANES), BF16),
                          pltpu.VMEM((kd_rows, LANES), BF16)]
                       + [state] * (3 * (len(DILATED_PATTERNS) - 1)),
        compiler_params=_cparams(("parallel", "parallel")),
        name="dilated_attention",
    )(proj, proj, proj, proj, bias)


def _na_bias(rpb, rows):
    assert min(NA_ROWS, rows) % 2 == 0
    cols = np.arange(GRID_W)
    cs = np.clip(cols - NA_COLS // 2, 0, GRID_W - NA_COLS)
    kc = np.arange(GRID_W)
    inside = (kc[None, :] >= cs[:, None]) & (kc[None, :] < cs[:, None] + NA_COLS)
    col_idx = kc[None, :] - cols[:, None] + NA_COLS - 1
    onehot = (col_idx[:, :, None] == np.arange(2 * NA_COLS - 1)[None, None, :]).astype(np.float32)
    e = jnp.einsum("hrd,cqd->hrcq", rpb.astype(F32), jnp.asarray(onehot), precision=lax.Precision.HIGHEST)
    e = jnp.where(jnp.asarray(inside)[None, None], e, NEG_INF)
    return jnp.concatenate([e[:, :-1], e[:, 1:]], axis=-1)


def _na_kernel(q_ref, k_ref, v_ref, g_ref, bias_ref, o_ref):
    seq = q_ref.shape[1]
    rows = seq // GRID_W
    kr_n = min(NA_ROWS, rows)
    first_head = lax.broadcasted_iota(jnp.int32, (GRID_W, LANES), 1) < HEAD_DIM

    def row_group(it):
        def scores(batch):
            out = []
            for u in range(batch * NA_BATCH, (batch + 1) * NA_BATCH):
                r = it * NA_UNROLL + u
                rs = min(max(r - kr_n // 2, 0), rows - kr_n)
                shift = rs - r + NA_ROWS - 1
                q0 = r * GRID_W
                k0 = rs * GRID_W
                qrow = q_ref[0, pl.ds(q0, GRID_W), :] * (HEAD_DIM ** -0.5)
                qcat = jnp.concatenate([jnp.where(first_head, qrow, jnp.zeros_like(qrow)),
                                        jnp.where(first_head, jnp.zeros_like(qrow), qrow)], axis=0)
                s = lax.dot_general(qcat, k_ref[0, pl.ds(k0, kr_n * GRID_W), :], (((1,), (1,)), ((), ())),
                                    preferred_element_type=F32)
                out.append((q0, k0, shift, s))
            return out

        def finish(chains):
            probs = []
            for q0, k0, shift, s in chains:
                bias = [jnp.concatenate([bias_ref[h, shift + i] for i in range(0, kr_n, 2)], axis=1)
                        for h in range(2)]
                s = s + jnp.concatenate(bias, axis=0)
                p = jnp.exp(s - jnp.max(s, axis=1, keepdims=True))
                probs.append((p.astype(BF16), jnp.sum(p, axis=1, keepdims=True)))
            for (q0, k0, shift, s), (p, l) in zip(chains, probs):
                o = jnp.dot(p, v_ref[0, pl.ds(k0, kr_n * GRID_W), :], preferred_element_type=F32) / l
                g = g_ref[0, pl.ds(q0, GRID_W), :].astype(F32)
                o = _head_select(o[:GRID_W], o[GRID_W:]) * (g * jax.nn.sigmoid(g))
                o_ref[0, pl.ds(q0, GRID_W), :] = o.astype(o_ref.dtype)

        n_batches = NA_UNROLL // NA_BATCH
        pending = scores(0)
        for batch in range(n_batches):
            ahead = scores(batch + 1) if batch + 1 < n_batches else None
            finish(pending)
            pending = ahead

    for it in range(rows // NA_UNROLL):
        row_group(it)


def _neighbourhood_attention(proj, bias, n_heads):
    _, b, seq, _ = proj.shape
    npair = n_heads // 2
    blk = lambda off: pl.BlockSpec((None, 1, seq, LANES), lambda j, i: (off * npair + j, i, 0, 0))
    return pl.pallas_call(
        _na_kernel,
        grid=(npair, b),
        in_specs=[blk(0), blk(1), blk(2), blk(3),
                  pl.BlockSpec((2,) + bias.shape[1:], lambda j, i: (j, 0, 0, 0))],
        out_specs=pl.BlockSpec((None, 1, seq, LANES), lambda j, i: (j, i, 0, 0)),
        out_shape=jax.ShapeDtypeStruct((npair, b, seq, LANES), BF16),
        compiler_params=_cparams(("parallel", "parallel")),
        name="neighbourhood_attention",
    )(proj, proj, proj, proj, bias)


def _out_kernel(n_y, n_t, *refs):
    n = n_y + n_t
    y_refs = refs[:n]
    w_refs = refs[n:2 * n]
    h_ref, p_ref, wp_ref, wg_ref, gpost_ref, gple_ref, gnext_ref, h_out, u_out = refs[2 * n:]
    t = None
    for idx, (y_ref, w_ref) in enumerate(zip(y_refs, w_refs)):
        if idx < n_y:
            y = jnp.concatenate([y_ref[c] for c in range(y_ref.shape[0])], axis=1)
            part = jnp.dot(y, w_ref[...], preferred_element_type=F32)
        else:
            part = lax.dot_general(y_ref[0], w_ref[...], (((0,), (0,)), ((), ())), preferred_element_type=F32)
        t = part if t is None else t + part
    h = h_ref[...] + _rms(t, gpost_ref[...])
    e = _rms(jnp.dot(p_ref[...].astype(BF16), wp_ref[...], preferred_element_type=F32), gple_ref[...])
    gate = jax.nn.sigmoid(jnp.dot(h.astype(BF16), wg_ref[...], preferred_element_type=F32))
    h = h + gate * e
    h_out[...] = h
    u_out[...] = _rms(h, gnext_ref[...]).astype(u_out.dtype)


def _out_block(ys, ys_t, ws, h, p, layer, w_ple, w_gate, g_post, g_ple, g_next, tm=1024):
    m, d = h.shape
    row = lambda a: pl.BlockSpec((tm, a.shape[1]), lambda i: (i, 0))
    slab = lambda a: pl.BlockSpec((a.shape[0], tm, LANES), lambda i: (0, i, 0))
    p_spec = pl.BlockSpec((None, tm, p.shape[2]), lambda i: (layer, i, 0))
    full = lambda a: pl.BlockSpec(a.shape, lambda i: (0, 0))
    vec = lambda g: g.reshape(1, d)

    def col(a):
        tiles = a.shape[2] // tm
        return pl.BlockSpec((1, a.shape[1], tm), lambda i: (i // tiles, 0, i % tiles))

    return pl.pallas_call(
        functools.partial(_out_kernel, len(ys), len(ys_t)),
        grid=(m // tm,),
        in_specs=[slab(y) for y in ys] + [col(y) for y in ys_t] + [full(w) for w in ws]
                 + [row(h), p_spec, full(w_ple), full(w_gate), full(vec(g_post)), full(vec(g_ple)), full(vec(g_next))],
        out_specs=[pl.BlockSpec((tm, d), lambda i: (i, 0)), pl.BlockSpec((tm, d), lambda i: (i, 0))],
        out_shape=[jax.ShapeDtypeStruct((m, d), F32), jax.ShapeDtypeStruct((m, d), BF16)],
        compiler_params=_cparams(("parallel",)),
        name="out_proj_residual_ple",
    )(*ys, *ys_t, *ws, h, p, w_ple, w_gate, vec(g_post), vec(g_ple), vec(g_next))


def kernel(x, p, w_in, w_out, norm_pre, norm_post, hyena_conv_w, hyena_conv_b, hyena_w1, hyena_b1, hyena_w2, hyena_b2, hyena_w3, hyena_b3, hyena_w4, hyena_freq, hyena_skip, rel_bias, na_rpb, ple_proj, ple_norm, ple_gate):
    b, seq, d = x.shape
    depth = w_in.shape[0]
    d_inner = w_out.shape[1]
    hw = d_inner // 2
    m = b * seq
    consts = _fft_consts(seq)
    dil_bias = _dilated_bias(rel_bias)
    h = x.reshape(m, d)
    u = _rmsnorm_bf16(h, norm_pre[0])
    w_all = w_in.astype(BF16)
    p_all = p.reshape(depth, m, -1)
    for i in range(depth):
        j = i // 2
        if i % 2 == 0:
            w_a = w_all[i, :, :4 * hw].T
            ident = jnp.tile(jnp.array([0.0, 1.0, 0.0], F32)[:, None], (1, hw))
            conv_w = jnp.concatenate([hyena_conv_w[j], ident], axis=1).T
            conv_b = jnp.concatenate([hyena_conv_b[j], jnp.zeros((hw,), F32)])[:, None]
            hy = _proj_t(w_a, u.reshape(b, seq, d), conv_w, conv_b)
            halves = _hyena_filters_t(seq, hw, hyena_w1[j], hyena_b1[j], hyena_w2[j], hyena_b2[j],
                                      hyena_w3[j], hyena_b3[j], hyena_w4[j], hyena_freq[j])
            kr, ki = _filter_spectrum(halves, consts)
            skip = hyena_skip[j][:, :, None, None]
            z = _hyena_conv(False, hy, 0, hy, hw, hw, skip[0], kr, ki, 0, consts)
            ya_t = _hyena_conv(True, z, 0, hy, 2 * hw, 3 * hw, skip[1], kr, ki, 1, consts)
            proj_b = _matmul_bf16(u, w_all[i:i + 1, :, 4 * hw:], 0, 0, 4 * hw).reshape(-1, b, seq, LANES)
            yb = _dilated_attention(proj_b, dil_bias, hw // HEAD_DIM)
            ys, ys_t = [yb.reshape(-1, m, LANES)], [ya_t]
            ws = [w_out[i, hw:].astype(BF16), w_out[i, :hw].astype(BF16)]
        else:
            proj = _matmul_bf16(u, w_all, i, 0, 4 * d_inner).reshape(-1, b, seq, LANES)
            yc = _neighbourhood_attention(proj, _na_bias(na_rpb[j], seq // GRID_W), d_inner // HEAD_DIM)
            ys, ys_t = [yc.reshape(-1, m, LANES)], []
            ws = [w_out[i].astype(BF16)]
        g_next = norm_pre[i + 1] if i + 1 < depth else norm_pre[i]
        h, u = _out_block(ys, ys_t, ws, h, p_all, i, ple_proj[i].astype(BF16), ple_gate[i].astype(BF16),
                          norm_post[i], ple_norm[i], g_next)
    return h.reshape(b, seq, d)
```

```python
import functools
import math

import numpy as np
import jax
import jax.numpy as jnp
from jax import lax
from jax.experimental import pallas as pl
from jax.experimental.pallas import tpu as pltpu

F32 = jnp.float32
BF16 = jnp.bfloat16

HEAD_DIM = 64
LANES = 128
HYENA_EMB_DIM = 33
HYENA_FAST_DECAY = 0.3
HYENA_SLOW_DECAY = 1.5
HYENA_TARGET = 1e-2
DILATED_PATTERNS = ((128, 1), (512, 4), (2048, 16))
GRID_W = 64
NA_ROWS = 8
NA_COLS = 16
REL_BUCKETS = 32
REL_MAX_DIST = 1024
RMS_EPS = 1e-6
NEG_INF = -1e30
ATT_QB = 128
ATT_HALF = 64
ATT_KW = ATT_QB + 2 * ATT_HALF
ATT_UNROLL = 4
ATT_STEP = 4
assert all(b[1] == a[1] * ATT_STEP for a, b in zip(DILATED_PATTERNS, DILATED_PATTERNS[1:])) and DILATED_PATTERNS[0][1] == 1
NA_BATCH = 4
NA_UNROLL = 64
VMEM_LIMIT = 56 * 1024 * 1024


def _cparams(sem):
    return pltpu.CompilerParams(dimension_semantics=sem, vmem_limit_bytes=VMEM_LIMIT)


def _rms(x, g):
    return x * lax.rsqrt(jnp.mean(x * x, axis=-1, keepdims=True) + RMS_EPS) * g


def _rmsnorm_kernel(x_ref, g_ref, o_ref):
    o_ref[...] = _rms(x_ref[...], g_ref[...]).astype(o_ref.dtype)


def _rmsnorm_bf16(x2d, g, tm=1024):
    m, d = x2d.shape
    return pl.pallas_call(
        _rmsnorm_kernel,
        grid=(m // tm,),
        in_specs=[pl.BlockSpec((tm, d), lambda i: (i, 0)), pl.BlockSpec((1, d), lambda i: (0, 0))],
        out_specs=pl.BlockSpec((tm, d), lambda i: (i, 0)),
        out_shape=jax.ShapeDtypeStruct((m, d), BF16),
        compiler_params=_cparams(("parallel",)),
        name="rmsnorm",
    )(x2d, g.reshape(1, d))


def _matmul_kernel(a_ref, b_ref, o_ref):
    res = jnp.dot(a_ref[...], b_ref[...], preferred_element_type=F32).astype(o_ref.dtype)
    for c in range(o_ref.shape[0]):
        o_ref[c] = res[:, c * LANES:(c + 1) * LANES]


def _matmul_bf16(a, w, layer, col0, n, tm=2048, tn=1024):
    m, k = a.shape
    return pl.pallas_call(
        _matmul_kernel,
        grid=(m // tm, n // tn),
        in_specs=[pl.BlockSpec((tm, k), lambda i, j: (i, 0)), pl.BlockSpec((None, k, tn), lambda i, j: (layer, 0, col0 // tn + j))],
        out_specs=pl.BlockSpec((tn // LANES, tm, LANES), lambda i, j: (j, i, 0)),
        out_shape=jax.ShapeDtypeStruct((n // LANES, m, LANES), BF16),
        compiler_params=_cparams(("parallel", "parallel")),
        name="in_proj",
    )(a, w)


def _proj_t_kernel(w_ref, u_ref, cw_ref, cb_ref, o_ref):
    p = lax.dot_general(w_ref[...], u_ref[0], (((1,), (1,)), ((), ())), preferred_element_type=F32)
    seq = p.shape[1]
    t = lax.broadcasted_iota(jnp.int32, p.shape, 1)
    prev = jnp.where(t == 0, 0.0, pltpu.roll(p, 1, axis=1))
    nxt = jnp.where(t == seq - 1, 0.0, pltpu.roll(p, seq - 1, axis=1))
    cw = cw_ref[...]
    o_ref[0] = (prev * cw[:, 0:1] + p * cw[:, 1:2] + nxt * cw[:, 2:3] + cb_ref[...]).astype(o_ref.dtype)


def _proj_t(w_t, u, conv_w, conv_b, tc=256):
    c, d = w_t.shape
    b, seq, _ = u.shape
    return pl.pallas_call(
        _proj_t_kernel,
        grid=(b, c // tc),
        in_specs=[
            pl.BlockSpec((tc, d), lambda i, j: (j, 0)),
            pl.BlockSpec((1, seq, d), lambda i, j: (i, 0, 0)),
            pl.BlockSpec((tc, 3), lambda i, j: (j, 0)),
            pl.BlockSpec((tc, 1), lambda i, j: (j, 0)),
        ],
        out_specs=pl.BlockSpec((1, tc, seq), lambda i, j: (i, j, 0)),
        out_shape=jax.ShapeDtypeStruct((b, c, seq), BF16),
        compiler_params=_cparams(("parallel", "parallel")),
        name="in_proj_t_conv3",
    )(w_t, u, conv_w, conv_b)


def _hyena_filter_kernel(z_ref, w1_ref, b1_ref, w2_ref, b2_ref, w3_ref, b3_ref, fq_ref, w4_ref, dl_ref, t_ref,
                         o_ref, a_scr):
    hp = lax.Precision.HIGHEST

    @pl.when((pl.program_id(0) == 0) & (pl.program_id(1) == 0))
    def _():
        for d in range(2):
            a = z_ref[d]
            for w_ref, b_ref in ((w1_ref, b1_ref), (w2_ref, b2_ref), (w3_ref, b3_ref)):
                a = jnp.sin(fq_ref[...] * (jnp.dot(w_ref[...], a, precision=hp, preferred_element_type=F32) + b_ref[...]))
            a_scr[d] = a

    seq = o_ref.shape[-1]
    proj = lambda d: jnp.dot(w4_ref[0, d].astype(BF16), a_scr[d].astype(BF16), preferred_element_type=F32)
    lo = proj(0) * jnp.exp(-t_ref[0] * dl_ref[...])
    hi = proj(1) * jnp.exp(-t_ref[1] * dl_ref[...])
    hi = jnp.where(lax.broadcasted_iota(jnp.int32, (1, seq), 1) == 0, 0.0, hi)
    norm = jnp.sum(jnp.abs(lo), axis=1, keepdims=True) + jnp.sum(jnp.abs(hi), axis=1, keepdims=True)
    o_ref[0, 0] = (lo / norm).astype(o_ref.dtype)
    o_ref[1, 0] = (hi / norm).astype(o_ref.dtype)


def _hyena_filters_t(seq, c, w1, b1, w2, b2, w3, b3, w4, freq, tc=256):
    tc = min(tc, c)
    fo = w2.shape[0]
    emb = -(-HYENA_EMB_DIM // 8) * 8
    t = jnp.linspace(0.0, 1.0, seq, dtype=F32)
    bands = (HYENA_EMB_DIM - 1) // 2
    fr = jnp.linspace(1e-4, bands - 1, bands, dtype=F32)[:, None]
    mirror = (seq - jnp.arange(seq)) % seq
    tables, times = [], []
    for pos, tp in ((jnp.arange(seq), t), (mirror, jnp.roll(t[::-1], 1))):
        wpos = (2.0 * math.pi / seq) * pos.astype(F32)[None, :]
        tp = tp[None, :]
        z = jnp.concatenate([tp, jnp.cos(fr * wpos), -jnp.sin(fr * wpos),
                             jnp.zeros((emb - HYENA_EMB_DIM, seq), F32)], axis=0)
        tables.append(z)
        times.append(tp)
    z = jnp.stack(tables)
    tt = jnp.stack(times)
    w1_t = jnp.pad(w1.astype(F32).T, ((0, 0), (0, emb - HYENA_EMB_DIM)))
    w4_t = jnp.transpose(w4.astype(F32).reshape(fo, 2, 2, c), (1, 2, 3, 0))
    min_decay = math.log(HYENA_TARGET) / HYENA_SLOW_DECAY
    max_decay = math.log(HYENA_TARGET) / HYENA_FAST_DECAY
    deltas = jnp.abs(jnp.linspace(min_decay, max_decay, c, dtype=F32))[:, None]
    col = lambda v: v.astype(F32)[:, None]
    full = lambda a: pl.BlockSpec(a.shape, lambda f, i: (0,) * a.ndim)
    args = (z, w1_t, col(b1), w2.astype(F32).T, col(b2), w3.astype(F32).T, col(b3), col(freq))
    return pl.pallas_call(
        _hyena_filter_kernel,
        grid=(2, c // tc),
        in_specs=[full(a) for a in args]
                 + [pl.BlockSpec((1, 2, tc, fo), lambda f, i: (f, 0, i, 0)),
                    pl.BlockSpec((tc, 1), lambda f, i: (i, 0)), full(tt)],
        out_specs=pl.BlockSpec((2, 1, tc, seq), lambda f, i: (0, f, i, 0)),
        out_shape=jax.ShapeDtypeStruct((2, 2, c, seq), BF16),
        scratch_shapes=[pltpu.VMEM((2, fo, seq), F32)],
        compiler_params=_cparams(("arbitrary", "arbitrary")),
        name="hyena_filter_mlp",
    )(*args, w4_t, deltas, tt)


def _fft_consts(seq):
    n_fft = 2 * seq
    n1 = n_fft // LANES
    nb = n1 // 2 + 1
    kp = -(-nb // 16) * 16
    k1 = jnp.arange(kp, dtype=jnp.int32)
    keep = (k1 < nb)[:, None]
    two_pi = 2.0 * math.pi

    th = ((k1[:, None] * jnp.arange(n1, dtype=jnp.int32)[None, :]) % n1).astype(F32) * (two_pi / n1)
    f1 = jnp.concatenate([jnp.where(keep, jnp.cos(th), 0.0), jnp.where(keep, -jnp.sin(th), 0.0)], axis=0)
    ph = (k1[:, None] * jnp.arange(LANES, dtype=jnp.int32)[None, :]).astype(F32) * (two_pi / n_fft)
    tw = jnp.stack([jnp.cos(ph), -jnp.sin(ph)])
    om = ((jnp.arange(LANES, dtype=jnp.int32)[:, None] * jnp.arange(LANES, dtype=jnp.int32)[None, :]) % LANES
          ).astype(F32) * (two_pi / LANES)
    cplx = lambda c, s: jnp.concatenate([jnp.concatenate([c, s], axis=1), jnp.concatenate([-s, c], axis=1)], axis=0)
    f2 = cplx(jnp.cos(om), -jnp.sin(om))
    f2i = cplx(jnp.cos(om), jnp.sin(om))
    weight = jnp.where((k1 == 0) | (k1 == nb - 1), 1.0, 2.0)[:, None]
    g = jnp.concatenate([jnp.where(keep, weight * jnp.cos(th[:, :n1 // 2]), 0.0),
                         jnp.where(keep, -weight * jnp.sin(th[:, :n1 // 2]), 0.0)], axis=0).T
    return dict(f1=f1.astype(BF16), tw=tw.astype(BF16), f2=f2.astype(BF16), f2i=f2i.astype(BF16), g=g.astype(BF16), kp=kp)


def _per_channel_matmul(mat_ref, planes, out_scr, ct):
    for c in range(0, ct, 2):
        pair = jnp.concatenate([planes(c), planes(c + 1)], axis=1)
        res = jnp.dot(mat_ref[...], pair, preferred_element_type=F32)
        out_scr[c] = res[:, :LANES].astype(out_scr.dtype)
        out_scr[c + 1] = res[:, LANES:].astype(out_scr.dtype)


def _twiddle_stage2(a, tw_ref, f2_ref):
    ct, kp2, _ = a.shape
    kp = kp2 // 2
    ar, ai = a[:, :kp], a[:, kp:]
    twr, twi = tw_ref[0], tw_ref[1]
    st = jnp.concatenate([ar * twr - ai * twi, ar * twi + ai * twr], axis=2)
    p = jnp.dot(st.reshape(ct * kp, 2 * LANES), f2_ref[...], preferred_element_type=F32).reshape(ct, kp, 2 * LANES)
    return p[:, :, :LANES].astype(BF16), p[:, :, LANES:].astype(BF16)


def _stage2_twiddle_inv(yr, yi, tw_ref, f2i_ref):
    ct, kp, _ = yr.shape
    st = jnp.concatenate([yr, yi], axis=2)
    p = jnp.dot(st.reshape(ct * kp, 2 * LANES), f2i_ref[...], preferred_element_type=F32).reshape(ct, kp, 2 * LANES)
    tr, ti = p[:, :, :LANES].astype(BF16), p[:, :, LANES:].astype(BF16)
    twr, twi = tw_ref[0], tw_ref[1]
    return jnp.concatenate([tr * twr + ti * twi, ti * twr - tr * twi], axis=1)


def _filter_spec_kernel(lo_ref, hi_ref, f1_ref, tw_ref, f2_ref, kr_ref, ki_ref, a_scr):
    ct = lo_ref.shape[2]
    planes = lambda c: jnp.concatenate([lo_ref[0, 0, c], hi_ref[0, 0, c]], axis=0)
    _per_channel_matmul(f1_ref, planes, a_scr, ct)
    xr, xi = _twiddle_stage2(a_scr[...], tw_ref, f2_ref)
    n_fft = f1_ref.shape[1] * LANES
    scale = 1.0 / n_fft
    kr_ref[0] = xr * scale
    ki_ref[0] = xi * scale


def _filter_spectrum(halves, consts, ct=128):
    _, nf, c, seq = halves.shape
    ct = min(ct, c)
    kp = consts["kp"]
    h5 = halves.reshape(2, nf, c, seq // LANES, LANES)
    half = lambda s: pl.BlockSpec((1, 1, ct, seq // LANES, LANES), lambda f, i: (s, f, i, 0, 0))
    full = lambda a: pl.BlockSpec(a.shape, lambda f, i: (0,) * a.ndim)
    spec = jax.ShapeDtypeStruct((nf, c, kp, LANES), BF16)
    out = pl.BlockSpec((1, ct, kp, LANES), lambda f, i: (f, i, 0, 0))
    return pl.pallas_call(
        _filter_spec_kernel,
        grid=(nf, c // ct),
        in_specs=[half(0), half(1), full(consts["f1"]), full(consts["tw"]), full(consts["f2"])],
        out_specs=[out, out],
        out_shape=[spec, spec],
        scratch_shapes=[pltpu.VMEM((ct, 2 * kp, LANES), BF16)],
        compiler_params=_cparams(("parallel", "parallel")),
        name="hyena_filter_spectrum",
    )(h5, h5, consts["f1"], consts["tw"], consts["f2"])


def _hyena_conv_kernel(second, x_ref, m_ref, e_ref, skip_ref, kr_ref, ki_ref, f1_ref, tw_ref, f2_ref, f2i_ref, g_ref,
                       o_ref, a_scr, t_scr, y_scr):
    ct = x_ref.shape[1]
    _per_channel_matmul(f1_ref, lambda c: x_ref[0, c], a_scr, ct)
    xr, xi = _twiddle_stage2(a_scr[...], tw_ref, f2_ref)
    kr, ki = kr_ref[0], ki_ref[0]
    t_scr[...] = _stage2_twiddle_inv(xr * kr - xi * ki, xr * ki + xi * kr, tw_ref, f2i_ref)

    _per_channel_matmul(g_ref, lambda c: t_scr[c], y_scr, ct)

    out = m_ref[0].astype(F32) * (y_scr[...] + x_ref[0].astype(F32) * skip_ref[...])
    if second:
        e = e_ref[0].astype(F32)
        out = out * (e * jax.nn.sigmoid(e))
    o_ref[0] = out.astype(o_ref.dtype)


def _hyena_conv(second, x, x_row0, hy, m_row0, e_row0, skip, kr, ki, filt, consts, ct=128):
    b, _, seq = x.shape
    c = kr.shape[1]
    ct = min(ct, c)
    kp = consts["kp"]
    n1h = seq // LANES
    view = lambda a: a.reshape(a.shape[0], a.shape[1], n1h, LANES)
    rows = lambda blk0: pl.BlockSpec((1, ct, n1h, LANES), lambda j, i: (i, blk0 + j, 0, 0))
    full = lambda a: pl.BlockSpec(a.shape, lambda j, i: (0,) * a.ndim)
    kspec = pl.BlockSpec((1, ct, kp, LANES), lambda j, i: (filt, j, 0, 0))
    mats = [consts["f1"][:, :n1h], consts["tw"], consts["f2"], consts["f2i"], consts["g"]]
    out = pl.pallas_call(
        functools.partial(_hyena_conv_kernel, second),
        grid=(c // ct, b),
        in_specs=[rows(x_row0 // ct), rows(m_row0 // ct), rows(e_row0 // ct),
                  pl.BlockSpec((ct, 1, 1), lambda j, i: (j, 0, 0)),
                  kspec, kspec] + [full(a) for a in mats],
        out_specs=pl.BlockSpec((1, ct, n1h, LANES), lambda j, i: (i, j, 0, 0)),
        out_shape=jax.ShapeDtypeStruct((b, c, n1h, LANES), BF16),
        scratch_shapes=[pltpu.VMEM((ct, 2 * kp, LANES), BF16), pltpu.VMEM((ct, 2 * kp, LANES), BF16),
                        pltpu.VMEM((ct, n1h, LANES), F32)],
        compiler_params=_cparams(("parallel", "parallel")),
        name="hyena_conv2" if second else "hyena_conv1",
    )(view(x), view(hy), view(hy), skip, kr, ki, *mats)
    return out.reshape(b, c, seq)


def _t5_bucket(rel):
    half_b = REL_BUCKETS // 2
    max_exact = half_b // 2
    ret = jnp.where(rel > 0, half_b, 0)
    n = jnp.abs(rel)
    nf = jnp.maximum(n, 1).astype(F32)
    large = max_exact + (jnp.log(nf / max_exact) / math.log(REL_MAX_DIST / max_exact)
                         * (half_b - max_exact)).astype(jnp.int32)
    large = jnp.minimum(large, half_b - 1)
    return ret + jnp.where(n < max_exact, n, large)


def _dilated_bias(rel_bias):
    rel = jnp.arange(ATT_KW)[None, :] - ATT_HALF - jnp.arange(ATT_QB)[:, None]
    col = jnp.arange(ATT_KW)
    edge = jnp.stack([col >= ATT_HALF, col >= 0, col < ATT_KW - ATT_HALF])
    out = []
    for _, dil in DILATED_PATTERNS:
        onehot = jax.nn.one_hot(_t5_bucket(rel * dil), REL_BUCKETS, dtype=F32)
        bias = jnp.einsum("qkb,bh->qkh", onehot, rel_bias.astype(F32), precision=lax.Precision.HIGHEST)
        bias = jnp.where((jnp.abs(rel) <= ATT_HALF)[:, :, None], bias, NEG_INF)
        bias = jnp.transpose(bias, (2, 0, 1))[:, None]
        out.append(jnp.where(edge[None, :, None, :], bias, NEG_INF))
    return jnp.stack(out)


def _head_select(first, second):
    lane = lax.broadcasted_iota(jnp.int32, first.shape[:-1] + (LANES,), len(first.shape) - 1)
    return jnp.where(lane < HEAD_DIM, first, second)


def _dilated_kernel(q_ref, k_ref, v_ref, g_ref, bias_ref, o_ref, *scratch):
    stage_a, stage_b = scratch[0:3], scratch[3:6]
    qd, kd, vd = scratch[6:9]
    last = len(DILATED_PATTERNS) - 1
    states = [scratch[9 + 3 * p:12 + 3 * p] for p in range(last)]
    seq = q_ref.shape[1]
    stage_a[0][...] = q_ref[0].astype(F32) * (HEAD_DIM ** -0.5)
    stage_a[1][...] = k_ref[0].astype(F32)
    stage_a[2][...] = v_ref[0].astype(F32)
    zero_halo = jnp.zeros((ATT_HALF, LANES), BF16)
    first_head = lax.broadcasted_iota(jnp.int32, (ATT_QB, LANES), 1) < HEAD_DIM

    src, dst = stage_a, stage_b
    for pi, (_, dil) in enumerate(DILATED_PATTERNS):
        n = seq // dil
        nblk = n // ATT_QB
        if pi > 0:
            prev_dil = dil // ATT_STEP
            for r1 in range(prev_dil):
                for r2 in range(ATT_STEP):
                    r = r1 + prev_dil * r2
                    for a_src, a_dst in zip(src, dst):
                        a_dst[r * n:(r + 1) * n, :] = a_src[pl.ds(r1 * n * ATT_STEP + r2, n, stride=ATT_STEP), :]
            src, dst = dst, src
        stride_k = n + 2 * ATT_HALF
        qd[...] = src[0][...].astype(BF16)
        for r in range(dil):
            base = r * stride_k
            for a_src, a_dst in ((src[1], kd), (src[2], vd)):
                a_dst[base:base + ATT_HALF, :] = zero_halo
                a_dst[base + ATT_HALF:base + ATT_HALF + n, :] = a_src[r * n:(r + 1) * n, :].astype(BF16)
                a_dst[base + ATT_HALF + n:base + stride_k, :] = zero_halo
        if pi == last:
            states.append(dst)
        acc_w, mx_w, den_w = states[pi]

        def block_group(it, pi=pi, nblk=nblk, acc_w=acc_w, mx_w=mx_w, den_w=den_w):
            chains = []
            for u in range(ATT_UNROLL):
                blk = it * ATT_UNROLL + u
                r = blk // nblk
                qb = blk % nblk
                q0 = blk * ATT_QB
                k0 = q0 + r * (2 * ATT_HALF)
                edge = 0 if qb == 0 else (2 if qb == nblk - 1 else 1)
                qblk = qd[pl.ds(q0, ATT_QB), :]
                qcat = jnp.concatenate([jnp.where(first_head, qblk, jnp.zeros_like(qblk)),
                                        jnp.where(first_head, jnp.zeros_like(qblk), qblk)], axis=0)
                s = lax.dot_general(qcat, kd[pl.ds(k0, ATT_KW), :], (((1,), (1,)), ((), ())),
                                    preferred_element_type=F32)
                chains.append((q0, k0, edge, s))
            probs = []
            for q0, k0, edge, s in chains:
                s = s + jnp.concatenate([bias_ref[pi, 0, edge], bias_ref[pi, 1, edge]], axis=0)
                m = jnp.max(s, axis=1, keepdims=True)
                p = jnp.exp(s - m)
                probs.append((p.astype(BF16), m, jnp.sum(p, axis=1, keepdims=True)))
            for (q0, k0, edge, s), (p, m, l) in zip(chains, probs):
                pv = jnp.dot(p, vd[pl.ds(k0, ATT_KW), :], preferred_element_type=F32)
                acc_w[pl.ds(q0, ATT_QB), :] = _head_select(pv[:ATT_QB], pv[ATT_QB:])
                mx_w[pl.ds(q0, ATT_QB), :] = _head_select(m[:ATT_QB], m[ATT_QB:])
                den_w[pl.ds(q0, ATT_QB), :] = _head_select(l[:ATT_QB], l[ATT_QB:])

        for it in range(dil * nblk // ATT_UNROLL):
            block_group(it)

    for pi in range(len(DILATED_PATTERNS) - 1, 0, -1):
        dil = DILATED_PATTERNS[pi][1]
        n = seq // dil
        prev_dil = dil // ATT_STEP
        acc_c, mx_c, den_c = states[pi - 1]
        acc_f, mx_f, den_f = states[pi]
        for r1 in range(prev_dil):
            for r2 in range(ATT_STEP):
                r = r1 + prev_dil * r2
                rows = pl.ds(r1 * n * ATT_STEP + r2, n, stride=ATT_STEP)
                fine = slice(r * n, (r + 1) * n)
                m_old = mx_c[rows, :]
                m_new = mx_f[fine, :]
                m_all = jnp.maximum(m_old, m_new)
                decay = jnp.exp(-jnp.abs(m_old - m_new))
                a_old = jnp.where(m_old >= m_new, 1.0, decay)
                a_new = jnp.where(m_old >= m_new, decay, 1.0)
                acc_c[rows, :] = acc_c[rows, :] * a_old + acc_f[fine, :] * a_new
                den_c[rows, :] = den_c[rows, :] * a_old + den_f[fine, :] * a_new
                mx_c[rows, :] = m_all

    acc, _, den = states[0]
    g = g_ref[0].astype(F32)
    o_ref[0] = (acc[...] / den[...] * (g * jax.nn.sigmoid(g))).astype(o_ref.dtype)


def _dilated_attention(proj, bias, n_heads):
    _, b, seq, _ = proj.shape
    npair = n_heads // 2
    max_dil = max(d for _, d in DILATED_PATTERNS)
    kd_rows = seq + 2 * ATT_HALF * max_dil
    blk = lambda off: pl.BlockSpec((None, 1, seq, LANES), lambda i, j: (off * npair + j, i, 0, 0))
    state = pltpu.VMEM((seq, LANES), F32)
    return pl.pallas_call(
        _dilated_kernel,
        grid=(b, npair),
        in_specs=[blk(0), blk(1), blk(2), blk(3),
                  pl.BlockSpec((len(DILATED_PATTERNS), 2, 3, ATT_QB, ATT_KW), lambda i, j: (0, j, 0, 0, 0))],
        out_specs=pl.BlockSpec((None, 1, seq, LANES), lambda i, j: (j, i, 0, 0)),
        out_shape=jax.ShapeDtypeStruct((npair, b, seq, LANES), BF16),
        scratch_shapes=[state] * 6
                       + [pltpu.VMEM((seq, LANES), BF16), pltpu.VMEM((kd_rows, LANES), BF16),
                          pltpu.VMEM((kd_rows, LANES), BF16)]
                       + [state] * (3 * (len(DILATED_PATTERNS) - 1)),
        compiler_params=_cparams(("parallel", "parallel")),
        name="dilated_attention",
    )(proj, proj, proj, proj, bias)


def _na_bias(rpb, rows):
    assert min(NA_ROWS, rows) % 2 == 0
    cols = np.arange(GRID_W)
    cs = np.clip(cols - NA_COLS // 2, 0, GRID_W - NA_COLS)
    kc = np.arange(GRID_W)
    inside = (kc[None, :] >= cs[:, None]) & (kc[None, :] < cs[:, None] + NA_COLS)
    col_idx = kc[None, :] - cols[:, None] + NA_COLS - 1
    onehot = (col_idx[:, :, None] == np.arange(2 * NA_COLS - 1)[None, None, :]).astype(np.float32)
    e = jnp.einsum("hrd,cqd->hrcq", rpb.astype(F32), jnp.asarray(onehot), precision=lax.Precision.HIGHEST)
    e = jnp.where(jnp.asarray(inside)[None, None], e, NEG_INF)
    return jnp.concatenate([e[:, :-1], e[:, 1:]], axis=-1)


def _na_kernel(q_ref, k_ref, v_ref, g_ref, bias_ref, o_ref):
    seq = q_ref.shape[1]
    rows = seq // GRID_W
    kr_n = min(NA_ROWS, rows)
    first_head = lax.broadcasted_iota(jnp.int32, (GRID_W, LANES), 1) < HEAD_DIM

    def row_group(it):
        def scores(batch):
            out = []
            for u in range(batch * NA_BATCH, (batch + 1) * NA_BATCH):
                r = it * NA_UNROLL + u
                rs = min(max(r - kr_n // 2, 0), rows - kr_n)
                shift = rs - r + NA_ROWS - 1
                q0 = r * GRID_W
                k0 = rs * GRID_W
                qrow = q_ref[0, pl.ds(q0, GRID_W), :] * (HEAD_DIM ** -0.5)
                qcat = jnp.concatenate([jnp.where(first_head, qrow, jnp.zeros_like(qrow)),
                                        jnp.where(first_head, jnp.zeros_like(qrow), qrow)], axis=0)
                s = lax.dot_general(qcat, k_ref[0, pl.ds(k0, kr_n * GRID_W), :], (((1,), (1,)), ((), ())),
                                    preferred_element_type=F32)
                out.append((q0, k0, shift, s))
            return out

        def finish(chains):
            probs = []
            for q0, k0, shift, s in chains:
                bias = [jnp.concatenate([bias_ref[h, shift + i] for i in range(0, kr_n, 2)], axis=1)
                        for h in range(2)]
                s = s + jnp.concatenate(bias, axis=0)
                p = jnp.exp(s - jnp.max(s, axis=1, keepdims=True))
                probs.append((p.astype(BF16), jnp.sum(p, axis=1, keepdims=True)))
            for (q0, k0, shift, s), (p, l) in zip(chains, probs):
                o = jnp.dot(p, v_ref[0, pl.ds(k0, kr_n * GRID_W), :], preferred_element_type=F32) / l
                g = g_ref[0, pl.ds(q0, GRID_W), :].astype(F32)
                o = _head_select(o[:GRID_W], o[GRID_W:]) * (g * jax.nn.sigmoid(g))
                o_ref[0, pl.ds(q0, GRID_W), :] = o.astype(o_ref.dtype)

        n_batches = NA_UNROLL // NA_BATCH
        pending = scores(0)
        for batch in range(n_batches):
            ahead = scores(batch + 1) if batch + 1 < n_batches else None
            finish(pending)
            pending = ahead

    for it in range(rows // NA_UNROLL):
        row_group(it)


def _neighbourhood_attention(proj, bias, n_heads):
    _, b, seq, _ = proj.shape
    npair = n_heads // 2
    blk = lambda off: pl.BlockSpec((None, 1, seq, LANES), lambda j, i: (off * npair + j, i, 0, 0))
    return pl.pallas_call(
        _na_kernel,
        grid=(npair, b),
        in_specs=[blk(0), blk(1), blk(2), blk(3),
                  pl.BlockSpec((2,) + bias.shape[1:], lambda j, i: (j, 0, 0, 0))],
        out_specs=pl.BlockSpec((None, 1, seq, LANES), lambda j, i: (j, i, 0, 0)),
        out_shape=jax.ShapeDtypeStruct((npair, b, seq, LANES), BF16),
        compiler_params=_cparams(("parallel", "parallel")),
        name="neighbourhood_attention",
    )(proj, proj, proj, proj, bias)


def _out_kernel(n_y, n_t, *refs):
    n = n_y + n_t
    y_refs = refs[:n]
    w_refs = refs[n:2 * n]
    h_ref, p_ref, wp_ref, wg_ref, gpost_ref, gple_ref, gnext_ref, h_out, u_out = refs[2 * n:]
    t = None
    for idx, (y_ref, w_ref) in enumerate(zip(y_refs, w_refs)):
        if idx < n_y:
            y = jnp.concatenate([y_ref[c] for c in range(y_ref.shape[0])], axis=1)
            part = jnp.dot(y, w_ref[...], preferred_element_type=F32)
        else:
            part = lax.dot_general(y_ref[0], w_ref[...], (((0,), (0,)), ((), ())), preferred_element_type=F32)
        t = part if t is None else t + part
    h = h_ref[...] + _rms(t, gpost_ref[...])
    e = _rms(jnp.dot(p_ref[...].astype(BF16), wp_ref[...], preferred_element_type=F32), gple_ref[...])
    gate = jax.nn.sigmoid(jnp.dot(h.astype(BF16), wg_ref[...], preferred_element_type=F32))
    h = h + gate * e
    h_out[...] = h
    u_out[...] = _rms(h, gnext_ref[...]).astype(u_out.dtype)


def _out_block(ys, ys_t, ws, h, p, layer, w_ple, w_gate, g_post, g_ple, g_next, tm=1024):
    m, d = h.shape
    row = lambda a: pl.BlockSpec((tm, a.shape[1]), lambda i: (i, 0))
    slab = lambda a: pl.BlockSpec((a.shape[0], tm, LANES), lambda i: (0, i, 0))
    p_spec = pl.BlockSpec((None, tm, p.shape[2]), lambda i: (layer, i, 0))
    full = lambda a: pl.BlockSpec(a.shape, lambda i: (0, 0))
    vec = lambda g: g.reshape(1, d)

    def col(a):
        tiles = a.shape[2] // tm
        return pl.BlockSpec((1, a.shape[1], tm), lambda i: (i // tiles, 0, i % tiles))

    return pl.pallas_call(
        functools.partial(_out_kernel, len(ys), len(ys_t)),
        grid=(m // tm,),
        in_specs=[slab(y) for y in ys] + [col(y) for y in ys_t] + [full(w) for w in ws]
                 + [row(h), p_spec, full(w_ple), full(w_gate), full(vec(g_post)), full(vec(g_ple)), full(vec(g_next))],
        out_specs=[pl.BlockSpec((tm, d), lambda i: (i, 0)), pl.BlockSpec((tm, d), lambda i: (i, 0))],
        out_shape=[jax.ShapeDtypeStruct((m, d), F32), jax.ShapeDtypeStruct((m, d), BF16)],
        compiler_params=_cparams(("parallel",)),
        name="out_proj_residual_ple",
    )(*ys, *ys_t, *ws, h, p, w_ple, w_gate, vec(g_post), vec(g_ple), vec(g_next))


def kernel(x, p, w_in, w_out, norm_pre, norm_post, hyena_conv_w, hyena_conv_b, hyena_w1, hyena_b1, hyena_w2, hyena_b2, hyena_w3, hyena_b3, hyena_w4, hyena_freq, hyena_skip, rel_bias, na_rpb, ple_proj, ple_norm, ple_gate):
    b, seq, d = x.shape
    depth = w_in.shape[0]
    d_inner = w_out.shape[1]
    hw = d_inner // 2
    m = b * seq
    consts = _fft_consts(seq)
    dil_bias = _dilated_bias(rel_bias)
    h = x.reshape(m, d)
    u = _rmsnorm_bf16(h, norm_pre[0])
    w_all = w_in.astype(BF16)
    p_all = p.reshape(depth, m, -1)
    for i in range(depth):
        j = i // 2
        if i % 2 == 0:
            w_a = w_all[i, :, :4 * hw].T
            ident = jnp.tile(jnp.array([0.0, 1.0, 0.0], F32)[:, None], (1, hw))
            conv_w = jnp.concatenate([hyena_conv_w[j], ident], axis=1).T
            conv_b = jnp.concatenate([hyena_conv_b[j], jnp.zeros((hw,), F32)])[:, None]
            hy = _proj_t(w_a, u.reshape(b, seq, d), conv_w, conv_b)
            halves = _hyena_filters_t(seq, hw, hyena_w1[j], hyena_b1[j], hyena_w2[j], hyena_b2[j],
                                      hyena_w3[j], hyena_b3[j], hyena_w4[j], hyena_freq[j])
            kr, ki = _filter_spectrum(halves, consts)
            skip = hyena_skip[j][:, :, None, None]
            z = _hyena_conv(False, hy, 0, hy, hw, hw, skip[0], kr, ki, 0, consts)
            ya_t = _hyena_conv(True, z, 0, hy, 2 * hw, 3 * hw, skip[1], kr, ki, 1, consts)
            proj_b = _matmul_bf16(u, w_all[i:i + 1, :, 4 * hw:], 0, 0, 4 * hw).reshape(-1, b, seq, LANES)
            yb = _dilated_attention(proj_b, dil_bias, hw // HEAD_DIM)
            ys, ys_t = [yb.reshape(-1, m, LANES)], [ya_t]
            ws = [w_out[i, hw:].astype(BF16), w_out[i, :hw].astype(BF16)]
        else:
            proj = _matmul_bf16(u, w_all, i, 0, 4 * d_inner).reshape(-1, b, seq, LANES)
            yc = _neighbourhood_attention(proj, _na_bias(na_rpb[j], seq // GRID_W), d_inner // HEAD_DIM)
            ys, ys_t = [yc.reshape(-1, m, LANES)], []
            ws = [w_out[i].astype(BF16)]
        g_next = norm_pre[i + 1] if i + 1 < depth else norm_pre[i]
        h, u = _out_block(ys, ys_t, ws, h, p_all, i, ple_proj[i].astype(BF16), ple_gate[i].astype(BF16),
                          norm_post[i], ple_norm[i], g_next)
    return h.reshape(b, seq, d)
```

```python
import functools
import math

import numpy as np
import jax
import jax.numpy as jnp
from jax import lax
from jax.experimental import pallas as pl
from jax.experimental.pallas import tpu as pltpu

F32 = jnp.float32
BF16 = jnp.bfloat16

HEAD_DIM = 64
LANES = 128
HYENA_EMB_DIM = 33
HYENA_FAST_DECAY = 0.3
HYENA_SLOW_DECAY = 1.5
HYENA_TARGET = 1e-2
DILATED_PATTERNS = ((128, 1), (512, 4), (2048, 16))
GRID_W = 64
NA_ROWS = 8
NA_COLS = 16
REL_BUCKETS = 32
REL_MAX_DIST = 1024
RMS_EPS = 1e-6
NEG_INF = -1e30
ATT_QB = 128
ATT_HALF = 64
ATT_KW = ATT_QB + 2 * ATT_HALF
ATT_UNROLL = 16
ATT_STEP = 4
assert all(b[1] == a[1] * ATT_STEP for a, b in zip(DILATED_PATTERNS, DILATED_PATTERNS[1:])) and DILATED_PATTERNS[0][1] == 1
NA_BATCH = 16
NA_UNROLL = 64
VMEM_LIMIT = 56 * 1024 * 1024


def _cparams(sem):
    return pltpu.CompilerParams(dimension_semantics=sem, vmem_limit_bytes=VMEM_LIMIT)


def _rms(x, g):
    return x * lax.rsqrt(jnp.mean(x * x, axis=-1, keepdims=True) + RMS_EPS) * g


def _rmsnorm_kernel(x_ref, g_ref, o_ref):
    o_ref[...] = _rms(x_ref[...], g_ref[...]).astype(o_ref.dtype)


def _rmsnorm_bf16(x2d, g, tm=1024):
    m, d = x2d.shape
    return pl.pallas_call(
        _rmsnorm_kernel,
        grid=(m // tm,),
        in_specs=[pl.BlockSpec((tm, d), lambda i: (i, 0)), pl.BlockSpec((1, d), lambda i: (0, 0))],
        out_specs=pl.BlockSpec((tm, d), lambda i: (i, 0)),
        out_shape=jax.ShapeDtypeStruct((m, d), BF16),
        compiler_params=_cparams(("parallel",)),
        name="rmsnorm",
    )(x2d, g.reshape(1, d))


def _matmul_kernel(a_ref, b_ref, o_ref):
    res = jnp.dot(a_ref[...], b_ref[...], preferred_element_type=F32).astype(o_ref.dtype)
    for c in range(o_ref.shape[0]):
        o_ref[c] = res[:, c * LANES:(c + 1) * LANES]


def _matmul_bf16(a, w, layer, col0, n, tm=2048, tn=1024):
    m, k = a.shape
    return pl.pallas_call(
        _matmul_kernel,
        grid=(m // tm, n // tn),
        in_specs=[pl.BlockSpec((tm, k), lambda i, j: (i, 0)), pl.BlockSpec((None, k, tn), lambda i, j: (layer, 0, col0 // tn + j))],
        out_specs=pl.BlockSpec((tn // LANES, tm, LANES), lambda i, j: (j, i, 0)),
        out_shape=jax.ShapeDtypeStruct((n // LANES, m, LANES), BF16),
        compiler_params=_cparams(("parallel", "parallel")),
        name="in_proj",
    )(a, w)


def _proj_t_kernel(w_ref, u_ref, cw_ref, cb_ref, o_ref):
    p = lax.dot_general(w_ref[...], u_ref[0], (((1,), (1,)), ((), ())), preferred_element_type=F32)
    seq = p.shape[1]
    t = lax.broadcasted_iota(jnp.int32, p.shape, 1)
    prev = jnp.where(t == 0, 0.0, pltpu.roll(p, 1, axis=1))
    nxt = jnp.where(t == seq - 1, 0.0, pltpu.roll(p, seq - 1, axis=1))
    cw = cw_ref[...]
    o_ref[0] = (prev * cw[:, 0:1] + p * cw[:, 1:2] + nxt * cw[:, 2:3] + cb_ref[...]).astype(o_ref.dtype)


def _proj_t(w_t, u, conv_w, conv_b, tc=256):
    c, d = w_t.shape
    b, seq, _ = u.shape
    return pl.pallas_call(
        _proj_t_kernel,
        grid=(b, c // tc),
        in_specs=[
            pl.BlockSpec((tc, d), lambda i, j: (j, 0)),
            pl.BlockSpec((1, seq, d), lambda i, j: (i, 0, 0)),
            pl.BlockSpec((tc, 3), lambda i, j: (j, 0)),
            pl.BlockSpec((tc, 1), lambda i, j: (j, 0)),
        ],
        out_specs=pl.BlockSpec((1, tc, seq), lambda i, j: (i, j, 0)),
        out_shape=jax.ShapeDtypeStruct((b, c, seq), BF16),
        compiler_params=_cparams(("parallel", "parallel")),
        name="in_proj_t_conv3",
    )(w_t, u, conv_w, conv_b)


def _hyena_filter_kernel(z_ref, w1_ref, b1_ref, w2_ref, b2_ref, w3_ref, b3_ref, fq_ref, w4_ref, dl_ref, t_ref,
                         o_ref, a_scr):
    hp = lax.Precision.HIGHEST

    @pl.when((pl.program_id(0) == 0) & (pl.program_id(1) == 0))
    def _():
        for d in range(2):
            a = z_ref[d]
            for w_ref, b_ref in ((w1_ref, b1_ref), (w2_ref, b2_ref), (w3_ref, b3_ref)):
                a = jnp.sin(fq_ref[...] * (jnp.dot(w_ref[...], a, precision=hp, preferred_element_type=F32) + b_ref[...]))
            a_scr[d] = a

    seq = o_ref.shape[-1]
    proj = lambda d: jnp.dot(w4_ref[0, d].astype(BF16), a_scr[d].astype(BF16), preferred_element_type=F32)
    lo = proj(0) * jnp.exp(-t_ref[0] * dl_ref[...])
    hi = proj(1) * jnp.exp(-t_ref[1] * dl_ref[...])
    hi = jnp.where(lax.broadcasted_iota(jnp.int32, (1, seq), 1) == 0, 0.0, hi)
    norm = jnp.sum(jnp.abs(lo), axis=1, keepdims=True) + jnp.sum(jnp.abs(hi), axis=1, keepdims=True)
    o_ref[0, 0] = (lo / norm).astype(o_ref.dtype)
    o_ref[1, 0] = (hi / norm).astype(o_ref.dtype)


def _hyena_filters_t(seq, c, w1, b1, w2, b2, w3, b3, w4, freq, tc=256):
    tc = min(tc, c)
    fo = w2.shape[0]
    emb = -(-HYENA_EMB_DIM // 8) * 8
    t = jnp.linspace(0.0, 1.0, seq, dtype=F32)
    bands = (HYENA_EMB_DIM - 1) // 2
    fr = jnp.linspace(1e-4, bands - 1, bands, dtype=F32)[:, None]
    mirror = (seq - jnp.arange(seq)) % seq
    tables, times = [], []
    for pos, tp in ((jnp.arange(seq), t), (mirror, jnp.roll(t[::-1], 1))):
        wpos = (2.0 * math.pi / seq) * pos.astype(F32)[None, :]
        tp = tp[None, :]
        z = jnp.concatenate([tp, jnp.cos(fr * wpos), -jnp.sin(fr * wpos),
                             jnp.zeros((emb - HYENA_EMB_DIM, seq), F32)], axis=0)
        tables.append(z)
        times.append(tp)
    z = jnp.stack(tables)
    tt = jnp.stack(times)
    w1_t = jnp.pad(w1.astype(F32).T, ((0, 0), (0, emb - HYENA_EMB_DIM)))
    w4_t = jnp.transpose(w4.astype(F32).reshape(fo, 2, 2, c), (1, 2, 3, 0))
    min_decay = math.log(HYENA_TARGET) / HYENA_SLOW_DECAY
    max_decay = math.log(HYENA_TARGET) / HYENA_FAST_DECAY
    deltas = jnp.abs(jnp.linspace(min_decay, max_decay, c, dtype=F32))[:, None]
    col = lambda v: v.astype(F32)[:, None]
    full = lambda a: pl.BlockSpec(a.shape, lambda f, i: (0,) * a.ndim)
    args = (z, w1_t, col(b1), w2.astype(F32).T, col(b2), w3.astype(F32).T, col(b3), col(freq))
    return pl.pallas_call(
        _hyena_filter_kernel,
        grid=(2, c // tc),
        in_specs=[full(a) for a in args]
                 + [pl.BlockSpec((1, 2, tc, fo), lambda f, i: (f, 0, i, 0)),
                    pl.BlockSpec((tc, 1), lambda f, i: (i, 0)), full(tt)],
        out_specs=pl.BlockSpec((2, 1, tc, seq), lambda f, i: (0, f, i, 0)),
        out_shape=jax.ShapeDtypeStruct((2, 2, c, seq), BF16),
        scratch_shapes=[pltpu.VMEM((2, fo, seq), F32)],
        compiler_params=_cparams(("arbitrary", "arbitrary")),
        name="hyena_filter_mlp",
    )(*args, w4_t, deltas, tt)


def _fft_consts(seq):
    n_fft = 2 * seq
    n1 = n_fft // LANES
    nb = n1 // 2 + 1
    kp = -(-nb // 16) * 16
    k1 = jnp.arange(kp, dtype=jnp.int32)
    keep = (k1 < nb)[:, None]
    two_pi = 2.0 * math.pi

    th = ((k1[:, None] * jnp.arange(n1, dtype=jnp.int32)[None, :]) % n1).astype(F32) * (two_pi / n1)
    f1 = jnp.concatenate([jnp.where(keep, jnp.cos(th), 0.0), jnp.where(keep, -jnp.sin(th), 0.0)], axis=0)
    ph = (k1[:, None] * jnp.arange(LANES, dtype=jnp.int32)[None, :]).astype(F32) * (two_pi / n_fft)
    tw = jnp.stack([jnp.cos(ph), -jnp.sin(ph)])
    om = ((jnp.arange(LANES, dtype=jnp.int32)[:, None] * jnp.arange(LANES, dtype=jnp.int32)[None, :]) % LANES
          ).astype(F32) * (two_pi / LANES)
    cplx = lambda c, s: jnp.concatenate([jnp.concatenate([c, s], axis=1), jnp.concatenate([-s, c], axis=1)], axis=0)
    f2 = cplx(jnp.cos(om), -jnp.sin(om))
    f2i = cplx(jnp.cos(om), jnp.sin(om))
    weight = jnp.where((k1 == 0) | (k1 == nb - 1), 1.0, 2.0)[:, None]
    g = jnp.concatenate([jnp.where(keep, weight * jnp.cos(th[:, :n1 // 2]), 0.0),
                         jnp.where(keep, -weight * jnp.sin(th[:, :n1 // 2]), 0.0)], axis=0).T
    return dict(f1=f1.astype(BF16), tw=tw.astype(BF16), f2=f2.astype(BF16), f2i=f2i.astype(BF16), g=g.astype(BF16), kp=kp)


def _per_channel_matmul(mat_ref, planes, out_scr, ct):
    for c in range(0, ct, 2):
        pair = jnp.concatenate([planes(c), planes(c + 1)], axis=1)
        res = jnp.dot(mat_ref[...], pair, preferred_element_type=F32)
        out_scr[c] = res[:, :LANES].astype(out_scr.dtype)
        out_scr[c + 1] = res[:, LANES:].astype(out_scr.dtype)


def _twiddle_stage2(a, tw_ref, f2_ref):
    ct, kp2, _ = a.shape
    kp = kp2 // 2
    ar, ai = a[:, :kp], a[:, kp:]
    twr, twi = tw_ref[0], tw_ref[1]
    st = jnp.concatenate([ar * twr - ai * twi, ar * twi + ai * twr], axis=2)
    p = jnp.dot(st.reshape(ct * kp, 2 * LANES), f2_ref[...], preferred_element_type=F32).reshape(ct, kp, 2 * LANES)
    return p[:, :, :LANES].astype(BF16), p[:, :, LANES:].astype(BF16)


def _stage2_twiddle_inv(yr, yi, tw_ref, f2i_ref):
    ct, kp, _ = yr.shape
    st = jnp.concatenate([yr, yi], axis=2)
    p = jnp.dot(st.reshape(ct * kp, 2 * LANES), f2i_ref[...], preferred_element_type=F32).reshape(ct, kp, 2 * LANES)
    tr, ti = p[:, :, :LANES].astype(BF16), p[:, :, LANES:].astype(BF16)
    twr, twi = tw_ref[0], tw_ref[1]
    return jnp.concatenate([tr * twr + ti * twi, ti * twr - tr * twi], axis=1)


def _filter_spec_kernel(lo_ref, hi_ref, f1_ref, tw_ref, f2_ref, kr_ref, ki_ref, a_scr):
    ct = lo_ref.shape[2]
    planes = lambda c: jnp.concatenate([lo_ref[0, 0, c], hi_ref[0, 0, c]], axis=0)
    _per_channel_matmul(f1_ref, planes, a_scr, ct)
    xr, xi = _twiddle_stage2(a_scr[...], tw_ref, f2_ref)
    n_fft = f1_ref.shape[1] * LANES
    scale = 1.0 / n_fft
    kr_ref[0] = xr * scale
    ki_ref[0] = xi * scale


def _filter_spectrum(halves, consts, ct=128):
    _, nf, c, seq = halves.shape
    ct = min(ct, c)
    kp = consts["kp"]
    h5 = halves.reshape(2, nf, c, seq // LANES, LANES)
    half = lambda s: pl.BlockSpec((1, 1, ct, seq // LANES, LANES), lambda f, i: (s, f, i, 0, 0))
    full = lambda a: pl.BlockSpec(a.shape, lambda f, i: (0,) * a.ndim)
    spec = jax.ShapeDtypeStruct((nf, c, kp, LANES), BF16)
    out = pl.BlockSpec((1, ct, kp, LANES), lambda f, i: (f, i, 0, 0))
    return pl.pallas_call(
        _filter_spec_kernel,
        grid=(nf, c // ct),
        in_specs=[half(0), half(1), full(consts["f1"]), full(consts["tw"]), full(consts["f2"])],
        out_specs=[out, out],
        out_shape=[spec, spec],
        scratch_shapes=[pltpu.VMEM((ct, 2 * kp, LANES), BF16)],
        compiler_params=_cparams(("parallel", "parallel")),
        name="hyena_filter_spectrum",
    )(h5, h5, consts["f1"], consts["tw"], consts["f2"])


def _hyena_conv_kernel(second, x_ref, m_ref, e_ref, skip_ref, kr_ref, ki_ref, f1_ref, tw_ref, f2_ref, f2i_ref, g_ref,
                       o_ref, a_scr, t_scr, y_scr):
    ct = x_ref.shape[1]
    _per_channel_matmul(f1_ref, lambda c: x_ref[0, c], a_scr, ct)
    xr, xi = _twiddle_stage2(a_scr[...], tw_ref, f2_ref)
    kr, ki = kr_ref[0], ki_ref[0]
    t_scr[...] = _stage2_twiddle_inv(xr * kr - xi * ki, xr * ki + xi * kr, tw_ref, f2i_ref)

    _per_channel_matmul(g_ref, lambda c: t_scr[c], y_scr, ct)

    out = m_ref[0].astype(F32) * (y_scr[...] + x_ref[0].astype(F32) * skip_ref[...])
    if second:
        e = e_ref[0].astype(F32)
        out = out * (e * jax.nn.sigmoid(e))
    o_ref[0] = out.astype(o_ref.dtype)


def _hyena_conv(second, x, x_row0, hy, m_row0, e_row0, skip, kr, ki, filt, consts, ct=128):
    b, _, seq = x.shape
    c = kr.shape[1]
    ct = min(ct, c)
    kp = consts["kp"]
    n1h = seq // LANES
    view = lambda a: a.reshape(a.shape[0], a.shape[1], n1h, LANES)
    rows = lambda blk0: pl.BlockSpec((1, ct, n1h, LANES), lambda j, i: (i, blk0 + j, 0, 0))
    full = lambda a: pl.BlockSpec(a.shape, lambda j, i: (0,) * a.ndim)
    kspec = pl.BlockSpec((1, ct, kp, LANES), lambda j, i: (filt, j, 0, 0))
    mats = [consts["f1"][:, :n1h], consts["tw"], consts["f2"], consts["f2i"], consts["g"]]
    out = pl.pallas_call(
        functools.partial(_hyena_conv_kernel, second),
        grid=(c // ct, b),
        in_specs=[rows(x_row0 // ct), rows(m_row0 // ct), rows(e_row0 // ct),
                  pl.BlockSpec((ct, 1, 1), lambda j, i: (j, 0, 0)),
                  kspec, kspec] + [full(a) for a in mats],
        out_specs=pl.BlockSpec((1, ct, n1h, LANES), lambda j, i: (i, j, 0, 0)),
        out_shape=jax.ShapeDtypeStruct((b, c, n1h, LANES), BF16),
        scratch_shapes=[pltpu.VMEM((ct, 2 * kp, LANES), BF16), pltpu.VMEM((ct, 2 * kp, LANES), BF16),
                        pltpu.VMEM((ct, n1h, LANES), F32)],
        compiler_params=_cparams(("parallel", "parallel")),
        name="hyena_conv2" if second else "hyena_conv1",
    )(view(x), view(hy), view(hy), skip, kr, ki, *mats)
    return out.reshape(b, c, seq)


def _t5_bucket(rel):
    half_b = REL_BUCKETS // 2
    max_exact = half_b // 2
    ret = jnp.where(rel > 0, half_b, 0)
    n = jnp.abs(rel)
    nf = jnp.maximum(n, 1).astype(F32)
    large = max_exact + (jnp.log(nf / max_exact) / math.log(REL_MAX_DIST / max_exact)
                         * (half_b - max_exact)).astype(jnp.int32)
    large = jnp.minimum(large, half_b - 1)
    return ret + jnp.where(n < max_exact, n, large)


def _dilated_bias(rel_bias):
    rel = jnp.arange(ATT_KW)[None, :] - ATT_HALF - jnp.arange(ATT_QB)[:, None]
    col = jnp.arange(ATT_KW)
    edge = jnp.stack([col >= ATT_HALF, col >= 0, col < ATT_KW - ATT_HALF])
    out = []
    for _, dil in DILATED_PATTERNS:
        onehot = jax.nn.one_hot(_t5_bucket(rel * dil), REL_BUCKETS, dtype=F32)
        bias = jnp.einsum("qkb,bh->qkh", onehot, rel_bias.astype(F32), precision=lax.Precision.HIGHEST)
        bias = jnp.where((jnp.abs(rel) <= ATT_HALF)[:, :, None], bias, NEG_INF)
        bias = jnp.transpose(bias, (2, 0, 1))[:, None]
        out.append(jnp.where(edge[None, :, None, :], bias, NEG_INF))
    return jnp.stack(out)


def _head_select(first, second):
    lane = lax.broadcasted_iota(jnp.int32, first.shape[:-1] + (LANES,), len(first.shape) - 1)
    return jnp.where(lane < HEAD_DIM, first, second)


def _dilated_kernel(q_ref, k_ref, v_ref, g_ref, bias_ref, o_ref, *scratch):
    stage_a, stage_b = scratch[0:3], scratch[3:6]
    qd, kd, vd = scratch[6:9]
    last = len(DILATED_PATTERNS) - 1
    states = [scratch[9 + 3 * p:12 + 3 * p] for p in range(last)]
    seq = q_ref.shape[1]
    stage_a[0][...] = q_ref[0].astype(F32) * (HEAD_DIM ** -0.5)
    stage_a[1][...] = k_ref[0].astype(F32)
    stage_a[2][...] = v_ref[0].astype(F32)
    zero_halo = jnp.zeros((ATT_HALF, LANES), BF16)
    first_head = lax.broadcasted_iota(jnp.int32, (ATT_QB, LANES), 1) < HEAD_DIM

    src, dst = stage_a, stage_b
    for pi, (_, dil) in enumerate(DILATED_PATTERNS):
        n = seq // dil
        nblk = n // ATT_QB
        if pi > 0:
            prev_dil = dil // ATT_STEP
            for r1 in range(prev_dil):
                for r2 in range(ATT_STEP):
                    r = r1 + prev_dil * r2
                    for a_src, a_dst in zip(src, dst):
                        a_dst[r * n:(r + 1) * n, :] = a_src[pl.ds(r1 * n * ATT_STEP + r2, n, stride=ATT_STEP), :]
            src, dst = dst, src
        stride_k = n + 2 * ATT_HALF
        qd[...] = src[0][...].astype(BF16)
        for r in range(dil):
            base = r * stride_k
            for a_src, a_dst in ((src[1], kd), (src[2], vd)):
                a_dst[base:base + ATT_HALF, :] = zero_halo
                a_dst[base + ATT_HALF:base + ATT_HALF + n, :] = a_src[r * n:(r + 1) * n, :].astype(BF16)
                a_dst[base + ATT_HALF + n:base + stride_k, :] = zero_halo
        if pi == last:
            states.append(dst)
        acc_w, mx_w, den_w = states[pi]

        def block_group(it, pi=pi, nblk=nblk, acc_w=acc_w, mx_w=mx_w, den_w=den_w):
            chains = []
            for u in range(ATT_UNROLL):
                blk = it * ATT_UNROLL + u
                r = blk // nblk
                qb = blk % nblk
                q0 = blk * ATT_QB
                k0 = q0 + r * (2 * ATT_HALF)
                edge = 0 if qb == 0 else (2 if qb == nblk - 1 else 1)
                qblk = qd[pl.ds(q0, ATT_QB), :]
                qcat = jnp.concatenate([jnp.where(first_head, qblk, jnp.zeros_like(qblk)),
                                        jnp.where(first_head, jnp.zeros_like(qblk), qblk)], axis=0)
                s = lax.dot_general(qcat, kd[pl.ds(k0, ATT_KW), :], (((1,), (1,)), ((), ())),
                                    preferred_element_type=F32)
                chains.append((q0, k0, edge, s))
            probs = []
            for q0, k0, edge, s in chains:
                s = s + jnp.concatenate([bias_ref[pi, 0, edge], bias_ref[pi, 1, edge]], axis=0)
                m = jnp.max(s, axis=1, keepdims=True)
                p = jnp.exp(s - m)
                probs.append((p.astype(BF16), m, jnp.sum(p, axis=1, keepdims=True)))
            for (q0, k0, edge, s), (p, m, l) in zip(chains, probs):
                pv = jnp.dot(p, vd[pl.ds(k0, ATT_KW), :], preferred_element_type=F32)
                acc_w[pl.ds(q0, ATT_QB), :] = _head_select(pv[:ATT_QB], pv[ATT_QB:])
                mx_w[pl.ds(q0, ATT_QB), :] = _head_select(m[:ATT_QB], m[ATT_QB:])
                den_w[pl.ds(q0, ATT_QB), :] = _head_select(l[:ATT_QB], l[ATT_QB:])

        for it in range(dil * nblk // ATT_UNROLL):
            block_group(it)

    for pi in range(len(DILATED_PATTERNS) - 1, 0, -1):
        dil = DILATED_PATTERNS[pi][1]
        n = seq // dil
        prev_dil = dil // ATT_STEP
        acc_c, mx_c, den_c = states[pi - 1]
        acc_f, mx_f, den_f = states[pi]
        for r1 in range(prev_dil):
            for r2 in range(ATT_STEP):
                r = r1 + prev_dil * r2
                rows = pl.ds(r1 * n * ATT_STEP + r2, n, stride=ATT_STEP)
                fine = slice(r * n, (r + 1) * n)
                m_old = mx_c[rows, :]
                m_new = mx_f[fine, :]
                m_all = jnp.maximum(m_old, m_new)
                decay = jnp.exp(-jnp.abs(m_old - m_new))
                a_old = jnp.where(m_old >= m_new, 1.0, decay)
                a_new = jnp.where(m_old >= m_new, decay, 1.0)
                acc_c[rows, :] = acc_c[rows, :] * a_old + acc_f[fine, :] * a_new
                den_c[rows, :] = den_c[rows, :] * a_old + den_f[fine, :] * a_new
                mx_c[rows, :] = m_all

    acc, _, den = states[0]
    g = g_ref[0].astype(F32)
    o_ref[0] = (acc[...] / den[...] * (g * jax.nn.sigmoid(g))).astype(o_ref.dtype)


def _dilated_attention(proj, bias, n_heads):
    _, b, seq, _ = proj.shape
    npair = n_heads // 2
    max_dil = max(d for _, d in DILATED_PATTERNS)
    kd_rows = seq + 2 * ATT_HALF * max_dil
    blk = lambda off: pl.BlockSpec((None, 1, seq, LANES), lambda i, j: (off * npair + j, i, 0, 0))
    state = pltpu.VMEM((seq, LANES), F32)
    return pl.pallas_call(
        _dilated_kernel,
        grid=(b, npair),
        in_specs=[blk(0), blk(1), blk(2), blk(3),
                  pl.BlockSpec((len(DILATED_PATTERNS), 2, 3, ATT_QB, ATT_KW), lambda i, j: (0, j, 0, 0, 0))],
        out_specs=pl.BlockSpec((None, 1, seq, LANES), lambda i, j: (j, i, 0, 0)),
        out_shape=jax.ShapeDtypeStruct((npair, b, seq, LANES), BF16),
        scratch_shapes=[state] * 6
                       + [pltpu.VMEM((seq, LANES), BF16), pltpu.VMEM((kd_rows, LANES), BF16),
                          pltpu.VMEM((kd_rows, LANES), BF16)]
                       + [state] * (3 * (len(DILATED_PATTERNS) - 1)),
        compiler_params=_cparams(("parallel", "parallel")),
        name="dilated_attention",
    )(proj, proj, proj, proj, bias)


def _na_bias(rpb, rows):
    assert min(NA_ROWS, rows) % 2 == 0
    cols = np.arange(GRID_W)
    cs = np.clip(cols - NA_COLS // 2, 0, GRID_W - NA_COLS)
    kc = np.arange(GRID_W)
    inside = (kc[None, :] >= cs[:, None]) & (kc[None, :] < cs[:, None] + NA_COLS)
    col_idx = kc[None, :] - cols[:, None] + NA_COLS - 1
    onehot = (col_idx[:, :, None] == np.arange(2 * NA_COLS - 1)[None, None, :]).astype(np.float32)
    e = jnp.einsum("hrd,cqd->hrcq", rpb.astype(F32), jnp.asarray(onehot), precision=lax.Precision.HIGHEST)
    e = jnp.where(jnp.asarray(inside)[None, None], e, NEG_INF)
    return jnp.concatenate([e[:, :-1], e[:, 1:]], axis=-1)


def _na_kernel(q_ref, k_ref, v_ref, g_ref, bias_ref, o_ref):
    seq = q_ref.shape[1]
    rows = seq // GRID_W
    kr_n = min(NA_ROWS, rows)
    first_head = lax.broadcasted_iota(jnp.int32, (GRID_W, LANES), 1) < HEAD_DIM

    def row_group(it):
        def scores(batch):
            out = []
            for u in range(batch * NA_BATCH, (batch + 1) * NA_BATCH):
                r = it * NA_UNROLL + u
                rs = min(max(r - kr_n // 2, 0), rows - kr_n)
                shift = rs - r + NA_ROWS - 1
                q0 = r * GRID_W
                k0 = rs * GRID_W
                qrow = q_ref[0, pl.ds(q0, GRID_W), :] * (HEAD_DIM ** -0.5)
                qcat = jnp.concatenate([jnp.where(first_head, qrow, jnp.zeros_like(qrow)),
                                        jnp.where(first_head, jnp.zeros_like(qrow), qrow)], axis=0)
                s = lax.dot_general(qcat, k_ref[0, pl.ds(k0, kr_n * GRID_W), :], (((1,), (1,)), ((), ())),
                                    preferred_element_type=F32)
                out.append((q0, k0, shift, s))
            return out

        def finish(chains):
            probs = []
            for q0, k0, shift, s in chains:
                bias = [jnp.concatenate([bias_ref[h, shift + i] for i in range(0, kr_n, 2)], axis=1)
                        for h in range(2)]
                s = s + jnp.concatenate(bias, axis=0)
                p = jnp.exp(s - jnp.max(s, axis=1, keepdims=True))
                probs.append((p.astype(BF16), jnp.sum(p, axis=1, keepdims=True)))
            for (q0, k0, shift, s), (p, l) in zip(chains, probs):
                o = jnp.dot(p, v_ref[0, pl.ds(k0, kr_n * GRID_W), :], preferred_element_type=F32) / l
                g = g_ref[0, pl.ds(q0, GRID_W), :].astype(F32)
                o = _head_select(o[:GRID_W], o[GRID_W:]) * (g * jax.nn.sigmoid(g))
                o_ref[0, pl.ds(q0, GRID_W), :] = o.astype(o_ref.dtype)

        n_batches = NA_UNROLL // NA_BATCH
        pending = scores(0)
        for batch in range(n_batches):
            ahead = scores(batch + 1) if batch + 1 < n_batches else None
            finish(pending)
            pending = ahead

    for it in range(rows // NA_UNROLL):
        row_group(it)


def _neighbourhood_attention(proj, bias, n_heads):
    _, b, seq, _ = proj.shape
    npair = n_heads // 2
    blk = lambda off: pl.BlockSpec((None, 1, seq, LANES), lambda j, i: (off * npair + j, i, 0, 0))
    return pl.pallas_call(
        _na_kernel,
        grid=(npair, b),
        in_specs=[blk(0), blk(1), blk(2), blk(3),
                  pl.BlockSpec((2,) + bias.shape[1:], lambda j, i: (j, 0, 0, 0))],
        out_specs=pl.BlockSpec((None, 1, seq, LANES), lambda j, i: (j, i, 0, 0)),
        out_shape=jax.ShapeDtypeStruct((npair, b, seq, LANES), BF16),
        compiler_params=_cparams(("parallel", "parallel")),
        name="neighbourhood_attention",
    )(proj, proj, proj, proj, bias)


def _out_kernel(n_y, n_t, *refs):
    n = n_y + n_t
    y_refs = refs[:n]
    w_refs = refs[n:2 * n]
    h_ref, p_ref, wp_ref, wg_ref, gpost_ref, gple_ref, gnext_ref, h_out, u_out = refs[2 * n:]
    t = None
    for idx, (y_ref, w_ref) in enumerate(zip(y_refs, w_refs)):
        if idx < n_y:
            y = jnp.concatenate([y_ref[c] for c in range(y_ref.shape[0])], axis=1)
            part = jnp.dot(y, w_ref[...], preferred_element_type=F32)
        else:
            part = lax.dot_general(y_ref[0], w_ref[...], (((0,), (0,)), ((), ())), preferred_element_type=F32)
        t = part if t is None else t + part
    h = h_ref[...] + _rms(t, gpost_ref[...])
    e = _rms(jnp.dot(p_ref[...].astype(BF16), wp_ref[...], preferred_element_type=F32), gple_ref[...])
    gate = jax.nn.sigmoid(jnp.dot(h.astype(BF16), wg_ref[...], preferred_element_type=F32))
    h = h + gate * e
    h_out[...] = h
    u_out[...] = _rms(h, gnext_ref[...]).astype(u_out.dtype)


def _out_block(ys, ys_t, ws, h, p, layer, w_ple, w_gate, g_post, g_ple, g_next, tm=1024):
    m, d = h.shape
    row = lambda a: pl.BlockSpec((tm, a.shape[1]), lambda i: (i, 0))
    slab = lambda a: pl.BlockSpec((a.shape[0], tm, LANES), lambda i: (0, i, 0))
    p_spec = pl.BlockSpec((None, tm, p.shape[2]), lambda i: (layer, i, 0))
    full = lambda a: pl.BlockSpec(a.shape, lambda i: (0, 0))
    vec = lambda g: g.reshape(1, d)

    def col(a):
        tiles = a.shape[2] // tm
        return pl.BlockSpec((1, a.shape[1], tm), lambda i: (i // tiles, 0, i % tiles))

    return pl.pallas_call(
        functools.partial(_out_kernel, len(ys), len(ys_t)),
        grid=(m // tm,),
        in_specs=[slab(y) for y in ys] + [col(y) for y in ys_t] + [full(w) for w in ws]
                 + [row(h), p_spec, full(w_ple), full(w_gate), full(vec(g_post)), full(vec(g_ple)), full(vec(g_next))],
        out_specs=[pl.BlockSpec((tm, d), lambda i: (i, 0)), pl.BlockSpec((tm, d), lambda i: (i, 0))],
        out_shape=[jax.ShapeDtypeStruct((m, d), F32), jax.ShapeDtypeStruct((m, d), BF16)],
        compiler_params=_cparams(("parallel",)),
        name="out_proj_residual_ple",
    )(*ys, *ys_t, *ws, h, p, w_ple, w_gate, vec(g_post), vec(g_ple), vec(g_next))


def kernel(x, p, w_in, w_out, norm_pre, norm_post, hyena_conv_w, hyena_conv_b, hyena_w1, hyena_b1, hyena_w2, hyena_b2, hyena_w3, hyena_b3, hyena_w4, hyena_freq, hyena_skip, rel_bias, na_rpb, ple_proj, ple_norm, ple_gate):
    b, seq, d = x.shape
    depth = w_in.shape[0]
    d_inner = w_out.shape[1]
    hw = d_inner // 2
    m = b * seq
    consts = _fft_consts(seq)
    dil_bias = _dilated_bias(rel_bias)
    h = x.reshape(m, d)
    u = _rmsnorm_bf16(h, norm_pre[0])
    w_all = w_in.astype(BF16)
    p_all = p.reshape(depth, m, -1)
    for i in range(depth):
        j = i // 2
        if i % 2 == 0:
            w_a = w_all[i, :, :4 * hw].T
            ident = jnp.tile(jnp.array([0.0, 1.0, 0.0], F32)[:, None], (1, hw))
            conv_w = jnp.concatenate([hyena_conv_w[j], ident], axis=1).T
            conv_b = jnp.concatenate([hyena_conv_b[j], jnp.zeros((hw,), F32)])[:, None]
            hy = _proj_t(w_a, u.reshape(b, seq, d), conv_w, conv_b)
            halves = _hyena_filters_t(seq, hw, hyena_w1[j], hyena_b1[j], hyena_w2[j], hyena_b2[j],
                                      hyena_w3[j], hyena_b3[j], hyena_w4[j], hyena_freq[j])
            kr, ki = _filter_spectrum(halves, consts)
            skip = hyena_skip[j][:, :, None, None]
            z = _hyena_conv(False, hy, 0, hy, hw, hw, skip[0], kr, ki, 0, consts)
            ya_t = _hyena_conv(True, z, 0, hy, 2 * hw, 3 * hw, skip[1], kr, ki, 1, consts)
            proj_b = _matmul_bf16(u, w_all[i:i + 1, :, 4 * hw:], 0, 0, 4 * hw).reshape(-1, b, seq, LANES)
            yb = _dilated_attention(proj_b, dil_bias, hw // HEAD_DIM)
            ys, ys_t = [yb.reshape(-1, m, LANES)], [ya_t]
            ws = [w_out[i, hw:].astype(BF16), w_out[i, :hw].astype(BF16)]
        else:
            proj = _matmul_bf16(u, w_all, i, 0, 4 * d_inner).reshape(-1, b, seq, LANES)
            yc = _neighbourhood_attention(proj, _na_bias(na_rpb[j], seq // GRID_W), d_inner // HEAD_DIM)
            ys, ys_t = [yc.reshape(-1, m, LANES)], []
            ws = [w_out[i].astype(BF16)]
        g_next = norm_pre[i + 1] if i + 1 < depth else norm_pre[i]
        h, u = _out_block(ys, ys_t, ws, h, p_all, i, ple_proj[i].astype(BF16), ple_gate[i].astype(BF16),
                          norm_post[i], ple_norm[i], g_next)
    return h.reshape(b, seq, d)
```

```python
import functools
import math

import numpy as np
import jax
import jax.numpy as jnp
from jax import lax
from jax.experimental import pallas as pl
from jax.experimental.pallas import tpu as pltpu

F32 = jnp.float32
BF16 = jnp.bfloat16

HEAD_DIM = 64
LANES = 128
HYENA_EMB_DIM = 33
HYENA_FAST_DECAY = 0.3
HYENA_SLOW_DECAY = 1.5
HYENA_TARGET = 1e-2
DILATED_PATTERNS = ((128, 1), (512, 4), (2048, 16))
GRID_W = 64
NA_ROWS = 8
NA_COLS = 16
REL_BUCKETS = 32
REL_MAX_DIST = 1024
RMS_EPS = 1e-6
NEG_INF = -1e30
ATT_QB = 128
ATT_HALF = 64
ATT_KW = ATT_QB + 2 * ATT_HALF
ATT_UNROLL = 8
ATT_STEP = 4
assert all(b[1] == a[1] * ATT_STEP for a, b in zip(DILATED_PATTERNS, DILATED_PATTERNS[1:])) and DILATED_PATTERNS[0][1] == 1
NA_BATCH = 8
NA_UNROLL = 64
VMEM_LIMIT = 56 * 1024 * 1024


def _cparams(sem):
    return pltpu.CompilerParams(dimension_semantics=sem, vmem_limit_bytes=VMEM_LIMIT)


def _rms(x, g):
    return x * lax.rsqrt(jnp.mean(x * x, axis=-1, keepdims=True) + RMS_EPS) * g


def _rmsnorm_kernel(x_ref, g_ref, o_ref):
    o_ref[...] = _rms(x_ref[...], g_ref[...]).astype(o_ref.dtype)


def _rmsnorm_bf16(x2d, g, tm=1024):
    m, d = x2d.shape
    return pl.pallas_call(
        _rmsnorm_kernel,
        grid=(m // tm,),
        in_specs=[pl.BlockSpec((tm, d), lambda i: (i, 0)), pl.BlockSpec((1, d), lambda i: (0, 0))],
        out_specs=pl.BlockSpec((tm, d), lambda i: (i, 0)),
        out_shape=jax.ShapeDtypeStruct((m, d), BF16),
        compiler_params=_cparams(("parallel",)),
        name="rmsnorm",
    )(x2d, g.reshape(1, d))


def _matmul_kernel(a_ref, b_ref, o_ref):
    res = jnp.dot(a_ref[...], b_ref[...], preferred_element_type=F32).astype(o_ref.dtype)
    for c in range(o_ref.shape[0]):
        o_ref[c] = res[:, c * LANES:(c + 1) * LANES]


def _matmul_bf16(a, w, layer, col0, n, tm=2048, tn=1024):
    m, k = a.shape
    return pl.pallas_call(
        _matmul_kernel,
        grid=(m // tm, n // tn),
        in_specs=[pl.BlockSpec((tm, k), lambda i, j: (i, 0)), pl.BlockSpec((None, k, tn), lambda i, j: (layer, 0, col0 // tn + j))],
        out_specs=pl.BlockSpec((tn // LANES, tm, LANES), lambda i, j: (j, i, 0)),
        out_shape=jax.ShapeDtypeStruct((n // LANES, m, LANES), BF16),
        compiler_params=_cparams(("parallel", "parallel")),
        name="in_proj",
    )(a, w)


def _proj_t_kernel(w_ref, u_ref, cw_ref, cb_ref, o_ref):
    p = lax.dot_general(w_ref[...], u_ref[0], (((1,), (1,)), ((), ())), preferred_element_type=F32)
    seq = p.shape[1]
    t = lax.broadcasted_iota(jnp.int32, p.shape, 1)
    prev = jnp.where(t == 0, 0.0, pltpu.roll(p, 1, axis=1))
    nxt = jnp.where(t == seq - 1, 0.0, pltpu.roll(p, seq - 1, axis=1))
    cw = cw_ref[...]
    o_ref[0] = (prev * cw[:, 0:1] + p * cw[:, 1:2] + nxt * cw[:, 2:3] + cb_ref[...]).astype(o_ref.dtype)


def _proj_t(w_t, u, conv_w, conv_b, tc=256):
    c, d = w_t.shape
    b, seq, _ = u.shape
    return pl.pallas_call(
        _proj_t_kernel,
        grid=(b, c // tc),
        in_specs=[
            pl.BlockSpec((tc, d), lambda i, j: (j, 0)),
            pl.BlockSpec((1, seq, d), lambda i, j: (i, 0, 0)),
            pl.BlockSpec((tc, 3), lambda i, j: (j, 0)),
            pl.BlockSpec((tc, 1), lambda i, j: (j, 0)),
        ],
        out_specs=pl.BlockSpec((1, tc, seq), lambda i, j: (i, j, 0)),
        out_shape=jax.ShapeDtypeStruct((b, c, seq), BF16),
        compiler_params=_cparams(("parallel", "parallel")),
        name="in_proj_t_conv3",
    )(w_t, u, conv_w, conv_b)


def _hyena_filter_kernel(z_ref, w1_ref, b1_ref, w2_ref, b2_ref, w3_ref, b3_ref, fq_ref, w4_ref, dl_ref, t_ref,
                         o_ref, a_scr):
    hp = lax.Precision.HIGHEST

    @pl.when((pl.program_id(0) == 0) & (pl.program_id(1) == 0))
    def _():
        for d in range(2):
            a = z_ref[d]
            for w_ref, b_ref in ((w1_ref, b1_ref), (w2_ref, b2_ref), (w3_ref, b3_ref)):
                a = jnp.sin(fq_ref[...] * (jnp.dot(w_ref[...], a, precision=hp, preferred_element_type=F32) + b_ref[...]))
            a_scr[d] = a

    seq = o_ref.shape[-1]
    proj = lambda d: jnp.dot(w4_ref[0, d].astype(BF16), a_scr[d].astype(BF16), preferred_element_type=F32)
    lo = proj(0) * jnp.exp(-t_ref[0] * dl_ref[...])
    hi = proj(1) * jnp.exp(-t_ref[1] * dl_ref[...])
    hi = jnp.where(lax.broadcasted_iota(jnp.int32, (1, seq), 1) == 0, 0.0, hi)
    norm = jnp.sum(jnp.abs(lo), axis=1, keepdims=True) + jnp.sum(jnp.abs(hi), axis=1, keepdims=True)
    o_ref[0, 0] = (lo / norm).astype(o_ref.dtype)
    o_ref[1, 0] = (hi / norm).astype(o_ref.dtype)


def _hyena_filters_t(seq, c, w1, b1, w2, b2, w3, b3, w4, freq, tc=256):
    tc = min(tc, c)
    fo = w2.shape[0]
    emb = -(-HYENA_EMB_DIM // 8) * 8
    t = jnp.linspace(0.0, 1.0, seq, dtype=F32)
    bands = (HYENA_EMB_DIM - 1) // 2
    fr = jnp.linspace(1e-4, bands - 1, bands, dtype=F32)[:, None]
    mirror = (seq - jnp.arange(seq)) % seq
    tables, times = [], []
    for pos, tp in ((jnp.arange(seq), t), (mirror, jnp.roll(t[::-1], 1))):
        wpos = (2.0 * math.pi / seq) * pos.astype(F32)[None, :]
        tp = tp[None, :]
        z = jnp.concatenate([tp, jnp.cos(fr * wpos), -jnp.sin(fr * wpos),
                             jnp.zeros((emb - HYENA_EMB_DIM, seq), F32)], axis=0)
        tables.append(z)
        times.append(tp)
    z = jnp.stack(tables)
    tt = jnp.stack(times)
    w1_t = jnp.pad(w1.astype(F32).T, ((0, 0), (0, emb - HYENA_EMB_DIM)))
    w4_t = jnp.transpose(w4.astype(F32).reshape(fo, 2, 2, c), (1, 2, 3, 0))
    min_decay = math.log(HYENA_TARGET) / HYENA_SLOW_DECAY
    max_decay = math.log(HYENA_TARGET) / HYENA_FAST_DECAY
    deltas = jnp.abs(jnp.linspace(min_decay, max_decay, c, dtype=F32))[:, None]
    col = lambda v: v.astype(F32)[:, None]
    full = lambda a: pl.BlockSpec(a.shape, lambda f, i: (0,) * a.ndim)
    args = (z, w1_t, col(b1), w2.astype(F32).T, col(b2), w3.astype(F32).T, col(b3), col(freq))
    return pl.pallas_call(
        _hyena_filter_kernel,
        grid=(2, c // tc),
        in_specs=[full(a) for a in args]
                 + [pl.BlockSpec((1, 2, tc, fo), lambda f, i: (f, 0, i, 0)),
                    pl.BlockSpec((tc, 1), lambda f, i: (i, 0)), full(tt)],
        out_specs=pl.BlockSpec((2, 1, tc, seq), lambda f, i: (0, f, i, 0)),
        out_shape=jax.ShapeDtypeStruct((2, 2, c, seq), BF16),
        scratch_shapes=[pltpu.VMEM((2, fo, seq), F32)],
        compiler_params=_cparams(("arbitrary", "arbitrary")),
        name="hyena_filter_mlp",
    )(*args, w4_t, deltas, tt)


def _fft_consts(seq):
    n_fft = 2 * seq
    n1 = n_fft // LANES
    nb = n1 // 2 + 1
    kp = -(-nb // 16) * 16
    k1 = jnp.arange(kp, dtype=jnp.int32)
    keep = (k1 < nb)[:, None]
    two_pi = 2.0 * math.pi

    th = ((k1[:, None] * jnp.arange(n1, dtype=jnp.int32)[None, :]) % n1).astype(F32) * (two_pi / n1)
    f1 = jnp.concatenate([jnp.where(keep, jnp.cos(th), 0.0), jnp.where(keep, -jnp.sin(th), 0.0)], axis=0)
    ph = (k1[:, None] * jnp.arange(LANES, dtype=jnp.int32)[None, :]).astype(F32) * (two_pi / n_fft)
    tw = jnp.stack([jnp.cos(ph), -jnp.sin(ph)])
    om = ((jnp.arange(LANES, dtype=jnp.int32)[:, None] * jnp.arange(LANES, dtype=jnp.int32)[None, :]) % LANES
          ).astype(F32) * (two_pi / LANES)
    cplx = lambda c, s: jnp.concatenate([jnp.concatenate([c, s], axis=1), jnp.concatenate([-s, c], axis=1)], axis=0)
    f2 = cplx(jnp.cos(om), -jnp.sin(om))
    f2i = cplx(jnp.cos(om), jnp.sin(om))
    weight = jnp.where((k1 == 0) | (k1 == nb - 1), 1.0, 2.0)[:, None]
    g = jnp.concatenate([jnp.where(keep, weight * jnp.cos(th[:, :n1 // 2]), 0.0),
                         jnp.where(keep, -weight * jnp.sin(th[:, :n1 // 2]), 0.0)], axis=0).T
    return dict(f1=f1.astype(BF16), tw=tw.astype(BF16), f2=f2.astype(BF16), f2i=f2i.astype(BF16), g=g.astype(BF16), kp=kp)


def _per_channel_matmul(mat_ref, planes, out_scr, ct):
    for c in range(0, ct, 2):
        pair = jnp.concatenate([planes(c), planes(c + 1)], axis=1)
        res = jnp.dot(mat_ref[...], pair, preferred_element_type=F32)
        out_scr[c] = res[:, :LANES].astype(out_scr.dtype)
        out_scr[c + 1] = res[:, LANES:].astype(out_scr.dtype)


def _twiddle_stage2(a, tw_ref, f2_ref):
    ct, kp2, _ = a.shape
    kp = kp2 // 2
    ar, ai = a[:, :kp], a[:, kp:]
    twr, twi = tw_ref[0], tw_ref[1]
    st = jnp.concatenate([ar * twr - ai * twi, ar * twi + ai * twr], axis=2)
    p = jnp.dot(st.reshape(ct * kp, 2 * LANES), f2_ref[...], preferred_element_type=F32).reshape(ct, kp, 2 * LANES)
    return p[:, :, :LANES].astype(BF16), p[:, :, LANES:].astype(BF16)


def _stage2_twiddle_inv(yr, yi, tw_ref, f2i_ref):
    ct, kp, _ = yr.shape
    st = jnp.concatenate([yr, yi], axis=2)
    p = jnp.dot(st.reshape(ct * kp, 2 * LANES), f2i_ref[...], preferred_element_type=F32).reshape(ct, kp, 2 * LANES)
    tr, ti = p[:, :, :LANES].astype(BF16), p[:, :, LANES:].astype(BF16)
    twr, twi = tw_ref[0], tw_ref[1]
    return jnp.concatenate([tr * twr + ti * twi, ti * twr - tr * twi], axis=1)


def _filter_spec_kernel(lo_ref, hi_ref, f1_ref, tw_ref, f2_ref, kr_ref, ki_ref, a_scr):
    ct = lo_ref.shape[2]
    planes = lambda c: jnp.concatenate([lo_ref[0, 0, c], hi_ref[0, 0, c]], axis=0)
    _per_channel_matmul(f1_ref, planes, a_scr, ct)
    xr, xi = _twiddle_stage2(a_scr[...], tw_ref, f2_ref)
    n_fft = f1_ref.shape[1] * LANES
    scale = 1.0 / n_fft
    kr_ref[0] = xr * scale
    ki_ref[0] = xi * scale


def _filter_spectrum(halves, consts, ct=128):
    _, nf, c, seq = halves.shape
    ct = min(ct, c)
    kp = consts["kp"]
    h5 = halves.reshape(2, nf, c, seq // LANES, LANES)
    half = lambda s: pl.BlockSpec((1, 1, ct, seq // LANES, LANES), lambda f, i: (s, f, i, 0, 0))
    full = lambda a: pl.BlockSpec(a.shape, lambda f, i: (0,) * a.ndim)
    spec = jax.ShapeDtypeStruct((nf, c, kp, LANES), BF16)
    out = pl.BlockSpec((1, ct, kp, LANES), lambda f, i: (f, i, 0, 0))
    return pl.pallas_call(
        _filter_spec_kernel,
        grid=(nf, c // ct),
        in_specs=[half(0), half(1), full(consts["f1"]), full(consts["tw"]), full(consts["f2"])],
        out_specs=[out, out],
        out_shape=[spec, spec],
        scratch_shapes=[pltpu.VMEM((ct, 2 * kp, LANES), BF16)],
        compiler_params=_cparams(("parallel", "parallel")),
        name="hyena_filter_spectrum",
    )(h5, h5, consts["f1"], consts["tw"], consts["f2"])


def _hyena_conv_kernel(second, x_ref, m_ref, e_ref, skip_ref, kr_ref, ki_ref, f1_ref, tw_ref, f2_ref, f2i_ref, g_ref,
                       o_ref, a_scr, t_scr, y_scr):
    ct = x_ref.shape[1]
    _per_channel_matmul(f1_ref, lambda c: x_ref[0, c], a_scr, ct)
    xr, xi = _twiddle_stage2(a_scr[...], tw_ref, f2_ref)
    kr, ki = kr_ref[0], ki_ref[0]
    t_scr[...] = _stage2_twiddle_inv(xr * kr - xi * ki, xr * ki + xi * kr, tw_ref, f2i_ref)

    _per_channel_matmul(g_ref, lambda c: t_scr[c], y_scr, ct)

    out = m_ref[0].astype(F32) * (y_scr[...] + x_ref[0].astype(F32) * skip_ref[...])
    if second:
        e = e_ref[0].astype(F32)
        out = out * (e * jax.nn.sigmoid(e))
    o_ref[0] = out.astype(o_ref.dtype)


def _hyena_conv(second, x, x_row0, hy, m_row0, e_row0, skip, kr, ki, filt, consts, ct=128):
    b, _, seq = x.shape
    c = kr.shape[1]
    ct = min(ct, c)
    kp = consts["kp"]
    n1h = seq // LANES
    view = lambda a: a.reshape(a.shape[0], a.shape[1], n1h, LANES)
    rows = lambda blk0: pl.BlockSpec((1, ct, n1h, LANES), lambda j, i: (i, blk0 + j, 0, 0))
    full = lambda a: pl.BlockSpec(a.shape, lambda j, i: (0,) * a.ndim)
    kspec = pl.BlockSpec((1, ct, kp, LANES), lambda j, i: (filt, j, 0, 0))
    mats = [consts["f1"][:, :n1h], consts["tw"], consts["f2"], consts["f2i"], consts["g"]]
    out = pl.pallas_call(
        functools.partial(_hyena_conv_kernel, second),
        grid=(c // ct, b),
        in_specs=[rows(x_row0 // ct), rows(m_row0 // ct), rows(e_row0 // ct),
                  pl.BlockSpec((ct, 1, 1), lambda j, i: (j, 0, 0)),
                  kspec, kspec] + [full(a) for a in mats],
        out_specs=pl.BlockSpec((1, ct, n1h, LANES), lambda j, i: (i, j, 0, 0)),
        out_shape=jax.ShapeDtypeStruct((b, c, n1h, LANES), BF16),
        scratch_shapes=[pltpu.VMEM((ct, 2 * kp, LANES), BF16), pltpu.VMEM((ct, 2 * kp, LANES), BF16),
                        pltpu.VMEM((ct, n1h, LANES), F32)],
        compiler_params=_cparams(("parallel", "parallel")),
        name="hyena_conv2" if second else "hyena_conv1",
    )(view(x), view(hy), view(hy), skip, kr, ki, *mats)
    return out.reshape(b, c, seq)


def _t5_bucket(rel):
    half_b = REL_BUCKETS // 2
    max_exact = half_b // 2
    ret = jnp.where(rel > 0, half_b, 0)
    n = jnp.abs(rel)
    nf = jnp.maximum(n, 1).astype(F32)
    large = max_exact + (jnp.log(nf / max_exact) / math.log(REL_MAX_DIST / max_exact)
                         * (half_b - max_exact)).astype(jnp.int32)
    large = jnp.minimum(large, half_b - 1)
    return ret + jnp.where(n < max_exact, n, large)


def _dilated_bias(rel_bias):
    rel = jnp.arange(ATT_KW)[None, :] - ATT_HALF - jnp.arange(ATT_QB)[:, None]
    col = jnp.arange(ATT_KW)
    edge = jnp.stack([col >= ATT_HALF, col >= 0, col < ATT_KW - ATT_HALF])
    out = []
    for _, dil in DILATED_PATTERNS:
        onehot = jax.nn.one_hot(_t5_bucket(rel * dil), REL_BUCKETS, dtype=F32)
        bias = jnp.einsum("qkb,bh->qkh", onehot, rel_bias.astype(F32), precision=lax.Precision.HIGHEST)
        bias = jnp.where((jnp.abs(rel) <= ATT_HALF)[:, :, None], bias, NEG_INF)
        bias = jnp.transpose(bias, (2, 0, 1))[:, None]
        out.append(jnp.where(edge[None, :, None, :], bias, NEG_INF))
    return jnp.stack(out)


def _head_select(first, second):
    lane = lax.broadcasted_iota(jnp.int32, first.shape[:-1] + (LANES,), len(first.shape) - 1)
    return jnp.where(lane < HEAD_DIM, first, second)


def _dilated_kernel(q_ref, k_ref, v_ref, g_ref, bias_ref, o_ref, *scratch):
    stage_a, stage_b = scratch[0:3], scratch[3:6]
    qd, kd, vd = scratch[6:9]
    last = len(DILATED_PATTERNS) - 1
    states = [scratch[9 + 3 * p:12 + 3 * p] for p in range(last)]
    seq = q_ref.shape[1]
    stage_a[0][...] = q_ref[0].astype(F32) * (HEAD_DIM ** -0.5)
    stage_a[1][...] = k_ref[0].astype(F32)
    stage_a[2][...] = v_ref[0].astype(F32)
    zero_halo = jnp.zeros((ATT_HALF, LANES), BF16)
    first_head = lax.broadcasted_iota(jnp.int32, (ATT_QB, LANES), 1) < HEAD_DIM

    src, dst = stage_a, stage_b
    for pi, (_, dil) in enumerate(DILATED_PATTERNS):
        n = seq // dil
        nblk = n // ATT_QB
        if pi > 0:
            prev_dil = dil // ATT_STEP
            for r1 in range(prev_dil):
                for r2 in range(ATT_STEP):
                    r = r1 + prev_dil * r2
                    for a_src, a_dst in zip(src, dst):
                        a_dst[r * n:(r + 1) * n, :] = a_src[pl.ds(r1 * n * ATT_STEP + r2, n, stride=ATT_STEP), :]
            src, dst = dst, src
        stride_k = n + 2 * ATT_HALF
        qd[...] = src[0][...].astype(BF16)
        for r in range(dil):
            base = r * stride_k
            for a_src, a_dst in ((src[1], kd), (src[2], vd)):
                a_dst[base:base + ATT_HALF, :] = zero_halo
                a_dst[base + ATT_HALF:base + ATT_HALF + n, :] = a_src[r * n:(r + 1) * n, :].astype(BF16)
                a_dst[base + ATT_HALF + n:base + stride_k, :] = zero_halo
        if pi == last:
            states.append(dst)
        acc_w, mx_w, den_w = states[pi]

        def block_group(it, pi=pi, nblk=nblk, acc_w=acc_w, mx_w=mx_w, den_w=den_w):
            chains = []
            for u in range(ATT_UNROLL):
                blk = it * ATT_UNROLL + u
                r = blk // nblk
                qb = blk % nblk
                q0 = blk * ATT_QB
                k0 = q0 + r * (2 * ATT_HALF)
                edge = 0 if qb == 0 else (2 if qb == nblk - 1 else 1)
                qblk = qd[pl.ds(q0, ATT_QB), :]
                qcat = jnp.concatenate([jnp.where(first_head, qblk, jnp.zeros_like(qblk)),
                                        jnp.where(first_head, jnp.zeros_like(qblk), qblk)], axis=0)
                s = lax.dot_general(qcat, kd[pl.ds(k0, ATT_KW), :], (((1,), (1,)), ((), ())),
                                    preferred_element_type=F32)
                chains.append((q0, k0, edge, s))
            probs = []
            for q0, k0, edge, s in chains:
                s = s + jnp.concatenate([bias_ref[pi, 0, edge], bias_ref[pi, 1, edge]], axis=0)
                m = jnp.max(s, axis=1, keepdims=True)
                p = jnp.exp(s - m)
                probs.append((p.astype(BF16), m, jnp.sum(p, axis=1, keepdims=True)))
            for (q0, k0, edge, s), (p, m, l) in zip(chains, probs):
                pv = jnp.dot(p, vd[pl.ds(k0, ATT_KW), :], preferred_element_type=F32)
                acc_w[pl.ds(q0, ATT_QB), :] = _head_select(pv[:ATT_QB], pv[ATT_QB:])
                mx_w[pl.ds(q0, ATT_QB), :] = _head_select(m[:ATT_QB], m[ATT_QB:])
                den_w[pl.ds(q0, ATT_QB), :] = _head_select(l[:ATT_QB], l[ATT_QB:])

        for it in range(dil * nblk // ATT_UNROLL):
            block_group(it)

    for pi in range(len(DILATED_PATTERNS) - 1, 0, -1):
        dil = DILATED_PATTERNS[pi][1]
        n = seq // dil
        prev_dil = dil // ATT_STEP
        acc_c, mx_c, den_c = states[pi - 1]
        acc_f, mx_f, den_f = states[pi]
        for r1 in range(prev_dil):
            for r2 in range(ATT_STEP):
                r = r1 + prev_dil * r2
                rows = pl.ds(r1 * n * ATT_STEP + r2, n, stride=ATT_STEP)
                fine = slice(r * n, (r + 1) * n)
                m_old = mx_c[rows, :]
                m_new = mx_f[fine, :]
                m_all = jnp.maximum(m_old, m_new)
                decay = jnp.exp(-jnp.abs(m_old - m_new))
                a_old = jnp.where(m_old >= m_new, 1.0, decay)
                a_new = jnp.where(m_old >= m_new, decay, 1.0)
                acc_c[rows, :] = acc_c[rows, :] * a_old + acc_f[fine, :] * a_new
                den_c[rows, :] = den_c[rows, :] * a_old + den_f[fine, :] * a_new
                mx_c[rows, :] = m_all

    acc, _, den = states[0]
    g = g_ref[0].astype(F32)
    o_ref[0] = (acc[...] / den[...] * (g * jax.nn.sigmoid(g))).astype(o_ref.dtype)


def _dilated_attention(proj, bias, n_heads):
    _, b, seq, _ = proj.shape
    npair = n_heads // 2
    max_dil = max(d for _, d in DILATED_PATTERNS)
    kd_rows = seq + 2 * ATT_HALF * max_dil
    blk = lambda off: pl.BlockSpec((None, 1, seq, LANES), lambda i, j: (off * npair + j, i, 0, 0))
    state = pltpu.VMEM((seq, LANES), F32)
    return pl.pallas_call(
        _dilated_kernel,
        grid=(b, npair),
        in_specs=[blk(0), blk(1), blk(2), blk(3),
                  pl.BlockSpec((len(DILATED_PATTERNS), 2, 3, ATT_QB, ATT_KW), lambda i, j: (0, j, 0, 0, 0))],
        out_specs=pl.BlockSpec((None, 1, seq, LANES), lambda i, j: (j, i, 0, 0)),
        out_shape=jax.ShapeDtypeStruct((npair, b, seq, LANES), BF16),
        scratch_shapes=[state] * 6
                       + [pltpu.VMEM((seq, LANES), BF16), pltpu.VMEM((kd_rows, LANES), BF16),
                          pltpu.VMEM((kd_rows, LANES), BF16)]
                       + [state] * (3 * (len(DILATED_PATTERNS) - 1)),
        compiler_params=_cparams(("parallel", "parallel")),
        name="dilated_attention",
    )(proj, proj, proj, proj, bias)


def _na_bias(rpb, rows):
    assert min(NA_ROWS, rows) % 2 == 0
    cols = np.arange(GRID_W)
    cs = np.clip(cols - NA_COLS // 2, 0, GRID_W - NA_COLS)
    kc = np.arange(GRID_W)
    inside = (kc[None, :] >= cs[:, None]) & (kc[None, :] < cs[:, None] + NA_COLS)
    col_idx = kc[None, :] - cols[:, None] + NA_COLS - 1
    onehot = (col_idx[:, :, None] == np.arange(2 * NA_COLS - 1)[None, None, :]).astype(np.float32)
    e = jnp.einsum("hrd,cqd->hrcq", rpb.astype(F32), jnp.asarray(onehot), precision=lax.Precision.HIGHEST)
    e = jnp.where(jnp.asarray(inside)[None, None], e, NEG_INF)
    return jnp.concatenate([e[:, :-1], e[:, 1:]], axis=-1)


def _na_kernel(q_ref, k_ref, v_ref, g_ref, bias_ref, o_ref):
    seq = q_ref.shape[1]
    rows = seq // GRID_W
    kr_n = min(NA_ROWS, rows)
    first_head = lax.broadcasted_iota(jnp.int32, (GRID_W, LANES), 1) < HEAD_DIM

    def row_group(it):
        def scores(batch):
            out = []
            for u in range(batch * NA_BATCH, (batch + 1) * NA_BATCH):
                r = it * NA_UNROLL + u
                rs = min(max(r - kr_n // 2, 0), rows - kr_n)
                shift = rs - r + NA_ROWS - 1
                q0 = r * GRID_W
                k0 = rs * GRID_W
                qrow = q_ref[0, pl.ds(q0, GRID_W), :] * (HEAD_DIM ** -0.5)
                qcat = jnp.concatenate([jnp.where(first_head, qrow, jnp.zeros_like(qrow)),
                                        jnp.where(first_head, jnp.zeros_like(qrow), qrow)], axis=0)
                s = lax.dot_general(qcat, k_ref[0, pl.ds(k0, kr_n * GRID_W), :], (((1,), (1,)), ((), ())),
                                    preferred_element_type=F32)
                out.append((q0, k0, shift, s))
            return out

        def finish(chains):
            probs = []
            for q0, k0, shift, s in chains:
                bias = [jnp.concatenate([bias_ref[h, shift + i] for i in range(0, kr_n, 2)], axis=1)
                        for h in range(2)]
                s = s + jnp.concatenate(bias, axis=0)
                p = jnp.exp(s - jnp.max(s, axis=1, keepdims=True))
                probs.append((p.astype(BF16), jnp.sum(p, axis=1, keepdims=True)))
            for (q0, k0, shift, s), (p, l) in zip(chains, probs):
                o = jnp.dot(p, v_ref[0, pl.ds(k0, kr_n * GRID_W), :], preferred_element_type=F32) / l
                g = g_ref[0, pl.ds(q0, GRID_W), :].astype(F32)
                o = _head_select(o[:GRID_W], o[GRID_W:]) * (g * jax.nn.sigmoid(g))
                o_ref[0, pl.ds(q0, GRID_W), :] = o.astype(o_ref.dtype)

        n_batches = NA_UNROLL // NA_BATCH
        pending = scores(0)
        for batch in range(n_batches):
            ahead = scores(batch + 1) if batch + 1 < n_batches else None
            finish(pending)
            pending = ahead

    for it in range(rows // NA_UNROLL):
        row_group(it)


def _neighbourhood_attention(proj, bias, n_heads):
    _, b, seq, _ = proj.shape
    npair = n_heads // 2
    blk = lambda off: pl.BlockSpec((None, 1, seq, LANES), lambda j, i: (off * npair + j, i, 0, 0))
    return pl.pallas_call(
        _na_kernel,
        grid=(npair, b),
        in_specs=[blk(0), blk(1), blk(2), blk(3),
                  pl.BlockSpec((2,) + bias.shape[1:], lambda j, i: (j, 0, 0, 0))],
        out_specs=pl.BlockSpec((None, 1, seq, LANES), lambda j, i: (j, i, 0, 0)),
        out_shape=jax.ShapeDtypeStruct((npair, b, seq, LANES), BF16),
        compiler_params=_cparams(("parallel", "parallel")),
        name="neighbourhood_attention",
    )(proj, proj, proj, proj, bias)


def _out_kernel(n_y, n_t, *refs):
    n = n_y + n_t
    y_refs = refs[:n]
    w_refs = refs[n:2 * n]
    h_ref, p_ref, wp_ref, wg_ref, gpost_ref, gple_ref, gnext_ref, h_out, u_out = refs[2 * n:]
    t = None
    for idx, (y_ref, w_ref) in enumerate(zip(y_refs, w_refs)):
        if idx < n_y:
            y = jnp.concatenate([y_ref[c] for c in range(y_ref.shape[0])], axis=1)
            part = jnp.dot(y, w_ref[...], preferred_element_type=F32)
        else:
            part = lax.dot_general(y_ref[0], w_ref[...], (((0,), (0,)), ((), ())), preferred_element_type=F32)
        t = part if t is None else t + part
    h = h_ref[...] + _rms(t, gpost_ref[...])
    e = _rms(jnp.dot(p_ref[...].astype(BF16), wp_ref[...], preferred_element_type=F32), gple_ref[...])
    gate = jax.nn.sigmoid(jnp.dot(h.astype(BF16), wg_ref[...], preferred_element_type=F32))
    h = h + gate * e
    h_out[...] = h
    u_out[...] = _rms(h, gnext_ref[...]).astype(u_out.dtype)


def _out_block(ys, ys_t, ws, h, p, layer, w_ple, w_gate, g_post, g_ple, g_next, tm=1024):
    m, d = h.shape
    row = lambda a: pl.BlockSpec((tm, a.shape[1]), lambda i: (i, 0))
    slab = lambda a: pl.BlockSpec((a.shape[0], tm, LANES), lambda i: (0, i, 0))
    p_spec = pl.BlockSpec((None, tm, p.shape[2]), lambda i: (layer, i, 0))
    full = lambda a: pl.BlockSpec(a.shape, lambda i: (0, 0))
    vec = lambda g: g.reshape(1, d)

    def col(a):
        tiles = a.shape[2] // tm
        return pl.BlockSpec((1, a.shape[1], tm), lambda i: (i // tiles, 0, i % tiles))

    return pl.pallas_call(
        functools.partial(_out_kernel, len(ys), len(ys_t)),
        grid=(m // tm,),
        in_specs=[slab(y) for y in ys] + [col(y) for y in ys_t] + [full(w) for w in ws]
                 + [row(h), p_spec, full(w_ple), full(w_gate), full(vec(g_post)), full(vec(g_ple)), full(vec(g_next))],
        out_specs=[pl.BlockSpec((tm, d), lambda i: (i, 0)), pl.BlockSpec((tm, d), lambda i: (i, 0))],
        out_shape=[jax.ShapeDtypeStruct((m, d), F32), jax.ShapeDtypeStruct((m, d), BF16)],
        compiler_params=_cparams(("parallel",)),
        name="out_proj_residual_ple",
    )(*ys, *ys_t, *ws, h, p, w_ple, w_gate, vec(g_post), vec(g_ple), vec(g_next))


def kernel(x, p, w_in, w_out, norm_pre, norm_post, hyena_conv_w, hyena_conv_b, hyena_w1, hyena_b1, hyena_w2, hyena_b2, hyena_w3, hyena_b3, hyena_w4, hyena_freq, hyena_skip, rel_bias, na_rpb, ple_proj, ple_norm, ple_gate):
    b, seq, d = x.shape
    depth = w_in.shape[0]
    d_inner = w_out.shape[1]
    hw = d_inner // 2
    m = b * seq
    consts = _fft_consts(seq)
    dil_bias = _dilated_bias(rel_bias)
    h = x.reshape(m, d)
    u = _rmsnorm_bf16(h, norm_pre[0])
    w_all = w_in.astype(BF16)
    p_all = p.reshape(depth, m, -1)
    for i in range(depth):
        j = i // 2
        if i % 2 == 0:
            w_a = w_all[i, :, :4 * hw].T
            ident = jnp.tile(jnp.array([0.0, 1.0, 0.0], F32)[:, None], (1, hw))
            conv_w = jnp.concatenate([hyena_conv_w[j], ident], axis=1).T
            conv_b = jnp.concatenate([hyena_conv_b[j], jnp.zeros((hw,), F32)])[:, None]
            hy = _proj_t(w_a, u.reshape(b, seq, d), conv_w, conv_b)
            halves = _hyena_filters_t(seq, hw, hyena_w1[j], hyena_b1[j], hyena_w2[j], hyena_b2[j],
                                      hyena_w3[j], hyena_b3[j], hyena_w4[j], hyena_freq[j])
            kr, ki = _filter_spectrum(halves, consts)
            skip = hyena_skip[j][:, :, None, None]
            z = _hyena_conv(False, hy, 0, hy, hw, hw, skip[0], kr, ki, 0, consts)
            ya_t = _hyena_conv(True, z, 0, hy, 2 * hw, 3 * hw, skip[1], kr, ki, 1, consts)
            proj_b = _matmul_bf16(u, w_all[i:i + 1, :, 4 * hw:], 0, 0, 4 * hw).reshape(-1, b, seq, LANES)
            yb = _dilated_attention(proj_b, dil_bias, hw // HEAD_DIM)
            ys, ys_t = [yb.reshape(-1, m, LANES)], [ya_t]
            ws = [w_out[i, hw:].astype(BF16), w_out[i, :hw].astype(BF16)]
        else:
            proj = _matmul_bf16(u, w_all, i, 0, 4 * d_inner).reshape(-1, b, seq, LANES)
            yc = _neighbourhood_attention(proj, _na_bias(na_rpb[j], seq // GRID_W), d_inner // HEAD_DIM)
            ys, ys_t = [yc.reshape(-1, m, LANES)], []
            ws = [w_out[i].astype(BF16)]
        g_next = norm_pre[i + 1] if i + 1 < depth else norm_pre[i]
        h, u = _out_block(ys, ys_t, ws, h, p_all, i, ple_proj[i].astype(BF16), ple_gate[i].astype(BF16),
                          norm_post[i], ple_norm[i], g_next)
    return h.reshape(b, seq, d)
```
